```python
import math
import jax, jax.numpy as jnp
from jax import lax
import numpy as np

D_MODEL = 1024
BATCH = 8
SEQ = 2048
DEPTH = 4

CTX_LEN = 256
GRID_W = 64
N_EVEN = (DEPTH + 1) // 2
N_ODD = DEPTH // 2
N_MOD = 6
EPS = 1e-6
NEG_INF = -1e30

FOURIER_GROUPS = 4
FOURIER_GROUP_DIM = 128
FOURIER_WIDTH = FOURIER_GROUPS * FOURIER_GROUP_DIM

N_HEADS = 8
N_KV_HEADS = 2
HEAD_GROUP = N_HEADS // N_KV_HEADS
HEAD_DIM = 64
ATTN_WIDTH = N_HEADS * HEAD_DIM
KV_WIDTH = N_KV_HEADS * HEAD_DIM
WINDOW = 128
BLOCK = WINDOW
ROPE_AXIS_DIM = HEAD_DIM // 2
ROPE_BASE = 10000.0

IN_WIDTH = FOURIER_WIDTH + ATTN_WIDTH + 2 * KV_WIDTH
MIX_WIDTH = FOURIER_WIDTH + ATTN_WIDTH

SSM_GROUP_DIM = 16
SSM_GROUPS = D_MODEL // SSM_GROUP_DIM
SSM_STATE = 64
DT_MIN = 1e-3
DT_MAX = 1e-1

D_FF = 4 * D_MODEL

kernel_name = "hybrid_fourier_swa_s5_diffusion_trunk"


def rmsnorm(x, g):
    xf = x.astype(jnp.float32)
    y = xf * lax.rsqrt(jnp.mean(xf * xf, axis=-1, keepdims=True) + EPS)
    return (y * g.astype(jnp.float32)).astype(x.dtype)


def modulation(cond, w, b):
    m = jax.nn.silu(cond) @ w + b
    m = m.reshape(m.shape[:-1] + (1, N_MOD, D_MODEL))
    return tuple(m[..., i, :] for i in range(N_MOD))


def axial_rope_tables(n):
    rows = n // GRID_W
    row = jnp.repeat(jnp.arange(rows), GRID_W).astype(jnp.float32)
    col = jnp.tile(jnp.arange(GRID_W), rows).astype(jnp.float32)
    inv = ROPE_BASE ** (-jnp.arange(0, ROPE_AXIS_DIM, 2, dtype=jnp.float32) / ROPE_AXIS_DIM)
    ang_r = row[:, None] * inv
    ang_c = col[:, None] * inv
    return (jnp.cos(ang_r), jnp.sin(ang_r), jnp.cos(ang_c), jnp.sin(ang_c))


def _rotate(xh, cos, sin):
    x1, x2 = jnp.split(xh, 2, axis=-1)
    cos = cos[None, :, None, :]
    sin = sin[None, :, None, :]
    return jnp.concatenate([x1 * cos - x2 * sin, x2 * cos + x1 * sin], axis=-1)


def apply_axial_rope(x, rope):
    cos_r, sin_r, cos_c, sin_c = rope
    xf = x.astype(jnp.float32)
    out = jnp.concatenate([_rotate(xf[..., :ROPE_AXIS_DIM], cos_r, sin_r),
                           _rotate(xf[..., ROPE_AXIS_DIM:], cos_c, sin_c)], axis=-1)
    return out.astype(x.dtype)


def fourier_mix(f):
    b, l, _ = f.shape
    fg = f.astype(jnp.float32).reshape(b, l, FOURIER_GROUPS, FOURIER_GROUP_DIM)
    out = jnp.fft.fftn(fg, axes=(1, 3), norm="ortho").real
    return out.reshape(b, l, FOURIER_WIDTH).astype(f.dtype)


def split_projection(p):
    b, l, _ = p.shape
    f = p[..., :FOURIER_WIDTH]
    q = p[..., FOURIER_WIDTH:FOURIER_WIDTH + ATTN_WIDTH].reshape(b, l, N_HEADS, HEAD_DIM)
    k = p[..., FOURIER_WIDTH + ATTN_WIDTH:FOURIER_WIDTH + ATTN_WIDTH + KV_WIDTH].reshape(b, l, N_KV_HEADS, HEAD_DIM)
    v = p[..., FOURIER_WIDTH + ATTN_WIDTH + KV_WIDTH:].reshape(b, l, N_KV_HEADS, HEAD_DIM)
    return f, q, k, v


def window_attention(q, k, v, kc, vc, sink):
    b, s = q.shape[0], q.shape[1]
    nb = s // BLOCK
    n_ctx = kc.shape[1]
    scale = HEAD_DIM ** -0.5
    qb = q.reshape(b, nb, BLOCK, N_KV_HEADS, HEAD_GROUP, HEAD_DIM)

    def band(t):
        tp = jnp.pad(t, ((0, 0), (WINDOW, WINDOW), (0, 0), (0, 0)))
        tp = tp.reshape(b, nb + 2, BLOCK, N_KV_HEADS, HEAD_DIM)
        return jnp.concatenate([tp[:, :-2], tp[:, 1:-1], tp[:, 2:]], axis=2)

    kw = band(k)
    vw = band(v)
    qi = jnp.arange(BLOCK)[:, None]
    kj = jnp.arange(3 * BLOCK)[None, :]
    in_band = (kj - qi >= 0) & (kj - qi <= 2 * WINDOW)
    kpos = jnp.arange(nb)[:, None] * BLOCK - WINDOW + jnp.arange(3 * BLOCK)[None, :]
    in_seq = (kpos >= 0) & (kpos < s)
    mask = in_band[None] & in_seq[:, None, :]

    s_loc = jnp.einsum('bnqhgd,bnkhd->bnhgqk', qb, kw).astype(jnp.float32) * scale
    s_loc = jnp.where(mask[None, :, None, None], s_loc, NEG_INF)
    s_ctx = jnp.einsum('bnqhgd,bchd->bnhgqc', qb, kc).astype(jnp.float32) * scale
    s_sink = jnp.broadcast_to(sink.astype(jnp.float32).reshape(1, 1, N_KV_HEADS, HEAD_GROUP, 1, 1),
                              s_loc.shape[:-1] + (1,))
    p = jax.nn.softmax(jnp.concatenate([s_loc, s_ctx, s_sink], axis=-1), axis=-1).astype(q.dtype)
    p_loc = p[..., :3 * BLOCK]
    p_ctx = p[..., 3 * BLOCK:3 * BLOCK + n_ctx]
    out = (jnp.einsum('bnhgqk,bnkhd->bnqhgd', p_loc, vw)
           + jnp.einsum('bnhgqc,bchd->bnqhgd', p_ctx, vc))
    return out.reshape(b, s, ATTN_WIDTH)


def context_attention(qc, kc, vc, sink):
    b, n_ctx = qc.shape[0], qc.shape[1]
    scale = HEAD_DIM ** -0.5
    qg = qc.reshape(b, n_ctx, N_KV_HEADS, HEAD_GROUP, HEAD_DIM)
    sc = jnp.einsum('bqhgd,bkhd->bhgqk', qg, kc).astype(jnp.float32) * scale
    s_sink = jnp.broadcast_to(sink.astype(jnp.float32).reshape(1, N_KV_HEADS, HEAD_GROUP, 1, 1),
                              sc.shape[:-1] + (1,))
    p = jax.nn.softmax(jnp.concatenate([sc, s_sink], axis=-1), axis=-1)[..., :n_ctx].astype(qc.dtype)
    out = jnp.einsum('bhgqk,bkhd->bqhgd', p, vc)
    return out.reshape(b, n_ctx, ATTN_WIDTH)


def fourier_attention_mixer(h, hc, w_in, w_out, sink, rope, need_ctx):
    f, q, k, v = split_projection(h @ w_in)
    fc, qc, kc, vc = split_projection(hc @ w_in)
    q = apply_axial_rope(q, rope)
    k = apply_axial_rope(k, rope)
    y = jnp.concatenate([fourier_mix(f), window_attention(q, k, v, kc, vc, sink)], axis=-1) @ w_out
    if not need_ctx:
        return y, None
    yc = jnp.concatenate([fourier_mix(fc), context_attention(qc, kc, vc, sink)], axis=-1) @ w_out
    return y, yc


def _linear_recurrence(e1, e2):
    a1, b1 = e1
    a2, b2 = e2
    return a1 * a2, a2 * b1 + b2


def s5_scan(u_seq, a_re, a_im, log_dt, b_re, b_im):
    lam = lax.complex(a_re.astype(jnp.float32), a_im.astype(jnp.float32))
    dt = jnp.exp(log_dt.astype(jnp.float32))[:, None]
    a_bar = jnp.exp(lam * dt)
    b_bar = ((a_bar - 1.0) / lam)[..., None] * lax.complex(b_re.astype(jnp.float32), b_im.astype(jnp.float32))
    b, l, _ = u_seq.shape
    ug = u_seq.reshape(b, l, SSM_GROUPS, SSM_GROUP_DIM)
    bu = lax.complex(jnp.einsum('blgc,gpc->lbgp', ug, b_bar.real),
                     jnp.einsum('blgc,gpc->lbgp', ug, b_bar.imag))
    a_seq = jnp.broadcast_to(a_bar[None, None], (l, 1, SSM_GROUPS, SSM_STATE))
    _, states = lax.associative_scan(_linear_recurrence, (a_seq, bu), axis=0)
    return states


def s5_readout(states, c_re, c_im):
    y = (jnp.einsum('lbgp,gcp->blgc', states.real, c_re.astype(jnp.float32))
         - jnp.einsum('lbgp,gcp->blgc', states.imag, c_im.astype(jnp.float32)))
    return y.reshape(y.shape[0], y.shape[1], D_MODEL)


def _gelu_glu(y, glu_w, dtype):
    z = jax.nn.gelu(y).astype(dtype) @ glu_w
    return z[..., :D_MODEL] * jax.nn.sigmoid(z[..., D_MODEL:])


def s5_mixer(h, hc, a_re, a_im, log_dt, b_re, b_im, c_re, c_im, d_skip, glu_w, need_ctx):
    n_ctx = hc.shape[1]
    u = h.astype(jnp.float32)
    uc = hc.astype(jnp.float32)
    d = d_skip.astype(jnp.float32)
    st_f = s5_scan(jnp.concatenate([uc, u], axis=1), a_re[0], a_im[0], log_dt[0], b_re[0], b_im[0])
    st_b = s5_scan(jnp.flip(jnp.concatenate([u, uc], axis=1), axis=1), a_re[1], a_im[1], log_dt[1], b_re[1], b_im[1])
    y = (s5_readout(st_f[n_ctx:], c_re[0], c_im[0])
         + jnp.flip(s5_readout(st_b[n_ctx:], c_re[1], c_im[1]), axis=1) + d * u)
    out = _gelu_glu(y, glu_w, h.dtype)
    if not need_ctx:
        return out, None
    yc = (s5_readout(st_f[:n_ctx], c_re[0], c_im[0])
          + jnp.flip(s5_readout(st_b[:n_ctx], c_re[1], c_im[1]), axis=1) + d * uc)
    return out, _gelu_glu(yc, glu_w, hc.dtype)


def sq_relu_mlp(h, w1, w2):
    a = jax.nn.relu(h @ w1)
    return (a * a) @ w2


def setup_inputs(seed: int = 0) -> dict:
    key = jax.random.key(seed)
    ks = jax.random.split(key, 24)
    f32 = jnp.float32
    nrm = lambda k, shape, s: jax.random.normal(k, shape, f32) * s
    x = nrm(ks[0], (BATCH, SEQ, D_MODEL), 1.0)
    c = nrm(ks[1], (BATCH, D_MODEL), 1.0)
    ctx = nrm(ks[2], (BATCH, CTX_LEN, D_MODEL), 1.0)
    c_ctx = nrm(ks[3], (D_MODEL,), 1.0)
    mod_w = nrm(ks[4], (DEPTH, D_MODEL, N_MOD * D_MODEL), 0.3 * D_MODEL ** -0.5)
    mod_b = nrm(ks[5], (DEPTH, N_MOD * D_MODEL), 0.01)
    mix_pre_g = 1.0 + nrm(ks[6], (DEPTH, D_MODEL), 0.05)
    mix_post_g = 1.0 + nrm(ks[7], (DEPTH, D_MODEL), 0.05)
    ffn_pre_g = 1.0 + nrm(ks[8], (DEPTH, D_MODEL), 0.05)
    ffn_post_g = 1.0 + nrm(ks[9], (DEPTH, D_MODEL), 0.05)
    ffn_w1 = nrm(ks[10], (DEPTH, D_MODEL, D_FF), D_MODEL ** -0.5)
    ffn_w2 = nrm(ks[11], (DEPTH, D_FF, D_MODEL), D_FF ** -0.5)
    even_w_in = nrm(ks[12], (N_EVEN, D_MODEL, IN_WIDTH), D_MODEL ** -0.5)
    even_w_out = nrm(ks[13], (N_EVEN, MIX_WIDTH, D_MODEL), MIX_WIDTH ** -0.5)
    even_sink = nrm(ks[14], (N_EVEN, N_HEADS), 0.5)
    ssm_shape = (N_ODD, 2, SSM_GROUPS, SSM_STATE)
    ssm_a_re = -0.5 * (1.0 + 0.05 * jax.random.uniform(ks[15], ssm_shape, f32, -1.0, 1.0))
    ssm_a_im = (math.pi * jnp.arange(SSM_STATE, dtype=f32)) + nrm(ks[16], ssm_shape, 0.01)
    ssm_log_dt = jax.random.uniform(ks[17], (N_ODD, 2, SSM_GROUPS), f32, math.log(DT_MIN), math.log(DT_MAX))
    ssm_b_re = nrm(ks[18], (N_ODD, 2, SSM_GROUPS, SSM_STATE, SSM_GROUP_DIM), (2 * SSM_GROUP_DIM) ** -0.5)
    ssm_b_im = nrm(ks[19], (N_ODD, 2, SSM_GROUPS, SSM_STATE, SSM_GROUP_DIM), (2 * SSM_GROUP_DIM) ** -0.5)
    ssm_c_re = nrm(ks[20], (N_ODD, 2, SSM_GROUPS, SSM_GROUP_DIM, SSM_STATE), SSM_STATE ** -0.5)
    ssm_c_im = nrm(ks[21], (N_ODD, 2, SSM_GROUPS, SSM_GROUP_DIM, SSM_STATE), SSM_STATE ** -0.5)
    ssm_d = nrm(ks[22], (N_ODD, D_MODEL), 1.0)
    ssm_glu_w = nrm(ks[23], (N_ODD, D_MODEL, 2 * D_MODEL), D_MODEL ** -0.5)
    return {"x": x, "c": c, "ctx": ctx, "c_ctx": c_ctx,
            "mod_w": mod_w, "mod_b": mod_b,
            "mix_pre_g": mix_pre_g, "mix_post_g": mix_post_g,
            "ffn_pre_g": ffn_pre_g, "ffn_post_g": ffn_post_g,
            "ffn_w1": ffn_w1, "ffn_w2": ffn_w2,
            "even_w_in": even_w_in, "even_w_out": even_w_out, "even_sink": even_sink,
            "ssm_a_re": ssm_a_re, "ssm_a_im": ssm_a_im, "ssm_log_dt": ssm_log_dt,
            "ssm_b_re": ssm_b_re, "ssm_b_im": ssm_b_im,
            "ssm_c_re": ssm_c_re, "ssm_c_im": ssm_c_im,
            "ssm_d": ssm_d, "ssm_glu_w": ssm_glu_w}


def reference(x, c, ctx, c_ctx, mod_w, mod_b, mix_pre_g, mix_post_g, ffn_pre_g, ffn_post_g,
              ffn_w1, ffn_w2, even_w_in, even_w_out, even_sink,
              ssm_a_re, ssm_a_im, ssm_log_dt, ssm_b_re, ssm_b_im, ssm_c_re, ssm_c_im,
              ssm_d, ssm_glu_w):
    rope = axial_rope_tables(x.shape[1])
    xc = ctx.astype(x.dtype)
    for layer in range(DEPTH):
        need_ctx = layer < DEPTH - 1
        sh1, sc1, g1, sh2, sc2, g2 = modulation(c, mod_w[layer], mod_b[layer])
        sh1c, sc1c, g1c, sh2c, sc2c, g2c = modulation(c_ctx, mod_w[layer], mod_b[layer])
        h = rmsnorm(x, mix_pre_g[layer]) * (1.0 + sc1) + sh1
        hc = rmsnorm(xc, mix_pre_g[layer]) * (1.0 + sc1c) + sh1c
        i = layer // 2
        if layer % 2 == 0:
            y, yc = fourier_attention_mixer(h, hc, even_w_in[i], even_w_out[i], even_sink[i], rope, need_ctx)
        else:
            y, yc = s5_mixer(h, hc, ssm_a_re[i], ssm_a_im[i], ssm_log_dt[i], ssm_b_re[i], ssm_b_im[i],
                             ssm_c_re[i], ssm_c_im[i], ssm_d[i], ssm_glu_w[i], need_ctx)
        x = x + g1 * rmsnorm(y, mix_post_g[layer])
        h2 = rmsnorm(x, ffn_pre_g[layer]) * (1.0 + sc2) + sh2
        x = x + g2 * rmsnorm(sq_relu_mlp(h2, ffn_w1[layer], ffn_w2[layer]), ffn_post_g[layer])
        if need_ctx:
            xc = xc + g1c * rmsnorm(yc, mix_post_g[layer])
            hc2 = rmsnorm(xc, ffn_pre_g[layer]) * (1.0 + sc2c) + sh2c
            xc = xc + g2c * rmsnorm(sq_relu_mlp(hc2, ffn_w1[layer], ffn_w2[layer]), ffn_post_g[layer])
    return x
```

```python
import functools
import math

import numpy as np
import jax
import jax.numpy as jnp
from jax import lax
from jax.experimental import pallas as pl
from jax.experimental.pallas import tpu as pltpu

D_MODEL = 1024
DEPTH = 4
N_MOD = 6
EPS = 1e-6
NEG_INF = -1e30
GRID_W = 64

FOURIER_GROUPS = 4
FOURIER_GROUP_DIM = 128
FOURIER_WIDTH = FOURIER_GROUPS * FOURIER_GROUP_DIM

N_HEADS = 8
N_KV_HEADS = 2
HEAD_GROUP = N_HEADS // N_KV_HEADS
HEAD_DIM = 64
ATTN_WIDTH = N_HEADS * HEAD_DIM
KV_WIDTH = N_KV_HEADS * HEAD_DIM
WINDOW = 128
ROPE_AXIS_DIM = HEAD_DIM // 2
ROPE_BASE = 10000.0
IN_WIDTH = FOURIER_WIDTH + ATTN_WIDTH + 2 * KV_WIDTH

SSM_GROUP_DIM = 16
SSM_GROUPS = D_MODEL // SSM_GROUP_DIM
SSM_STATE = 64
SSM_CHUNK = 16
SSM_CW = SSM_CHUNK * SSM_GROUP_DIM
SSM_GROUPS_PER_STEP = 4

D_FF = 4 * D_MODEL

LANES = 128
VMEM_LIMIT = 56 * 1024 * 1024

F32 = jnp.float32
BF16 = jnp.bfloat16


def _cparams(n_axes):
    return pltpu.CompilerParams(dimension_semantics=("arbitrary",) * n_axes, vmem_limit_bytes=VMEM_LIMIT)


def _resident(shape):
    nd = len(shape)
    return pl.BlockSpec(shape, lambda *_: (0,) * nd, pipeline_mode=pl.Buffered(1))


def _rms(x, g):
    return x * lax.rsqrt(jnp.mean(x * x, axis=-1, keepdims=True) + EPS) * g


def _norm_mod(x, g, sc, sh):
    return _rms(x, g) * (1.0 + sc) + sh


def _mod_kernel(cond_ref, w_ref, b_ref, o_ref):
    cond = cond_ref[...]
    s = cond * jax.nn.sigmoid(cond)
    o_ref[0] = jnp.dot(s, w_ref[0], precision=lax.Precision.HIGHEST, preferred_element_type=F32) + b_ref[0]


def _modulation(cond, mod_w, mod_b):
    rows = cond.shape[0]
    tn = 1024
    n = N_MOD * D_MODEL
    return pl.pallas_call(
        _mod_kernel,
        out_shape=jax.ShapeDtypeStruct((DEPTH, rows, n), F32),
        grid=(DEPTH, n // tn),
        in_specs=[pl.BlockSpec((rows, D_MODEL), lambda l, j: (0, 0)),
                  pl.BlockSpec((1, D_MODEL, tn), lambda l, j: (l, 0, j)),
                  pl.BlockSpec((1, 1, tn), lambda l, j: (l, 0, j))],
        out_specs=pl.BlockSpec((1, rows, tn), lambda l, j: (l, 0, j)),
        compiler_params=_cparams(2),
        name="modulation",
    )(cond, mod_w, mod_b.reshape(DEPTH, 1, n))


class _Tokens:
    def __init__(self, batch, seq, n_ctx):
        self.batch, self.seq, self.n_ctx = batch, seq, n_ctx
        self.n_lat = batch * seq
        self.n_all = self.n_lat + batch * n_ctx

    def mod_map(self, tm):
        per_batch = self.seq // tm
        return lambda i: (jnp.minimum(i // per_batch, self.batch), 0, 0)

    def mod_spec(self, tm):
        return pl.BlockSpec((1, N_MOD, D_MODEL), self.mod_map(tm))


def _row_spec(tm, width):
    return pl.BlockSpec((tm, width), lambda i: (i, 0))


def _vec_spec(width):
    return pl.BlockSpec((1, width), lambda i: (0, 0))


FFN_CHUNK = 512


def _ffn_kernel(x_ref, mod_ref, pre_ref, post_ref, w1_ref, w2_ref, o_ref, acc_ref):
    x = x_ref[...]
    h = _norm_mod(x, pre_ref[...], mod_ref[0, 4:5, :], mod_ref[0, 3:4, :]).astype(BF16)
    for c in range(D_FF // FFN_CHUNK):
        sl = slice(c * FFN_CHUNK, (c + 1) * FFN_CHUNK)
        a = jnp.maximum(jnp.dot(h, w1_ref[:, sl], preferred_element_type=F32), 0.0)
        part = jnp.dot((a * a).astype(BF16), w2_ref[sl, :], preferred_element_type=F32)
        if c == 0:
            acc_ref[...] = part
        else:
            acc_ref[...] += part
    o_ref[...] = x + mod_ref[0, 5:6, :] * _rms(acc_ref[...], post_ref[...])


def _ffn(tok, xa, n_rows, mod_l, pre_g, post_g, w1, w2, tm):
    return pl.pallas_call(
        _ffn_kernel,
        out_shape=jax.ShapeDtypeStruct((n_rows, D_MODEL), F32),
        grid=(n_rows // tm,),
        in_specs=[_row_spec(tm, D_MODEL), tok.mod_spec(tm), _vec_spec(D_MODEL), _vec_spec(D_MODEL),
                  _resident((D_MODEL, D_FF)), _resident((D_FF, D_MODEL))],
        out_specs=_row_spec(tm, D_MODEL),
        scratch_shapes=[pltpu.VMEM((tm, D_MODEL), F32)],
        compiler_params=_cparams(1),
        name="ffn",
    )(xa, mod_l, pre_g, post_g, w1, w2)


def _rope_block(x, cos, sin_hi, sin_lo):
    half = ROPE_AXIS_DIM // 2
    return (x * cos + pltpu.roll(x, half, axis=1) * sin_hi
            + pltpu.roll(x, LANES - half, axis=1) * sin_lo)


def _inproj_kernel(x_ref, mod_ref, pre_ref, w_ref, cos_ref, shi_ref, slo_ref, f_ref, q_ref, k_ref, v_ref):
    h = _norm_mod(x_ref[...], pre_ref[...], mod_ref[0, 1:2, :], mod_ref[0, 0:1, :]).astype(BF16)
    p = jnp.dot(h, w_ref[...], preferred_element_type=F32)
    cos, shi, slo = cos_ref[...], shi_ref[...], slo_ref[...]
    f_ref[...] = p[:, :FOURIER_WIDTH].astype(BF16)
    scale = HEAD_DIM ** -0.5
    for j in range(ATTN_WIDTH // LANES):
        lo = FOURIER_WIDTH + j * LANES
        q_ref[:, j * LANES:(j + 1) * LANES] = (_rope_block(p[:, lo:lo + LANES], cos, shi, slo) * scale).astype(BF16)
    k0 = FOURIER_WIDTH + ATTN_WIDTH
    k_ref[...] = _rope_block(p[:, k0:k0 + KV_WIDTH], cos, shi, slo).astype(BF16)
    v_ref[...] = p[:, k0 + KV_WIDTH:].astype(BF16)


def _rope_tables(seq, n_pad):
    pos = np.arange(seq)
    row = (pos // GRID_W).astype(np.float64)
    col = (pos % GRID_W).astype(np.float64)
    lane = np.arange(LANES)
    d = lane % HEAD_DIM
    j = d % (ROPE_AXIS_DIM // 2)
    inv = jnp.asarray(ROPE_BASE, F32) ** (-jnp.asarray(2 * j, F32) / ROPE_AXIS_DIM)
    use_col = jnp.asarray(d >= ROPE_AXIS_DIM)
    posv = jnp.where(use_col[None, :], jnp.asarray(col, F32)[:, None], jnp.asarray(row, F32)[:, None])
    ang = posv * inv[None, :]
    upper = jnp.asarray((d % ROPE_AXIS_DIM) >= ROPE_AXIS_DIM // 2)[None, :]
    cos, sin = jnp.cos(ang), jnp.sin(ang)
    sin_hi = jnp.where(upper, sin, 0.0)
    sin_lo = jnp.where(upper, 0.0, -sin)
    pad = lambda t, v: jnp.concatenate([t, jnp.full((n_pad, LANES), v, F32)], axis=0)
    return pad(cos, 1.0), pad(sin_hi, 0.0), pad(sin_lo, 0.0)


def _inproj(tok, xa, mod_l, pre_g, w_in, tables, tm):
    per_batch = tok.seq // tm
    n_lat_tiles = tok.n_lat // tm
    tab_map = lambda i: (jnp.where(i < n_lat_tiles, i % per_batch, per_batch), 0)
    tab_spec = pl.BlockSpec((tm, LANES), tab_map)
    n = tok.n_all
    return pl.pallas_call(
        _inproj_kernel,
        out_shape=(jax.ShapeDtypeStruct((n, FOURIER_WIDTH), BF16), jax.ShapeDtypeStruct((n, ATTN_WIDTH), BF16),
                   jax.ShapeDtypeStruct((n, KV_WIDTH), BF16), jax.ShapeDtypeStruct((n, KV_WIDTH), BF16)),
        grid=(n // tm,),
        in_specs=[_row_spec(tm, D_MODEL), tok.mod_spec(tm), _vec_spec(D_MODEL), _resident((D_MODEL, IN_WIDTH)),
                  tab_spec, tab_spec, tab_spec],
        out_specs=(_row_spec(tm, FOURIER_WIDTH), _row_spec(tm, ATTN_WIDTH), _row_spec(tm, KV_WIDTH),
                   _row_spec(tm, KV_WIDTH)),
        compiler_params=_cparams(1),
        name="inproj",
    )(xa, mod_l, pre_g, w_in, *tables)


def _head_pair_perm():
    cols = []
    for i in range(HEAD_GROUP):
        for h in (i, i + HEAD_GROUP):
            cols.extend(range(h * HEAD_DIM, (h + 1) * HEAD_DIM))
    return np.asarray(cols)


def _dft_tables(length):
    def cs(n):
        k = np.arange(n)
        ang = 2.0 * np.pi * ((k[:, None] * k[None, :]) % n) / n
        return np.cos(ang) / math.sqrt(n), np.sin(ang) / math.sqrt(n)
    cl, sl = cs(length)
    cc, sc = cs(FOURIER_GROUP_DIM)
    pos = np.concatenate([cl, -sl], axis=1).astype(np.float32)
    chan = np.concatenate([cc, sc], axis=1).astype(np.float32)
    return jnp.asarray(pos).astype(BF16), jnp.asarray(chan).astype(BF16)


def _fourier_kernel(f_ref, chan_ref, pos_ref, o_ref, stk_ref, *, length, row_chunk):
    gd = FOURIER_GROUP_DIM
    for g in range(FOURIER_GROUPS):
        z = jnp.dot(f_ref[:, g * gd:(g + 1) * gd], chan_ref[...], preferred_element_type=F32)
        stk_ref[0:length, g * gd:(g + 1) * gd] = z[:, :gd].astype(BF16)
        stk_ref[length:2 * length, g * gd:(g + 1) * gd] = z[:, gd:].astype(BF16)
    for r in range(length // row_chunk):
        rows = slice(r * row_chunk, (r + 1) * row_chunk)
        o_ref[rows, :] = jnp.dot(pos_ref[rows, :], stk_ref[...], preferred_element_type=F32).astype(BF16)


def _fourier(f_all, batch, length, first_block, prev_out):
    pos_tab, chan_tab = _dft_tables(length)
    kern = functools.partial(_fourier_kernel, length=length, row_chunk=min(length, 512))
    blk = pl.BlockSpec((length, FOURIER_WIDTH), lambda b: (first_block + b, 0))
    in_specs = [blk, _resident(chan_tab.shape), _resident(pos_tab.shape)]
    args = [f_all, chan_tab, pos_tab]
    aliases = {}
    if prev_out is not None:
        in_specs.append(pl.BlockSpec(memory_space=pl.ANY))
        args.append(prev_out)
        aliases = {3: 0}
        body = lambda f, c, p, _prev, o, stk: kern(f, c, p, o, stk)
    else:
        body = kern
    return pl.pallas_call(
        body,
        out_shape=jax.ShapeDtypeStruct(f_all.shape, BF16),
        grid=(batch,),
        in_specs=in_specs,
        out_specs=blk,
        scratch_shapes=[pltpu.VMEM((2 * length, FOURIER_WIDTH), BF16)],
        input_output_aliases=aliases,
        compiler_params=_cparams(1),
        name=f"fourier_{length}",
    )(*args)


def _attn_kernel(sink_ref, q_ref, kp_ref, kc_ref, kn_ref, kx_ref, vp_ref, vc_ref, vn_ref, vx_ref, o_ref,
                 *, n_qblk):
    j = pl.program_id(1)
    blk = WINDOW
    n_loc = 3 * blk
    rows = HEAD_GROUP * blk
    q = jnp.concatenate([q_ref[:, i * LANES:(i + 1) * LANES] for i in range(HEAD_GROUP)], axis=0)
    k_loc = jnp.concatenate([kp_ref[...], kc_ref[...], kn_ref[...]], axis=0)
    v_loc = jnp.concatenate([vp_ref[...], vc_ref[...], vn_ref[...]], axis=0)
    k_ctx, v_ctx = kx_ref[...], vx_ref[...]
    is_lat = j < n_qblk
    col_lo = jnp.where(is_lat, jnp.where(j >= 1, 0, blk), n_loc)
    col_hi = jnp.where(is_lat, jnp.where(j + 1 < n_qblk, n_loc, 2 * blk), 0)
    qi = lax.broadcasted_iota(jnp.int32, (rows, 1), 0) % blk
    kj = lax.broadcasted_iota(jnp.int32, (rows, n_loc), 1)
    mask = (kj >= jnp.maximum(qi, col_lo)) & (kj <= jnp.minimum(qi + 2 * WINDOW, col_hi - 1))
    row_blk = lax.broadcasted_iota(jnp.int32, (rows, 1), 0) // blk
    lane_o = lax.broadcasted_iota(jnp.int32, (rows, LANES), 1)
    nt = (((1,), (1,)), ((), ()))
    outs = []
    for kvh in range(N_KV_HEADS):
        def own_lanes(t):
            lane = lax.broadcasted_iota(jnp.int32, t.shape, 1)
            return jnp.where((lane >= kvh * HEAD_DIM) & (lane < (kvh + 1) * HEAD_DIM), t, jnp.zeros_like(t))
        s_loc = lax.dot_general(q, own_lanes(k_loc), nt, preferred_element_type=F32)
        s_loc = jnp.where(mask, s_loc, NEG_INF)
        s_ctx = lax.dot_general(q, own_lanes(k_ctx), nt, preferred_element_type=F32)
        sink = jnp.zeros((rows, 1), F32)
        for i in range(HEAD_GROUP):
            sink = jnp.where(row_blk == i, sink_ref[kvh * HEAD_GROUP + i], sink)
        m = jnp.maximum(jnp.maximum(jnp.max(s_loc, axis=-1, keepdims=True), jnp.max(s_ctx, axis=-1, keepdims=True)),
                        sink)
        p_loc = jnp.exp(s_loc - m)
        p_ctx = jnp.exp(s_ctx - m)
        denom = (jnp.sum(p_loc, axis=-1, keepdims=True) + jnp.sum(p_ctx, axis=-1, keepdims=True)
                 + jnp.exp(sink - m))
        o = (jnp.dot(p_loc.astype(BF16), v_loc, preferred_element_type=F32)
             + jnp.dot(p_ctx.astype(BF16), v_ctx, preferred_element_type=F32))
        outs.append(o / denom)
    merged = jnp.where(lane_o < HEAD_DIM, outs[0], outs[1])
    for i in range(HEAD_GROUP):
        o_ref[:, i * LANES:(i + 1) * LANES] = merged[i * blk:(i + 1) * blk, :].astype(BF16)


def _attention(tok, q, k, v, sink):
    blk = WINDOW
    n_qblk = tok.seq // blk
    n_cblk = tok.n_ctx // blk
    lat_blocks = tok.n_lat // blk
    ctx0 = tok.n_lat // tok.n_ctx

    def q_map(b, j):
        return (jnp.where(j < n_qblk, b * n_qblk + j, lat_blocks + b * n_cblk + (j - n_qblk)), 0)

    def k_map(off):
        return lambda b, j: (b * n_qblk + jnp.clip(j + off, 0, n_qblk - 1), 0)

    ctx_map = lambda b, j: (ctx0 + b, 0)
    kv_blk = lambda m: pl.BlockSpec((blk, KV_WIDTH), m)
    ctx_blk = pl.BlockSpec((tok.n_ctx, KV_WIDTH), ctx_map)
    kern = functools.partial(_attn_kernel, n_qblk=n_qblk)
    return pl.pallas_call(
        kern,
        out_shape=jax.ShapeDtypeStruct((tok.n_all, ATTN_WIDTH), BF16),
        grid=(tok.batch, n_qblk + n_cblk),
        in_specs=[pl.BlockSpec(memory_space=pltpu.SMEM),
                  pl.BlockSpec((blk, ATTN_WIDTH), q_map),
                  kv_blk(k_map(-1)), kv_blk(k_map(0)), kv_blk(k_map(1)), ctx_blk,
                  kv_blk(k_map(-1)), kv_blk(k_map(0)), kv_blk(k_map(1)), ctx_blk],
        out_specs=pl.BlockSpec((blk, ATTN_WIDTH), q_map),
        compiler_params=_cparams(2),
        name="window_attention",
    )(sink, q, k, k, k, k, v, v, v, v)


def _outproj_kernel(x_ref, fm_ref, ao_ref, mod_ref, post_ref, wf_ref, wa_ref, o_ref):
    y = (jnp.dot(fm_ref[...], wf_ref[...], preferred_element_type=F32)
         + jnp.dot(ao_ref[...], wa_ref[...], preferred_element_type=F32))
    o_ref[...] = x_ref[...] + mod_ref[0, 2:3, :] * _rms(y, post_ref[...])


def _outproj(tok, xa, fm, ao, mod_l, post_g, w_f, w_a, tm):
    n = tok.n_all
    return pl.pallas_call(
        _outproj_kernel,
        out_shape=jax.ShapeDtypeStruct((n, D_MODEL), F32),
        grid=(n // tm,),
        in_specs=[_row_spec(tm, D_MODEL), _row_spec(tm, FOURIER_WIDTH), _row_spec(tm, ATTN_WIDTH), tok.mod_spec(tm),
                  _vec_spec(D_MODEL), _resident((FOURIER_WIDTH, D_MODEL)), _resident((ATTN_WIDTH, D_MODEL))],
        out_specs=_row_spec(tm, D_MODEL),
        compiler_params=_cparams(1),
        name="outproj",
    )(xa, fm, ao, mod_l, post_g, w_f, w_a)


def _prenorm_kernel(x_ref, mod_ref, pre_ref, h_ref):
    h_ref[...] = _norm_mod(x_ref[...], pre_ref[...], mod_ref[0, 1:2, :], mod_ref[0, 0:1, :])


def _prenorm(tok, xa, mod_l, pre_g, tm):
    n = tok.n_all
    return pl.pallas_call(
        _prenorm_kernel,
        out_shape=jax.ShapeDtypeStruct((n, D_MODEL), F32),
        grid=(n // tm,),
        in_specs=[_row_spec(tm, D_MODEL), tok.mod_spec(tm), _vec_spec(D_MODEL)],
        out_specs=_row_spec(tm, D_MODEL),
        compiler_params=_cparams(1),
        name="prenorm",
    )(xa, mod_l, pre_g)


def _s5_operators(a_re, a_im, log_dt, b_re, b_im, c_re, c_im):
    hp = lax.Precision.HIGHEST
    t_n, gd, ns = SSM_CHUNK, SSM_GROUP_DIM, SSM_STATE
    dt = jnp.exp(log_dt.astype(F32))[..., None]
    l_re, l_im = a_re.astype(F32), a_im.astype(F32)
    steps = jnp.arange(t_n + 1, dtype=F32)[None, None, :, None]
    mag = jnp.exp(l_re[:, :, None, :] * dt[:, :, None, :] * steps)
    ang = l_im[:, :, None, :] * dt[:, :, None, :] * steps
    p_re, p_im = mag * jnp.cos(ang), mag * jnp.sin(ang)
    num_re, num_im = p_re[:, :, 1, :] - 1.0, p_im[:, :, 1, :]
    den = l_re * l_re + l_im * l_im
    r_re = (num_re * l_re + num_im * l_im) / den
    r_im = (num_im * l_re - num_re * l_im) / den
    bb_re = r_re[..., None] * b_re - r_im[..., None] * b_im
    bb_im = r_re[..., None] * b_im + r_im[..., None] * b_re
    w_re = c_re[:, :, None] * p_re[:, :, :, None, :] - c_im[:, :, None] * p_im[:, :, :, None, :]
    w_im = c_re[:, :, None] * p_im[:, :, :, None, :] + c_im[:, :, None] * p_re[:, :, :, None, :]
    lag = (jnp.einsum('dgnop,dgpc->dgnco', w_re, bb_re, precision=hp)
           - jnp.einsum('dgnop,dgpc->dgnco', w_im, bb_im, precision=hp))
    t_in = np.arange(t_n)[:, None]
    t_out = np.arange(t_n)[None, :]

    def toeplitz(k, delta):
        blocks = jnp.where(jnp.asarray(delta >= 0)[None, :, :, None, None], k[:, np.clip(delta, 0, t_n)], 0.0)
        return blocks.transpose(0, 1, 3, 2, 4).reshape(-1, t_n * gd, t_n * gd)

    m_intra = toeplitz(lag[0], t_out - t_in) + toeplitz(lag[1], t_in - t_out)

    def increments(d, power):
        pr, pi = p_re[d][:, power, :], p_im[d][:, power, :]
        re = pr[:, :, None, :] * bb_re[d].transpose(0, 2, 1)[:, None] - pi[:, :, None, :] * bb_im[d].transpose(0, 2, 1)[:, None]
        im = pr[:, :, None, :] * bb_im[d].transpose(0, 2, 1)[:, None] + pi[:, :, None, :] * bb_re[d].transpose(0, 2, 1)[:, None]
        return re.reshape(-1, t_n * gd, ns), im.reshape(-1, t_n * gd, ns)

    qf_re, qf_im = increments(0, t_n - 1 - np.arange(t_n))
    qb_re, qb_im = increments(1, np.arange(t_n))
    q = jnp.concatenate([qf_re, qf_im, qf_im, qf_re, qb_re, qb_im, qb_im, qb_re], axis=-1)

    def readout(d, power):
        wr = w_re[d][:, power].transpose(0, 3, 1, 2).reshape(-1, ns, t_n * gd)
        wi = w_im[d][:, power].transpose(0, 3, 1, 2).reshape(-1, ns, t_n * gd)
        return jnp.concatenate([wr, -wi], axis=1)

    mp = jnp.concatenate([m_intra, readout(0, np.arange(t_n) + 1), readout(1, t_n - np.arange(t_n))], axis=1)

    def decay(d):
        ar, ai = p_re[d][:, t_n, :], p_im[d][:, t_n, :]
        return [jnp.concatenate([ar, ar], axis=-1), jnp.concatenate([-ai, ai], axis=-1)]

    dec = jnp.concatenate(decay(0) + decay(1), axis=-1)[:, None, :]
    return q.astype(BF16), mp.astype(BF16), dec


def _s5_kernel(u_ref, q_ref, mp_ref, dec_ref, y_ref, v_ref, xin_ref, *, batch, n_chunks, n_ctx_chunks):
    gps = SSM_GROUPS_PER_STEP
    ns2 = 2 * SSM_STATE
    for g in range(gps):
        v_ref[g] = jnp.dot(u_ref[g], q_ref[g], preferred_element_type=F32)
    a1f, a2f, a1b, a2b = (jnp.broadcast_to(dec_ref[:, :, n * ns2:(n + 1) * ns2], (gps, batch, ns2))
                          for n in range(4))

    def step(k, carry):
        xf, xfs, xb, xbs = carry
        kb = jnp.where(k < n_ctx_chunks, n_ctx_chunks - 1 - k, n_chunks + n_ctx_chunks - 1 - k)
        rf = pl.multiple_of(k * batch, batch)
        rb = pl.multiple_of(kb * batch, batch)
        xin_ref[:, pl.ds(rf, batch), 0:ns2] = xf
        xin_ref[:, pl.ds(rb, batch), ns2:2 * ns2] = xb
        vf = v_ref[:, pl.ds(rf, batch), 0:ns2]
        vfs = v_ref[:, pl.ds(rf, batch), ns2:2 * ns2]
        vb = v_ref[:, pl.ds(rb, batch), 2 * ns2:3 * ns2]
        vbs = v_ref[:, pl.ds(rb, batch), 3 * ns2:4 * ns2]
        return (a1f * xf + a2f * xfs + vf, a1f * xfs - a2f * xf + vfs,
                a1b * xb + a2b * xbs + vb, a1b * xbs - a2b * xb + vbs)

    zero = jnp.zeros((gps, batch, ns2), F32)
    lax.fori_loop(0, n_chunks, step, (zero, zero, zero, zero))
    cw = SSM_CW
    for g in range(gps):
        y_ref[g] = (jnp.dot(u_ref[g], mp_ref[g, 0:cw, :], preferred_element_type=F32)
                    + jnp.dot(xin_ref[g].astype(BF16), mp_ref[g, cw:2 * cw, :], preferred_element_type=F32))


def _s5(u_t, ops, batch, n_chunks, n_ctx_chunks):
    q, mp, dec = ops
    gps = SSM_GROUPS_PER_STEP
    rows = n_chunks * batch
    kern = functools.partial(_s5_kernel, batch=batch, n_chunks=n_chunks, n_ctx_chunks=n_ctx_chunks)
    gspec = lambda r, c: pl.BlockSpec((gps, r, c), lambda i: (i, 0, 0))
    return pl.pallas_call(
        kern,
        out_shape=jax.ShapeDtypeStruct((SSM_GROUPS, rows, SSM_CW), F32),
        grid=(SSM_GROUPS // gps,),
        in_specs=[gspec(rows, SSM_CW), gspec(SSM_CW, 2 * SSM_CW), gspec(2 * SSM_CW, SSM_CW), gspec(1, 2 * SSM_CW)],
        out_specs=gspec(rows, SSM_CW),
        scratch_shapes=[pltpu.VMEM((gps, rows, 2 * SSM_CW), F32), pltpu.VMEM((gps, rows, SSM_CW), F32)],
        compiler_params=_cparams(1),
        name="s5_scan",
    )(u_t, q, mp, dec)


def _to_chunk_major(tok, h):
    t_n, gd = SSM_CHUNK, SSM_GROUP_DIM
    lat = h[:tok.n_lat].reshape(tok.batch, tok.seq // t_n, t_n, SSM_GROUPS, gd)
    ctx = h[tok.n_lat:].reshape(tok.batch, tok.n_ctx // t_n, t_n, SSM_GROUPS, gd)
    both = jnp.concatenate([ctx, lat], axis=1)
    return both.transpose(3, 1, 0, 2, 4).reshape(SSM_GROUPS, -1, t_n * gd)


def _from_chunk_major(tok, y_t, n_rows):
    t_n, gd = SSM_CHUNK, SSM_GROUP_DIM
    n_cc = tok.n_ctx // t_n
    y = y_t.reshape(SSM_GROUPS, -1, tok.batch, t_n, gd).transpose(2, 1, 3, 0, 4)
    lat = y[:, n_cc:].reshape(tok.n_lat, D_MODEL)
    if n_rows == tok.n_lat:
        return lat
    return jnp.concatenate([lat, y[:, :n_cc].reshape(tok.batch * tok.n_ctx, D_MODEL)], axis=0)


def _glu_kernel(x_ref, y_ref, h_ref, d_ref, mod_ref, post_ref, w_ref, o_ref):
    y = y_ref[...] + d_ref[...] * h_ref[...]
    z = jnp.dot(jax.nn.gelu(y).astype(BF16), w_ref[...], preferred_element_type=F32)
    out = z[:, :D_MODEL] * jax.nn.sigmoid(z[:, D_MODEL:])
    o_ref[...] = x_ref[...] + mod_ref[0, 2:3, :] * _rms(out, post_ref[...])


def _glu(tok, xa, n_rows, y_tok, h, d_skip, mod_l, post_g, glu_w, tm):
    return pl.pallas_call(
        _glu_kernel,
        out_shape=jax.ShapeDtypeStruct((n_rows, D_MODEL), F32),
        grid=(n_rows // tm,),
        in_specs=[_row_spec(tm, D_MODEL), _row_spec(tm, D_MODEL), _row_spec(tm, D_MODEL), _vec_spec(D_MODEL),
                  tok.mod_spec(tm), _vec_spec(D_MODEL), _resident((D_MODEL, 2 * D_MODEL))],
        out_specs=_row_spec(tm, D_MODEL),
        compiler_params=_cparams(1),
        name="gelu_glu",
    )(xa, y_tok, h, d_skip, mod_l, post_g, glu_w)


def _tile(limit, *sizes):
    tm = limit
    while any(s % tm for s in sizes):
        tm //= 2
    return tm


def kernel(x, c, ctx, c_ctx, mod_w, mod_b, mix_pre_g, mix_post_g, ffn_pre_g, ffn_post_g, ffn_w1, ffn_w2,
           even_w_in, even_w_out, even_sink, ssm_a_re, ssm_a_im, ssm_log_dt, ssm_b_re, ssm_b_im, ssm_c_re,
           ssm_c_im, ssm_d, ssm_glu_w):
    batch, seq, _ = x.shape
    n_ctx = ctx.shape[1]
    tok = _Tokens(batch, seq, n_ctx)
    assert seq % WINDOW == 0 and n_ctx % WINDOW == 0 and tok.n_lat % n_ctx == 0
    tm = _tile(256, seq, n_ctx)
    tm_wide = _tile(512, seq, batch * n_ctx)

    xa = jnp.concatenate([x.reshape(tok.n_lat, D_MODEL), ctx.astype(x.dtype).reshape(-1, D_MODEL)], axis=0)

    n_cond = 16
    cond = jnp.zeros((n_cond, D_MODEL), F32).at[:batch].set(c).at[batch].set(c_ctx)
    mod = _modulation(cond, mod_w, mod_b).reshape(DEPTH, n_cond, N_MOD, D_MODEL)

    rope = _rope_tables(seq, tm)
    perm = _head_pair_perm()
    vec = lambda g: g.reshape(1, D_MODEL)

    for layer in range(DEPTH):
        need_ctx = layer < DEPTH - 1
        n_rows = tok.n_all if need_ctx else tok.n_lat
        mod_l = mod[layer]
        i = layer // 2
        if layer % 2 == 0:
            w_in = even_w_in[i]
            q0 = FOURIER_WIDTH
            w_in = jnp.concatenate([w_in[:, :q0], w_in[:, q0:q0 + ATTN_WIDTH][:, perm], w_in[:, q0 + ATTN_WIDTH:]],
                                   axis=1).astype(BF16)
            f, q, k, v = _inproj(tok, xa, mod_l, vec(mix_pre_g[layer]), w_in, rope, tm)
            fm = _fourier(f, batch, seq, 0, None)
            fm = _fourier(f, batch, n_ctx, tok.n_lat // n_ctx, fm)
            ao = _attention(tok, q, k, v, even_sink[i])
            w_out = even_w_out[i]
            w_f = w_out[:FOURIER_WIDTH].astype(BF16)
            w_a = w_out[FOURIER_WIDTH:][perm].astype(BF16)
            xa = _outproj(tok, xa, fm, ao, mod_l, vec(mix_post_g[layer]), w_f, w_a, tm)
        else:
            h = _prenorm(tok, xa, mod_l, vec(mix_pre_g[layer]), tm)
            ops = _s5_operators(ssm_a_re[i], ssm_a_im[i], ssm_log_dt[i], ssm_b_re[i], ssm_b_im[i],
                                ssm_c_re[i], ssm_c_im[i])
            u_t = _to_chunk_major(tok, h).astype(BF16)
            n_chunks = (seq + n_ctx) // SSM_CHUNK
            y_t = _s5(u_t, ops, batch, n_chunks, n_ctx // SSM_CHUNK)
            y_tok = _from_chunk_major(tok, y_t, n_rows)
            xa = _glu(tok, xa, n_rows, y_tok, h, vec(ssm_d[i]), mod_l, vec(mix_post_g[layer]),
                      ssm_glu_w[i].astype(BF16), tm)
        xa = _ffn(tok, xa, n_rows, mod_l, vec(ffn_pre_g[layer]), vec(ffn_post_g[layer]),
                  ffn_w1[layer].astype(BF16), ffn_w2[layer].astype(BF16), tm_wide)
    return xa[:tok.n_lat].reshape(batch, seq, D_MODEL)
```

```python
import functools
import math

import numpy as np
import jax
import jax.numpy as jnp
from jax import lax
from jax.experimental import pallas as pl
from jax.experimental.pallas import tpu as pltpu

D_MODEL = 1024
DEPTH = 4
N_MOD = 6
EPS = 1e-6
NEG_INF = -1e30
GRID_W = 64

FOURIER_GROUPS = 4
FOURIER_GROUP_DIM = 128
FOURIER_WIDTH = FOURIER_GROUPS * FOURIER_GROUP_DIM

N_HEADS = 8
N_KV_HEADS = 2
HEAD_GROUP = N_HEADS // N_KV_HEADS
HEAD_DIM = 64
ATTN_WIDTH = N_HEADS * HEAD_DIM
KV_WIDTH = N_KV_HEADS * HEAD_DIM
WINDOW = 128
ROPE_AXIS_DIM = HEAD_DIM // 2
ROPE_BASE = 10000.0
IN_WIDTH = FOURIER_WIDTH + ATTN_WIDTH + 2 * KV_WIDTH

LANES = 128
SUBLANES = 8
VMEM_LIMIT = 56 * 1024 * 1024

SSM_GROUP_DIM = 16
SSM_GROUPS = D_MODEL // SSM_GROUP_DIM
SSM_STATE = 64
SSM_CHUNK = 16
SSM_CW = SSM_CHUNK * SSM_GROUP_DIM
SSM_GROUPS_PER_STEP = 4
SSM_SLABS = D_MODEL // LANES
SSM_PIECES = LANES // SSM_GROUP_DIM

D_FF = 4 * D_MODEL

F32 = jnp.float32
BF16 = jnp.bfloat16


def _cparams(n_axes):
    return pltpu.CompilerParams(dimension_semantics=("arbitrary",) * n_axes, vmem_limit_bytes=VMEM_LIMIT)


def _resident(shape):
    nd = len(shape)
    return pl.BlockSpec(shape, lambda *_: (0,) * nd, pipeline_mode=pl.Buffered(1))


def _rms(x, g):
    return x * lax.rsqrt(jnp.mean(x * x, axis=-1, keepdims=True) + EPS) * g


def _norm_mod(x, g, sc, sh):
    return _rms(x, g) * (1.0 + sc) + sh


def _mod_kernel(cond_ref, w_ref, b_ref, o_ref):
    cond = cond_ref[...]
    s = cond * jax.nn.sigmoid(cond)
    o_ref[0] = jnp.dot(s, w_ref[0], precision=lax.Precision.HIGHEST, preferred_element_type=F32) + b_ref[0]


def _modulation(cond, mod_w, mod_b):
    rows = cond.shape[0]
    tn = 1024
    n = N_MOD * D_MODEL
    return pl.pallas_call(
        _mod_kernel,
        out_shape=jax.ShapeDtypeStruct((DEPTH, rows, n), F32),
        grid=(DEPTH, n // tn),
        in_specs=[pl.BlockSpec((rows, D_MODEL), lambda l, j: (0, 0)),
                  pl.BlockSpec((1, D_MODEL, tn), lambda l, j: (l, 0, j)),
                  pl.BlockSpec((1, 1, tn), lambda l, j: (l, 0, j))],
        out_specs=pl.BlockSpec((1, rows, tn), lambda l, j: (l, 0, j)),
        compiler_params=_cparams(2),
        name="modulation",
    )(cond, mod_w, mod_b.reshape(DEPTH, 1, n))


class _Tokens:
    def __init__(self, batch, seq, n_ctx):
        self.batch, self.seq, self.n_ctx = batch, seq, n_ctx
        self.n_lat = batch * seq
        self.n_all = self.n_lat + batch * n_ctx

    def mod_spec(self, tm):
        per_batch = self.seq // tm
        return pl.BlockSpec((1, N_MOD, D_MODEL), lambda i: (jnp.minimum(i // per_batch, self.batch), 0, 0))

    def pos_grid(self, blk, with_ctx):
        n_lat_blk = self.seq // blk
        n_ctx_blk = self.n_ctx // blk
        lat_blocks = self.n_lat // blk
        row_map = lambda p, b: (jnp.where(p < n_lat_blk, b * n_lat_blk + p, lat_blocks + b * n_ctx_blk + (p - n_lat_blk)), 0)
        mod_map = lambda p, b: (jnp.where(p < n_lat_blk, b, self.batch), 0, 0)
        grid = (n_lat_blk + (n_ctx_blk if with_ctx else 0), self.batch)
        return grid, row_map, mod_map


def _row_spec(tm, width):
    return pl.BlockSpec((tm, width), lambda i: (i, 0))


def _vec_spec(width):
    return pl.BlockSpec((1, width), lambda *_: (0, 0))


FFN_CHUNK = 512


def _ffn_kernel(x_ref, mod_ref, pre_ref, post_ref, w1_ref, w2_ref, o_ref, acc_ref):
    x = x_ref[...]
    h = _norm_mod(x, pre_ref[...], mod_ref[0, 4:5, :], mod_ref[0, 3:4, :]).astype(BF16)
    for c in range(D_FF // FFN_CHUNK):
        sl = slice(c * FFN_CHUNK, (c + 1) * FFN_CHUNK)
        a = jnp.maximum(jnp.dot(h, w1_ref[:, sl], preferred_element_type=F32), 0.0)
        part = jnp.dot((a * a).astype(BF16), w2_ref[sl, :], preferred_element_type=F32)
        if c == 0:
            acc_ref[...] = part
        else:
            acc_ref[...] += part
    o_ref[...] = x + mod_ref[0, 5:6, :] * _rms(acc_ref[...], post_ref[...])


def _ffn(tok, xa, n_rows, mod_l, pre_g, post_g, w1, w2, tm):
    return pl.pallas_call(
        _ffn_kernel,
        out_shape=jax.ShapeDtypeStruct((n_rows, D_MODEL), F32),
        grid=(n_rows // tm,),
        in_specs=[_row_spec(tm, D_MODEL), tok.mod_spec(tm), _vec_spec(D_MODEL), _vec_spec(D_MODEL),
                  _resident((D_MODEL, D_FF)), _resident((D_FF, D_MODEL))],
        out_specs=_row_spec(tm, D_MODEL),
        scratch_shapes=[pltpu.VMEM((tm, D_MODEL), F32)],
        compiler_params=_cparams(1),
        name="ffn",
    )(xa, mod_l, pre_g, post_g, w1, w2)


def _rope_block(x, cos, sin_hi, sin_lo):
    half = ROPE_AXIS_DIM // 2
    return (x * cos + pltpu.roll(x, half, axis=1) * sin_hi
            + pltpu.roll(x, LANES - half, axis=1) * sin_lo)


def _inproj_kernel(x_ref, mod_ref, pre_ref, w_ref, cos_ref, shi_ref, slo_ref, f_ref, q_ref, k_ref, v_ref):
    h = _norm_mod(x_ref[...], pre_ref[...], mod_ref[0, 1:2, :], mod_ref[0, 0:1, :]).astype(BF16)
    p = jnp.dot(h, w_ref[...], preferred_element_type=F32)
    cos, shi, slo = cos_ref[...], shi_ref[...], slo_ref[...]
    f_ref[...] = p[:, :FOURIER_WIDTH].astype(BF16)
    scale = HEAD_DIM ** -0.5
    for j in range(ATTN_WIDTH // LANES):
        lo = FOURIER_WIDTH + j * LANES
        q_ref[:, j * LANES:(j + 1) * LANES] = (_rope_block(p[:, lo:lo + LANES], cos, shi, slo) * scale).astype(BF16)
    k0 = FOURIER_WIDTH + ATTN_WIDTH
    k_ref[...] = _rope_block(p[:, k0:k0 + KV_WIDTH], cos, shi, slo).astype(BF16)
    v_ref[...] = p[:, k0 + KV_WIDTH:].astype(BF16)


def _rope_tables(seq, n_pad):
    pos = np.arange(seq)
    row = (pos // GRID_W).astype(np.float64)
    col = (pos % GRID_W).astype(np.float64)
    lane = np.arange(LANES)
    d = lane % HEAD_DIM
    j = d % (ROPE_AXIS_DIM // 2)
    inv = jnp.asarray(ROPE_BASE, F32) ** (-jnp.asarray(2 * j, F32) / ROPE_AXIS_DIM)
    use_col = jnp.asarray(d >= ROPE_AXIS_DIM)
    posv = jnp.where(use_col[None, :], jnp.asarray(col, F32)[:, None], jnp.asarray(row, F32)[:, None])
    ang = posv * inv[None, :]
    upper = jnp.asarray((d % ROPE_AXIS_DIM) >= ROPE_AXIS_DIM // 2)[None, :]
    cos, sin = jnp.cos(ang), jnp.sin(ang)
    sin_hi = jnp.where(upper, sin, 0.0)
    sin_lo = jnp.where(upper, 0.0, -sin)
    pad = lambda t, v: jnp.concatenate([t, jnp.full((n_pad, LANES), v, F32)], axis=0)
    return pad(cos, 1.0), pad(sin_hi, 0.0), pad(sin_lo, 0.0)


def _inproj(tok, xa, mod_l, pre_g, w_in, tables, tm):
    per_batch = tok.seq // tm
    n_lat_tiles = tok.n_lat // tm
    tab_map = lambda i: (jnp.where(i < n_lat_tiles, i % per_batch, per_batch), 0)
    tab_spec = pl.BlockSpec((tm, LANES), tab_map)
    n = tok.n_all
    return pl.pallas_call(
        _inproj_kernel,
        out_shape=(jax.ShapeDtypeStruct((n, FOURIER_WIDTH), BF16), jax.ShapeDtypeStruct((n, ATTN_WIDTH), BF16),
                   jax.ShapeDtypeStruct((n, KV_WIDTH), BF16), jax.ShapeDtypeStruct((n, KV_WIDTH), BF16)),
        grid=(n // tm,),
        in_specs=[_row_spec(tm, D_MODEL), tok.mod_spec(tm), _vec_spec(D_MODEL), _resident((D_MODEL, IN_WIDTH)),
                  tab_spec, tab_spec, tab_spec],
        out_specs=(_row_spec(tm, FOURIER_WIDTH), _row_spec(tm, ATTN_WIDTH), _row_spec(tm, KV_WIDTH),
                   _row_spec(tm, KV_WIDTH)),
        compiler_params=_cparams(1),
        name="inproj",
    )(xa, mod_l, pre_g, w_in, *tables)


def _head_pair_perm():
    cols = []
    for i in range(HEAD_GROUP):
        for h in (i, i + HEAD_GROUP):
            cols.extend(range(h * HEAD_DIM, (h + 1) * HEAD_DIM))
    return np.asarray(cols)


def _dft_tables(length):
    def cs(n):
        k = np.arange(n)
        ang = 2.0 * np.pi * ((k[:, None] * k[None, :]) % n) / n
        return np.cos(ang) / math.sqrt(n), np.sin(ang) / math.sqrt(n)
    cl, sl = cs(length)
    cc, sc = cs(FOURIER_GROUP_DIM)
    pos = np.concatenate([cl, -sl], axis=1).astype(np.float32)
    chan = np.concatenate([cc, sc], axis=1).astype(np.float32)
    return jnp.asarray(pos).astype(BF16), jnp.asarray(chan).astype(BF16)


def _fourier_kernel(f_ref, chan_ref, pos_ref, o_ref, stk_ref, *, length, row_chunk):
    gd = FOURIER_GROUP_DIM
    for g in range(FOURIER_GROUPS):
        z = jnp.dot(f_ref[:, g * gd:(g + 1) * gd], chan_ref[...], preferred_element_type=F32)
        stk_ref[0:length, g * gd:(g + 1) * gd] = z[:, :gd].astype(BF16)
        stk_ref[length:2 * length, g * gd:(g + 1) * gd] = z[:, gd:].astype(BF16)
    for r in range(length // row_chunk):
        rows = slice(r * row_chunk, (r + 1) * row_chunk)
        o_ref[rows, :] = jnp.dot(pos_ref[rows, :], stk_ref[...], preferred_element_type=F32).astype(BF16)


def _fourier(f_all, batch, length, first_block, prev_out):
    pos_tab, chan_tab = _dft_tables(length)
    kern = functools.partial(_fourier_kernel, length=length, row_chunk=min(length, 512))
    blk = pl.BlockSpec((length, FOURIER_WIDTH), lambda b: (first_block + b, 0))
    in_specs = [blk, _resident(chan_tab.shape), _resident(pos_tab.shape)]
    args = [f_all, chan_tab, pos_tab]
    aliases = {}
    if prev_out is not None:
        in_specs.append(pl.BlockSpec(memory_space=pl.ANY))
        args.append(prev_out)
        aliases = {3: 0}
        body = lambda f, c, p, _prev, o, stk: kern(f, c, p, o, stk)
    else:
        body = kern
    return pl.pallas_call(
        body,
        out_shape=jax.ShapeDtypeStruct(f_all.shape, BF16),
        grid=(batch,),
        in_specs=in_specs,
        out_specs=blk,
        scratch_shapes=[pltpu.VMEM((2 * length, FOURIER_WIDTH), BF16)],
        input_output_aliases=aliases,
        compiler_params=_cparams(1),
        name=f"fourier_{length}",
    )(*args)


def _attn_kernel(sink_ref, q_ref, kp_ref, kc_ref, kn_ref, kx_ref, vp_ref, vc_ref, vn_ref, vx_ref, o_ref,
                 *, n_qblk):
    j = pl.program_id(1)
    blk = WINDOW
    n_loc = 3 * blk
    rows = HEAD_GROUP * blk
    q = jnp.concatenate([q_ref[:, i * LANES:(i + 1) * LANES] for i in range(HEAD_GROUP)], axis=0)
    k_loc = jnp.concatenate([kp_ref[...], kc_ref[...], kn_ref[...]], axis=0)
    v_loc = jnp.concatenate([vp_ref[...], vc_ref[...], vn_ref[...]], axis=0)
    k_ctx, v_ctx = kx_ref[...], vx_ref[...]
    is_lat = j < n_qblk
    col_lo = jnp.where(is_lat, jnp.where(j >= 1, 0, blk), n_loc)
    col_hi = jnp.where(is_lat, jnp.where(j + 1 < n_qblk, n_loc, 2 * blk), 0)
    qi = lax.broadcasted_iota(jnp.int32, (rows, 1), 0) % blk
    kj = lax.broadcasted_iota(jnp.int32, (rows, n_loc), 1)
    mask = (kj >= jnp.maximum(qi, col_lo)) & (kj <= jnp.minimum(qi + 2 * WINDOW, col_hi - 1))
    row_blk = lax.broadcasted_iota(jnp.int32, (rows, 1), 0) // blk
    lane_o = lax.broadcasted_iota(jnp.int32, (rows, LANES), 1)
    nt = (((1,), (1,)), ((), ()))
    outs = []
    for kvh in range(N_KV_HEADS):
        def own_lanes(t):
            lane = lax.broadcasted_iota(jnp.int32, t.shape, 1)
            return jnp.where((lane >= kvh * HEAD_DIM) & (lane < (kvh + 1) * HEAD_DIM), t, jnp.zeros_like(t))
        s_loc = lax.dot_general(q, own_lanes(k_loc), nt, preferred_element_type=F32)
        s_loc = jnp.where(mask, s_loc, NEG_INF)
        s_ctx = lax.dot_general(q, own_lanes(k_ctx), nt, preferred_element_type=F32)
        sink = jnp.zeros((rows, 1), F32)
        for i in range(HEAD_GROUP):
            sink = jnp.where(row_blk == i, sink_ref[kvh * HEAD_GROUP + i], sink)
        m = jnp.maximum(jnp.maximum(jnp.max(s_loc, axis=-1, keepdims=True), jnp.max(s_ctx, axis=-1, keepdims=True)),
                        sink)
        p_loc = jnp.exp(s_loc - m)
        p_ctx = jnp.exp(s_ctx - m)
        denom = (jnp.sum(p_loc, axis=-1, keepdims=True) + jnp.sum(p_ctx, axis=-1, keepdims=True)
                 + jnp.exp(sink - m))
        o = (jnp.dot(p_loc.astype(BF16), v_loc, preferred_element_type=F32)
             + jnp.dot(p_ctx.astype(BF16), v_ctx, preferred_element_type=F32))
        outs.append(o / denom)
    merged = jnp.where(lane_o < HEAD_DIM, outs[0], outs[1])
    for i in range(HEAD_GROUP):
        o_ref[:, i * LANES:(i + 1) * LANES] = merged[i * blk:(i + 1) * blk, :].astype(BF16)


def _attention(tok, q, k, v, sink):
    blk = WINDOW
    n_qblk = tok.seq // blk
    n_cblk = tok.n_ctx // blk
    lat_blocks = tok.n_lat // blk
    ctx0 = tok.n_lat // tok.n_ctx

    def q_map(b, j):
        return (jnp.where(j < n_qblk, b * n_qblk + j, lat_blocks + b * n_cblk + (j - n_qblk)), 0)

    def k_map(off):
        return lambda b, j: (b * n_qblk + jnp.clip(j + off, 0, n_qblk - 1), 0)

    ctx_map = lambda b, j: (ctx0 + b, 0)
    kv_blk = lambda m: pl.BlockSpec((blk, KV_WIDTH), m)
    ctx_blk = pl.BlockSpec((tok.n_ctx, KV_WIDTH), ctx_map)
    kern = functools.partial(_attn_kernel, n_qblk=n_qblk)
    return pl.pallas_call(
        kern,
        out_shape=jax.ShapeDtypeStruct((tok.n_all, ATTN_WIDTH), BF16),
        grid=(tok.batch, n_qblk + n_cblk),
        in_specs=[pl.BlockSpec(memory_space=pltpu.SMEM),
                  pl.BlockSpec((blk, ATTN_WIDTH), q_map),
                  kv_blk(k_map(-1)), kv_blk(k_map(0)), kv_blk(k_map(1)), ctx_blk,
                  kv_blk(k_map(-1)), kv_blk(k_map(0)), kv_blk(k_map(1)), ctx_blk],
        out_specs=pl.BlockSpec((blk, ATTN_WIDTH), q_map),
        compiler_params=_cparams(2),
        name="window_attention",
    )(sink, q, k, k, k, k, v, v, v, v)


def _outproj_kernel(x_ref, fm_ref, ao_ref, mod_ref, post_ref, wf_ref, wa_ref, o_ref):
    y = (jnp.dot(fm_ref[...], wf_ref[...], preferred_element_type=F32)
         + jnp.dot(ao_ref[...], wa_ref[...], preferred_element_type=F32))
    o_ref[...] = x_ref[...] + mod_ref[0, 2:3, :] * _rms(y, post_ref[...])


def _outproj(tok, xa, fm, ao, mod_l, post_g, w_f, w_a, tm):
    n = tok.n_all
    return pl.pallas_call(
        _outproj_kernel,
        out_shape=jax.ShapeDtypeStruct((n, D_MODEL), F32),
        grid=(n // tm,),
        in_specs=[_row_spec(tm, D_MODEL), _row_spec(tm, FOURIER_WIDTH), _row_spec(tm, ATTN_WIDTH), tok.mod_spec(tm),
                  _vec_spec(D_MODEL), _resident((FOURIER_WIDTH, D_MODEL)), _resident((ATTN_WIDTH, D_MODEL))],
        out_specs=_row_spec(tm, D_MODEL),
        compiler_params=_cparams(1),
        name="outproj",
    )(xa, fm, ao, mod_l, post_g, w_f, w_a)


def _piece_transpose(vregs):
    piece = lax.broadcasted_iota(jnp.int32, (SUBLANES, LANES), 1) // SSM_GROUP_DIM
    v = list(vregs)
    d = SSM_PIECES // 2
    while d >= 1:
        keep_low = (piece & d) == 0
        for a in range(SSM_PIECES):
            if a & d:
                continue
            lo, hi = v[a], v[a + d]
            v[a] = jnp.where(keep_low, lo, pltpu.roll(hi, d * SSM_GROUP_DIM, axis=1))
            v[a + d] = jnp.where(keep_low, pltpu.roll(lo, LANES - d * SSM_GROUP_DIM, axis=1), hi)
        d //= 2
    return v


def _slab_pitch(blk):
    return blk + SUBLANES


def _s5_pre_kernel(x_ref, mod_ref, pre_ref, u_ref, hs_ref, *, blk, batch):
    b = pl.program_id(1)
    pitch = _slab_pitch(blk)
    h = _norm_mod(x_ref[...], pre_ref[...], mod_ref[0, 1:2, :], mod_ref[0, 0:1, :])
    row0 = pl.multiple_of(b * pitch, SUBLANES)
    for s in range(SSM_SLABS):
        hs_ref[s, pl.ds(row0, blk), :] = h[:, s * LANES:(s + 1) * LANES]

    @pl.when(b == batch - 1)
    def _():
        half = SSM_CHUNK // 2

        def slab(s, carry):
            for ip in range(blk // (2 * SSM_CHUNK)):
                for hh in range(2):
                    outs = []
                    for i in (2 * ip, 2 * ip + 1):
                        t0 = i * SSM_CHUNK + hh * half
                        rows = [hs_ref[s, pl.ds(t0 + j, batch, stride=pitch), :] for j in range(half)]
                        outs.append(_piece_transpose(rows))
                    for g in range(SSM_PIECES):
                        pair = jnp.concatenate([outs[0][g], outs[1][g]], axis=0).astype(BF16)
                        u_ref[s * SSM_PIECES + g, 2 * ip * batch:(2 * ip + 2) * batch, hh * LANES:(hh + 1) * LANES] = pair
            return carry

        lax.fori_loop(0, SSM_SLABS, slab, 0)


def _s5_pre(tok, xa, mod_l, pre_g, blk):
    grid, row_map, mod_map = tok.pos_grid(blk, True)
    n_lat_blk = tok.seq // blk
    n_ctx_blk = tok.n_ctx // blk
    cpb = blk // SSM_CHUNK
    n_chunks = (tok.seq + tok.n_ctx) // SSM_CHUNK
    u_map = lambda p, b: (0, jnp.where(p < n_lat_blk, n_ctx_blk + p, p - n_lat_blk), 0)
    kern = functools.partial(_s5_pre_kernel, blk=blk, batch=tok.batch)
    return pl.pallas_call(
        kern,
        out_shape=jax.ShapeDtypeStruct((SSM_GROUPS, n_chunks * tok.batch, SSM_CW), BF16),
        grid=grid,
        in_specs=[pl.BlockSpec((blk, D_MODEL), row_map), pl.BlockSpec((1, N_MOD, D_MODEL), mod_map),
                  _vec_spec(D_MODEL)],
        out_specs=pl.BlockSpec((SSM_GROUPS, cpb * tok.batch, SSM_CW), u_map),
        scratch_shapes=[pltpu.VMEM((SSM_SLABS, tok.batch * _slab_pitch(blk), LANES), F32)],
        compiler_params=_cparams(2),
        name="s5_pre",
    )(xa, mod_l, pre_g)


def _cmul(ar, ai, br, bi):
    return ar * br - ai * bi, ar * bi + ai * br


def _s5_factors(a_re, a_im, log_dt, b_re, b_im, c_re, c_im):
    t_n, gd, ns = SSM_CHUNK, SSM_GROUP_DIM, SSM_STATE
    dt = jnp.exp(log_dt.astype(F32))[..., None]
    l_re, l_im = a_re.astype(F32), a_im.astype(F32)
    z_re, z_im = l_re * dt, l_im * dt

    def apow(d, n):
        n = jnp.asarray(n, F32)[None, :, None]
        mag = jnp.exp(z_re[d][:, None, :] * n)
        ang = z_im[d][:, None, :] * n
        return mag * jnp.cos(ang), mag * jnp.sin(ang)

    t = np.arange(t_n)
    es, fs, qs, ps, decs = [], [], [], [], []
    for d in range(2):
        a1_re, a1_im = apow(d, [1.0])
        num_re, num_im = a1_re[:, 0] - 1.0, a1_im[:, 0]
        den = l_re[d] * l_re[d] + l_im[d] * l_im[d]
        r_re = (num_re * l_re[d] + num_im * l_im[d]) / den
        r_im = (num_im * l_re[d] - num_re * l_im[d]) / den
        bb_re, bb_im = _cmul(r_re[..., None], r_im[..., None], b_re[d], b_im[d])
        bb_re, bb_im = bb_re.transpose(0, 2, 1)[:, None], bb_im.transpose(0, 2, 1)[:, None]
        cc_re, cc_im = c_re[d].transpose(0, 2, 1)[:, :, None, :], c_im[d].transpose(0, 2, 1)[:, :, None, :]

        def rows_tc(power):
            pr, pi = apow(d, power)
            re, im = _cmul(pr[:, :, None, :], pi[:, :, None, :], bb_re, bb_im)
            return re.reshape(-1, t_n * gd, ns), im.reshape(-1, t_n * gd, ns)

        def cols_tc(power):
            pr, pi = apow(d, power)
            pr, pi = pr.transpose(0, 2, 1)[..., None], pi.transpose(0, 2, 1)[..., None]
            re, im = _cmul(cc_re, cc_im, pr, pi)
            return re.reshape(-1, ns, t_n * gd), im.reshape(-1, ns, t_n * gd)

        sign = 1.0 if d == 0 else -1.0
        e_re, e_im = rows_tc(-sign * t)
        f_re, f_im = cols_tc(sign * t)
        es.append(jnp.concatenate([e_re, e_im], axis=-1))
        fs.append(jnp.concatenate([f_re, -f_im], axis=1))
        q_re, q_im = rows_tc((t_n - 1 - t) if d == 0 else t)
        qs += [q_re, q_im, q_im, q_re]
        p_re, p_im = cols_tc((t + 1) if d == 0 else (t_n - t))
        ps += [p_re, -p_im]
        ar, ai = apow(d, [float(t_n)])
        decs += [ar[:, 0], ar[:, 0], -ai[:, 0], ai[:, 0]]
    e = jnp.concatenate(es, axis=-1)
    f = jnp.concatenate(fs, axis=1)
    q = jnp.concatenate(qs, axis=-1).astype(BF16)
    p = jnp.concatenate(ps, axis=1).astype(BF16)
    dec = jnp.concatenate(decs, axis=-1)[:, None, :]
    return e, f, q, p, dec


def _s5_kernel(u_ref, e_ref, f_ref, q_ref, p_ref, dec_ref, y_ref, v_ref, xin_ref, *, batch, n_chunks, n_ctx_chunks):
    gps = SSM_GROUPS_PER_STEP
    ns2 = 2 * SSM_STATE
    cw = SSM_CW
    for g in range(gps):
        v_ref[g] = jnp.dot(u_ref[g], q_ref[g], preferred_element_type=F32)
    a1f, a2f, a1b, a2b = (jnp.broadcast_to(dec_ref[:, :, n * ns2:(n + 1) * ns2], (gps, batch, ns2))
                          for n in range(4))

    def step(k, carry):
        xf, xfs, xb, xbs = carry
        kb = jnp.where(k < n_ctx_chunks, n_ctx_chunks - 1 - k, n_chunks + n_ctx_chunks - 1 - k)
        rf = pl.multiple_of(k * batch, batch)
        rb = pl.multiple_of(kb * batch, batch)
        xin_ref[:, pl.ds(rf, batch), 0:ns2] = xf
        xin_ref[:, pl.ds(rb, batch), ns2:2 * ns2] = xb
        vf = v_ref[:, pl.ds(rf, batch), 0:ns2]
        vfs = v_ref[:, pl.ds(rf, batch), ns2:2 * ns2]
        vb = v_ref[:, pl.ds(rb, batch), 2 * ns2:3 * ns2]
        vbs = v_ref[:, pl.ds(rb, batch), 3 * ns2:4 * ns2]
        return (a1f * xf + a2f * xfs + vf, a1f * xfs - a2f * xf + vfs,
                a1b * xb + a2b * xbs + vb, a1b * xbs - a2b * xb + vbs)

    zero = jnp.zeros((gps, batch, ns2), F32)
    lax.fori_loop(0, n_chunks, step, (zero, zero, zero, zero))
    t_in = lax.broadcasted_iota(jnp.int32, (cw, cw), 0) // SSM_GROUP_DIM
    t_out = lax.broadcasted_iota(jnp.int32, (cw, cw), 1) // SSM_GROUP_DIM
    hp = lax.Precision.HIGHEST
    for g in range(gps):
        m_f = jnp.dot(e_ref[g, :, 0:ns2], f_ref[g, 0:ns2, :], precision=hp, preferred_element_type=F32)
        m_b = jnp.dot(e_ref[g, :, ns2:2 * ns2], f_ref[g, ns2:2 * ns2, :], precision=hp, preferred_element_type=F32)
        m = (jnp.where(t_out >= t_in, m_f, 0.0) + jnp.where(t_in >= t_out, m_b, 0.0)).astype(BF16)
        y_ref[g] = (jnp.dot(u_ref[g], m, preferred_element_type=F32)
                    + jnp.dot(xin_ref[g].astype(BF16), p_ref[g], preferred_element_type=F32))


def _s5(u_t, factors, batch, n_chunks, n_ctx_chunks):
    e, f, q, p, dec = factors
    gps = SSM_GROUPS_PER_STEP
    rows = n_chunks * batch
    kern = functools.partial(_s5_kernel, batch=batch, n_chunks=n_chunks, n_ctx_chunks=n_ctx_chunks)
    gspec = lambda r, c: pl.BlockSpec((gps, r, c), lambda i: (i, 0, 0))
    return pl.pallas_call(
        kern,
        out_shape=jax.ShapeDtypeStruct((SSM_GROUPS, rows, SSM_CW), F32),
        grid=(SSM_GROUPS // gps,),
        in_specs=[gspec(rows, SSM_CW), gspec(SSM_CW, SSM_CW), gspec(SSM_CW, SSM_CW), gspec(SSM_CW, 2 * SSM_CW),
                  gspec(SSM_CW, SSM_CW), gspec(1, 2 * SSM_CW)],
        out_specs=gspec(rows, SSM_CW),
        scratch_shapes=[pltpu.VMEM((gps, rows, 2 * SSM_CW), F32), pltpu.VMEM((gps, rows, SSM_CW), F32)],
        compiler_params=_cparams(1),
        name="s5_scan",
    )(u_t, e, f, q, p, dec)


def _s5_post_kernel(x_ref, y_ref, d_ref, mod_ref, pre_ref, post_ref, w_ref, o_ref, ys_ref, *, blk, batch):
    b = pl.program_id(1)
    pitch = _slab_pitch(blk)

    @pl.when(b == 0)
    def _():
        half = SSM_CHUNK // 2

        def slab(s, carry):
            for i in range(blk // SSM_CHUNK):
                for hh in range(2):
                    cols = [y_ref[s * SSM_PIECES + g, i * batch:(i + 1) * batch, hh * LANES:(hh + 1) * LANES]
                            for g in range(SSM_PIECES)]
                    rows = _piece_transpose(cols)
                    t0 = i * SSM_CHUNK + hh * half
                    for j in range(half):
                        ys_ref[s, pl.ds(t0 + j, batch, stride=pitch), :] = rows[j]
            return carry

        lax.fori_loop(0, SSM_SLABS, slab, 0)

    x = x_ref[...]
    h = _norm_mod(x, pre_ref[...], mod_ref[0, 1:2, :], mod_ref[0, 0:1, :])
    row0 = pl.multiple_of(b * pitch, SUBLANES)
    y_ssm = jnp.concatenate([ys_ref[s, pl.ds(row0, blk), :] for s in range(SSM_SLABS)], axis=1)
    y = y_ssm + d_ref[...] * h
    z = jnp.dot(jax.nn.gelu(y).astype(BF16), w_ref[...], preferred_element_type=F32)
    out = z[:, :D_MODEL] * jax.nn.sigmoid(z[:, D_MODEL:])
    o_ref[...] = x + mod_ref[0, 2:3, :] * _rms(out, post_ref[...])


def _s5_post(tok, xa, y_t, d_skip, mod_l, pre_g, post_g, glu_w, blk, with_ctx):
    grid, row_map, mod_map = tok.pos_grid(blk, with_ctx)
    n_lat_blk = tok.seq // blk
    n_ctx_blk = tok.n_ctx // blk
    cpb = blk // SSM_CHUNK
    y_map = lambda p, b: (0, jnp.where(p < n_lat_blk, n_ctx_blk + p, p - n_lat_blk), 0)
    n_rows = tok.n_all if with_ctx else tok.n_lat
    kern = functools.partial(_s5_post_kernel, blk=blk, batch=tok.batch)
    return pl.pallas_call(
        kern,
        out_shape=jax.ShapeDtypeStruct((n_rows, D_MODEL), F32),
        grid=grid,
        in_specs=[pl.BlockSpec((blk, D_MODEL), row_map),
                  pl.BlockSpec((SSM_GROUPS, cpb * tok.batch, SSM_CW), y_map),
                  _vec_spec(D_MODEL), pl.BlockSpec((1, N_MOD, D_MODEL), mod_map), _vec_spec(D_MODEL),
                  _vec_spec(D_MODEL), _resident((D_MODEL, 2 * D_MODEL))],
        out_specs=pl.BlockSpec((blk, D_MODEL), row_map),
        scratch_shapes=[pltpu.VMEM((SSM_SLABS, tok.batch * _slab_pitch(blk), LANES), F32)],
        compiler_params=_cparams(2),
        name="s5_post",
    )(xa, y_t, d_skip, mod_l, pre_g, post_g, glu_w)


def _tile(limit, *sizes):
    tm = limit
    while any(s % tm for s in sizes):
        tm //= 2
    return tm


def kernel(x, c, ctx, c_ctx, mod_w, mod_b, mix_pre_g, mix_post_g, ffn_pre_g, ffn_post_g, ffn_w1, ffn_w2,
           even_w_in, even_w_out, even_sink, ssm_a_re, ssm_a_im, ssm_log_dt, ssm_b_re, ssm_b_im, ssm_c_re,
           ssm_c_im, ssm_d, ssm_glu_w):
    batch, seq, _ = x.shape
    n_ctx = ctx.shape[1]
    tok = _Tokens(batch, seq, n_ctx)
    assert seq % WINDOW == 0 and n_ctx % WINDOW == 0 and tok.n_lat % n_ctx == 0
    assert batch == SUBLANES
    tm = _tile(256, seq, n_ctx)
    tm_wide = _tile(512, seq, batch * n_ctx)

    xa = jnp.concatenate([x.reshape(tok.n_lat, D_MODEL), ctx.astype(x.dtype).reshape(-1, D_MODEL)], axis=0)

    n_cond = 2 * SUBLANES
    cond = jnp.zeros((n_cond, D_MODEL), F32).at[:batch].set(c).at[batch].set(c_ctx)
    mod = _modulation(cond, mod_w, mod_b).reshape(DEPTH, n_cond, N_MOD, D_MODEL)

    rope = _rope_tables(seq, tm)
    perm = _head_pair_perm()
    vec = lambda g: g.reshape(1, D_MODEL)

    for layer in range(DEPTH):
        need_ctx = layer < DEPTH - 1
        n_rows = tok.n_all if need_ctx else tok.n_lat
        mod_l = mod[layer]
        i = layer // 2
        if layer % 2 == 0:
            w_in = even_w_in[i]
            q0 = FOURIER_WIDTH
            w_in = jnp.concatenate([w_in[:, :q0], w_in[:, q0:q0 + ATTN_WIDTH][:, perm], w_in[:, q0 + ATTN_WIDTH:]],
                                   axis=1).astype(BF16)
            f, q, k, v = _inproj(tok, xa, mod_l, vec(mix_pre_g[layer]), w_in, rope, tm)
            fm = _fourier(f, batch, seq, 0, None)
            fm = _fourier(f, batch, n_ctx, tok.n_lat // n_ctx, fm)
            ao = _attention(tok, q, k, v, even_sink[i])
            w_out = even_w_out[i]
            w_f = w_out[:FOURIER_WIDTH].astype(BF16)
            w_a = w_out[FOURIER_WIDTH:][perm].astype(BF16)
            xa = _outproj(tok, xa, fm, ao, mod_l, vec(mix_post_g[layer]), w_f, w_a, tm)
        else:
            pre_g = vec(mix_pre_g[layer])
            u_t = _s5_pre(tok, xa, mod_l, pre_g, tm)
            factors = _s5_factors(ssm_a_re[i], ssm_a_im[i], ssm_log_dt[i], ssm_b_re[i], ssm_b_im[i],
                                  ssm_c_re[i], ssm_c_im[i])
            y_t = _s5(u_t, factors, batch, (seq + n_ctx) // SSM_CHUNK, n_ctx // SSM_CHUNK)
            xa = _s5_post(tok, xa, y_t, vec(ssm_d[i]), mod_l, pre_g, vec(mix_post_g[layer]),
                          ssm_glu_w[i].astype(BF16), tm, need_ctx)
        xa = _ffn(tok, xa, n_rows, mod_l, vec(ffn_pre_g[layer]), vec(ffn_post_g[layer]),
                  ffn_w1[layer].astype(BF16), ffn_w2[layer].astype(BF16), tm_wide)
    return xa[:tok.n_lat].reshape(batch, seq, D_MODEL)
```

```python
import functools
import math

import numpy as np
import jax
import jax.numpy as jnp
from jax import lax
from jax.experimental import pallas as pl
from jax.experimental.pallas import tpu as pltpu

D_MODEL = 1024
DEPTH = 4
N_MOD = 6
EPS = 1e-6
NEG_INF = -1e30
GRID_W = 64

FOURIER_GROUPS = 4
FOURIER_GROUP_DIM = 128
FOURIER_WIDTH = FOURIER_GROUPS * FOURIER_GROUP_DIM

N_HEADS = 8
N_KV_HEADS = 2
HEAD_GROUP = N_HEADS // N_KV_HEADS
HEAD_DIM = 64
ATTN_WIDTH = N_HEADS * HEAD_DIM
KV_WIDTH = N_KV_HEADS * HEAD_DIM
WINDOW = 128
ROPE_AXIS_DIM = HEAD_DIM // 2
ROPE_BASE = 10000.0
IN_WIDTH = FOURIER_WIDTH + ATTN_WIDTH + 2 * KV_WIDTH

LANES = 128
SUBLANES = 8
VMEM_LIMIT = 56 * 1024 * 1024

SSM_GROUP_DIM = 16
SSM_GROUPS = D_MODEL // SSM_GROUP_DIM
SSM_STATE = 64
SSM_CHUNK = 16
SSM_CW = SSM_CHUNK * SSM_GROUP_DIM
SSM_GROUPS_PER_STEP = 4
SSM_SLABS = D_MODEL // LANES
SSM_PIECES = LANES // SSM_GROUP_DIM

D_FF = 4 * D_MODEL

F32 = jnp.float32
BF16 = jnp.bfloat16


def _cparams(n_axes):
    return pltpu.CompilerParams(dimension_semantics=("arbitrary",) * n_axes, vmem_limit_bytes=VMEM_LIMIT)


def _resident(shape):
    nd = len(shape)
    return pl.BlockSpec(shape, lambda *_: (0,) * nd, pipeline_mode=pl.Buffered(1))


def _rms(x, g):
    return x * lax.rsqrt(jnp.mean(x * x, axis=-1, keepdims=True) + EPS) * g


def _norm_mod(x, g, sc, sh):
    return _rms(x, g) * (1.0 + sc) + sh


def _mod_kernel(cond_ref, w_ref, b_ref, o_ref):
    cond = cond_ref[...]
    s = cond * jax.nn.sigmoid(cond)
    o_ref[0] = jnp.dot(s, w_ref[0], precision=lax.Precision.HIGHEST, preferred_element_type=F32) + b_ref[0]


def _modulation(cond, mod_w, mod_b):
    rows = cond.shape[0]
    tn = 1024
    n = N_MOD * D_MODEL
    return pl.pallas_call(
        _mod_kernel,
        out_shape=jax.ShapeDtypeStruct((DEPTH, rows, n), F32),
        grid=(DEPTH, n // tn),
        in_specs=[pl.BlockSpec((rows, D_MODEL), lambda l, j: (0, 0)),
                  pl.BlockSpec((1, D_MODEL, tn), lambda l, j: (l, 0, j)),
                  pl.BlockSpec((1, 1, tn), lambda l, j: (l, 0, j))],
        out_specs=pl.BlockSpec((1, rows, tn), lambda l, j: (l, 0, j)),
        compiler_params=_cparams(2),
        name="modulation",
    )(cond, mod_w, mod_b.reshape(DEPTH, 1, n))


class _Tokens:
    def __init__(self, batch, seq, n_ctx):
        self.batch, self.seq, self.n_ctx = batch, seq, n_ctx
        self.n_lat = batch * seq
        self.n_all = self.n_lat + batch * n_ctx

    def mod_spec(self, tm):
        per_batch = self.seq // tm
        return pl.BlockSpec((1, N_MOD, D_MODEL), lambda i: (jnp.minimum(i // per_batch, self.batch), 0, 0))

    def pos_grid(self, blk, with_ctx):
        n_lat_blk = self.seq // blk
        n_ctx_blk = self.n_ctx // blk
        lat_blocks = self.n_lat // blk
        row_map = lambda p, b: (jnp.where(p < n_lat_blk, b * n_lat_blk + p, lat_blocks + b * n_ctx_blk + (p - n_lat_blk)), 0)
        mod_map = lambda p, b: (jnp.where(p < n_lat_blk, b, self.batch), 0, 0)
        grid = (n_lat_blk + (n_ctx_blk if with_ctx else 0), self.batch)
        return grid, row_map, mod_map


def _row_spec(tm, width):
    return pl.BlockSpec((tm, width), lambda i: (i, 0))


def _vec_spec(width):
    return pl.BlockSpec((1, width), lambda *_: (0, 0))


FFN_CHUNK = 512


def _ffn_body(x, mod_ref, pre_ref, post_ref, w1_ref, w2_ref, o_ref, acc_ref):
    h = _norm_mod(x, pre_ref[...], mod_ref[0, 4:5, :], mod_ref[0, 3:4, :]).astype(BF16)
    for c in range(D_FF // FFN_CHUNK):
        sl = slice(c * FFN_CHUNK, (c + 1) * FFN_CHUNK)
        a = jnp.maximum(jnp.dot(h, w1_ref[:, sl], preferred_element_type=F32), 0.0)
        part = jnp.dot((a * a).astype(BF16), w2_ref[sl, :], preferred_element_type=F32)
        if c == 0:
            acc_ref[...] = part
        else:
            acc_ref[...] += part
    o_ref[...] = x + mod_ref[0, 5:6, :] * _rms(acc_ref[...], post_ref[...])


def _ffn_kernel(x_ref, mod_ref, pre_ref, post_ref, w1_ref, w2_ref, o_ref, acc_ref):
    _ffn_body(x_ref[...], mod_ref, pre_ref, post_ref, w1_ref, w2_ref, o_ref, acc_ref)


def _ffn(tok, xa, n_rows, mod_l, pre_g, post_g, w1, w2, tm):
    return pl.pallas_call(
        _ffn_kernel,
        out_shape=jax.ShapeDtypeStruct((n_rows, D_MODEL), F32),
        grid=(n_rows // tm,),
        in_specs=[_row_spec(tm, D_MODEL), tok.mod_spec(tm), _vec_spec(D_MODEL), _vec_spec(D_MODEL),
                  _resident((D_MODEL, D_FF)), _resident((D_FF, D_MODEL))],
        out_specs=_row_spec(tm, D_MODEL),
        scratch_shapes=[pltpu.VMEM((tm, D_MODEL), F32)],
        compiler_params=_cparams(1),
        name="ffn",
    )(xa, mod_l, pre_g, post_g, w1, w2)


def _mix_ffn_kernel(x_ref, fl_ref, fc_ref, ao_ref, mod_ref, mpost_ref, pre_ref, post_ref, wf_ref, wa_ref,
                    w1_ref, w2_ref, o_ref, acc_ref, *, n_lat_tiles):
    fm = jnp.where(pl.program_id(0) < n_lat_tiles, fl_ref[...], fc_ref[...])
    y = (jnp.dot(fm, wf_ref[...], preferred_element_type=F32)
         + jnp.dot(ao_ref[...], wa_ref[...], preferred_element_type=F32))
    x = x_ref[...] + mod_ref[0, 2:3, :] * _rms(y, mpost_ref[...])
    _ffn_body(x, mod_ref, pre_ref, post_ref, w1_ref, w2_ref, o_ref, acc_ref)


def _mix_ffn(tok, xa, fm_lat, fm_ctx, ao, mod_l, mix_post_g, pre_g, post_g, w_f, w_a, w1, w2, tm):
    n = tok.n_all
    n_lat_tiles = tok.n_lat // tm
    lat_spec = pl.BlockSpec((tm, FOURIER_WIDTH), lambda i: (jnp.minimum(i, n_lat_tiles - 1), 0))
    ctx_spec = pl.BlockSpec((tm, FOURIER_WIDTH), lambda i: (jnp.maximum(i - n_lat_tiles, 0), 0))
    kern = functools.partial(_mix_ffn_kernel, n_lat_tiles=n_lat_tiles)
    return pl.pallas_call(
        kern,
        out_shape=jax.ShapeDtypeStruct((n, D_MODEL), F32),
        grid=(n // tm,),
        in_specs=[_row_spec(tm, D_MODEL), lat_spec, ctx_spec, _row_spec(tm, ATTN_WIDTH), tok.mod_spec(tm),
                  _vec_spec(D_MODEL), _vec_spec(D_MODEL), _vec_spec(D_MODEL),
                  _resident((FOURIER_WIDTH, D_MODEL)), _resident((ATTN_WIDTH, D_MODEL)),
                  _resident((D_MODEL, D_FF)), _resident((D_FF, D_MODEL))],
        out_specs=_row_spec(tm, D_MODEL),
        scratch_shapes=[pltpu.VMEM((tm, D_MODEL), F32)],
        compiler_params=_cparams(1),
        name="mix_ffn",
    )(xa, fm_lat, fm_ctx, ao, mod_l, mix_post_g, pre_g, post_g, w_f, w_a, w1, w2)


def _rope_block(x, cos, sin_hi, sin_lo):
    half = ROPE_AXIS_DIM // 2
    return (x * cos + pltpu.roll(x, half, axis=1) * sin_hi
            + pltpu.roll(x, LANES - half, axis=1) * sin_lo)


def _inproj_kernel(x_ref, mod_ref, pre_ref, w_ref, cos_ref, shi_ref, slo_ref, f_ref, q_ref, k_ref, v_ref):
    h = _norm_mod(x_ref[...], pre_ref[...], mod_ref[0, 1:2, :], mod_ref[0, 0:1, :]).astype(BF16)
    p = jnp.dot(h, w_ref[...], preferred_element_type=F32)
    cos, shi, slo = cos_ref[...], shi_ref[...], slo_ref[...]
    f_ref[...] = p[:, :FOURIER_WIDTH].astype(BF16)
    scale = HEAD_DIM ** -0.5
    for j in range(ATTN_WIDTH // LANES):
        lo = FOURIER_WIDTH + j * LANES
        q_ref[:, j * LANES:(j + 1) * LANES] = (_rope_block(p[:, lo:lo + LANES], cos, shi, slo) * scale).astype(BF16)
    k0 = FOURIER_WIDTH + ATTN_WIDTH
    k_ref[...] = _rope_block(p[:, k0:k0 + KV_WIDTH], cos, shi, slo).astype(BF16)
    v_ref[...] = p[:, k0 + KV_WIDTH:].astype(BF16)


def _rope_tables(seq, n_pad):
    pos = np.arange(seq)
    row = (pos // GRID_W).astype(np.float64)
    col = (pos % GRID_W).astype(np.float64)
    lane = np.arange(LANES)
    d = lane % HEAD_DIM
    j = d % (ROPE_AXIS_DIM // 2)
    inv = jnp.asarray(ROPE_BASE, F32) ** (-jnp.asarray(2 * j, F32) / ROPE_AXIS_DIM)
    use_col = jnp.asarray(d >= ROPE_AXIS_DIM)
    posv = jnp.where(use_col[None, :], jnp.asarray(col, F32)[:, None], jnp.asarray(row, F32)[:, None])
    ang = posv * inv[None, :]
    upper = jnp.asarray((d % ROPE_AXIS_DIM) >= ROPE_AXIS_DIM // 2)[None, :]
    cos, sin = jnp.cos(ang), jnp.sin(ang)
    sin_hi = jnp.where(upper, sin, 0.0)
    sin_lo = jnp.where(upper, 0.0, -sin)
    pad = lambda t, v: jnp.concatenate([t, jnp.full((n_pad, LANES), v, F32)], axis=0)
    return pad(cos, 1.0), pad(sin_hi, 0.0), pad(sin_lo, 0.0)


def _inproj(tok, xa, mod_l, pre_g, w_in, tables, tm):
    per_batch = tok.seq // tm
    n_lat_tiles = tok.n_lat // tm
    tab_map = lambda i: (jnp.where(i < n_lat_tiles, i % per_batch, per_batch), 0)
    tab_spec = pl.BlockSpec((tm, LANES), tab_map)
    n = tok.n_all
    return pl.pallas_call(
        _inproj_kernel,
        out_shape=(jax.ShapeDtypeStruct((n, FOURIER_WIDTH), BF16), jax.ShapeDtypeStruct((n, ATTN_WIDTH), BF16),
                   jax.ShapeDtypeStruct((n, KV_WIDTH), BF16), jax.ShapeDtypeStruct((n, KV_WIDTH), BF16)),
        grid=(n // tm,),
        in_specs=[_row_spec(tm, D_MODEL), tok.mod_spec(tm), _vec_spec(D_MODEL), _resident((D_MODEL, IN_WIDTH)),
                  tab_spec, tab_spec, tab_spec],
        out_specs=(_row_spec(tm, FOURIER_WIDTH), _row_spec(tm, ATTN_WIDTH), _row_spec(tm, KV_WIDTH),
                   _row_spec(tm, KV_WIDTH)),
        compiler_params=_cparams(1),
        name="inproj",
    )(xa, mod_l, pre_g, w_in, *tables)


def _head_pair_perm():
    cols = []
    for i in range(HEAD_GROUP):
        for h in (i, i + HEAD_GROUP):
            cols.extend(range(h * HEAD_DIM, (h + 1) * HEAD_DIM))
    return np.asarray(cols)


def _dft_tables(length):
    def cs(n):
        k = np.arange(n)
        ang = 2.0 * np.pi * ((k[:, None] * k[None, :]) % n) / n
        return np.cos(ang) / math.sqrt(n), np.sin(ang) / math.sqrt(n)
    cl, sl = cs(length)
    cc, sc = cs(FOURIER_GROUP_DIM)
    pos = np.concatenate([cl, -sl], axis=1).astype(np.float32)
    chan = np.concatenate([cc, sc], axis=1).astype(np.float32)
    return jnp.asarray(pos).astype(BF16), jnp.asarray(chan).astype(BF16)


def _fourier_one(f_ref, chan_ref, pos_ref, o_ref, stk_ref):
    gd = FOURIER_GROUP_DIM
    length = f_ref.shape[0]
    row_chunk = min(length, 512)
    for g in range(FOURIER_GROUPS):
        z = jnp.dot(f_ref[:, g * gd:(g + 1) * gd], chan_ref[...], preferred_element_type=F32)
        stk_ref[0:length, g * gd:(g + 1) * gd] = z[:, :gd].astype(BF16)
        stk_ref[length:2 * length, g * gd:(g + 1) * gd] = z[:, gd:].astype(BF16)
    for r in range(length // row_chunk):
        rows = slice(r * row_chunk, (r + 1) * row_chunk)
        o_ref[rows, :] = jnp.dot(pos_ref[rows, :], stk_ref[...], preferred_element_type=F32).astype(BF16)


def _fourier_kernel(fl_ref, fc_ref, chan_ref, posl_ref, posc_ref, ol_ref, oc_ref, stkl_ref, stkc_ref):
    _fourier_one(fl_ref, chan_ref, posl_ref, ol_ref, stkl_ref)
    _fourier_one(fc_ref, chan_ref, posc_ref, oc_ref, stkc_ref)


def _fourier(tok, f_all, tabs_lat, tabs_ctx):
    pos_l, chan = tabs_lat
    pos_c, _ = tabs_ctx
    ctx0 = tok.n_lat // tok.n_ctx
    lat_blk = lambda m: pl.BlockSpec((tok.seq, FOURIER_WIDTH), m)
    ctx_blk = lambda m: pl.BlockSpec((tok.n_ctx, FOURIER_WIDTH), m)
    return pl.pallas_call(
        _fourier_kernel,
        out_shape=(jax.ShapeDtypeStruct((tok.n_lat, FOURIER_WIDTH), BF16),
                   jax.ShapeDtypeStruct((tok.batch * tok.n_ctx, FOURIER_WIDTH), BF16)),
        grid=(tok.batch,),
        in_specs=[lat_blk(lambda b: (b, 0)), ctx_blk(lambda b: (ctx0 + b, 0)), _resident(chan.shape),
                  _resident(pos_l.shape), _resident(pos_c.shape)],
        out_specs=(lat_blk(lambda b: (b, 0)), ctx_blk(lambda b: (b, 0))),
        scratch_shapes=[pltpu.VMEM((2 * tok.seq, FOURIER_WIDTH), BF16),
                        pltpu.VMEM((2 * tok.n_ctx, FOURIER_WIDTH), BF16)],
        compiler_params=_cparams(1),
        name="fourier",
    )(f_all, f_all, chan, pos_l, pos_c)


def _attn_kernel(sink_ref, q_ref, kp_ref, kc_ref, kn_ref, kx_ref, vp_ref, vc_ref, vn_ref, vx_ref, o_ref,
                 *, n_qblk):
    j = pl.program_id(1)
    blk = WINDOW
    n_loc = 3 * blk
    rows = HEAD_GROUP * blk
    q = jnp.concatenate([q_ref[:, i * LANES:(i + 1) * LANES] for i in range(HEAD_GROUP)], axis=0)
    k_loc = jnp.concatenate([kp_ref[...], kc_ref[...], kn_ref[...]], axis=0)
    v_loc = jnp.concatenate([vp_ref[...], vc_ref[...], vn_ref[...]], axis=0)
    k_ctx, v_ctx = kx_ref[...], vx_ref[...]
    is_lat = j < n_qblk
    col_lo = jnp.where(is_lat, jnp.where(j >= 1, 0, blk), n_loc)
    col_hi = jnp.where(is_lat, jnp.where(j + 1 < n_qblk, n_loc, 2 * blk), 0)
    qi = lax.broadcasted_iota(jnp.int32, (rows, 1), 0) % blk
    kj = lax.broadcasted_iota(jnp.int32, (rows, n_loc), 1)
    mask = (kj >= jnp.maximum(qi, col_lo)) & (kj <= jnp.minimum(qi + 2 * WINDOW, col_hi - 1))
    row_blk = lax.broadcasted_iota(jnp.int32, (rows, 1), 0) // blk
    lane_o = lax.broadcasted_iota(jnp.int32, (rows, LANES), 1)
    nt = (((1,), (1,)), ((), ()))
    outs = []
    for kvh in range(N_KV_HEADS):
        def own_lanes(t):
            lane = lax.broadcasted_iota(jnp.int32, t.shape, 1)
            return jnp.where((lane >= kvh * HEAD_DIM) & (lane < (kvh + 1) * HEAD_DIM), t, jnp.zeros_like(t))
        s_loc = lax.dot_general(q, own_lanes(k_loc), nt, preferred_element_type=F32)
        s_loc = jnp.where(mask, s_loc, NEG_INF)
        s_ctx = lax.dot_general(q, own_lanes(k_ctx), nt, preferred_element_type=F32)
        sink = jnp.zeros((rows, 1), F32)
        for i in range(HEAD_GROUP):
            sink = jnp.where(row_blk == i, sink_ref[kvh * HEAD_GROUP + i], sink)
        m = jnp.maximum(jnp.maximum(jnp.max(s_loc, axis=-1, keepdims=True), jnp.max(s_ctx, axis=-1, keepdims=True)),
                        sink)
        p_loc = jnp.exp(s_loc - m)
        p_ctx = jnp.exp(s_ctx - m)
        denom = (jnp.sum(p_loc, axis=-1, keepdims=True) + jnp.sum(p_ctx, axis=-1, keepdims=True)
                 + jnp.exp(sink - m))
        o = (jnp.dot(p_loc.astype(BF16), v_loc, preferred_element_type=F32)
             + jnp.dot(p_ctx.astype(BF16), v_ctx, preferred_element_type=F32))
        outs.append(o / denom)
    merged = jnp.where(lane_o < HEAD_DIM, outs[0], outs[1])
    for i in range(HEAD_GROUP):
        o_ref[:, i * LANES:(i + 1) * LANES] = merged[i * blk:(i + 1) * blk, :].astype(BF16)


def _attention(tok, q, k, v, sink):
    blk = WINDOW
    n_qblk = tok.seq // blk
    n_cblk = tok.n_ctx // blk
    lat_blocks = tok.n_lat // blk
    ctx0 = tok.n_lat // tok.n_ctx

    def q_map(b, j):
        return (jnp.where(j < n_qblk, b * n_qblk + j, lat_blocks + b * n_cblk + (j - n_qblk)), 0)

    def k_map(off):
        return lambda b, j: (b * n_qblk + jnp.clip(j + off, 0, n_qblk - 1), 0)

    ctx_map = lambda b, j: (ctx0 + b, 0)
    kv_blk = lambda m: pl.BlockSpec((blk, KV_WIDTH), m)
    ctx_blk = pl.BlockSpec((tok.n_ctx, KV_WIDTH), ctx_map)
    kern = functools.partial(_attn_kernel, n_qblk=n_qblk)
    return pl.pallas_call(
        kern,
        out_shape=jax.ShapeDtypeStruct((tok.n_all, ATTN_WIDTH), BF16),
        grid=(tok.batch, n_qblk + n_cblk),
        in_specs=[pl.BlockSpec(memory_space=pltpu.SMEM),
                  pl.BlockSpec((blk, ATTN_WIDTH), q_map),
                  kv_blk(k_map(-1)), kv_blk(k_map(0)), kv_blk(k_map(1)), ctx_blk,
                  kv_blk(k_map(-1)), kv_blk(k_map(0)), kv_blk(k_map(1)), ctx_blk],
        out_specs=pl.BlockSpec((blk, ATTN_WIDTH), q_map),
        compiler_params=_cparams(2),
        name="window_attention",
    )(sink, q, k, k, k, k, v, v, v, v)


def _piece_perm():
    idx = np.arange(D_MODEL)
    a, b, c = idx // LANES, (idx // SSM_GROUP_DIM) % SSM_PIECES, idx % SSM_GROUP_DIM
    perm = np.zeros((D_MODEL, D_MODEL), np.float32)
    perm[idx, b * LANES + a * SSM_GROUP_DIM + c] = 1.0
    return jnp.asarray(perm).astype(BF16)


def _slab_pitch(blk):
    return blk + SUBLANES


def _s5_pre_kernel(x_ref, mod_ref, pre_ref, perm_ref, u_ref, hs_ref, *, blk, batch):
    b = pl.program_id(1)
    pitch = _slab_pitch(blk)
    h = _norm_mod(x_ref[...], pre_ref[...], mod_ref[0, 1:2, :], mod_ref[0, 0:1, :])
    row0 = pl.multiple_of(b * pitch, SUBLANES)
    for s in range(SSM_SLABS):
        hs_ref[s, pl.ds(row0, blk), :] = h[:, s * LANES:(s + 1) * LANES]

    @pl.when(b == batch - 1)
    def _():
        half = SSM_CHUNK // 2
        cpb = blk // SSM_CHUNK
        rows_h = cpb * batch

        def slab(s, carry):
            lhs = []
            for hh in range(2):
                for i in range(cpb):
                    t0 = i * SSM_CHUNK + hh * half
                    lhs.append(jnp.concatenate(
                        [hs_ref[s, pl.ds(t0 + j, batch, stride=pitch), :] for j in range(half)], axis=1))
            lhs = jnp.concatenate(lhs, axis=0).astype(BF16)
            out = jnp.dot(lhs, perm_ref[...], preferred_element_type=F32).astype(BF16)
            for hh in range(2):
                for g in range(SSM_PIECES):
                    u_ref[s * SSM_PIECES + g, :, hh * LANES:(hh + 1) * LANES] = (
                        out[hh * rows_h:(hh + 1) * rows_h, g * LANES:(g + 1) * LANES])
            return carry

        lax.fori_loop(0, SSM_SLABS, slab, 0)


def _s5_pre(tok, xa, mod_l, pre_g, perm, blk):
    grid, row_map, mod_map = tok.pos_grid(blk, True)
    n_lat_blk = tok.seq // blk
    n_ctx_blk = tok.n_ctx // blk
    cpb = blk // SSM_CHUNK
    n_chunks = (tok.seq + tok.n_ctx) // SSM_CHUNK
    u_map = lambda p, b: (0, jnp.where(p < n_lat_blk, n_ctx_blk + p, p - n_lat_blk), 0)
    kern = functools.partial(_s5_pre_kernel, blk=blk, batch=tok.batch)
    return pl.pallas_call(
        kern,
        out_shape=jax.ShapeDtypeStruct((SSM_GROUPS, n_chunks * tok.batch, SSM_CW), BF16),
        grid=grid,
        in_specs=[pl.BlockSpec((blk, D_MODEL), row_map), pl.BlockSpec((1, N_MOD, D_MODEL), mod_map),
                  _vec_spec(D_MODEL), _resident((D_MODEL, D_MODEL))],
        out_specs=pl.BlockSpec((SSM_GROUPS, cpb * tok.batch, SSM_CW), u_map),
        scratch_shapes=[pltpu.VMEM((SSM_SLABS, tok.batch * _slab_pitch(blk), LANES), F32)],
        compiler_params=_cparams(2),
        name="s5_pre",
    )(xa, mod_l, pre_g, perm)


def _cmul(ar, ai, br, bi):
    return ar * br - ai * bi, ar * bi + ai * br


def _s5_factors(a_re, a_im, log_dt, b_re, b_im, c_re, c_im):
    t_n, gd, ns = SSM_CHUNK, SSM_GROUP_DIM, SSM_STATE
    dt = jnp.exp(log_dt.astype(F32))[..., None]
    l_re, l_im = a_re.astype(F32), a_im.astype(F32)
    z_re, z_im = l_re * dt, l_im * dt

    def apow(d, n):
        n = jnp.asarray(n, F32)[None, :, None]
        mag = jnp.exp(z_re[d][:, None, :] * n)
        ang = z_im[d][:, None, :] * n
        return mag * jnp.cos(ang), mag * jnp.sin(ang)

    t = np.arange(t_n)
    es, fs, qs, ps, decs = [], [], [], [], []
    for d in range(2):
        a1_re, a1_im = apow(d, [1.0])
        num_re, num_im = a1_re[:, 0] - 1.0, a1_im[:, 0]
        den = l_re[d] * l_re[d] + l_im[d] * l_im[d]
        r_re = (num_re * l_re[d] + num_im * l_im[d]) / den
        r_im = (num_im * l_re[d] - num_re * l_im[d]) / den
        bb_re, bb_im = _cmul(r_re[..., None], r_im[..., None], b_re[d], b_im[d])
        bb_re, bb_im = bb_re.transpose(0, 2, 1)[:, None], bb_im.transpose(0, 2, 1)[:, None]
        cc_re, cc_im = c_re[d].transpose(0, 2, 1)[:, :, None, :], c_im[d].transpose(0, 2, 1)[:, :, None, :]

        def rows_tc(power):
            pr, pi = apow(d, power)
            re, im = _cmul(pr[:, :, None, :], pi[:, :, None, :], bb_re, bb_im)
            return re.reshape(-1, t_n * gd, ns), im.reshape(-1, t_n * gd, ns)

        def cols_tc(power):
            pr, pi = apow(d, power)
            pr, pi = pr.transpose(0, 2, 1)[..., None], pi.transpose(0, 2, 1)[..., None]
            re, im = _cmul(cc_re, cc_im, pr, pi)
            return re.reshape(-1, ns, t_n * gd), im.reshape(-1, ns, t_n * gd)

        sign = 1.0 if d == 0 else -1.0
        e_re, e_im = rows_tc(-sign * t)
        f_re, f_im = cols_tc(sign * t)
        es.append(jnp.concatenate([e_re, e_im], axis=-1))
        fs.append(jnp.concatenate([f_re, -f_im], axis=1))
        q_re, q_im = rows_tc((t_n - 1 - t) if d == 0 else t)
        qs += [q_re, q_im, q_im, q_re]
        p_re, p_im = cols_tc((t + 1) if d == 0 else (t_n - t))
        ps += [p_re, -p_im]
        ar, ai = apow(d, [float(t_n)])
        decs += [ar[:, 0], ar[:, 0], -ai[:, 0], ai[:, 0]]
    e = jnp.concatenate(es, axis=-1)
    f = jnp.concatenate(fs, axis=1)
    q = jnp.concatenate(qs, axis=-1).astype(BF16)
    p = jnp.concatenate(ps, axis=1).astype(BF16)
    dec = jnp.concatenate(decs, axis=-1)[:, None, :]
    return e, f, q, p, dec


def _s5_kernel(u_ref, e_ref, f_ref, q_ref, p_ref, dec_ref, y_ref, v_ref, xin_ref, *, batch, n_chunks, n_ctx_chunks):
    gps = SSM_GROUPS_PER_STEP
    ns2 = 2 * SSM_STATE
    cw = SSM_CW
    for g in range(gps):
        v_ref[g] = jnp.dot(u_ref[g], q_ref[g], preferred_element_type=F32)
    a1f, a2f, a1b, a2b = (jnp.broadcast_to(dec_ref[:, :, n * ns2:(n + 1) * ns2], (gps, batch, ns2))
                          for n in range(4))

    def step(k, carry):
        xf, xfs, xb, xbs = carry
        kb = jnp.where(k < n_ctx_chunks, n_ctx_chunks - 1 - k, n_chunks + n_ctx_chunks - 1 - k)
        rf = pl.multiple_of(k * batch, batch)
        rb = pl.multiple_of(kb * batch, batch)
        xin_ref[:, pl.ds(rf, batch), 0:ns2] = xf
        xin_ref[:, pl.ds(rb, batch), ns2:2 * ns2] = xb
        vf = v_ref[:, pl.ds(rf, batch), 0:ns2]
        vfs = v_ref[:, pl.ds(rf, batch), ns2:2 * ns2]
        vb = v_ref[:, pl.ds(rb, batch), 2 * ns2:3 * ns2]
        vbs = v_ref[:, pl.ds(rb, batch), 3 * ns2:4 * ns2]
        return (a1f * xf + a2f * xfs + vf, a1f * xfs - a2f * xf + vfs,
                a1b * xb + a2b * xbs + vb, a1b * xbs - a2b * xb + vbs)

    zero = jnp.zeros((gps, batch, ns2), F32)
    lax.fori_loop(0, n_chunks, step, (zero, zero, zero, zero))
    t_in = lax.broadcasted_iota(jnp.int32, (cw, cw), 0) // SSM_GROUP_DIM
    t_out = lax.broadcasted_iota(jnp.int32, (cw, cw), 1) // SSM_GROUP_DIM
    hp = lax.Precision.HIGHEST
    for g in range(gps):
        m_f = jnp.dot(e_ref[g, :, 0:ns2], f_ref[g, 0:ns2, :], precision=hp, preferred_element_type=F32)
        m_b = jnp.dot(e_ref[g, :, ns2:2 * ns2], f_ref[g, ns2:2 * ns2, :], precision=hp, preferred_element_type=F32)
        m = (jnp.where(t_out >= t_in, m_f, 0.0) + jnp.where(t_in >= t_out, m_b, 0.0)).astype(BF16)
        y_ref[g] = (jnp.dot(u_ref[g], m, preferred_element_type=F32)
                    + jnp.dot(xin_ref[g].astype(BF16), p_ref[g], preferred_element_type=F32)).astype(BF16)


def _s5(u_t, factors, batch, n_chunks, n_ctx_chunks):
    e, f, q, p, dec = factors
    gps = SSM_GROUPS_PER_STEP
    rows = n_chunks * batch
    kern = functools.partial(_s5_kernel, batch=batch, n_chunks=n_chunks, n_ctx_chunks=n_ctx_chunks)
    gspec = lambda r, c: pl.BlockSpec((gps, r, c), lambda i: (i, 0, 0))
    return pl.pallas_call(
        kern,
        out_shape=jax.ShapeDtypeStruct((SSM_GROUPS, rows, SSM_CW), BF16),
        grid=(SSM_GROUPS // gps,),
        in_specs=[gspec(rows, SSM_CW), gspec(SSM_CW, SSM_CW), gspec(SSM_CW, SSM_CW), gspec(SSM_CW, 2 * SSM_CW),
                  gspec(SSM_CW, SSM_CW), gspec(1, 2 * SSM_CW)],
        out_specs=gspec(rows, SSM_CW),
        scratch_shapes=[pltpu.VMEM((gps, rows, 2 * SSM_CW), F32), pltpu.VMEM((gps, rows, SSM_CW), F32)],
        compiler_params=_cparams(1),
        name="s5_scan",
    )(u_t, e, f, q, p, dec)


def _s5_post_kernel(x_ref, y_ref, d_ref, mod_ref, pre_ref, post_ref, w_ref, perm_ref, o_ref, ys_ref, tmp_ref,
                    *, blk, batch):
    b = pl.program_id(1)
    pitch = _slab_pitch(blk)

    @pl.when(b == 0)
    def _():
        half = SSM_CHUNK // 2
        cpb = blk // SSM_CHUNK
        rows_h = cpb * batch

        def slab(s, carry):
            lhs = jnp.concatenate(
                [jnp.concatenate([y_ref[s * SSM_PIECES + g, :, hh * LANES:(hh + 1) * LANES]
                                  for g in range(SSM_PIECES)], axis=1) for hh in range(2)], axis=0)
            tmp_ref[...] = jnp.dot(lhs, perm_ref[...], preferred_element_type=F32)
            for hh in range(2):
                for i in range(cpb):
                    r0 = hh * rows_h + i * batch
                    t0 = i * SSM_CHUNK + hh * half
                    for j in range(half):
                        ys_ref[s, pl.ds(t0 + j, batch, stride=pitch), :] = (
                            tmp_ref[r0:r0 + batch, j * LANES:(j + 1) * LANES])
            return carry

        lax.fori_loop(0, SSM_SLABS, slab, 0)

    x = x_ref[...]
    h = _norm_mod(x, pre_ref[...], mod_ref[0, 1:2, :], mod_ref[0, 0:1, :])
    row0 = pl.multiple_of(b * pitch, SUBLANES)
    y_ssm = jnp.concatenate([ys_ref[s, pl.ds(row0, blk), :] for s in range(SSM_SLABS)], axis=1)
    y = y_ssm + d_ref[...] * h
    z = jnp.dot(jax.nn.gelu(y).astype(BF16), w_ref[...], preferred_element_type=F32)
    out = z[:, :D_MODEL] * jax.nn.sigmoid(z[:, D_MODEL:])
    o_ref[...] = x + mod_ref[0, 2:3, :] * _rms(out, post_ref[...])


def _s5_post(tok, xa, y_t, d_skip, mod_l, pre_g, post_g, glu_w, perm, blk, with_ctx):
    grid, row_map, mod_map = tok.pos_grid(blk, with_ctx)
    n_lat_blk = tok.seq // blk
    n_ctx_blk = tok.n_ctx // blk
    cpb = blk // SSM_CHUNK
    y_map = lambda p, b: (0, jnp.where(p < n_lat_blk, n_ctx_blk + p, p - n_lat_blk), 0)
    n_rows = tok.n_all if with_ctx else tok.n_lat
    kern = functools.partial(_s5_post_kernel, blk=blk, batch=tok.batch)
    return pl.pallas_call(
        kern,
        out_shape=jax.ShapeDtypeStruct((n_rows, D_MODEL), F32),
        grid=grid,
        in_specs=[pl.BlockSpec((blk, D_MODEL), row_map),
                  pl.BlockSpec((SSM_GROUPS, cpb * tok.batch, SSM_CW), y_map),
                  _vec_spec(D_MODEL), pl.BlockSpec((1, N_MOD, D_MODEL), mod_map), _vec_spec(D_MODEL),
                  _vec_spec(D_MODEL), _resident((D_MODEL, 2 * D_MODEL)), _resident((D_MODEL, D_MODEL))],
        out_specs=pl.BlockSpec((blk, D_MODEL), row_map),
        scratch_shapes=[pltpu.VMEM((SSM_SLABS, tok.batch * _slab_pitch(blk), LANES), F32),
                        pltpu.VMEM((2 * cpb * tok.batch, D_MODEL), F32)],
        compiler_params=_cparams(2),
        name="s5_post",
    )(xa, y_t, d_skip, mod_l, pre_g, post_g, glu_w, perm)


def _tile(limit, *sizes):
    tm = limit
    while any(s % tm for s in sizes):
        tm //= 2
    return tm


def kernel(x, c, ctx, c_ctx, mod_w, mod_b, mix_pre_g, mix_post_g, ffn_pre_g, ffn_post_g, ffn_w1, ffn_w2,
           even_w_in, even_w_out, even_sink, ssm_a_re, ssm_a_im, ssm_log_dt, ssm_b_re, ssm_b_im, ssm_c_re,
           ssm_c_im, ssm_d, ssm_glu_w):
    batch, seq, _ = x.shape
    n_ctx = ctx.shape[1]
    tok = _Tokens(batch, seq, n_ctx)
    assert seq % WINDOW == 0 and n_ctx % WINDOW == 0 and tok.n_lat % n_ctx == 0
    assert batch == SUBLANES
    tm = _tile(256, seq, n_ctx)
    tm_wide = _tile(512, seq, batch * n_ctx)

    xa = jnp.concatenate([x.reshape(tok.n_lat, D_MODEL), ctx.astype(x.dtype).reshape(-1, D_MODEL)], axis=0)

    n_cond = 2 * SUBLANES
    cond = jnp.zeros((n_cond, D_MODEL), F32).at[:batch].set(c).at[batch].set(c_ctx)
    mod = _modulation(cond, mod_w, mod_b).reshape(DEPTH, n_cond, N_MOD, D_MODEL)

    rope = _rope_tables(seq, tm)
    head_perm = _head_pair_perm()
    piece_perm = _piece_perm()
    dft_lat, dft_ctx = _dft_tables(seq), _dft_tables(n_ctx)
    vec = lambda g: g.reshape(1, D_MODEL)

    for layer in range(DEPTH):
        need_ctx = layer < DEPTH - 1
        n_rows = tok.n_all if need_ctx else tok.n_lat
        mod_l = mod[layer]
        i = layer // 2
        w1, w2 = ffn_w1[layer].astype(BF16), ffn_w2[layer].astype(BF16)
        ffn_pre, ffn_post = vec(ffn_pre_g[layer]), vec(ffn_post_g[layer])
        if layer % 2 == 0:
            w_in = even_w_in[i]
            q0 = FOURIER_WIDTH
            w_in = jnp.concatenate([w_in[:, :q0], w_in[:, q0:q0 + ATTN_WIDTH][:, head_perm],
                                    w_in[:, q0 + ATTN_WIDTH:]], axis=1).astype(BF16)
            f, q, k, v = _inproj(tok, xa, mod_l, vec(mix_pre_g[layer]), w_in, rope, tm)
            fm_lat, fm_ctx = _fourier(tok, f, dft_lat, dft_ctx)
            ao = _attention(tok, q, k, v, even_sink[i])
            w_out = even_w_out[i]
            w_f = w_out[:FOURIER_WIDTH].astype(BF16)
            w_a = w_out[FOURIER_WIDTH:][head_perm].astype(BF16)
            xa = _mix_ffn(tok, xa, fm_lat, fm_ctx, ao, mod_l, vec(mix_post_g[layer]), ffn_pre, ffn_post,
                          w_f, w_a, w1, w2, tm_wide)
        else:
            pre_g = vec(mix_pre_g[layer])
            u_t = _s5_pre(tok, xa, mod_l, pre_g, piece_perm, tm)
            factors = _s5_factors(ssm_a_re[i], ssm_a_im[i], ssm_log_dt[i], ssm_b_re[i], ssm_b_im[i],
                                  ssm_c_re[i], ssm_c_im[i])
            y_t = _s5(u_t, factors, batch, (seq + n_ctx) // SSM_CHUNK, n_ctx // SSM_CHUNK)
            xa = _s5_post(tok, xa, y_t, vec(ssm_d[i]), mod_l, pre_g, vec(mix_post_g[layer]),
                          ssm_glu_w[i].astype(BF16), piece_perm, tm, need_ctx)
            xa = _ffn(tok, xa, n_rows, mod_l, ffn_pre, ffn_post, w1, w2, tm_wide)
    return xa[:tok.n_lat].reshape(batch, seq, D_MODEL)
```

```python
import functools
import math

import numpy as np
import jax
import jax.numpy as jnp
from jax import lax
from jax.experimental import pallas as pl
from jax.experimental.pallas import tpu as pltpu

D_MODEL = 1024
DEPTH = 4
N_MOD = 6
EPS = 1e-6
NEG_INF = -1e30
GRID_W = 64

FOURIER_GROUPS = 4
FOURIER_GROUP_DIM = 128
FOURIER_WIDTH = FOURIER_GROUPS * FOURIER_GROUP_DIM

N_HEADS = 8
N_KV_HEADS = 2
HEAD_GROUP = N_HEADS // N_KV_HEADS
HEAD_DIM = 64
ATTN_WIDTH = N_HEADS * HEAD_DIM
KV_WIDTH = N_KV_HEADS * HEAD_DIM
WINDOW = 128
ROPE_AXIS_DIM = HEAD_DIM // 2
ROPE_BASE = 10000.0
IN_WIDTH = FOURIER_WIDTH + ATTN_WIDTH + 2 * KV_WIDTH

LANES = 128
SUBLANES = 8
VMEM_LIMIT = 56 * 1024 * 1024

SSM_GROUP_DIM = 16
SSM_GROUPS = D_MODEL // SSM_GROUP_DIM
SSM_STATE = 64
SSM_CHUNK = 16
SSM_CW = SSM_CHUNK * SSM_GROUP_DIM
SSM_GROUPS_PER_STEP = 4
SSM_SLABS = D_MODEL // LANES
SSM_PIECES = LANES // SSM_GROUP_DIM

D_FF = 4 * D_MODEL

F32 = jnp.float32
BF16 = jnp.bfloat16


def _cparams(n_axes):
    return pltpu.CompilerParams(dimension_semantics=("arbitrary",) * n_axes, vmem_limit_bytes=VMEM_LIMIT)


def _resident(shape):
    nd = len(shape)
    return pl.BlockSpec(shape, lambda *_: (0,) * nd, pipeline_mode=pl.Buffered(1))


def _rms(x, g):
    return x * lax.rsqrt(jnp.mean(x * x, axis=-1, keepdims=True) + EPS) * g


def _norm_mod(x, g, sc, sh):
    return _rms(x, g) * (1.0 + sc) + sh


def _mod_kernel(cond_ref, w_ref, b_ref, o_ref):
    cond = cond_ref[...]
    s = cond * jax.nn.sigmoid(cond)
    o_ref[0] = jnp.dot(s, w_ref[0], precision=lax.Precision.HIGHEST, preferred_element_type=F32) + b_ref[0]


def _modulation(cond, mod_w, mod_b):
    rows = cond.shape[0]
    tn = 1024
    n = N_MOD * D_MODEL
    return pl.pallas_call(
        _mod_kernel,
        out_shape=jax.ShapeDtypeStruct((DEPTH, rows, n), F32),
        grid=(DEPTH, n // tn),
        in_specs=[pl.BlockSpec((rows, D_MODEL), lambda l, j: (0, 0)),
                  pl.BlockSpec((1, D_MODEL, tn), lambda l, j: (l, 0, j)),
                  pl.BlockSpec((1, 1, tn), lambda l, j: (l, 0, j))],
        out_specs=pl.BlockSpec((1, rows, tn), lambda l, j: (l, 0, j)),
        compiler_params=_cparams(2),
        name="modulation",
    )(cond, mod_w, mod_b.reshape(DEPTH, 1, n))


class _Tokens:
    def __init__(self, batch, seq, n_ctx):
        self.batch, self.seq, self.n_ctx = batch, seq, n_ctx
        self.n_lat = batch * seq
        self.n_all = self.n_lat + batch * n_ctx

    def mod_spec(self, tm):
        per_batch = self.seq // tm
        return pl.BlockSpec((1, N_MOD, D_MODEL), lambda i: (jnp.minimum(i // per_batch, self.batch), 0, 0))

    def split_specs(self, tm, width, joined):
        nlt = self.n_lat // tm
        ctx_map = (lambda i: (jnp.maximum(i, nlt), 0)) if joined else (lambda i: (jnp.maximum(i - nlt, 0), 0))
        return pl.BlockSpec((tm, width), lambda i: (jnp.minimum(i, nlt - 1), 0)), pl.BlockSpec((tm, width), ctx_map)

    def pos_grid(self, blk, with_ctx):
        n_lat_blk = self.seq // blk
        n_ctx_blk = self.n_ctx // blk
        lat_blocks = self.n_lat // blk
        row_map = lambda p, b: (jnp.where(p < n_lat_blk, b * n_lat_blk + p, lat_blocks + b * n_ctx_blk + (p - n_lat_blk)), 0)
        mod_map = lambda p, b: (jnp.where(p < n_lat_blk, b, self.batch), 0, 0)
        grid = (n_lat_blk + (n_ctx_blk if with_ctx else 0), self.batch)
        return grid, row_map, mod_map


def _pick(n_lat_tiles, lat_ref, ctx_ref):
    return jnp.where(pl.program_id(0) < n_lat_tiles, lat_ref[...], ctx_ref[...])


def _layer_resident(layer, shape):
    nd = len(shape)
    return pl.BlockSpec((None,) + tuple(shape), lambda *_: (layer,) + (0,) * nd, pipeline_mode=pl.Buffered(1))


def _row_spec(tm, width):
    return pl.BlockSpec((tm, width), lambda i: (i, 0))


def _vec_spec(width):
    return pl.BlockSpec((1, width), lambda *_: (0, 0))


FFN_CHUNK = 512


def _ffn_body(x, mod_ref, pre_ref, post_ref, w1_ref, w2_ref, o_ref, acc_ref):
    h = _norm_mod(x, pre_ref[...], mod_ref[0, 4:5, :], mod_ref[0, 3:4, :]).astype(BF16)
    for c in range(D_FF // FFN_CHUNK):
        sl = slice(c * FFN_CHUNK, (c + 1) * FFN_CHUNK)
        a = jnp.maximum(jnp.dot(h, w1_ref[:, sl], preferred_element_type=F32), 0.0)
        part = jnp.dot((a * a).astype(BF16), w2_ref[sl, :], preferred_element_type=F32)
        if c == 0:
            acc_ref[...] = part
        else:
            acc_ref[...] += part
    o_ref[...] = x + mod_ref[0, 5:6, :] * _rms(acc_ref[...], post_ref[...])


def _ffn_kernel(x_ref, mod_ref, pre_ref, post_ref, w1_ref, w2_ref, o_ref, acc_ref):
    _ffn_body(x_ref[...], mod_ref, pre_ref, post_ref, w1_ref, w2_ref, o_ref, acc_ref)


def _ffn(tok, xa, n_rows, mod_l, pre_g, post_g, w1_all, w2_all, layer, tm):
    return pl.pallas_call(
        _ffn_kernel,
        out_shape=jax.ShapeDtypeStruct((n_rows, D_MODEL), F32),
        grid=(n_rows // tm,),
        in_specs=[_row_spec(tm, D_MODEL), tok.mod_spec(tm), _vec_spec(D_MODEL), _vec_spec(D_MODEL),
                  _layer_resident(layer, (D_MODEL, D_FF)), _layer_resident(layer, (D_FF, D_MODEL))],
        out_specs=_row_spec(tm, D_MODEL),
        scratch_shapes=[pltpu.VMEM((tm, D_MODEL), F32)],
        compiler_params=_cparams(1),
        name="ffn",
    )(xa, mod_l, pre_g, post_g, w1_all, w2_all)


def _mix_ffn_kernel(xl_ref, xc_ref, fl_ref, fc_ref, ao_ref, mod_ref, mpost_ref, pre_ref, post_ref, wf_ref, wa_ref,
                    w1_ref, w2_ref, o_ref, acc_ref, *, n_lat_tiles):
    y = (jnp.dot(_pick(n_lat_tiles, fl_ref, fc_ref), wf_ref[...], preferred_element_type=F32)
         + jnp.dot(ao_ref[...], wa_ref[...], preferred_element_type=F32))
    x = _pick(n_lat_tiles, xl_ref, xc_ref) + mod_ref[0, 2:3, :] * _rms(y, mpost_ref[...])
    _ffn_body(x, mod_ref, pre_ref, post_ref, w1_ref, w2_ref, o_ref, acc_ref)


def _mix_ffn(tok, x_lat, x_ctx, fm_lat, fm_ctx, ao, mod_l, mix_post_g, pre_g, post_g, w_f, w_a, w1_all, w2_all,
             layer, tm):
    n = tok.n_all
    kern = functools.partial(_mix_ffn_kernel, n_lat_tiles=tok.n_lat // tm)
    return pl.pallas_call(
        kern,
        out_shape=jax.ShapeDtypeStruct((n, D_MODEL), F32),
        grid=(n // tm,),
        in_specs=[*tok.split_specs(tm, D_MODEL, x_lat is x_ctx), *tok.split_specs(tm, FOURIER_WIDTH, False),
                  _row_spec(tm, ATTN_WIDTH), tok.mod_spec(tm),
                  _vec_spec(D_MODEL), _vec_spec(D_MODEL), _vec_spec(D_MODEL),
                  _resident((FOURIER_WIDTH, D_MODEL)), _resident((ATTN_WIDTH, D_MODEL)),
                  _layer_resident(layer, (D_MODEL, D_FF)), _layer_resident(layer, (D_FF, D_MODEL))],
        out_specs=_row_spec(tm, D_MODEL),
        scratch_shapes=[pltpu.VMEM((tm, D_MODEL), F32)],
        compiler_params=_cparams(1),
        name="mix_ffn",
    )(x_lat, x_ctx, fm_lat, fm_ctx, ao, mod_l, mix_post_g, pre_g, post_g, w_f, w_a, w1_all, w2_all)


def _rope_block(x, cos, sin_hi, sin_lo):
    half = ROPE_AXIS_DIM // 2
    return (x * cos + pltpu.roll(x, half, axis=1) * sin_hi
            + pltpu.roll(x, LANES - half, axis=1) * sin_lo)


def _inproj_kernel(xl_ref, xc_ref, mod_ref, pre_ref, w_ref, cos_ref, shi_ref, slo_ref, f_ref, q_ref, k_ref, v_ref,
                   *, n_lat_tiles):
    x = _pick(n_lat_tiles, xl_ref, xc_ref)
    h = _norm_mod(x, pre_ref[...], mod_ref[0, 1:2, :], mod_ref[0, 0:1, :]).astype(BF16)
    p = jnp.dot(h, w_ref[...], preferred_element_type=F32)
    cos, shi, slo = cos_ref[...], shi_ref[...], slo_ref[...]
    f_ref[...] = p[:, :FOURIER_WIDTH].astype(BF16)
    scale = HEAD_DIM ** -0.5
    for j in range(ATTN_WIDTH // LANES):
        lo = FOURIER_WIDTH + j * LANES
        q_ref[:, j * LANES:(j + 1) * LANES] = (_rope_block(p[:, lo:lo + LANES], cos, shi, slo) * scale).astype(BF16)
    k0 = FOURIER_WIDTH + ATTN_WIDTH
    k_ref[...] = _rope_block(p[:, k0:k0 + KV_WIDTH], cos, shi, slo).astype(BF16)
    v_ref[...] = p[:, k0 + KV_WIDTH:].astype(BF16)


def _rope_tables(seq, n_pad):
    pos = np.arange(seq)
    row = (pos // GRID_W).astype(np.float64)
    col = (pos % GRID_W).astype(np.float64)
    lane = np.arange(LANES)
    d = lane % HEAD_DIM
    j = d % (ROPE_AXIS_DIM // 2)
    inv = jnp.asarray(ROPE_BASE, F32) ** (-jnp.asarray(2 * j, F32) / ROPE_AXIS_DIM)
    use_col = jnp.asarray(d >= ROPE_AXIS_DIM)
    posv = jnp.where(use_col[None, :], jnp.asarray(col, F32)[:, None], jnp.asarray(row, F32)[:, None])
    ang = posv * inv[None, :]
    upper = jnp.asarray((d % ROPE_AXIS_DIM) >= ROPE_AXIS_DIM // 2)[None, :]
    cos, sin = jnp.cos(ang), jnp.sin(ang)
    sin_hi = jnp.where(upper, sin, 0.0)
    sin_lo = jnp.where(upper, 0.0, -sin)
    pad = lambda t, v: jnp.concatenate([t, jnp.full((n_pad, LANES), v, F32)], axis=0)
    return pad(cos, 1.0), pad(sin_hi, 0.0), pad(sin_lo, 0.0)


def _inproj(tok, x_lat, x_ctx, mod_l, pre_g, w_in, tables, tm):
    per_batch = tok.seq // tm
    n_lat_tiles = tok.n_lat // tm
    tab_map = lambda i: (jnp.where(i < n_lat_tiles, i % per_batch, per_batch), 0)
    tab_spec = pl.BlockSpec((tm, LANES), tab_map)
    n = tok.n_all
    kern = functools.partial(_inproj_kernel, n_lat_tiles=n_lat_tiles)
    return pl.pallas_call(
        kern,
        out_shape=(jax.ShapeDtypeStruct((n, FOURIER_WIDTH), BF16), jax.ShapeDtypeStruct((n, ATTN_WIDTH), BF16),
                   jax.ShapeDtypeStruct((n, KV_WIDTH), BF16), jax.ShapeDtypeStruct((n, KV_WIDTH), BF16)),
        grid=(n // tm,),
        in_specs=[*tok.split_specs(tm, D_MODEL, x_lat is x_ctx), tok.mod_spec(tm), _vec_spec(D_MODEL),
                  _resident((D_MODEL, IN_WIDTH)), tab_spec, tab_spec, tab_spec],
        out_specs=(_row_spec(tm, FOURIER_WIDTH), _row_spec(tm, ATTN_WIDTH), _row_spec(tm, KV_WIDTH),
                   _row_spec(tm, KV_WIDTH)),
        compiler_params=_cparams(1),
        name="inproj",
    )(x_lat, x_ctx, mod_l, pre_g, w_in, *tables)


def _head_pair_perm():
    cols = []
    for i in range(HEAD_GROUP):
        for h in (i, i + HEAD_GROUP):
            cols.extend(range(h * HEAD_DIM, (h + 1) * HEAD_DIM))
    return np.asarray(cols)


def _dft_tables(length):
    def cs(n):
        k = np.arange(n)
        ang = 2.0 * np.pi * ((k[:, None] * k[None, :]) % n) / n
        return np.cos(ang) / math.sqrt(n), np.sin(ang) / math.sqrt(n)
    cl, sl = cs(length)
    cc, sc = cs(FOURIER_GROUP_DIM)
    pos = np.concatenate([cl, -sl], axis=1).astype(np.float32)
    chan = np.concatenate([cc, sc], axis=1).astype(np.float32)
    return jnp.asarray(pos).astype(BF16), jnp.asarray(chan).astype(BF16)


def _fourier_one(f_ref, chan_ref, pos_ref, o_ref, stk_ref):
    gd = FOURIER_GROUP_DIM
    length = f_ref.shape[0]
    row_chunk = min(length, 512)
    for g in range(FOURIER_GROUPS):
        z = jnp.dot(f_ref[:, g * gd:(g + 1) * gd], chan_ref[...], preferred_element_type=F32)
        stk_ref[0:length, g * gd:(g + 1) * gd] = z[:, :gd].astype(BF16)
        stk_ref[length:2 * length, g * gd:(g + 1) * gd] = z[:, gd:].astype(BF16)
    for r in range(length // row_chunk):
        rows = slice(r * row_chunk, (r + 1) * row_chunk)
        o_ref[rows, :] = jnp.dot(pos_ref[rows, :], stk_ref[...], preferred_element_type=F32).astype(BF16)


def _fourier_kernel(fl_ref, fc_ref, chan_ref, posl_ref, posc_ref, ol_ref, oc_ref, stkl_ref, stkc_ref):
    _fourier_one(fl_ref, chan_ref, posl_ref, ol_ref, stkl_ref)
    _fourier_one(fc_ref, chan_ref, posc_ref, oc_ref, stkc_ref)


def _fourier(tok, f_all, tabs_lat, tabs_ctx):
    pos_l, chan = tabs_lat
    pos_c, _ = tabs_ctx
    ctx0 = tok.n_lat // tok.n_ctx
    lat_blk = lambda m: pl.BlockSpec((tok.seq, FOURIER_WIDTH), m)
    ctx_blk = lambda m: pl.BlockSpec((tok.n_ctx, FOURIER_WIDTH), m)
    return pl.pallas_call(
        _fourier_kernel,
        out_shape=(jax.ShapeDtypeStruct((tok.n_lat, FOURIER_WIDTH), BF16),
                   jax.ShapeDtypeStruct((tok.batch * tok.n_ctx, FOURIER_WIDTH), BF16)),
        grid=(tok.batch,),
        in_specs=[lat_blk(lambda b: (b, 0)), ctx_blk(lambda b: (ctx0 + b, 0)), _resident(chan.shape),
                  _resident(pos_l.shape), _resident(pos_c.shape)],
        out_specs=(lat_blk(lambda b: (b, 0)), ctx_blk(lambda b: (b, 0))),
        scratch_shapes=[pltpu.VMEM((2 * tok.seq, FOURIER_WIDTH), BF16),
                        pltpu.VMEM((2 * tok.n_ctx, FOURIER_WIDTH), BF16)],
        compiler_params=_cparams(1),
        name="fourier",
    )(f_all, f_all, chan, pos_l, pos_c)


def _attn_kernel(sink_ref, q_ref, kp_ref, kc_ref, kn_ref, kx_ref, vp_ref, vc_ref, vn_ref, vx_ref, o_ref,
                 *, n_qblk):
    j = pl.program_id(1)
    blk = WINDOW
    n_loc = 3 * blk
    rows = HEAD_GROUP * blk
    q = jnp.concatenate([q_ref[:, i * LANES:(i + 1) * LANES] for i in range(HEAD_GROUP)], axis=0)
    k_loc = jnp.concatenate([kp_ref[...], kc_ref[...], kn_ref[...]], axis=0)
    v_loc = jnp.concatenate([vp_ref[...], vc_ref[...], vn_ref[...]], axis=0)
    k_ctx, v_ctx = kx_ref[...], vx_ref[...]
    is_lat = j < n_qblk
    col_lo = jnp.where(is_lat, jnp.where(j >= 1, 0, blk), n_loc)
    col_hi = jnp.where(is_lat, jnp.where(j + 1 < n_qblk, n_loc, 2 * blk), 0)
    qi = lax.broadcasted_iota(jnp.int32, (rows, 1), 0) % blk
    kj = lax.broadcasted_iota(jnp.int32, (rows, n_loc), 1)
    mask = (kj >= jnp.maximum(qi, col_lo)) & (kj <= jnp.minimum(qi + 2 * WINDOW, col_hi - 1))
    row_blk = lax.broadcasted_iota(jnp.int32, (rows, 1), 0) // blk
    lane_o = lax.broadcasted_iota(jnp.int32, (rows, LANES), 1)
    nt = (((1,), (1,)), ((), ()))
    outs = []
    for kvh in range(N_KV_HEADS):
        def own_lanes(t):
            lane = lax.broadcasted_iota(jnp.int32, t.shape, 1)
            return jnp.where((lane >= kvh * HEAD_DIM) & (lane < (kvh + 1) * HEAD_DIM), t, jnp.zeros_like(t))
        s_loc = lax.dot_general(q, own_lanes(k_loc), nt, preferred_element_type=F32)
        s_loc = jnp.where(mask, s_loc, NEG_INF)
        s_ctx = lax.dot_general(q, own_lanes(k_ctx), nt, preferred_element_type=F32)
        sink = jnp.zeros((rows, 1), F32)
        for i in range(HEAD_GROUP):
            sink = jnp.where(row_blk == i, sink_ref[kvh * HEAD_GROUP + i], sink)
        m = jnp.maximum(jnp.maximum(jnp.max(s_loc, axis=-1, keepdims=True), jnp.max(s_ctx, axis=-1, keepdims=True)),
                        sink)
        p_loc = jnp.exp(s_loc - m)
        p_ctx = jnp.exp(s_ctx - m)
        denom = (jnp.sum(p_loc, axis=-1, keepdims=True) + jnp.sum(p_ctx, axis=-1, keepdims=True)
                 + jnp.exp(sink - m))
        o = (jnp.dot(p_loc.astype(BF16), v_loc, preferred_element_type=F32)
             + jnp.dot(p_ctx.astype(BF16), v_ctx, preferred_element_type=F32))
        outs.append(o / denom)
    merged = jnp.where(lane_o < HEAD_DIM, outs[0], outs[1])
    for i in range(HEAD_GROUP):
        o_ref[:, i * LANES:(i + 1) * LANES] = merged[i * blk:(i + 1) * blk, :].astype(BF16)


def _attention(tok, q, k, v, sink):
    blk = WINDOW
    n_qblk = tok.seq // blk
    n_cblk = tok.n_ctx // blk
    lat_blocks = tok.n_lat // blk
    ctx0 = tok.n_lat // tok.n_ctx

    def q_map(b, j):
        return (jnp.where(j < n_qblk, b * n_qblk + j, lat_blocks + b * n_cblk + (j - n_qblk)), 0)

    def k_map(off):
        return lambda b, j: (b * n_qblk + jnp.clip(j + off, 0, n_qblk - 1), 0)

    ctx_map = lambda b, j: (ctx0 + b, 0)
    kv_blk = lambda m: pl.BlockSpec((blk, KV_WIDTH), m)
    ctx_blk = pl.BlockSpec((tok.n_ctx, KV_WIDTH), ctx_map)
    kern = functools.partial(_attn_kernel, n_qblk=n_qblk)
    return pl.pallas_call(
        kern,
        out_shape=jax.ShapeDtypeStruct((tok.n_all, ATTN_WIDTH), BF16),
        grid=(tok.batch, n_qblk + n_cblk),
        in_specs=[pl.BlockSpec(memory_space=pltpu.SMEM),
                  pl.BlockSpec((blk, ATTN_WIDTH), q_map),
                  kv_blk(k_map(-1)), kv_blk(k_map(0)), kv_blk(k_map(1)), ctx_blk,
                  kv_blk(k_map(-1)), kv_blk(k_map(0)), kv_blk(k_map(1)), ctx_blk],
        out_specs=pl.BlockSpec((blk, ATTN_WIDTH), q_map),
        compiler_params=_cparams(2),
        name="window_attention",
    )(sink, q, k, k, k, k, v, v, v, v)


def _piece_perm():
    idx = np.arange(D_MODEL)
    a, b, c = idx // LANES, (idx // SSM_GROUP_DIM) % SSM_PIECES, idx % SSM_GROUP_DIM
    perm = np.zeros((D_MODEL, D_MODEL), np.float32)
    perm[idx, b * LANES + a * SSM_GROUP_DIM + c] = 1.0
    return jnp.asarray(perm).astype(BF16)


def _slab_pitch(blk):
    return blk + SUBLANES


def _s5_pre_kernel(x_ref, mod_ref, pre_ref, perm_ref, u_ref, hs_ref, *, blk, batch):
    b = pl.program_id(1)
    pitch = _slab_pitch(blk)
    h = _norm_mod(x_ref[...], pre_ref[...], mod_ref[0, 1:2, :], mod_ref[0, 0:1, :])
    row0 = pl.multiple_of(b * pitch, SUBLANES)
    for s in range(SSM_SLABS):
        hs_ref[s, pl.ds(row0, blk), :] = h[:, s * LANES:(s + 1) * LANES]

    @pl.when(b == batch - 1)
    def _():
        half = SSM_CHUNK // 2
        cpb = blk // SSM_CHUNK
        rows_h = cpb * batch

        def slab(s, carry):
            lhs = []
            for hh in range(2):
                for i in range(cpb):
                    t0 = i * SSM_CHUNK + hh * half
                    lhs.append(jnp.concatenate(
                        [hs_ref[s, pl.ds(t0 + j, batch, stride=pitch), :] for j in range(half)], axis=1))
            lhs = jnp.concatenate(lhs, axis=0).astype(BF16)
            out = jnp.dot(lhs, perm_ref[...], preferred_element_type=F32).astype(BF16)
            for hh in range(2):
                for g in range(SSM_PIECES):
                    u_ref[s * SSM_PIECES + g, :, hh * LANES:(hh + 1) * LANES] = (
                        out[hh * rows_h:(hh + 1) * rows_h, g * LANES:(g + 1) * LANES])
            return carry

        lax.fori_loop(0, SSM_SLABS, slab, 0)


def _s5_pre(tok, xa, mod_l, pre_g, perm, blk):
    grid, row_map, mod_map = tok.pos_grid(blk, True)
    n_lat_blk = tok.seq // blk
    n_ctx_blk = tok.n_ctx // blk
    cpb = blk // SSM_CHUNK
    n_chunks = (tok.seq + tok.n_ctx) // SSM_CHUNK
    u_map = lambda p, b: (0, jnp.where(p < n_lat_blk, n_ctx_blk + p, p - n_lat_blk), 0)
    kern = functools.partial(_s5_pre_kernel, blk=blk, batch=tok.batch)
    return pl.pallas_call(
        kern,
        out_shape=jax.ShapeDtypeStruct((SSM_GROUPS, n_chunks * tok.batch, SSM_CW), BF16),
        grid=grid,
        in_specs=[pl.BlockSpec((blk, D_MODEL), row_map), pl.BlockSpec((1, N_MOD, D_MODEL), mod_map),
                  _vec_spec(D_MODEL), _resident((D_MODEL, D_MODEL))],
        out_specs=pl.BlockSpec((SSM_GROUPS, cpb * tok.batch, SSM_CW), u_map),
        scratch_shapes=[pltpu.VMEM((SSM_SLABS, tok.batch * _slab_pitch(blk), LANES), F32)],
        compiler_params=_cparams(2),
        name="s5_pre",
    )(xa, mod_l, pre_g, perm)


def _cmul(ar, ai, br, bi):
    return ar * br - ai * bi, ar * bi + ai * br


def _s5_factors(a_re, a_im, log_dt, b_re, b_im, c_re, c_im):
    t_n, gd = SSM_CHUNK, SSM_GROUP_DIM
    dt = jnp.exp(log_dt.astype(F32))[..., None]
    l_re, l_im = a_re.astype(F32), a_im.astype(F32)
    z_re, z_im = l_re * dt, l_im * dt

    def apow(d, n):
        n = jnp.asarray(n, F32)[None, :, None]
        mag = jnp.exp(z_re[d][:, None, :] * n)
        ang = z_im[d][:, None, :] * n
        return mag * jnp.cos(ang), mag * jnp.sin(ang)

    bb, cc = [], []
    for d in range(2):
        a1_re, a1_im = apow(d, [1.0])
        num_re, num_im = a1_re[:, 0] - 1.0, a1_im[:, 0]
        den = l_re[d] * l_re[d] + l_im[d] * l_im[d]
        r_re = (num_re * l_re[d] + num_im * l_im[d]) / den
        r_im = (num_im * l_re[d] - num_re * l_im[d]) / den
        bb_re, bb_im = _cmul(r_re[..., None], r_im[..., None], b_re[d], b_im[d])
        bb.append((bb_re.transpose(0, 2, 1), bb_im.transpose(0, 2, 1)))
        cc.append((c_re[d].astype(F32), c_im[d].astype(F32)))

    t = np.arange(t_n, dtype=np.float64)

    def table(blocks):
        pr, pi, wa, wb = [], [], [], []
        for d, power, (w_re, w_im), part in blocks:
            p_re, p_im = apow(d, power)
            pr.append(p_re)
            pi.append(p_im)
            wa.append({'re': w_re, 'im': w_im, '-im': -w_im}[part])
            wb.append({'re': -w_im, 'im': w_re, '-im': -w_re}[part])
        pr, pi, wa, wb = (jnp.concatenate(x, axis=-1) for x in (pr, pi, wa, wb))
        out = pr[:, :, None, :] * wa[:, None, :, :] + pi[:, :, None, :] * wb[:, None, :, :]
        return out.reshape(out.shape[0], t_n * gd, out.shape[-1])

    fwd_inc, bwd_inc = t_n - 1 - t, t
    q = table([(0, fwd_inc, bb[0], 're'), (1, bwd_inc, bb[1], 're'),
               (0, fwd_inc, bb[0], 'im'), (1, bwd_inc, bb[1], 'im')]).astype(BF16)
    e = table([(0, -t, bb[0], 're'), (0, -t, bb[0], 'im'), (1, t, bb[1], 're'), (1, t, bb[1], 'im')])
    ft = table([(0, t, cc[0], 're'), (0, t, cc[0], '-im'), (1, -t, cc[1], 're'), (1, -t, cc[1], '-im')])
    fwd_out, bwd_out = t + 1, t_n - t
    pt = table([(0, fwd_out, cc[0], 're'), (1, bwd_out, cc[1], 're'),
                (0, fwd_out, cc[0], '-im'), (1, bwd_out, cc[1], '-im')]).astype(BF16)

    def split(x):
        hi = x.astype(BF16)
        return hi, (x - hi.astype(F32)).astype(BF16)

    (ar_f, ai_f), (ar_b, ai_b) = apow(0, [float(t_n)]), apow(1, [float(t_n)])
    dec = jnp.concatenate([ar_f, ar_b, ai_f, ai_b], axis=-1)
    return (*split(e), *split(ft), q, pt, dec)


def _s5_kernel(u_ref, eh_ref, el_ref, fh_ref, fl_ref, q_ref, pt_ref, dec_ref, y_ref, v_ref, xin_ref,
               *, batch, n_chunks, n_ctx_chunks):
    gps = SSM_GROUPS_PER_STEP
    ns = SSM_STATE
    ns2 = 2 * ns
    cw = SSM_CW
    nt = (((1,), (1,)), ((), ()))
    for g in range(gps):
        v_ref[g] = jnp.dot(u_ref[g], q_ref[g], preferred_element_type=F32)
    a_re = jnp.broadcast_to(dec_ref[:, :, 0:ns2], (gps, batch, ns2))
    a_im = jnp.broadcast_to(dec_ref[:, :, ns2:2 * ns2], (gps, batch, ns2))
    is_fwd = lax.broadcasted_iota(jnp.int32, (gps, batch, ns2), 2) < ns

    def step(k, carry):
        x_re, x_im = carry
        kb = jnp.where(k < n_ctx_chunks, n_ctx_chunks - 1 - k, n_chunks + n_ctx_chunks - 1 - k)
        rf = pl.multiple_of(k * batch, batch)
        rb = pl.multiple_of(kb * batch, batch)
        xin_ref[:, pl.ds(rf, batch), 0:ns] = x_re[:, :, 0:ns]
        xin_ref[:, pl.ds(rb, batch), ns:ns2] = x_re[:, :, ns:ns2]
        xin_ref[:, pl.ds(rf, batch), ns2:ns2 + ns] = x_im[:, :, 0:ns]
        xin_ref[:, pl.ds(rb, batch), ns2 + ns:2 * ns2] = x_im[:, :, ns:ns2]
        v_re = jnp.where(is_fwd, v_ref[:, pl.ds(rf, batch), 0:ns2], v_ref[:, pl.ds(rb, batch), 0:ns2])
        v_im = jnp.where(is_fwd, v_ref[:, pl.ds(rf, batch), ns2:2 * ns2], v_ref[:, pl.ds(rb, batch), ns2:2 * ns2])
        return a_re * x_re - a_im * x_im + v_re, a_re * x_im + a_im * x_re + v_im

    zero = jnp.zeros((gps, batch, ns2), F32)
    lax.fori_loop(0, n_chunks, step, (zero, zero))
    t_in = lax.broadcasted_iota(jnp.int32, (cw, cw), 0) // SSM_GROUP_DIM
    t_out = lax.broadcasted_iota(jnp.int32, (cw, cw), 1) // SSM_GROUP_DIM

    def lag_kernel(g, lanes):
        eh, el, fh, fl = eh_ref[g, :, lanes], el_ref[g, :, lanes], fh_ref[g, :, lanes], fl_ref[g, :, lanes]
        return (lax.dot_general(eh, fh, nt, preferred_element_type=F32)
                + lax.dot_general(eh, fl, nt, preferred_element_type=F32)
                + lax.dot_general(el, fh, nt, preferred_element_type=F32))

    for g in range(gps):
        m = (jnp.where(t_out >= t_in, lag_kernel(g, slice(0, ns2)), 0.0)
             + jnp.where(t_in >= t_out, lag_kernel(g, slice(ns2, 2 * ns2)), 0.0)).astype(BF16)
        y_ref[g] = (jnp.dot(u_ref[g], m, preferred_element_type=F32)
                    + lax.dot_general(xin_ref[g].astype(BF16), pt_ref[g], nt, preferred_element_type=F32)
                    ).astype(BF16)


def _s5(u_t, factors, batch, n_chunks, n_ctx_chunks):
    gps = SSM_GROUPS_PER_STEP
    rows = n_chunks * batch
    kern = functools.partial(_s5_kernel, batch=batch, n_chunks=n_chunks, n_ctx_chunks=n_ctx_chunks)
    gspec = lambda r, c: pl.BlockSpec((gps, r, c), lambda i: (i, 0, 0))
    sq = gspec(SSM_CW, SSM_CW)
    return pl.pallas_call(
        kern,
        out_shape=jax.ShapeDtypeStruct((SSM_GROUPS, rows, SSM_CW), BF16),
        grid=(SSM_GROUPS // gps,),
        in_specs=[gspec(rows, SSM_CW), sq, sq, sq, sq, sq, sq, gspec(1, SSM_CW)],
        out_specs=gspec(rows, SSM_CW),
        scratch_shapes=[pltpu.VMEM((gps, rows, SSM_CW), F32), pltpu.VMEM((gps, rows, SSM_CW), F32)],
        compiler_params=_cparams(1),
        name="s5_scan",
    )(u_t, *factors)


def _s5_post_kernel(x_ref, y_ref, d_ref, mod_ref, pre_ref, post_ref, w_ref, perm_ref, o_ref, ys_ref, tmp_ref,
                    *, blk, batch):
    b = pl.program_id(1)
    pitch = _slab_pitch(blk)

    @pl.when(b == 0)
    def _():
        half = SSM_CHUNK // 2
        cpb = blk // SSM_CHUNK
        rows_h = cpb * batch

        def slab(s, carry):
            lhs = jnp.concatenate(
                [jnp.concatenate([y_ref[s * SSM_PIECES + g, :, hh * LANES:(hh + 1) * LANES]
                                  for g in range(SSM_PIECES)], axis=1) for hh in range(2)], axis=0)
            tmp_ref[...] = jnp.dot(lhs, perm_ref[...], preferred_element_type=F32)
            for hh in range(2):
                for i in range(cpb):
                    r0 = hh * rows_h + i * batch
                    t0 = i * SSM_CHUNK + hh * half
                    for j in range(half):
                        ys_ref[s, pl.ds(t0 + j, batch, stride=pitch), :] = (
                            tmp_ref[r0:r0 + batch, j * LANES:(j + 1) * LANES])
            return carry

        lax.fori_loop(0, SSM_SLABS, slab, 0)

    x = x_ref[...]
    h = _norm_mod(x, pre_ref[...], mod_ref[0, 1:2, :], mod_ref[0, 0:1, :])
    row0 = pl.multiple_of(b * pitch, SUBLANES)
    y_ssm = jnp.concatenate([ys_ref[s, pl.ds(row0, blk), :] for s in range(SSM_SLABS)], axis=1)
    y = y_ssm + d_ref[...] * h
    z = jnp.dot(jax.nn.gelu(y).astype(BF16), w_ref[...], preferred_element_type=F32)
    out = z[:, :D_MODEL] * jax.nn.sigmoid(z[:, D_MODEL:])
    o_ref[...] = x + mod_ref[0, 2:3, :] * _rms(out, post_ref[...])


def _s5_post(tok, xa, y_t, d_skip, mod_l, pre_g, post_g, glu_w, perm, blk, with_ctx):
    grid, row_map, mod_map = tok.pos_grid(blk, with_ctx)
    n_lat_blk = tok.seq // blk
    n_ctx_blk = tok.n_ctx // blk
    cpb = blk // SSM_CHUNK
    y_map = lambda p, b: (0, jnp.where(p < n_lat_blk, n_ctx_blk + p, p - n_lat_blk), 0)
    n_rows = tok.n_all if with_ctx else tok.n_lat
    kern = functools.partial(_s5_post_kernel, blk=blk, batch=tok.batch)
    return pl.pallas_call(
        kern,
        out_shape=jax.ShapeDtypeStruct((n_rows, D_MODEL), F32),
        grid=grid,
        in_specs=[pl.BlockSpec((blk, D_MODEL), row_map),
                  pl.BlockSpec((SSM_GROUPS, cpb * tok.batch, SSM_CW), y_map),
                  _vec_spec(D_MODEL), pl.BlockSpec((1, N_MOD, D_MODEL), mod_map), _vec_spec(D_MODEL),
                  _vec_spec(D_MODEL), _resident((D_MODEL, 2 * D_MODEL)), _resident((D_MODEL, D_MODEL))],
        out_specs=pl.BlockSpec((blk, D_MODEL), row_map),
        scratch_shapes=[pltpu.VMEM((SSM_SLABS, tok.batch * _slab_pitch(blk), LANES), F32),
                        pltpu.VMEM((2 * cpb * tok.batch, D_MODEL), F32)],
        compiler_params=_cparams(2),
        name="s5_post",
    )(xa, y_t, d_skip, mod_l, pre_g, post_g, glu_w, perm)


def _tile(limit, *sizes):
    tm = limit
    while any(s % tm for s in sizes):
        tm //= 2
    return tm


def kernel(x, c, ctx, c_ctx, mod_w, mod_b, mix_pre_g, mix_post_g, ffn_pre_g, ffn_post_g, ffn_w1, ffn_w2,
           even_w_in, even_w_out, even_sink, ssm_a_re, ssm_a_im, ssm_log_dt, ssm_b_re, ssm_b_im, ssm_c_re,
           ssm_c_im, ssm_d, ssm_glu_w):
    batch, seq, _ = x.shape
    n_ctx = ctx.shape[1]
    tok = _Tokens(batch, seq, n_ctx)
    assert seq % WINDOW == 0 and n_ctx % WINDOW == 0 and tok.n_lat % n_ctx == 0
    assert batch == SUBLANES
    tm = _tile(256, seq, n_ctx)
    tm_wide = _tile(512, seq, batch * n_ctx)

    x_lat, x_ctx = x.reshape(tok.n_lat, D_MODEL), ctx.astype(x.dtype).reshape(-1, D_MODEL)

    n_cond = 2 * SUBLANES
    cond = jnp.zeros((n_cond, D_MODEL), F32).at[:batch].set(c).at[batch].set(c_ctx)
    mod = _modulation(cond, mod_w, mod_b).reshape(DEPTH, n_cond, N_MOD, D_MODEL)

    rope = _rope_tables(seq, tm)
    head_perm = _head_pair_perm()
    piece_perm = _piece_perm()
    dft_lat, dft_ctx = _dft_tables(seq), _dft_tables(n_ctx)
    vec = lambda g: g.reshape(1, D_MODEL)
    w1_all, w2_all = ffn_w1.astype(BF16), ffn_w2.astype(BF16)

    for layer in range(DEPTH):
        need_ctx = layer < DEPTH - 1
        n_rows = tok.n_all if need_ctx else tok.n_lat
        mod_l = mod[layer]
        i = layer // 2
        ffn_pre, ffn_post = vec(ffn_pre_g[layer]), vec(ffn_post_g[layer])
        if layer % 2 == 0:
            w_in = even_w_in[i]
            q0 = FOURIER_WIDTH
            w_in = jnp.concatenate([w_in[:, :q0], w_in[:, q0:q0 + ATTN_WIDTH][:, head_perm],
                                    w_in[:, q0 + ATTN_WIDTH:]], axis=1).astype(BF16)
            f, q, k, v = _inproj(tok, x_lat, x_ctx, mod_l, vec(mix_pre_g[layer]), w_in, rope, tm)
            fm_lat, fm_ctx = _fourier(tok, f, dft_lat, dft_ctx)
            ao = _attention(tok, q, k, v, even_sink[i])
            w_out = even_w_out[i]
            w_f = w_out[:FOURIER_WIDTH].astype(BF16)
            w_a = w_out[FOURIER_WIDTH:][head_perm].astype(BF16)
            xa = _mix_ffn(tok, x_lat, x_ctx, fm_lat, fm_ctx, ao, mod_l, vec(mix_post_g[layer]), ffn_pre, ffn_post,
                          w_f, w_a, w1_all, w2_all, layer, tm_wide)
        else:
            pre_g = vec(mix_pre_g[layer])
            u_t = _s5_pre(tok, xa, mod_l, pre_g, piece_perm, tm)
            factors = _s5_factors(ssm_a_re[i], ssm_a_im[i], ssm_log_dt[i], ssm_b_re[i], ssm_b_im[i],
                                  ssm_c_re[i], ssm_c_im[i])
            y_t = _s5(u_t, factors, batch, (seq + n_ctx) // SSM_CHUNK, n_ctx // SSM_CHUNK)
            xa = _s5_post(tok, xa, y_t, vec(ssm_d[i]), mod_l, pre_g, vec(mix_post_g[layer]),
                          ssm_glu_w[i].astype(BF16), piece_perm, tm, need_ctx)
            xa = _ffn(tok, xa, n_rows, mod_l, ffn_pre, ffn_post, w1_all, w2_all, layer, tm_wide)
        x_lat = x_ctx = xa
    return xa[:tok.n_lat].reshape(batch, seq, D_MODEL)
```

```python
import functools
import math

import numpy as np
import jax
import jax.numpy as jnp
from jax import lax
from jax.experimental import pallas as pl
from jax.experimental.pallas import tpu as pltpu

D_MODEL = 1024
DEPTH = 4
N_MOD = 6
EPS = 1e-6
NEG_INF = -1e30
GRID_W = 64

FOURIER_GROUPS = 4
FOURIER_GROUP_DIM = 128
FOURIER_WIDTH = FOURIER_GROUPS * FOURIER_GROUP_DIM

N_HEADS = 8
N_KV_HEADS = 2
HEAD_GROUP = N_HEADS // N_KV_HEADS
HEAD_DIM = 64
ATTN_WIDTH = N_HEADS * HEAD_DIM
KV_WIDTH = N_KV_HEADS * HEAD_DIM
WINDOW = 128
ROPE_AXIS_DIM = HEAD_DIM // 2
ROPE_BASE = 10000.0
LOG2_E = math.log2(math.e)
IN_WIDTH = FOURIER_WIDTH + ATTN_WIDTH + 2 * KV_WIDTH

LANES = 128
SUBLANES = 8
VMEM_LIMIT = 56 * 1024 * 1024

SSM_GROUP_DIM = 16
SSM_GROUPS = D_MODEL // SSM_GROUP_DIM
SSM_STATE = 64
SSM_CHUNK = 16
SSM_CW = SSM_CHUNK * SSM_GROUP_DIM
SSM_GROUPS_PER_STEP = 4
SSM_SLABS = D_MODEL // LANES
SSM_PIECES = LANES // SSM_GROUP_DIM

D_FF = 4 * D_MODEL

F32 = jnp.float32
BF16 = jnp.bfloat16


def _cparams(n_axes):
    return pltpu.CompilerParams(dimension_semantics=("arbitrary",) * n_axes, vmem_limit_bytes=VMEM_LIMIT)


def _resident(shape):
    nd = len(shape)
    return pl.BlockSpec(shape, lambda *_: (0,) * nd, pipeline_mode=pl.Buffered(1))


def _rms(x, g):
    return x * lax.rsqrt(jnp.mean(x * x, axis=-1, keepdims=True) + EPS) * g


def _norm_mod(x, g, sc, sh):
    return _rms(x, g) * (1.0 + sc) + sh


def _mod_kernel(cond_ref, w_ref, b_ref, o_ref):
    cond = cond_ref[...]
    s = cond * jax.nn.sigmoid(cond)
    o_ref[0] = jnp.dot(s, w_ref[0], precision=lax.Precision.HIGHEST, preferred_element_type=F32) + b_ref[0]


def _modulation(cond, mod_w, mod_b):
    rows = cond.shape[0]
    tn = 1024
    n = N_MOD * D_MODEL
    return pl.pallas_call(
        _mod_kernel,
        out_shape=jax.ShapeDtypeStruct((DEPTH, rows, n), F32),
        grid=(DEPTH, n // tn),
        in_specs=[pl.BlockSpec((rows, D_MODEL), lambda l, j: (0, 0)),
                  pl.BlockSpec((1, D_MODEL, tn), lambda l, j: (l, 0, j)),
                  pl.BlockSpec((1, 1, tn), lambda l, j: (l, 0, j))],
        out_specs=pl.BlockSpec((1, rows, tn), lambda l, j: (l, 0, j)),
        compiler_params=_cparams(2),
        name="modulation",
    )(cond, mod_w, mod_b.reshape(DEPTH, 1, n))


class _Tokens:
    def __init__(self, batch, seq, n_ctx):
        self.batch, self.seq, self.n_ctx = batch, seq, n_ctx
        self.n_lat = batch * seq
        self.n_all = self.n_lat + batch * n_ctx

    def mod_spec(self, tm):
        per_batch = self.seq // tm
        return pl.BlockSpec((1, N_MOD, D_MODEL), lambda i: (jnp.minimum(i // per_batch, self.batch), 0, 0))

    def split_specs(self, tm, width, joined):
        nlt = self.n_lat // tm
        ctx_map = (lambda i: (jnp.maximum(i, nlt), 0)) if joined else (lambda i: (jnp.maximum(i - nlt, 0), 0))
        return pl.BlockSpec((tm, width), lambda i: (jnp.minimum(i, nlt - 1), 0)), pl.BlockSpec((tm, width), ctx_map)

    def pos_grid(self, blk, with_ctx):
        n_lat_blk = self.seq // blk
        n_ctx_blk = self.n_ctx // blk
        lat_blocks = self.n_lat // blk
        row_map = lambda p, b: (jnp.where(p < n_lat_blk, b * n_lat_blk + p, lat_blocks + b * n_ctx_blk + (p - n_lat_blk)), 0)
        mod_map = lambda p, b: (jnp.where(p < n_lat_blk, b, self.batch), 0, 0)
        grid = (n_lat_blk + (n_ctx_blk if with_ctx else 0), self.batch)
        return grid, row_map, mod_map


def _pick(n_lat_tiles, lat_ref, ctx_ref):
    return jnp.where(pl.program_id(0) < n_lat_tiles, lat_ref[...], ctx_ref[...])


def _layer_resident(layer, shape):
    nd = len(shape)
    return pl.BlockSpec((None,) + tuple(shape), lambda *_: (layer,) + (0,) * nd, pipeline_mode=pl.Buffered(1))


def _row_spec(tm, width):
    return pl.BlockSpec((tm, width), lambda i: (i, 0))


def _vec_spec(width):
    return pl.BlockSpec((1, width), lambda *_: (0, 0))


FFN_CHUNK = 512


def _ffn_body(x, mod_ref, pre_ref, post_ref, w1_ref, w2_ref, o_ref, acc_ref):
    h = _norm_mod(x, pre_ref[...], mod_ref[0, 4:5, :], mod_ref[0, 3:4, :]).astype(BF16)
    for c in range(D_FF // FFN_CHUNK):
        sl = slice(c * FFN_CHUNK, (c + 1) * FFN_CHUNK)
        a = jnp.maximum(jnp.dot(h, w1_ref[:, sl], preferred_element_type=F32), 0.0)
        part = jnp.dot((a * a).astype(BF16), w2_ref[sl, :], preferred_element_type=F32)
        if c == 0:
            acc_ref[...] = part
        else:
            acc_ref[...] += part
    o_ref[...] = x + mod_ref[0, 5:6, :] * _rms(acc_ref[...], post_ref[...])


def _mix_ffn_kernel(xl_ref, xc_ref, fl_ref, fc_ref, ao_ref, mod_ref, mpost_ref, pre_ref, post_ref, wf_ref, wa_ref,
                    w1_ref, w2_ref, o_ref, acc_ref, *, n_lat_tiles):
    y = (jnp.dot(_pick(n_lat_tiles, fl_ref, fc_ref), wf_ref[...], preferred_element_type=F32)
         + jnp.dot(ao_ref[...], wa_ref[...], preferred_element_type=F32))
    x = _pick(n_lat_tiles, xl_ref, xc_ref) + mod_ref[0, 2:3, :] * _rms(y, mpost_ref[...])
    _ffn_body(x, mod_ref, pre_ref, post_ref, w1_ref, w2_ref, o_ref, acc_ref)


def _mix_ffn(tok, x_lat, x_ctx, fm_lat, fm_ctx, ao, mod_l, mix_post_g, pre_g, post_g, w_f, w_a, w1_all, w2_all,
             layer, tm):
    n = tok.n_all
    kern = functools.partial(_mix_ffn_kernel, n_lat_tiles=tok.n_lat // tm)
    return pl.pallas_call(
        kern,
        out_shape=jax.ShapeDtypeStruct((n, D_MODEL), F32),
        grid=(n // tm,),
        in_specs=[*tok.split_specs(tm, D_MODEL, x_lat is x_ctx), *tok.split_specs(tm, FOURIER_WIDTH, False),
                  _row_spec(tm, ATTN_WIDTH), tok.mod_spec(tm),
                  _vec_spec(D_MODEL), _vec_spec(D_MODEL), _vec_spec(D_MODEL),
                  _resident((FOURIER_WIDTH, D_MODEL)), _resident((ATTN_WIDTH, D_MODEL)),
                  _layer_resident(layer, (D_MODEL, D_FF)), _layer_resident(layer, (D_FF, D_MODEL))],
        out_specs=_row_spec(tm, D_MODEL),
        scratch_shapes=[pltpu.VMEM((tm, D_MODEL), F32)],
        compiler_params=_cparams(1),
        name="mix_ffn",
    )(x_lat, x_ctx, fm_lat, fm_ctx, ao, mod_l, mix_post_g, pre_g, post_g, w_f, w_a, w1_all, w2_all)


def _rope_block(x, cos, sin_hi, sin_lo):
    half = ROPE_AXIS_DIM // 2
    return (x * cos + pltpu.roll(x, half, axis=1) * sin_hi
            + pltpu.roll(x, LANES - half, axis=1) * sin_lo)


def _inproj_kernel(xl_ref, xc_ref, mod_ref, pre_ref, w_ref, cos_ref, shi_ref, slo_ref, f_ref, q_ref, k_ref, v_ref,
                   *, n_lat_tiles):
    x = _pick(n_lat_tiles, xl_ref, xc_ref)
    h = _norm_mod(x, pre_ref[...], mod_ref[0, 1:2, :], mod_ref[0, 0:1, :]).astype(BF16)
    p = jnp.dot(h, w_ref[...], preferred_element_type=F32)
    cos, shi, slo = cos_ref[...], shi_ref[...], slo_ref[...]
    f_ref[...] = p[:, :FOURIER_WIDTH].astype(BF16)
    scale = HEAD_DIM ** -0.5 * LOG2_E
    for j in range(ATTN_WIDTH // LANES):
        lo = FOURIER_WIDTH + j * LANES
        q_ref[:, j * LANES:(j + 1) * LANES] = (_rope_block(p[:, lo:lo + LANES], cos, shi, slo) * scale).astype(BF16)
    k0 = FOURIER_WIDTH + ATTN_WIDTH
    k_ref[...] = _rope_block(p[:, k0:k0 + KV_WIDTH], cos, shi, slo).astype(BF16)
    v_ref[...] = p[:, k0 + KV_WIDTH:].astype(BF16)


def _rope_tables(seq, n_pad):
    pos = np.arange(seq)
    row = (pos // GRID_W).astype(np.float64)
    col = (pos % GRID_W).astype(np.float64)
    lane = np.arange(LANES)
    d = lane % HEAD_DIM
    j = d % (ROPE_AXIS_DIM // 2)
    inv = jnp.asarray(ROPE_BASE, F32) ** (-jnp.asarray(2 * j, F32) / ROPE_AXIS_DIM)
    use_col = jnp.asarray(d >= ROPE_AXIS_DIM)
    posv = jnp.where(use_col[None, :], jnp.asarray(col, F32)[:, None], jnp.asarray(row, F32)[:, None])
    ang = posv * inv[None, :]
    upper = jnp.asarray((d % ROPE_AXIS_DIM) >= ROPE_AXIS_DIM // 2)[None, :]
    cos, sin = jnp.cos(ang), jnp.sin(ang)
    sin_hi = jnp.where(upper, sin, 0.0)
    sin_lo = jnp.where(upper, 0.0, -sin)
    pad = lambda t, v: jnp.concatenate([t, jnp.full((n_pad, LANES), v, F32)], axis=0)
    return pad(cos, 1.0), pad(sin_hi, 0.0), pad(sin_lo, 0.0)


def _inproj(tok, x_lat, x_ctx, mod_l, pre_g, w_in, tables, tm):
    per_batch = tok.seq // tm
    n_lat_tiles = tok.n_lat // tm
    tab_map = lambda i: (jnp.where(i < n_lat_tiles, i % per_batch, per_batch), 0)
    tab_spec = pl.BlockSpec((tm, LANES), tab_map)
    n = tok.n_all
    kern = functools.partial(_inproj_kernel, n_lat_tiles=n_lat_tiles)
    return pl.pallas_call(
        kern,
        out_shape=(jax.ShapeDtypeStruct((n, FOURIER_WIDTH), BF16), jax.ShapeDtypeStruct((n, ATTN_WIDTH), BF16),
                   jax.ShapeDtypeStruct((n, KV_WIDTH), BF16), jax.ShapeDtypeStruct((n, KV_WIDTH), BF16)),
        grid=(n // tm,),
        in_specs=[*tok.split_specs(tm, D_MODEL, x_lat is x_ctx), tok.mod_spec(tm), _vec_spec(D_MODEL),
                  _resident((D_MODEL, IN_WIDTH)), tab_spec, tab_spec, tab_spec],
        out_specs=(_row_spec(tm, FOURIER_WIDTH), _row_spec(tm, ATTN_WIDTH), _row_spec(tm, KV_WIDTH),
                   _row_spec(tm, KV_WIDTH)),
        compiler_params=_cparams(1),
        name="inproj",
    )(x_lat, x_ctx, mod_l, pre_g, w_in, *tables)


def _head_pair_perm():
    cols = []
    for i in range(HEAD_GROUP):
        for h in (i, i + HEAD_GROUP):
            cols.extend(range(h * HEAD_DIM, (h + 1) * HEAD_DIM))
    return np.asarray(cols)


def _dft_tables(length):
    def cs(n):
        k = np.arange(n)
        ang = 2.0 * np.pi * ((k[:, None] * k[None, :]) % n) / n
        return np.cos(ang) / math.sqrt(n), np.sin(ang) / math.sqrt(n)
    cl, sl = cs(length)
    cc, sc = cs(FOURIER_GROUP_DIM)
    pos = np.concatenate([cl, -sl], axis=1).astype(np.float32)
    chan = np.concatenate([cc, sc], axis=1).astype(np.float32)
    return jnp.asarray(pos).astype(BF16), jnp.asarray(chan).astype(BF16)


def _fourier_one(f_ref, chan_ref, pos_ref, o_ref, stk_ref):
    gd = FOURIER_GROUP_DIM
    length = f_ref.shape[0]
    row_chunk = min(length, 512)
    for g in range(FOURIER_GROUPS):
        z = jnp.dot(f_ref[:, g * gd:(g + 1) * gd], chan_ref[...], preferred_element_type=F32)
        stk_ref[0:length, g * gd:(g + 1) * gd] = z[:, :gd].astype(BF16)
        stk_ref[length:2 * length, g * gd:(g + 1) * gd] = z[:, gd:].astype(BF16)
    for r in range(length // row_chunk):
        rows = slice(r * row_chunk, (r + 1) * row_chunk)
        o_ref[rows, :] = jnp.dot(pos_ref[rows, :], stk_ref[...], preferred_element_type=F32).astype(BF16)


def _fourier_kernel(fl_ref, fc_ref, chan_ref, posl_ref, posc_ref, ol_ref, oc_ref, stkl_ref, stkc_ref):
    _fourier_one(fl_ref, chan_ref, posl_ref, ol_ref, stkl_ref)
    _fourier_one(fc_ref, chan_ref, posc_ref, oc_ref, stkc_ref)


def _fourier(tok, f_all, tabs_lat, tabs_ctx):
    pos_l, chan = tabs_lat
    pos_c, _ = tabs_ctx
    ctx0 = tok.n_lat // tok.n_ctx
    lat_blk = lambda m: pl.BlockSpec((tok.seq, FOURIER_WIDTH), m)
    ctx_blk = lambda m: pl.BlockSpec((tok.n_ctx, FOURIER_WIDTH), m)
    return pl.pallas_call(
        _fourier_kernel,
        out_shape=(jax.ShapeDtypeStruct((tok.n_lat, FOURIER_WIDTH), BF16),
                   jax.ShapeDtypeStruct((tok.batch * tok.n_ctx, FOURIER_WIDTH), BF16)),
        grid=(tok.batch,),
        in_specs=[lat_blk(lambda b: (b, 0)), ctx_blk(lambda b: (ctx0 + b, 0)), _resident(chan.shape),
                  _resident(pos_l.shape), _resident(pos_c.shape)],
        out_specs=(lat_blk(lambda b: (b, 0)), ctx_blk(lambda b: (b, 0))),
        scratch_shapes=[pltpu.VMEM((2 * tok.seq, FOURIER_WIDTH), BF16),
                        pltpu.VMEM((2 * tok.n_ctx, FOURIER_WIDTH), BF16)],
        compiler_params=_cparams(1),
        name="fourier",
    )(f_all, f_all, chan, pos_l, pos_c)


def _attn_kernel(sink_ref, q_ref, kp_ref, kc_ref, kn_ref, kx_ref, vp_ref, vc_ref, vn_ref, vx_ref, o_ref,
                 *, n_qblk):
    j = pl.program_id(1)
    blk = WINDOW
    n_loc = 3 * blk
    rows = HEAD_GROUP * blk
    q = jnp.concatenate([q_ref[:, i * LANES:(i + 1) * LANES] for i in range(HEAD_GROUP)], axis=0)
    k_loc = jnp.concatenate([kp_ref[...], kc_ref[...], kn_ref[...]], axis=0)
    v_loc = jnp.concatenate([vp_ref[...], vc_ref[...], vn_ref[...]], axis=0)
    k_ctx, v_ctx = kx_ref[...], vx_ref[...]
    is_lat = j < n_qblk
    col_lo = jnp.where(is_lat, jnp.where(j >= 1, 0, blk), n_loc)
    col_hi = jnp.where(is_lat, jnp.where(j + 1 < n_qblk, n_loc, 2 * blk), 0)
    qi = lax.broadcasted_iota(jnp.int32, (blk, 1), 0)
    kj = lax.broadcasted_iota(jnp.int32, (blk, n_loc), 1)
    visible = (kj >= jnp.maximum(qi, col_lo)) & (kj <= jnp.minimum(qi + 2 * WINDOW, col_hi - 1))
    bias = jnp.where(visible, 0.0, NEG_INF)
    bias = jnp.concatenate([bias] * HEAD_GROUP, axis=0)
    row_blk = lax.broadcasted_iota(jnp.int32, (rows, 1), 0) // blk
    lane_o = lax.broadcasted_iota(jnp.int32, (rows, LANES), 1)
    nt = (((1,), (1,)), ((), ()))
    outs = []
    for kvh in range(N_KV_HEADS):
        def own_lanes(t):
            lane = lax.broadcasted_iota(jnp.int32, t.shape, 1)
            return jnp.where((lane >= kvh * HEAD_DIM) & (lane < (kvh + 1) * HEAD_DIM), t, jnp.zeros_like(t))
        s_loc = lax.dot_general(q, own_lanes(k_loc), nt, preferred_element_type=F32) + bias
        s_ctx = lax.dot_general(q, own_lanes(k_ctx), nt, preferred_element_type=F32)
        sink = jnp.zeros((rows, 1), F32)
        for i in range(HEAD_GROUP):
            sink = jnp.where(row_blk == i, sink_ref[kvh * HEAD_GROUP + i] * LOG2_E, sink)
        m = jnp.maximum(jnp.maximum(jnp.max(s_loc, axis=-1, keepdims=True), jnp.max(s_ctx, axis=-1, keepdims=True)),
                        sink)
        p_loc = jnp.exp2(s_loc - m)
        p_ctx = jnp.exp2(s_ctx - m)
        denom = (jnp.sum(p_loc, axis=-1, keepdims=True) + jnp.sum(p_ctx, axis=-1, keepdims=True)
                 + jnp.exp2(sink - m))
        o = (jnp.dot(p_loc.astype(BF16), v_loc, preferred_element_type=F32)
             + jnp.dot(p_ctx.astype(BF16), v_ctx, preferred_element_type=F32))
        outs.append(o / denom)
    merged = jnp.where(lane_o < HEAD_DIM, outs[0], outs[1])
    for i in range(HEAD_GROUP):
        o_ref[:, i * LANES:(i + 1) * LANES] = merged[i * blk:(i + 1) * blk, :].astype(BF16)


def _attention(tok, q, k, v, sink):
    blk = WINDOW
    n_qblk = tok.seq // blk
    n_cblk = tok.n_ctx // blk
    lat_blocks = tok.n_lat // blk
    ctx0 = tok.n_lat // tok.n_ctx

    def q_map(b, j):
        return (jnp.where(j < n_qblk, b * n_qblk + j, lat_blocks + b * n_cblk + (j - n_qblk)), 0)

    def k_map(off):
        return lambda b, j: (b * n_qblk + jnp.clip(j + off, 0, n_qblk - 1), 0)

    ctx_map = lambda b, j: (ctx0 + b, 0)
    kv_blk = lambda m: pl.BlockSpec((blk, KV_WIDTH), m)
    ctx_blk = pl.BlockSpec((tok.n_ctx, KV_WIDTH), ctx_map)
    kern = functools.partial(_attn_kernel, n_qblk=n_qblk)
    return pl.pallas_call(
        kern,
        out_shape=jax.ShapeDtypeStruct((tok.n_all, ATTN_WIDTH), BF16),
        grid=(tok.batch, n_qblk + n_cblk),
        in_specs=[pl.BlockSpec(memory_space=pltpu.SMEM),
                  pl.BlockSpec((blk, ATTN_WIDTH), q_map),
                  kv_blk(k_map(-1)), kv_blk(k_map(0)), kv_blk(k_map(1)), ctx_blk,
                  kv_blk(k_map(-1)), kv_blk(k_map(0)), kv_blk(k_map(1)), ctx_blk],
        out_specs=pl.BlockSpec((blk, ATTN_WIDTH), q_map),
        compiler_params=_cparams(2),
        name="window_attention",
    )(sink, q, k, k, k, k, v, v, v, v)


def _piece_perm():
    idx = np.arange(D_MODEL)
    a, b, c = idx // LANES, (idx // SSM_GROUP_DIM) % SSM_PIECES, idx % SSM_GROUP_DIM
    perm = np.zeros((D_MODEL, D_MODEL), np.float32)
    perm[idx, b * LANES + a * SSM_GROUP_DIM + c] = 1.0
    return jnp.asarray(perm).astype(BF16)


def _slab_pitch(blk):
    return blk + SUBLANES


def _s5_pre_kernel(x_ref, mod_ref, pre_ref, perm_ref, u_ref, hs_ref, *, blk, batch, n_blocks):
    p, b = pl.program_id(0), pl.program_id(1)
    pitch = _slab_pitch(blk)

    @pl.when(p < n_blocks)
    def _():
        h = _norm_mod(x_ref[...], pre_ref[...], mod_ref[0, 1:2, :], mod_ref[0, 0:1, :])
        row0 = pl.multiple_of(b * pitch, SUBLANES)
        slot0 = (p % 2) * SSM_SLABS
        for s in range(SSM_SLABS):
            hs_ref[slot0 + s, pl.ds(row0, blk), :] = h[:, s * LANES:(s + 1) * LANES]

    @pl.when(p >= 1)
    def _():
        half = SSM_CHUNK // 2
        cpb = blk // SSM_CHUNK
        rows_h = cpb * batch
        s = b
        src = ((p - 1) % 2) * SSM_SLABS + s
        lhs = []
        for hh in range(2):
            for i in range(cpb):
                t0 = i * SSM_CHUNK + hh * half
                lhs.append(jnp.concatenate(
                    [hs_ref[src, pl.ds(t0 + j, batch, stride=pitch), :] for j in range(half)], axis=1))
        lhs = jnp.concatenate(lhs, axis=0).astype(BF16)
        out = jnp.dot(lhs, perm_ref[...], preferred_element_type=F32).astype(BF16)
        for hh in range(2):
            for g in range(SSM_PIECES):
                u_ref[s * SSM_PIECES + g, :, hh * LANES:(hh + 1) * LANES] = (
                    out[hh * rows_h:(hh + 1) * rows_h, g * LANES:(g + 1) * LANES])


def _s5_pre(tok, xa, mod_l, pre_g, perm, blk):
    assert tok.batch == SSM_SLABS
    (n_blocks, _), row_map, mod_map = tok.pos_grid(blk, True)
    n_lat_blk = tok.seq // blk
    n_ctx_blk = tok.n_ctx // blk
    cpb = blk // SSM_CHUNK
    n_chunks = (tok.seq + tok.n_ctx) // SSM_CHUNK
    last = n_blocks - 1

    def u_map(p, b):
        q = jnp.maximum(p - 1, 0)
        return (0, jnp.where(q < n_lat_blk, n_ctx_blk + q, q - n_lat_blk), 0)

    clamp = lambda m: (lambda p, b: m(jnp.minimum(p, last), b))
    kern = functools.partial(_s5_pre_kernel, blk=blk, batch=tok.batch, n_blocks=n_blocks)
    return pl.pallas_call(
        kern,
        out_shape=jax.ShapeDtypeStruct((SSM_GROUPS, n_chunks * tok.batch, SSM_CW), BF16),
        grid=(n_blocks + 1, tok.batch),
        in_specs=[pl.BlockSpec((blk, D_MODEL), clamp(row_map)), pl.BlockSpec((1, N_MOD, D_MODEL), clamp(mod_map)),
                  _vec_spec(D_MODEL), _resident((D_MODEL, D_MODEL))],
        out_specs=pl.BlockSpec((SSM_GROUPS, cpb * tok.batch, SSM_CW), u_map),
        scratch_shapes=[pltpu.VMEM((2 * SSM_SLABS, tok.batch * _slab_pitch(blk), LANES), F32)],
        compiler_params=_cparams(2),
        name="s5_pre",
    )(xa, mod_l, pre_g, perm)


def _cmul(ar, ai, br, bi):
    return ar * br - ai * bi, ar * bi + ai * br


def _s5_factors(a_re, a_im, log_dt, b_re, b_im, c_re, c_im):
    t_n, gd = SSM_CHUNK, SSM_GROUP_DIM
    dt = jnp.exp(log_dt.astype(F32))[..., None]
    l_re, l_im = a_re.astype(F32), a_im.astype(F32)
    z_re, z_im = l_re * dt, l_im * dt

    def apow(d, n):
        n = jnp.asarray(n, F32)[None, :, None]
        mag = jnp.exp(z_re[d][:, None, :] * n)
        ang = z_im[d][:, None, :] * n
        return mag * jnp.cos(ang), mag * jnp.sin(ang)

    bb, cc = [], []
    for d in range(2):
        a1_re, a1_im = apow(d, [1.0])
        num_re, num_im = a1_re[:, 0] - 1.0, a1_im[:, 0]
        den = l_re[d] * l_re[d] + l_im[d] * l_im[d]
        r_re = (num_re * l_re[d] + num_im * l_im[d]) / den
        r_im = (num_im * l_re[d] - num_re * l_im[d]) / den
        bb_re, bb_im = _cmul(r_re[..., None], r_im[..., None], b_re[d], b_im[d])
        bb.append((bb_re.transpose(0, 2, 1), bb_im.transpose(0, 2, 1)))
        cc.append((c_re[d].astype(F32), c_im[d].astype(F32)))

    t = np.arange(t_n, dtype=np.float64)

    def table(blocks):
        pr, pi, wa, wb = [], [], [], []
        for d, power, (w_re, w_im), part in blocks:
            p_re, p_im = apow(d, power)
            pr.append(p_re)
            pi.append(p_im)
            wa.append({'re': w_re, 'im': w_im, '-im': -w_im}[part])
            wb.append({'re': -w_im, 'im': w_re, '-im': -w_re}[part])
        pr, pi, wa, wb = (jnp.concatenate(x, axis=-1) for x in (pr, pi, wa, wb))
        out = pr[:, :, None, :] * wa[:, None, :, :] + pi[:, :, None, :] * wb[:, None, :, :]
        return out.reshape(out.shape[0], t_n * gd, out.shape[-1])

    fwd_inc, bwd_inc = t_n - 1 - t, t
    q = table([(0, fwd_inc, bb[0], 're'), (1, bwd_inc, bb[1], 're'),
               (0, fwd_inc, bb[0], 'im'), (1, bwd_inc, bb[1], 'im')]).astype(BF16)
    e = table([(0, -t, bb[0], 're'), (0, -t, bb[0], 'im'), (1, t, bb[1], 're'), (1, t, bb[1], 'im')])
    ft = table([(0, t, cc[0], 're'), (0, t, cc[0], '-im'), (1, -t, cc[1], 're'), (1, -t, cc[1], '-im')])
    fwd_out, bwd_out = t + 1, t_n - t
    pt = table([(0, fwd_out, cc[0], 're'), (1, bwd_out, cc[1], 're'),
                (0, fwd_out, cc[0], '-im'), (1, bwd_out, cc[1], '-im')]).astype(BF16)

    def split(x):
        hi = x.astype(BF16)
        return hi, (x - hi.astype(F32)).astype(BF16)

    (ar_f, ai_f), (ar_b, ai_b) = apow(0, [float(t_n)]), apow(1, [float(t_n)])
    dec = jnp.concatenate([ar_f, ar_b, ai_f, ai_b], axis=-1)
    return (*split(e), *split(ft), q, pt, dec)


def _s5_kernel(u_ref, eh_ref, el_ref, fh_ref, fl_ref, q_ref, pt_ref, dec_ref, y_ref, v_ref, xin_ref,
               *, batch, n_chunks, n_ctx_chunks):
    gps = SSM_GROUPS_PER_STEP
    ns = SSM_STATE
    ns2 = 2 * ns
    cw = SSM_CW
    nt = (((1,), (1,)), ((), ()))
    for g in range(gps):
        v_ref[g] = jnp.dot(u_ref[g], q_ref[g], preferred_element_type=F32)
    a_re = jnp.broadcast_to(dec_ref[:, :, 0:ns2], (gps, batch, ns2))
    a_im = jnp.broadcast_to(dec_ref[:, :, ns2:2 * ns2], (gps, batch, ns2))
    is_fwd = lax.broadcasted_iota(jnp.int32, (gps, batch, ns2), 2) < ns

    def step(k, carry):
        x_re, x_im = carry
        kb = jnp.where(k < n_ctx_chunks, n_ctx_chunks - 1 - k, n_chunks + n_ctx_chunks - 1 - k)
        rf = pl.multiple_of(k * batch, batch)
        rb = pl.multiple_of(kb * batch, batch)
        xin_ref[:, pl.ds(rf, batch), 0:ns] = x_re[:, :, 0:ns]
        xin_ref[:, pl.ds(rb, batch), ns:ns2] = x_re[:, :, ns:ns2]
        xin_ref[:, pl.ds(rf, batch), ns2:ns2 + ns] = x_im[:, :, 0:ns]
        xin_ref[:, pl.ds(rb, batch), ns2 + ns:2 * ns2] = x_im[:, :, ns:ns2]
        v_re = jnp.where(is_fwd, v_ref[:, pl.ds(rf, batch), 0:ns2], v_ref[:, pl.ds(rb, batch), 0:ns2])
        v_im = jnp.where(is_fwd, v_ref[:, pl.ds(rf, batch), ns2:2 * ns2], v_ref[:, pl.ds(rb, batch), ns2:2 * ns2])
        return a_re * x_re - a_im * x_im + v_re, a_re * x_im + a_im * x_re + v_im

    zero = jnp.zeros((gps, batch, ns2), F32)
    lax.fori_loop(0, n_chunks, step, (zero, zero))
    t_in = lax.broadcasted_iota(jnp.int32, (cw, cw), 0) // SSM_GROUP_DIM
    t_out = lax.broadcasted_iota(jnp.int32, (cw, cw), 1) // SSM_GROUP_DIM

    def lag_kernel(g, lanes):
        eh, el, fh, fl = eh_ref[g, :, lanes], el_ref[g, :, lanes], fh_ref[g, :, lanes], fl_ref[g, :, lanes]
        return (lax.dot_general(eh, fh, nt, preferred_element_type=F32)
                + lax.dot_general(eh, fl, nt, preferred_element_type=F32)
                + lax.dot_general(el, fh, nt, preferred_element_type=F32))

    for g in range(gps):
        m = (jnp.where(t_out >= t_in, lag_kernel(g, slice(0, ns2)), 0.0)
             + jnp.where(t_in >= t_out, lag_kernel(g, slice(ns2, 2 * ns2)), 0.0)).astype(BF16)
        y_ref[g] = (jnp.dot(u_ref[g], m, preferred_element_type=F32)
                    + lax.dot_general(xin_ref[g].astype(BF16), pt_ref[g], nt, preferred_element_type=F32)
                    ).astype(BF16)


def _s5(u_t, factors, batch, n_chunks, n_ctx_chunks):
    gps = SSM_GROUPS_PER_STEP
    rows = n_chunks * batch
    kern = functools.partial(_s5_kernel, batch=batch, n_chunks=n_chunks, n_ctx_chunks=n_ctx_chunks)
    gspec = lambda r, c: pl.BlockSpec((gps, r, c), lambda i: (i, 0, 0))
    sq = gspec(SSM_CW, SSM_CW)
    return pl.pallas_call(
        kern,
        out_shape=jax.ShapeDtypeStruct((SSM_GROUPS, rows, SSM_CW), BF16),
        grid=(SSM_GROUPS // gps,),
        in_specs=[gspec(rows, SSM_CW), sq, sq, sq, sq, sq, sq, gspec(1, SSM_CW)],
        out_specs=gspec(rows, SSM_CW),
        scratch_shapes=[pltpu.VMEM((gps, rows, SSM_CW), F32), pltpu.VMEM((gps, rows, SSM_CW), F32)],
        compiler_params=_cparams(1),
        name="s5_scan",
    )(u_t, *factors)


def _s5_unpack_kernel(y_ref, perm_ref, o_ref, ys_ref, tmp_ref, *, blk, batch):
    b = pl.program_id(1)
    pitch = _slab_pitch(blk)

    @pl.when(b == 0)
    def _():
        half = SSM_CHUNK // 2
        cpb = blk // SSM_CHUNK
        rows_h = cpb * batch

        def slab(s, carry):
            lhs = jnp.concatenate(
                [jnp.concatenate([y_ref[s * SSM_PIECES + g, :, hh * LANES:(hh + 1) * LANES]
                                  for g in range(SSM_PIECES)], axis=1) for hh in range(2)], axis=0)
            tmp_ref[...] = jnp.dot(lhs, perm_ref[...], preferred_element_type=F32)
            for hh in range(2):
                for i in range(cpb):
                    r0 = hh * rows_h + i * batch
                    t0 = i * SSM_CHUNK + hh * half
                    for j in range(half):
                        ys_ref[s, pl.ds(t0 + j, batch, stride=pitch), :] = (
                            tmp_ref[r0:r0 + batch, j * LANES:(j + 1) * LANES])
            return carry

        lax.fori_loop(0, SSM_SLABS, slab, 0)

    row0 = pl.multiple_of(b * pitch, SUBLANES)
    for s in range(SSM_SLABS):
        o_ref[:, s * LANES:(s + 1) * LANES] = ys_ref[s, pl.ds(row0, blk), :].astype(BF16)


def _s5_unpack(tok, y_t, perm, blk, with_ctx):
    grid, row_map, _ = tok.pos_grid(blk, with_ctx)
    n_lat_blk = tok.seq // blk
    n_ctx_blk = tok.n_ctx // blk
    cpb = blk // SSM_CHUNK
    y_map = lambda p, b: (0, jnp.where(p < n_lat_blk, n_ctx_blk + p, p - n_lat_blk), 0)
    n_rows = tok.n_all if with_ctx else tok.n_lat
    kern = functools.partial(_s5_unpack_kernel, blk=blk, batch=tok.batch)
    return pl.pallas_call(
        kern,
        out_shape=jax.ShapeDtypeStruct((n_rows, D_MODEL), BF16),
        grid=grid,
        in_specs=[pl.BlockSpec((SSM_GROUPS, cpb * tok.batch, SSM_CW), y_map), _resident((D_MODEL, D_MODEL))],
        out_specs=pl.BlockSpec((blk, D_MODEL), row_map),
        scratch_shapes=[pltpu.VMEM((SSM_SLABS, tok.batch * _slab_pitch(blk), LANES), F32),
                        pltpu.VMEM((2 * cpb * tok.batch, D_MODEL), F32)],
        compiler_params=_cparams(2),
        name="s5_unpack",
    )(y_t, perm)


def _glu_ffn_kernel(x_ref, y_ref, d_ref, mod_ref, mpre_ref, mpost_ref, pre_ref, post_ref, wg_ref, w1_ref, w2_ref,
                    o_ref, acc_ref):
    x = x_ref[...]
    h = _norm_mod(x, mpre_ref[...], mod_ref[0, 1:2, :], mod_ref[0, 0:1, :])
    y = y_ref[...].astype(F32) + d_ref[...] * h
    z = jnp.dot(jax.nn.gelu(y).astype(BF16), wg_ref[...], preferred_element_type=F32)
    out = z[:, :D_MODEL] * jax.nn.sigmoid(z[:, D_MODEL:])
    x = x + mod_ref[0, 2:3, :] * _rms(out, mpost_ref[...])
    _ffn_body(x, mod_ref, pre_ref, post_ref, w1_ref, w2_ref, o_ref, acc_ref)


def _glu_ffn(tok, xa, n_rows, y_tok, d_skip, mod_l, mix_pre_g, mix_post_g, pre_g, post_g, glu_all, w1_all, w2_all,
             i, layer, tm):
    return pl.pallas_call(
        _glu_ffn_kernel,
        out_shape=jax.ShapeDtypeStruct((n_rows, D_MODEL), F32),
        grid=(n_rows // tm,),
        in_specs=[_row_spec(tm, D_MODEL), _row_spec(tm, D_MODEL), _vec_spec(D_MODEL), tok.mod_spec(tm),
                  _vec_spec(D_MODEL), _vec_spec(D_MODEL), _vec_spec(D_MODEL), _vec_spec(D_MODEL),
                  _layer_resident(i, (D_MODEL, 2 * D_MODEL)),
                  _layer_resident(layer, (D_MODEL, D_FF)), _layer_resident(layer, (D_FF, D_MODEL))],
        out_specs=_row_spec(tm, D_MODEL),
        scratch_shapes=[pltpu.VMEM((tm, D_MODEL), F32)],
        compiler_params=_cparams(1),
        name="glu_ffn",
    )(xa, y_tok, d_skip, mod_l, mix_pre_g, mix_post_g, pre_g, post_g, glu_all, w1_all, w2_all)


def _tile(limit, *sizes):
    tm = limit
    while any(s % tm for s in sizes):
        tm //= 2
    return tm


def kernel(x, c, ctx, c_ctx, mod_w, mod_b, mix_pre_g, mix_post_g, ffn_pre_g, ffn_post_g, ffn_w1, ffn_w2,
           even_w_in, even_w_out, even_sink, ssm_a_re, ssm_a_im, ssm_log_dt, ssm_b_re, ssm_b_im, ssm_c_re,
           ssm_c_im, ssm_d, ssm_glu_w):
    batch, seq, _ = x.shape
    n_ctx = ctx.shape[1]
    tok = _Tokens(batch, seq, n_ctx)
    assert seq % WINDOW == 0 and n_ctx % WINDOW == 0 and tok.n_lat % n_ctx == 0
    assert batch == SUBLANES
    tm = _tile(256, seq, n_ctx)
    tm_wide = _tile(512, seq, batch * n_ctx)

    x_lat, x_ctx = x.reshape(tok.n_lat, D_MODEL), ctx.astype(x.dtype).reshape(-1, D_MODEL)

    n_cond = 2 * SUBLANES
    cond = jnp.zeros((n_cond, D_MODEL), F32).at[:batch].set(c).at[batch].set(c_ctx)
    mod = _modulation(cond, mod_w, mod_b).reshape(DEPTH, n_cond, N_MOD, D_MODEL)

    rope = _rope_tables(seq, tm_wide)
    head_perm = _head_pair_perm()
    piece_perm = _piece_perm()
    dft_lat, dft_ctx = _dft_tables(seq), _dft_tables(n_ctx)
    vec = lambda g: g.reshape(1, D_MODEL)
    w1_all, w2_all, glu_all = ffn_w1.astype(BF16), ffn_w2.astype(BF16), ssm_glu_w.astype(BF16)

    for layer in range(DEPTH):
        need_ctx = layer < DEPTH - 1
        n_rows = tok.n_all if need_ctx else tok.n_lat
        mod_l = mod[layer]
        i = layer // 2
        ffn_pre, ffn_post = vec(ffn_pre_g[layer]), vec(ffn_post_g[layer])
        if layer % 2 == 0:
            w_in = even_w_in[i]
            q0 = FOURIER_WIDTH
            w_in = jnp.concatenate([w_in[:, :q0], w_in[:, q0:q0 + ATTN_WIDTH][:, head_perm],
                                    w_in[:, q0 + ATTN_WIDTH:]], axis=1).astype(BF16)
            f, q, k, v = _inproj(tok, x_lat, x_ctx, mod_l, vec(mix_pre_g[layer]), w_in, rope, tm_wide)
            fm_lat, fm_ctx = _fourier(tok, f, dft_lat, dft_ctx)
            ao = _attention(tok, q, k, v, even_sink[i])
            w_out = even_w_out[i]
            w_f = w_out[:FOURIER_WIDTH].astype(BF16)
            w_a = w_out[FOURIER_WIDTH:][head_perm].astype(BF16)
            xa = _mix_ffn(tok, x_lat, x_ctx, fm_lat, fm_ctx, ao, mod_l, vec(mix_post_g[layer]), ffn_pre, ffn_post,
                          w_f, w_a, w1_all, w2_all, layer, tm_wide)
        else:
            pre_g = vec(mix_pre_g[layer])
            u_t = _s5_pre(tok, xa, mod_l, pre_g, piece_perm, tm)
            factors = _s5_factors(ssm_a_re[i], ssm_a_im[i], ssm_log_dt[i], ssm_b_re[i], ssm_b_im[i],
                                  ssm_c_re[i], ssm_c_im[i])
            y_t = _s5(u_t, factors, batch, (seq + n_ctx) // SSM_CHUNK, n_ctx // SSM_CHUNK)
            y_tok = _s5_unpack(tok, y_t, piece_perm, tm, need_ctx)
            xa = _glu_ffn(tok, xa, n_rows, y_tok, vec(ssm_d[i]), mod_l, pre_g, vec(mix_post_g[layer]),
                          ffn_pre, ffn_post, glu_all, w1_all, w2_all, i, layer, tm_wide)
        x_lat = x_ctx = xa
    return xa[:tok.n_lat].reshape(batch, seq, D_MODEL)
```

```python
import functools
import math

import numpy as np
import jax
import jax.numpy as jnp
from jax import lax
from jax.experimental import pallas as pl
from jax.experimental.pallas import tpu as pltpu

D_MODEL = 1024
DEPTH = 4
N_MOD = 6
EPS = 1e-6
NEG_INF = -1e30
GRID_W = 64

FOURIER_GROUPS = 4
FOURIER_GROUP_DIM = 128
FOURIER_WIDTH = FOURIER_GROUPS * FOURIER_GROUP_DIM

N_HEADS = 8
N_KV_HEADS = 2
HEAD_GROUP = N_HEADS // N_KV_HEADS
HEAD_DIM = 64
ATTN_WIDTH = N_HEADS * HEAD_DIM
KV_WIDTH = N_KV_HEADS * HEAD_DIM
WINDOW = 128
ROPE_AXIS_DIM = HEAD_DIM // 2
ROPE_BASE = 10000.0
LOG2_E = math.log2(math.e)
IN_WIDTH = FOURIER_WIDTH + ATTN_WIDTH + 2 * KV_WIDTH

LANES = 128
SUBLANES = 8
VMEM_LIMIT = 56 * 1024 * 1024

SSM_GROUP_DIM = 16
SSM_GROUPS = D_MODEL // SSM_GROUP_DIM
SSM_STATE = 64
SSM_CHUNK = 16
SSM_CW = SSM_CHUNK * SSM_GROUP_DIM
SSM_GROUPS_PER_STEP = 4
SSM_SLABS = D_MODEL // LANES
SSM_PIECES = LANES // SSM_GROUP_DIM

D_FF = 4 * D_MODEL

F32 = jnp.float32
BF16 = jnp.bfloat16


def _cparams(n_axes):
    return pltpu.CompilerParams(dimension_semantics=("arbitrary",) * n_axes, vmem_limit_bytes=VMEM_LIMIT)


def _resident(shape):
    nd = len(shape)
    return pl.BlockSpec(shape, lambda *_: (0,) * nd, pipeline_mode=pl.Buffered(1))


def _rms(x, g):
    return x * lax.rsqrt(jnp.mean(x * x, axis=-1, keepdims=True) + EPS) * g


def _norm_mod(x, g, sc, sh):
    return _rms(x, g) * (1.0 + sc) + sh


def _mod_kernel(cond_ref, w_ref, b_ref, o_ref):
    cond = cond_ref[...]
    s = cond * jax.nn.sigmoid(cond)
    s_hi = s.astype(BF16)
    s_lo = (s - s_hi.astype(F32)).astype(BF16)
    w = w_ref[0].astype(BF16)
    o_ref[0] = (jnp.dot(s_hi, w, preferred_element_type=F32) + jnp.dot(s_lo, w, preferred_element_type=F32)
                + b_ref[0])


def _modulation(cond, mod_w, mod_b):
    rows = cond.shape[0]
    tn = 1024
    n = N_MOD * D_MODEL
    return pl.pallas_call(
        _mod_kernel,
        out_shape=jax.ShapeDtypeStruct((DEPTH, rows, n), F32),
        grid=(DEPTH, n // tn),
        in_specs=[pl.BlockSpec((rows, D_MODEL), lambda l, j: (0, 0)),
                  pl.BlockSpec((1, D_MODEL, tn), lambda l, j: (l, 0, j)),
                  pl.BlockSpec((1, 1, tn), lambda l, j: (l, 0, j))],
        out_specs=pl.BlockSpec((1, rows, tn), lambda l, j: (l, 0, j)),
        compiler_params=_cparams(2),
        name="modulation",
    )(cond, mod_w, mod_b.reshape(DEPTH, 1, n))


class _Tokens:
    def __init__(self, batch, seq, n_ctx):
        self.batch, self.seq, self.n_ctx = batch, seq, n_ctx
        self.n_lat = batch * seq
        self.n_all = self.n_lat + batch * n_ctx

    def mod_spec(self, tm):
        per_batch = self.seq // tm
        return pl.BlockSpec((1, N_MOD, D_MODEL), lambda i: (jnp.minimum(i // per_batch, self.batch), 0, 0))

    def split_specs(self, tm, width, joined):
        nlt = self.n_lat // tm
        ctx_map = (lambda i: (jnp.maximum(i, nlt), 0)) if joined else (lambda i: (jnp.maximum(i - nlt, 0), 0))
        return pl.BlockSpec((tm, width), lambda i: (jnp.minimum(i, nlt - 1), 0)), pl.BlockSpec((tm, width), ctx_map)

    def pos_grid(self, blk, with_ctx):
        n_lat_blk = self.seq // blk
        n_ctx_blk = self.n_ctx // blk
        lat_blocks = self.n_lat // blk
        row_map = lambda p, b: (jnp.where(p < n_lat_blk, b * n_lat_blk + p, lat_blocks + b * n_ctx_blk + (p - n_lat_blk)), 0)
        mod_map = lambda p, b: (jnp.where(p < n_lat_blk, b, self.batch), 0, 0)
        grid = (n_lat_blk + (n_ctx_blk if with_ctx else 0), self.batch)
        return grid, row_map, mod_map


def _pick(n_lat_tiles, lat_ref, ctx_ref):
    return jnp.where(pl.program_id(0) < n_lat_tiles, lat_ref[...], ctx_ref[...])


def _layer_resident(layer, shape):
    nd = len(shape)
    return pl.BlockSpec((None,) + tuple(shape), lambda *_: (layer,) + (0,) * nd, pipeline_mode=pl.Buffered(1))


def _row_spec(tm, width):
    return pl.BlockSpec((tm, width), lambda i: (i, 0))


def _vec_spec(width):
    return pl.BlockSpec((1, width), lambda *_: (0, 0))


FFN_CHUNK = 512


def _ffn_body(x, mod_ref, pre_ref, post_ref, w1_ref, w2_ref, o_ref, acc_ref):
    h = _norm_mod(x, pre_ref[...], mod_ref[0, 4:5, :], mod_ref[0, 3:4, :]).astype(BF16)
    for c in range(D_FF // FFN_CHUNK):
        sl = slice(c * FFN_CHUNK, (c + 1) * FFN_CHUNK)
        a = jnp.maximum(jnp.dot(h, w1_ref[:, sl], preferred_element_type=F32), 0.0)
        part = jnp.dot((a * a).astype(BF16), w2_ref[sl, :], preferred_element_type=F32)
        if c == 0:
            acc_ref[...] = part
        else:
            acc_ref[...] += part
    o_ref[...] = x + mod_ref[0, 5:6, :] * _rms(acc_ref[...], post_ref[...])


def _mix_ffn_kernel(xl_ref, xc_ref, fl_ref, fc_ref, ao_ref, mod_ref, mpost_ref, pre_ref, post_ref, wf_ref, wa_ref,
                    w1_ref, w2_ref, o_ref, acc_ref, *, n_lat_tiles):
    y = (jnp.dot(_pick(n_lat_tiles, fl_ref, fc_ref), wf_ref[...], preferred_element_type=F32)
         + jnp.dot(ao_ref[...], wa_ref[...], preferred_element_type=F32))
    x = _pick(n_lat_tiles, xl_ref, xc_ref) + mod_ref[0, 2:3, :] * _rms(y, mpost_ref[...])
    _ffn_body(x, mod_ref, pre_ref, post_ref, w1_ref, w2_ref, o_ref, acc_ref)


def _mix_ffn(tok, x_lat, x_ctx, fm_lat, fm_ctx, ao, mod_l, mix_post_g, pre_g, post_g, w_f, w_a, w1_all, w2_all,
             layer, tm):
    n = tok.n_all
    kern = functools.partial(_mix_ffn_kernel, n_lat_tiles=tok.n_lat // tm)
    return pl.pallas_call(
        kern,
        out_shape=jax.ShapeDtypeStruct((n, D_MODEL), F32),
        grid=(n // tm,),
        in_specs=[*tok.split_specs(tm, D_MODEL, x_lat is x_ctx), *tok.split_specs(tm, FOURIER_WIDTH, False),
                  _row_spec(tm, ATTN_WIDTH), tok.mod_spec(tm),
                  _vec_spec(D_MODEL), _vec_spec(D_MODEL), _vec_spec(D_MODEL),
                  _resident((FOURIER_WIDTH, D_MODEL)), _resident((ATTN_WIDTH, D_MODEL)),
                  _layer_resident(layer, (D_MODEL, D_FF)), _layer_resident(layer, (D_FF, D_MODEL))],
        out_specs=_row_spec(tm, D_MODEL),
        scratch_shapes=[pltpu.VMEM((tm, D_MODEL), F32)],
        compiler_params=_cparams(1),
        name="mix_ffn",
    )(x_lat, x_ctx, fm_lat, fm_ctx, ao, mod_l, mix_post_g, pre_g, post_g, w_f, w_a, w1_all, w2_all)


def _rope_block(x, cos, sin_hi, sin_lo):
    half = ROPE_AXIS_DIM // 2
    return (x * cos + pltpu.roll(x, half, axis=1) * sin_hi
            + pltpu.roll(x, LANES - half, axis=1) * sin_lo)


def _inproj_kernel(xl_ref, xc_ref, mod_ref, pre_ref, w_ref, cos_ref, shi_ref, slo_ref, f_ref, q_ref, k_ref, v_ref,
                   *, n_lat_tiles):
    x = _pick(n_lat_tiles, xl_ref, xc_ref)
    h = _norm_mod(x, pre_ref[...], mod_ref[0, 1:2, :], mod_ref[0, 0:1, :]).astype(BF16)
    p = jnp.dot(h, w_ref[...], preferred_element_type=F32)
    cos, shi, slo = cos_ref[...], shi_ref[...], slo_ref[...]
    f_ref[...] = p[:, :FOURIER_WIDTH].astype(BF16)
    scale = HEAD_DIM ** -0.5 * LOG2_E
    for j in range(ATTN_WIDTH // LANES):
        lo = FOURIER_WIDTH + j * LANES
        q_ref[:, j * LANES:(j + 1) * LANES] = (_rope_block(p[:, lo:lo + LANES], cos, shi, slo) * scale).astype(BF16)
    k0 = FOURIER_WIDTH + ATTN_WIDTH
    k_ref[...] = _rope_block(p[:, k0:k0 + KV_WIDTH], cos, shi, slo).astype(BF16)
    v_ref[...] = p[:, k0 + KV_WIDTH:].astype(BF16)


def _rope_tables(seq, n_pad):
    pos = np.arange(seq)
    row = (pos // GRID_W).astype(np.float64)
    col = (pos % GRID_W).astype(np.float64)
    lane = np.arange(LANES)
    d = lane % HEAD_DIM
    j = d % (ROPE_AXIS_DIM // 2)
    inv = jnp.asarray(ROPE_BASE, F32) ** (-jnp.asarray(2 * j, F32) / ROPE_AXIS_DIM)
    use_col = jnp.asarray(d >= ROPE_AXIS_DIM)
    posv = jnp.where(use_col[None, :], jnp.asarray(col, F32)[:, None], jnp.asarray(row, F32)[:, None])
    ang = posv * inv[None, :]
    upper = jnp.asarray((d % ROPE_AXIS_DIM) >= ROPE_AXIS_DIM // 2)[None, :]
    cos, sin = jnp.cos(ang), jnp.sin(ang)
    sin_hi = jnp.where(upper, sin, 0.0)
    sin_lo = jnp.where(upper, 0.0, -sin)
    pad = lambda t, v: jnp.concatenate([t, jnp.full((n_pad, LANES), v, F32)], axis=0)
    return pad(cos, 1.0), pad(sin_hi, 0.0), pad(sin_lo, 0.0)


def _inproj(tok, x_lat, x_ctx, mod_l, pre_g, w_in, tables, tm):
    per_batch = tok.seq // tm
    n_lat_tiles = tok.n_lat // tm
    tab_map = lambda i: (jnp.where(i < n_lat_tiles, i % per_batch, per_batch), 0)
    tab_spec = pl.BlockSpec((tm, LANES), tab_map)
    n = tok.n_all
    kern = functools.partial(_inproj_kernel, n_lat_tiles=n_lat_tiles)
    return pl.pallas_call(
        kern,
        out_shape=(jax.ShapeDtypeStruct((n, FOURIER_WIDTH), BF16), jax.ShapeDtypeStruct((n, ATTN_WIDTH), BF16),
                   jax.ShapeDtypeStruct((n, KV_WIDTH), BF16), jax.ShapeDtypeStruct((n, KV_WIDTH), BF16)),
        grid=(n // tm,),
        in_specs=[*tok.split_specs(tm, D_MODEL, x_lat is x_ctx), tok.mod_spec(tm), _vec_spec(D_MODEL),
                  _resident((D_MODEL, IN_WIDTH)), tab_spec, tab_spec, tab_spec],
        out_specs=(_row_spec(tm, FOURIER_WIDTH), _row_spec(tm, ATTN_WIDTH), _row_spec(tm, KV_WIDTH),
                   _row_spec(tm, KV_WIDTH)),
        compiler_params=_cparams(1),
        name="inproj",
    )(x_lat, x_ctx, mod_l, pre_g, w_in, *tables)


def _head_pair_perm():
    cols = []
    for i in range(HEAD_GROUP):
        for h in (i, i + HEAD_GROUP):
            cols.extend(range(h * HEAD_DIM, (h + 1) * HEAD_DIM))
    return np.asarray(cols)


def _dft_tables(length):
    half = length // 2
    k = np.arange(half)[:, None]
    m = np.arange(half)[None, :]

    def tab(n):
        ang = 2.0 * np.pi * ((k * n) % length) / length
        t = np.concatenate([np.cos(ang), -np.sin(ang)], axis=1) / math.sqrt(length)
        return jnp.asarray(t.astype(np.float32)).astype(BF16)

    return tab(2 * m), tab(2 * m + 1)


def _chan_table():
    n = FOURIER_GROUP_DIM
    k = np.arange(n)
    ang = 2.0 * np.pi * ((k[:, None] * k[None, :]) % n) / n
    t = np.concatenate([np.cos(ang), np.sin(ang)], axis=1) / math.sqrt(n)
    return jnp.asarray(t.astype(np.float32)).astype(BF16)


def _fourier_one(f_ref, chan_ref, pos_e_ref, pos_o_ref, o_ref, stk_ref):
    gd = FOURIER_GROUP_DIM
    half = f_ref.shape[0]
    row_chunk = min(half, 512)
    for parity in range(2):
        for g in range(FOURIER_GROUPS):
            lanes = slice(parity * FOURIER_WIDTH + g * gd, parity * FOURIER_WIDTH + (g + 1) * gd)
            z = jnp.dot(f_ref[:, lanes], chan_ref[...], preferred_element_type=F32)
            stk_ref[parity, 0:half, g * gd:(g + 1) * gd] = z[:, :gd].astype(BF16)
            stk_ref[parity, half:2 * half, g * gd:(g + 1) * gd] = z[:, gd:].astype(BF16)
    for r in range(half // row_chunk):
        rows = slice(r * row_chunk, (r + 1) * row_chunk)
        even = jnp.dot(pos_e_ref[rows, :], stk_ref[0], preferred_element_type=F32)
        odd = jnp.dot(pos_o_ref[rows, :], stk_ref[1], preferred_element_type=F32)
        o_ref[rows, :] = (even + odd).astype(BF16)
        o_ref[half + r * row_chunk:half + (r + 1) * row_chunk, :] = (even - odd).astype(BF16)


def _fourier_kernel(fl_ref, fc_ref, chan_ref, ple_ref, plo_ref, pce_ref, pco_ref, ol_ref, oc_ref, stkl_ref, stkc_ref):
    _fourier_one(fl_ref, chan_ref, ple_ref, plo_ref, ol_ref, stkl_ref)
    _fourier_one(fc_ref, chan_ref, pce_ref, pco_ref, oc_ref, stkc_ref)


def _fourier(tok, f_all, chan, tabs_lat, tabs_ctx):
    ctx0 = tok.n_lat // tok.n_ctx
    f_pairs = f_all.reshape(tok.n_all // 2, 2 * FOURIER_WIDTH)
    pair_blk = lambda rows, m: pl.BlockSpec((rows // 2, 2 * FOURIER_WIDTH), m)
    lat_blk = pl.BlockSpec((tok.seq, FOURIER_WIDTH), lambda b: (b, 0))
    ctx_blk = pl.BlockSpec((tok.n_ctx, FOURIER_WIDTH), lambda b: (b, 0))
    tabs = (*tabs_lat, *tabs_ctx)
    return pl.pallas_call(
        _fourier_kernel,
        out_shape=(jax.ShapeDtypeStruct((tok.n_lat, FOURIER_WIDTH), BF16),
                   jax.ShapeDtypeStruct((tok.batch * tok.n_ctx, FOURIER_WIDTH), BF16)),
        grid=(tok.batch,),
        in_specs=[pair_blk(tok.seq, lambda b: (b, 0)), pair_blk(tok.n_ctx, lambda b: (ctx0 + b, 0)),
                  _resident(chan.shape), *[_resident(t.shape) for t in tabs]],
        out_specs=(lat_blk, ctx_blk),
        scratch_shapes=[pltpu.VMEM((2, tok.seq, FOURIER_WIDTH), BF16),
                        pltpu.VMEM((2, tok.n_ctx, FOURIER_WIDTH), BF16)],
        compiler_params=_cparams(1),
        name="fourier",
    )(f_pairs, f_pairs, chan, *tabs)


def _attn_kernel(sink_ref, q_ref, kp_ref, kc_ref, kn_ref, kx_ref, vp_ref, vc_ref, vn_ref, vx_ref, o_ref,
                 *, n_qblk):
    j = pl.program_id(1)
    blk = WINDOW
    n_loc = 3 * blk
    rows = HEAD_GROUP * blk
    q = jnp.concatenate([q_ref[:, i * LANES:(i + 1) * LANES] for i in range(HEAD_GROUP)], axis=0)
    k_loc = jnp.concatenate([kp_ref[...], kc_ref[...], kn_ref[...]], axis=0)
    v_loc = jnp.concatenate([vp_ref[...], vc_ref[...], vn_ref[...]], axis=0)
    k_ctx, v_ctx = kx_ref[...], vx_ref[...]
    is_lat = j < n_qblk
    col_lo = jnp.where(is_lat, jnp.where(j >= 1, 0, blk), n_loc)
    col_hi = jnp.where(is_lat, jnp.where(j + 1 < n_qblk, n_loc, 2 * blk), 0)
    qi = lax.broadcasted_iota(jnp.int32, (blk, 1), 0)
    kj = lax.broadcasted_iota(jnp.int32, (blk, n_loc), 1)
    visible = (kj >= jnp.maximum(qi, col_lo)) & (kj <= jnp.minimum(qi + 2 * WINDOW, col_hi - 1))
    bias = jnp.where(visible, 0.0, NEG_INF)
    bias = jnp.concatenate([bias] * HEAD_GROUP, axis=0)
    row_blk = lax.broadcasted_iota(jnp.int32, (rows, 1), 0) // blk
    lane_o = lax.broadcasted_iota(jnp.int32, (rows, LANES), 1)
    nt = (((1,), (1,)), ((), ()))
    outs = []
    for kvh in range(N_KV_HEADS):
        def own_lanes(t):
            lane = lax.broadcasted_iota(jnp.int32, t.shape, 1)
            return jnp.where((lane >= kvh * HEAD_DIM) & (lane < (kvh + 1) * HEAD_DIM), t, jnp.zeros_like(t))
        s_loc = lax.dot_general(q, own_lanes(k_loc), nt, preferred_element_type=F32) + bias
        s_ctx = lax.dot_general(q, own_lanes(k_ctx), nt, preferred_element_type=F32)
        sink = jnp.zeros((rows, 1), F32)
        for i in range(HEAD_GROUP):
            sink = jnp.where(row_blk == i, sink_ref[kvh * HEAD_GROUP + i] * LOG2_E, sink)
        m = jnp.maximum(jnp.maximum(jnp.max(s_loc, axis=-1, keepdims=True), jnp.max(s_ctx, axis=-1, keepdims=True)),
                        sink)
        p_loc = jnp.exp2(s_loc - m)
        p_ctx = jnp.exp2(s_ctx - m)
        denom = (jnp.sum(p_loc, axis=-1, keepdims=True) + jnp.sum(p_ctx, axis=-1, keepdims=True)
                 + jnp.exp2(sink - m))
        o = (jnp.dot(p_loc.astype(BF16), v_loc, preferred_element_type=F32)
             + jnp.dot(p_ctx.astype(BF16), v_ctx, preferred_element_type=F32))
        outs.append(o / denom)
    merged = jnp.where(lane_o < HEAD_DIM, outs[0], outs[1])
    for i in range(HEAD_GROUP):
        o_ref[:, i * LANES:(i + 1) * LANES] = merged[i * blk:(i + 1) * blk, :].astype(BF16)


def _attention(tok, q, k, v, sink):
    blk = WINDOW
    n_qblk = tok.seq // blk
    n_cblk = tok.n_ctx // blk
    lat_blocks = tok.n_lat // blk
    ctx0 = tok.n_lat // tok.n_ctx

    def q_map(b, j):
        return (jnp.where(j < n_qblk, b * n_qblk + j, lat_blocks + b * n_cblk + (j - n_qblk)), 0)

    def k_map(off):
        return lambda b, j: (b * n_qblk + jnp.clip(j + off, 0, n_qblk - 1), 0)

    ctx_map = lambda b, j: (ctx0 + b, 0)
    kv_blk = lambda m: pl.BlockSpec((blk, KV_WIDTH), m)
    ctx_blk = pl.BlockSpec((tok.n_ctx, KV_WIDTH), ctx_map)
    kern = functools.partial(_attn_kernel, n_qblk=n_qblk)
    return pl.pallas_call(
        kern,
        out_shape=jax.ShapeDtypeStruct((tok.n_all, ATTN_WIDTH), BF16),
        grid=(tok.batch, n_qblk + n_cblk),
        in_specs=[pl.BlockSpec(memory_space=pltpu.SMEM),
                  pl.BlockSpec((blk, ATTN_WIDTH), q_map),
                  kv_blk(k_map(-1)), kv_blk(k_map(0)), kv_blk(k_map(1)), ctx_blk,
                  kv_blk(k_map(-1)), kv_blk(k_map(0)), kv_blk(k_map(1)), ctx_blk],
        out_specs=pl.BlockSpec((blk, ATTN_WIDTH), q_map),
        compiler_params=_cparams(2),
        name="window_attention",
    )(sink, q, k, k, k, k, v, v, v, v)


def _piece_perm():
    idx = np.arange(D_MODEL)
    a, b, c = idx // LANES, (idx // SSM_GROUP_DIM) % SSM_PIECES, idx % SSM_GROUP_DIM
    perm = np.zeros((D_MODEL, D_MODEL), np.float32)
    perm[idx, b * LANES + a * SSM_GROUP_DIM + c] = 1.0
    return jnp.asarray(perm).astype(BF16)


def _slab_pitch(blk):
    return blk + SUBLANES


def _s5_pre_kernel(x_ref, mod_ref, pre_ref, perm_ref, u_ref, hs_ref, *, blk, batch, n_blocks):
    p, b = pl.program_id(0), pl.program_id(1)
    pitch = _slab_pitch(blk)

    @pl.when(p < n_blocks)
    def _():
        h = _norm_mod(x_ref[...], pre_ref[...], mod_ref[0, 1:2, :], mod_ref[0, 0:1, :])
        row0 = pl.multiple_of(b * pitch, SUBLANES)
        slot0 = (p % 2) * SSM_SLABS
        for s in range(SSM_SLABS):
            hs_ref[slot0 + s, pl.ds(row0, blk), :] = h[:, s * LANES:(s + 1) * LANES]

    @pl.when(p >= 1)
    def _():
        half = SSM_CHUNK // 2
        cpb = blk // SSM_CHUNK
        rows_h = cpb * batch
        s = b
        src = ((p - 1) % 2) * SSM_SLABS + s
        lhs = []
        for hh in range(2):
            for i in range(cpb):
                t0 = i * SSM_CHUNK + hh * half
                lhs.append(jnp.concatenate(
                    [hs_ref[src, pl.ds(t0 + j, batch, stride=pitch), :] for j in range(half)], axis=1))
        lhs = jnp.concatenate(lhs, axis=0).astype(BF16)
        out = jnp.dot(lhs, perm_ref[...], preferred_element_type=F32).astype(BF16)
        for hh in range(2):
            for g in range(SSM_PIECES):
                u_ref[s * SSM_PIECES + g, :, hh * LANES:(hh + 1) * LANES] = (
                    out[hh * rows_h:(hh + 1) * rows_h, g * LANES:(g + 1) * LANES])


def _s5_pre(tok, xa, mod_l, pre_g, perm, blk):
    assert tok.batch == SSM_SLABS
    (n_blocks, _), row_map, mod_map = tok.pos_grid(blk, True)
    n_lat_blk = tok.seq // blk
    n_ctx_blk = tok.n_ctx // blk
    cpb = blk // SSM_CHUNK
    n_chunks = (tok.seq + tok.n_ctx) // SSM_CHUNK
    last = n_blocks - 1

    def u_map(p, b):
        q = jnp.maximum(p - 1, 0)
        return (0, jnp.where(q < n_lat_blk, n_ctx_blk + q, q - n_lat_blk), 0)

    clamp = lambda m: (lambda p, b: m(jnp.minimum(p, last), b))
    kern = functools.partial(_s5_pre_kernel, blk=blk, batch=tok.batch, n_blocks=n_blocks)
    return pl.pallas_call(
        kern,
        out_shape=jax.ShapeDtypeStruct((SSM_GROUPS, n_chunks * tok.batch, SSM_CW), BF16),
        grid=(n_blocks + 1, tok.batch),
        in_specs=[pl.BlockSpec((blk, D_MODEL), clamp(row_map)), pl.BlockSpec((1, N_MOD, D_MODEL), clamp(mod_map)),
                  _vec_spec(D_MODEL), _resident((D_MODEL, D_MODEL))],
        out_specs=pl.BlockSpec((SSM_GROUPS, cpb * tok.batch, SSM_CW), u_map),
        scratch_shapes=[pltpu.VMEM((2 * SSM_SLABS, tok.batch * _slab_pitch(blk), LANES), F32)],
        compiler_params=_cparams(2),
        name="s5_pre",
    )(xa, mod_l, pre_g, perm)


def _cmul(ar, ai, br, bi):
    return ar * br - ai * bi, ar * bi + ai * br


def _s5_factors(a_re, a_im, log_dt, b_re, b_im, c_re, c_im):
    t_n, gd = SSM_CHUNK, SSM_GROUP_DIM
    dt = jnp.exp(log_dt.astype(F32))[..., None]
    l_re, l_im = a_re.astype(F32), a_im.astype(F32)
    z_re, z_im = l_re * dt, l_im * dt

    n_all = jnp.arange(1 - t_n, t_n + 1, dtype=F32)[None, None, :, None]
    mag = jnp.exp(z_re[:, :, None, :] * n_all)
    ang = z_im[:, :, None, :] * n_all
    pow_re, pow_im = mag * jnp.cos(ang), mag * jnp.sin(ang)

    def apow(d, n):
        n = [int(v) for v in n]
        step = n[1] - n[0] if len(n) > 1 else 1
        assert step in (1, -1) and all(b - a == step for a, b in zip(n, n[1:]))
        lo, hi = min(n) + t_n - 1, max(n) + t_n
        re, im = pow_re[d, :, lo:hi], pow_im[d, :, lo:hi]
        return (re, im) if step == 1 else (re[:, ::-1], im[:, ::-1])

    bb, cc = [], []
    for d in range(2):
        a1_re, a1_im = apow(d, [1])
        num_re, num_im = a1_re[:, 0] - 1.0, a1_im[:, 0]
        den = l_re[d] * l_re[d] + l_im[d] * l_im[d]
        r_re = (num_re * l_re[d] + num_im * l_im[d]) / den
        r_im = (num_im * l_re[d] - num_re * l_im[d]) / den
        bb_re, bb_im = _cmul(r_re[..., None], r_im[..., None], b_re[d], b_im[d])
        bb.append((bb_re.transpose(0, 2, 1), bb_im.transpose(0, 2, 1)))
        cc.append((c_re[d].astype(F32), c_im[d].astype(F32)))

    t = np.arange(t_n, dtype=np.float64)

    def table(blocks):
        pr, pi, wa, wb = [], [], [], []
        for d, power, (w_re, w_im), part in blocks:
            p_re, p_im = apow(d, power)
            pr.append(p_re)
            pi.append(p_im)
            wa.append({'re': w_re, 'im': w_im, '-im': -w_im}[part])
            wb.append({'re': -w_im, 'im': w_re, '-im': -w_re}[part])
        pr, pi, wa, wb = (jnp.concatenate(x, axis=-1) for x in (pr, pi, wa, wb))
        out = pr[:, :, None, :] * wa[:, None, :, :] + pi[:, :, None, :] * wb[:, None, :, :]
        return out.reshape(out.shape[0], t_n * gd, out.shape[-1])

    fwd_inc, bwd_inc = t_n - 1 - t, t
    q = table([(0, fwd_inc, bb[0], 're'), (1, bwd_inc, bb[1], 're'),
               (0, fwd_inc, bb[0], 'im'), (1, bwd_inc, bb[1], 'im')]).astype(BF16)
    e = table([(0, -t, bb[0], 're'), (0, -t, bb[0], 'im'), (1, t, bb[1], 're'), (1, t, bb[1], 'im')])
    ft = table([(0, t, cc[0], 're'), (0, t, cc[0], '-im'), (1, -t, cc[1], 're'), (1, -t, cc[1], '-im')])
    fwd_out, bwd_out = t + 1, t_n - t
    pt = table([(0, fwd_out, cc[0], 're'), (1, bwd_out, cc[1], 're'),
                (0, fwd_out, cc[0], '-im'), (1, bwd_out, cc[1], '-im')]).astype(BF16)

    (ar_f, ai_f), (ar_b, ai_b) = apow(0, [t_n]), apow(1, [t_n])
    dec = jnp.concatenate([ar_f, ar_b, ai_f, ai_b], axis=-1)
    return e, ft, q, pt, dec


def _s5_kernel(u_ref, e_ref, ft_ref, q_ref, pt_ref, dec_ref, y_ref, v_ref, xin_ref,
               *, batch, n_chunks, n_ctx_chunks):
    gps = SSM_GROUPS_PER_STEP
    ns = SSM_STATE
    ns2 = 2 * ns
    cw = SSM_CW
    nt = (((1,), (1,)), ((), ()))
    for g in range(gps):
        v_ref[g] = jnp.dot(u_ref[g], q_ref[g], preferred_element_type=F32)
    a_re = jnp.broadcast_to(dec_ref[:, :, 0:ns2], (gps, batch, ns2))
    a_im = jnp.broadcast_to(dec_ref[:, :, ns2:2 * ns2], (gps, batch, ns2))
    is_fwd = lax.broadcasted_iota(jnp.int32, (gps, batch, ns2), 2) < ns

    def step(k, carry):
        x_re, x_im = carry
        kb = jnp.where(k < n_ctx_chunks, n_ctx_chunks - 1 - k, n_chunks + n_ctx_chunks - 1 - k)
        rf = pl.multiple_of(k * batch, batch)
        rb = pl.multiple_of(kb * batch, batch)
        xin_ref[:, pl.ds(rf, batch), 0:ns] = x_re[:, :, 0:ns]
        xin_ref[:, pl.ds(rb, batch), ns:ns2] = x_re[:, :, ns:ns2]
        xin_ref[:, pl.ds(rf, batch), ns2:ns2 + ns] = x_im[:, :, 0:ns]
        xin_ref[:, pl.ds(rb, batch), ns2 + ns:2 * ns2] = x_im[:, :, ns:ns2]
        v_re = jnp.where(is_fwd, v_ref[:, pl.ds(rf, batch), 0:ns2], v_ref[:, pl.ds(rb, batch), 0:ns2])
        v_im = jnp.where(is_fwd, v_ref[:, pl.ds(rf, batch), ns2:2 * ns2], v_ref[:, pl.ds(rb, batch), ns2:2 * ns2])
        return a_re * x_re - a_im * x_im + v_re, a_re * x_im + a_im * x_re + v_im

    zero = jnp.zeros((gps, batch, ns2), F32)
    lax.fori_loop(0, n_chunks, step, (zero, zero))
    t_in = lax.broadcasted_iota(jnp.int32, (cw, cw), 0) // SSM_GROUP_DIM
    t_out = lax.broadcasted_iota(jnp.int32, (cw, cw), 1) // SSM_GROUP_DIM

    def split(x):
        hi = x.astype(BF16)
        return hi, (x - hi.astype(F32)).astype(BF16)

    def lag_kernel(g, lanes):
        (eh, el), (fh, fl) = split(e_ref[g, :, lanes]), split(ft_ref[g, :, lanes])
        return (lax.dot_general(eh, fh, nt, preferred_element_type=F32)
                + lax.dot_general(eh, fl, nt, preferred_element_type=F32)
                + lax.dot_general(el, fh, nt, preferred_element_type=F32))

    for g in range(gps):
        m = (jnp.where(t_out >= t_in, lag_kernel(g, slice(0, ns2)), 0.0)
             + jnp.where(t_in >= t_out, lag_kernel(g, slice(ns2, 2 * ns2)), 0.0)).astype(BF16)
        y_ref[g] = (jnp.dot(u_ref[g], m, preferred_element_type=F32)
                    + lax.dot_general(xin_ref[g].astype(BF16), pt_ref[g], nt, preferred_element_type=F32)
                    ).astype(BF16)


def _s5(u_t, factors, batch, n_chunks, n_ctx_chunks):
    gps = SSM_GROUPS_PER_STEP
    rows = n_chunks * batch
    kern = functools.partial(_s5_kernel, batch=batch, n_chunks=n_chunks, n_ctx_chunks=n_ctx_chunks)
    gspec = lambda r, c: pl.BlockSpec((gps, r, c), lambda i: (i, 0, 0))
    sq = gspec(SSM_CW, SSM_CW)
    return pl.pallas_call(
        kern,
        out_shape=jax.ShapeDtypeStruct((SSM_GROUPS, rows, SSM_CW), BF16),
        grid=(SSM_GROUPS // gps,),
        in_specs=[gspec(rows, SSM_CW), sq, sq, sq, sq, gspec(1, SSM_CW)],
        out_specs=gspec(rows, SSM_CW),
        scratch_shapes=[pltpu.VMEM((gps, rows, SSM_CW), F32), pltpu.VMEM((gps, rows, SSM_CW), F32)],
        compiler_params=_cparams(1),
        name="s5_scan",
    )(u_t, *factors)


def _s5_unpack_kernel(y_ref, perm_ref, o_ref, ys_ref, tmp_ref, *, blk, batch):
    b = pl.program_id(1)
    pitch = _slab_pitch(blk)

    @pl.when(b == 0)
    def _():
        half = SSM_CHUNK // 2
        cpb = blk // SSM_CHUNK
        rows_h = cpb * batch

        def slab(s, carry):
            lhs = jnp.concatenate(
                [jnp.concatenate([y_ref[s * SSM_PIECES + g, :, hh * LANES:(hh + 1) * LANES]
                                  for g in range(SSM_PIECES)], axis=1) for hh in range(2)], axis=0)
            tmp_ref[...] = jnp.dot(lhs, perm_ref[...], preferred_element_type=F32)
            for hh in range(2):
                for i in range(cpb):
                    r0 = hh * rows_h + i * batch
                    t0 = i * SSM_CHUNK + hh * half
                    for j in range(half):
                        ys_ref[s, pl.ds(t0 + j, batch, stride=pitch), :] = (
                            tmp_ref[r0:r0 + batch, j * LANES:(j + 1) * LANES])
            return carry

        lax.fori_loop(0, SSM_SLABS, slab, 0)

    row0 = pl.multiple_of(b * pitch, SUBLANES)
    for s in range(SSM_SLABS):
        o_ref[:, s * LANES:(s + 1) * LANES] = ys_ref[s, pl.ds(row0, blk), :].astype(BF16)


def _s5_unpack(tok, y_t, perm, blk, with_ctx):
    grid, row_map, _ = tok.pos_grid(blk, with_ctx)
    n_lat_blk = tok.seq // blk
    n_ctx_blk = tok.n_ctx // blk
    cpb = blk // SSM_CHUNK
    y_map = lambda p, b: (0, jnp.where(p < n_lat_blk, n_ctx_blk + p, p - n_lat_blk), 0)
    n_rows = tok.n_all if with_ctx else tok.n_lat
    kern = functools.partial(_s5_unpack_kernel, blk=blk, batch=tok.batch)
    return pl.pallas_call(
        kern,
        out_shape=jax.ShapeDtypeStruct((n_rows, D_MODEL), BF16),
        grid=grid,
        in_specs=[pl.BlockSpec((SSM_GROUPS, cpb * tok.batch, SSM_CW), y_map), _resident((D_MODEL, D_MODEL))],
        out_specs=pl.BlockSpec((blk, D_MODEL), row_map),
        scratch_shapes=[pltpu.VMEM((SSM_SLABS, tok.batch * _slab_pitch(blk), LANES), F32),
                        pltpu.VMEM((2 * cpb * tok.batch, D_MODEL), F32)],
        compiler_params=_cparams(2),
        name="s5_unpack",
    )(y_t, perm)


def _glu_ffn_kernel(x_ref, y_ref, d_ref, mod_ref, mpre_ref, mpost_ref, pre_ref, post_ref, wg_ref, w1_ref, w2_ref,
                    o_ref, acc_ref):
    x = x_ref[...]
    h = _norm_mod(x, mpre_ref[...], mod_ref[0, 1:2, :], mod_ref[0, 0:1, :])
    y = y_ref[...].astype(F32) + d_ref[...] * h
    z = jnp.dot(jax.nn.gelu(y).astype(BF16), wg_ref[...], preferred_element_type=F32)
    out = z[:, :D_MODEL] * jax.nn.sigmoid(z[:, D_MODEL:])
    x = x + mod_ref[0, 2:3, :] * _rms(out, mpost_ref[...])
    _ffn_body(x, mod_ref, pre_ref, post_ref, w1_ref, w2_ref, o_ref, acc_ref)


def _glu_ffn(tok, xa, n_rows, y_tok, d_skip, mod_l, mix_pre_g, mix_post_g, pre_g, post_g, glu_all, w1_all, w2_all,
             i, layer, tm):
    return pl.pallas_call(
        _glu_ffn_kernel,
        out_shape=jax.ShapeDtypeStruct((n_rows, D_MODEL), F32),
        grid=(n_rows // tm,),
        in_specs=[_row_spec(tm, D_MODEL), _row_spec(tm, D_MODEL), _vec_spec(D_MODEL), tok.mod_spec(tm),
                  _vec_spec(D_MODEL), _vec_spec(D_MODEL), _vec_spec(D_MODEL), _vec_spec(D_MODEL),
                  _layer_resident(i, (D_MODEL, 2 * D_MODEL)),
                  _layer_resident(layer, (D_MODEL, D_FF)), _layer_resident(layer, (D_FF, D_MODEL))],
        out_specs=_row_spec(tm, D_MODEL),
        scratch_shapes=[pltpu.VMEM((tm, D_MODEL), F32)],
        compiler_params=_cparams(1),
        name="glu_ffn",
    )(xa, y_tok, d_skip, mod_l, mix_pre_g, mix_post_g, pre_g, post_g, glu_all, w1_all, w2_all)


def _tile(limit, *sizes):
    tm = limit
    while any(s % tm for s in sizes):
        tm //= 2
    return tm


def kernel(x, c, ctx, c_ctx, mod_w, mod_b, mix_pre_g, mix_post_g, ffn_pre_g, ffn_post_g, ffn_w1, ffn_w2,
           even_w_in, even_w_out, even_sink, ssm_a_re, ssm_a_im, ssm_log_dt, ssm_b_re, ssm_b_im, ssm_c_re,
           ssm_c_im, ssm_d, ssm_glu_w):
    batch, seq, _ = x.shape
    n_ctx = ctx.shape[1]
    tok = _Tokens(batch, seq, n_ctx)
    assert seq % WINDOW == 0 and n_ctx % WINDOW == 0 and tok.n_lat % n_ctx == 0
    assert batch == SUBLANES
    tm = _tile(256, seq, n_ctx)
    tm_wide = _tile(512, seq, batch * n_ctx)

    x_lat, x_ctx = x.reshape(tok.n_lat, D_MODEL), ctx.astype(x.dtype).reshape(-1, D_MODEL)

    n_cond = 2 * SUBLANES
    cond = jnp.zeros((n_cond, D_MODEL), F32).at[:batch].set(c).at[batch].set(c_ctx)
    mod = _modulation(cond, mod_w, mod_b).reshape(DEPTH, n_cond, N_MOD, D_MODEL)

    rope = _rope_tables(seq, tm_wide)
    head_perm = _head_pair_perm()
    piece_perm = _piece_perm()
    dft_chan, dft_lat, dft_ctx = _chan_table(), _dft_tables(seq), _dft_tables(n_ctx)
    vec = lambda g: g.reshape(1, D_MODEL)
    w1_all, w2_all, glu_all = ffn_w1.astype(BF16), ffn_w2.astype(BF16), ssm_glu_w.astype(BF16)

    for layer in range(DEPTH):
        need_ctx = layer < DEPTH - 1
        n_rows = tok.n_all if need_ctx else tok.n_lat
        mod_l = mod[layer]
        i = layer // 2
        ffn_pre, ffn_post = vec(ffn_pre_g[layer]), vec(ffn_post_g[layer])
        if layer % 2 == 0:
            w_in = even_w_in[i]
            q0 = FOURIER_WIDTH
            w_in = jnp.concatenate([w_in[:, :q0], w_in[:, q0:q0 + ATTN_WIDTH][:, head_perm],
                                    w_in[:, q0 + ATTN_WIDTH:]], axis=1).astype(BF16)
            f, q, k, v = _inproj(tok, x_lat, x_ctx, mod_l, vec(mix_pre_g[layer]), w_in, rope, tm_wide)
            fm_lat, fm_ctx = _fourier(tok, f, dft_chan, dft_lat, dft_ctx)
            ao = _attention(tok, q, k, v, even_sink[i])
            w_out = even_w_out[i]
            w_f = w_out[:FOURIER_WIDTH].astype(BF16)
            w_a = w_out[FOURIER_WIDTH:][head_perm].astype(BF16)
            xa = _mix_ffn(tok, x_lat, x_ctx, fm_lat, fm_ctx, ao, mod_l, vec(mix_post_g[layer]), ffn_pre, ffn_post,
                          w_f, w_a, w1_all, w2_all, layer, tm_wide)
        else:
            pre_g = vec(mix_pre_g[layer])
            u_t = _s5_pre(tok, xa, mod_l, pre_g, piece_perm, tm)
            factors = _s5_factors(ssm_a_re[i], ssm_a_im[i], ssm_log_dt[i], ssm_b_re[i], ssm_b_im[i],
                                  ssm_c_re[i], ssm_c_im[i])
            y_t = _s5(u_t, factors, batch, (seq + n_ctx) // SSM_CHUNK, n_ctx // SSM_CHUNK)
            y_tok = _s5_unpack(tok, y_t, piece_perm, tm, need_ctx)
            xa = _glu_ffn(tok, xa, n_rows, y_tok, vec(ssm_d[i]), mod_l, pre_g, vec(mix_post_g[layer]),
                          ffn_pre, ffn_post, glu_all, w1_all, w2_all, i, layer, tm_wide)
        x_lat = x_ctx = xa
    return xa[:tok.n_lat].reshape(batch, seq, D_MODEL)
```

```python
import functools
import math

import numpy as np
import jax
import jax.numpy as jnp
from jax import lax
from jax.experimental import pallas as pl
from jax.experimental.pallas import tpu as pltpu

D_MODEL = 1024
DEPTH = 4
N_MOD = 6
EPS = 1e-6
NEG_INF = -1e30
GRID_W = 64

FOURIER_GROUPS = 4
FOURIER_GROUP_DIM = 128
FOURIER_WIDTH = FOURIER_GROUPS * FOURIER_GROUP_DIM

N_HEADS = 8
N_KV_HEADS = 2
HEAD_GROUP = N_HEADS // N_KV_HEADS
HEAD_DIM = 64
ATTN_WIDTH = N_HEADS * HEAD_DIM
KV_WIDTH = N_KV_HEADS * HEAD_DIM
WINDOW = 128
ROPE_AXIS_DIM = HEAD_DIM // 2
ROPE_BASE = 10000.0
LOG2_E = math.log2(math.e)
IN_WIDTH = FOURIER_WIDTH + ATTN_WIDTH + 2 * KV_WIDTH

LANES = 128
SUBLANES = 8
VMEM_LIMIT = 56 * 1024 * 1024

SSM_GROUP_DIM = 16
SSM_GROUPS = D_MODEL // SSM_GROUP_DIM
SSM_STATE = 64
SSM_CHUNK = 16
SSM_CW = SSM_CHUNK * SSM_GROUP_DIM
SSM_GROUPS_PER_STEP = 4
SSM_SLABS = D_MODEL // LANES
SSM_PIECES = LANES // SSM_GROUP_DIM

D_FF = 4 * D_MODEL

F32 = jnp.float32
BF16 = jnp.bfloat16


def _cparams(n_axes):
    return pltpu.CompilerParams(dimension_semantics=("arbitrary",) * n_axes, vmem_limit_bytes=VMEM_LIMIT)


def _resident(shape):
    nd = len(shape)
    return pl.BlockSpec(shape, lambda *_: (0,) * nd, pipeline_mode=pl.Buffered(1))


def _rms(x, g):
    return x * lax.rsqrt(jnp.mean(x * x, axis=-1, keepdims=True) + EPS) * g


def _norm_mod(x, g, sc, sh):
    return _rms(x, g) * (1.0 + sc) + sh


def _mod_kernel(cond_ref, w_ref, b_ref, o_ref):
    cond = cond_ref[...]
    s = cond * jax.nn.sigmoid(cond)
    s_hi = s.astype(BF16)
    s_lo = (s - s_hi.astype(F32)).astype(BF16)
    w = w_ref[0].astype(BF16)
    o_ref[0] = (jnp.dot(s_hi, w, preferred_element_type=F32) + jnp.dot(s_lo, w, preferred_element_type=F32)
                + b_ref[0])


def _modulation(cond, mod_w, mod_b):
    rows = cond.shape[0]
    tn = 1024
    n = N_MOD * D_MODEL
    return pl.pallas_call(
        _mod_kernel,
        out_shape=jax.ShapeDtypeStruct((DEPTH, rows, n), F32),
        grid=(DEPTH, n // tn),
        in_specs=[pl.BlockSpec((rows, D_MODEL), lambda l, j: (0, 0)),
                  pl.BlockSpec((1, D_MODEL, tn), lambda l, j: (l, 0, j)),
                  pl.BlockSpec((1, 1, tn), lambda l, j: (l, 0, j))],
        out_specs=pl.BlockSpec((1, rows, tn), lambda l, j: (l, 0, j)),
        compiler_params=_cparams(2),
        name="modulation",
    )(cond, mod_w, mod_b.reshape(DEPTH, 1, n))


class _Tokens:
    def __init__(self, batch, seq, n_ctx):
        self.batch, self.seq, self.n_ctx = batch, seq, n_ctx
        self.n_lat = batch * seq
        self.n_all = self.n_lat + batch * n_ctx

    def mod_spec(self, tm):
        per_batch = self.seq // tm
        return pl.BlockSpec((1, N_MOD, D_MODEL), lambda i: (jnp.minimum(i // per_batch, self.batch), 0, 0))

    def split_specs(self, tm, width, joined):
        nlt = self.n_lat // tm
        ctx_map = (lambda i: (jnp.maximum(i, nlt), 0)) if joined else (lambda i: (jnp.maximum(i - nlt, 0), 0))
        return pl.BlockSpec((tm, width), lambda i: (jnp.minimum(i, nlt - 1), 0)), pl.BlockSpec((tm, width), ctx_map)

    def pos_grid(self, blk, with_ctx):
        n_lat_blk = self.seq // blk
        n_ctx_blk = self.n_ctx // blk
        lat_blocks = self.n_lat // blk
        row_map = lambda p, b: (jnp.where(p < n_lat_blk, b * n_lat_blk + p, lat_blocks + b * n_ctx_blk + (p - n_lat_blk)), 0)
        mod_map = lambda p, b: (jnp.where(p < n_lat_blk, b, self.batch), 0, 0)
        grid = (n_lat_blk + (n_ctx_blk if with_ctx else 0), self.batch)
        return grid, row_map, mod_map


def _pick(n_lat_tiles, lat_ref, ctx_ref):
    return jnp.where(pl.program_id(0) < n_lat_tiles, lat_ref[...], ctx_ref[...])


def _layer_resident(layer, shape):
    nd = len(shape)
    return pl.BlockSpec((None,) + tuple(shape), lambda *_: (layer,) + (0,) * nd, pipeline_mode=pl.Buffered(1))


def _row_spec(tm, width):
    return pl.BlockSpec((tm, width), lambda i: (i, 0))


def _vec_spec(width):
    return pl.BlockSpec((1, width), lambda *_: (0, 0))


FFN_CHUNK = 512


def _ffn_body(x, mod_ref, pre_ref, post_ref, w1_ref, w2_ref, o_ref, acc_ref):
    h = _norm_mod(x, pre_ref[...], mod_ref[0, 4:5, :], mod_ref[0, 3:4, :]).astype(BF16)
    for c in range(D_FF // FFN_CHUNK):
        sl = slice(c * FFN_CHUNK, (c + 1) * FFN_CHUNK)
        a = jnp.maximum(jnp.dot(h, w1_ref[:, sl], preferred_element_type=F32), 0.0)
        part = jnp.dot((a * a).astype(BF16), w2_ref[sl, :], preferred_element_type=F32)
        if c == 0:
            acc_ref[...] = part
        else:
            acc_ref[...] += part
    o_ref[...] = x + mod_ref[0, 5:6, :] * _rms(acc_ref[...], post_ref[...])


def _mix_ffn_kernel(xl_ref, xc_ref, fl_ref, fc_ref, ao_ref, mod_ref, mpost_ref, pre_ref, post_ref, wf_ref, wa_ref,
                    w1_ref, w2_ref, o_ref, acc_ref, *, n_lat_tiles):
    y = (jnp.dot(_pick(n_lat_tiles, fl_ref, fc_ref), wf_ref[...], preferred_element_type=F32)
         + jnp.dot(ao_ref[...], wa_ref[...], preferred_element_type=F32))
    x = _pick(n_lat_tiles, xl_ref, xc_ref) + mod_ref[0, 2:3, :] * _rms(y, mpost_ref[...])
    _ffn_body(x, mod_ref, pre_ref, post_ref, w1_ref, w2_ref, o_ref, acc_ref)


def _mix_ffn(tok, x_lat, x_ctx, fm_lat, fm_ctx, ao, mod_l, mix_post_g, pre_g, post_g, w_f, w_a, w1_all, w2_all,
             layer, tm):
    n = tok.n_all
    kern = functools.partial(_mix_ffn_kernel, n_lat_tiles=tok.n_lat // tm)
    return pl.pallas_call(
        kern,
        out_shape=jax.ShapeDtypeStruct((n, D_MODEL), F32),
        grid=(n // tm,),
        in_specs=[*tok.split_specs(tm, D_MODEL, x_lat is x_ctx), *tok.split_specs(tm, FOURIER_WIDTH, False),
                  _row_spec(tm, ATTN_WIDTH), tok.mod_spec(tm),
                  _vec_spec(D_MODEL), _vec_spec(D_MODEL), _vec_spec(D_MODEL),
                  _resident((FOURIER_WIDTH, D_MODEL)), _resident((ATTN_WIDTH, D_MODEL)),
                  _layer_resident(layer, (D_MODEL, D_FF)), _layer_resident(layer, (D_FF, D_MODEL))],
        out_specs=_row_spec(tm, D_MODEL),
        scratch_shapes=[pltpu.VMEM((tm, D_MODEL), F32)],
        compiler_params=_cparams(1),
        name="mix_ffn",
    )(x_lat, x_ctx, fm_lat, fm_ctx, ao, mod_l, mix_post_g, pre_g, post_g, w_f, w_a, w1_all, w2_all)


def _rope_block(x, cos, sin_hi, sin_lo):
    half = ROPE_AXIS_DIM // 2
    return (x * cos + pltpu.roll(x, half, axis=1) * sin_hi
            + pltpu.roll(x, LANES - half, axis=1) * sin_lo)


def _inproj_kernel(xl_ref, xc_ref, mod_ref, pre_ref, w_ref, cos_ref, shi_ref, slo_ref, f_ref, q_ref, k_ref, v_ref,
                   fs_ref, *, n_lat_tiles):
    x = _pick(n_lat_tiles, xl_ref, xc_ref)
    h = _norm_mod(x, pre_ref[...], mod_ref[0, 1:2, :], mod_ref[0, 0:1, :]).astype(BF16)
    p = jnp.dot(h, w_ref[...], preferred_element_type=F32)
    cos, shi, slo = cos_ref[...], shi_ref[...], slo_ref[...]
    pairs = x.shape[0] // 2
    n_slabs = FOURIER_WIDTH // LANES
    for s in range(n_slabs):
        fs_ref[s] = p[:, s * LANES:(s + 1) * LANES]
    for parity in range(2):
        for s in range(n_slabs):
            lo = parity * FOURIER_WIDTH + s * LANES
            f_ref[:, lo:lo + LANES] = fs_ref[s, pl.ds(parity, pairs, stride=2), :].astype(BF16)
    scale = HEAD_DIM ** -0.5 * LOG2_E
    for j in range(ATTN_WIDTH // LANES):
        lo = FOURIER_WIDTH + j * LANES
        q_ref[:, j * LANES:(j + 1) * LANES] = (_rope_block(p[:, lo:lo + LANES], cos, shi, slo) * scale).astype(BF16)
    k0 = FOURIER_WIDTH + ATTN_WIDTH
    k_ref[...] = _rope_block(p[:, k0:k0 + KV_WIDTH], cos, shi, slo).astype(BF16)
    v_ref[...] = p[:, k0 + KV_WIDTH:].astype(BF16)


def _rope_tables(seq, n_pad):
    pos = np.arange(seq)
    row = (pos // GRID_W).astype(np.float64)
    col = (pos % GRID_W).astype(np.float64)
    lane = np.arange(LANES)
    d = lane % HEAD_DIM
    j = d % (ROPE_AXIS_DIM // 2)
    inv = jnp.asarray(ROPE_BASE, F32) ** (-jnp.asarray(2 * j, F32) / ROPE_AXIS_DIM)
    use_col = jnp.asarray(d >= ROPE_AXIS_DIM)
    posv = jnp.where(use_col[None, :], jnp.asarray(col, F32)[:, None], jnp.asarray(row, F32)[:, None])
    ang = posv * inv[None, :]
    upper = jnp.asarray((d % ROPE_AXIS_DIM) >= ROPE_AXIS_DIM // 2)[None, :]
    cos, sin = jnp.cos(ang), jnp.sin(ang)
    sin_hi = jnp.where(upper, sin, 0.0)
    sin_lo = jnp.where(upper, 0.0, -sin)
    pad = lambda t, v: jnp.concatenate([t, jnp.full((n_pad, LANES), v, F32)], axis=0)
    return pad(cos, 1.0), pad(sin_hi, 0.0), pad(sin_lo, 0.0)


def _inproj(tok, x_lat, x_ctx, mod_l, pre_g, w_in, tables, tm):
    per_batch = tok.seq // tm
    n_lat_tiles = tok.n_lat // tm
    tab_map = lambda i: (jnp.where(i < n_lat_tiles, i % per_batch, per_batch), 0)
    tab_spec = pl.BlockSpec((tm, LANES), tab_map)
    n = tok.n_all
    kern = functools.partial(_inproj_kernel, n_lat_tiles=n_lat_tiles)
    return pl.pallas_call(
        kern,
        out_shape=(jax.ShapeDtypeStruct((n // 2, 2 * FOURIER_WIDTH), BF16),
                   jax.ShapeDtypeStruct((n, ATTN_WIDTH), BF16),
                   jax.ShapeDtypeStruct((n, KV_WIDTH), BF16), jax.ShapeDtypeStruct((n, KV_WIDTH), BF16)),
        grid=(n // tm,),
        in_specs=[*tok.split_specs(tm, D_MODEL, x_lat is x_ctx), tok.mod_spec(tm), _vec_spec(D_MODEL),
                  _resident((D_MODEL, IN_WIDTH)), tab_spec, tab_spec, tab_spec],
        out_specs=(_row_spec(tm // 2, 2 * FOURIER_WIDTH), _row_spec(tm, ATTN_WIDTH), _row_spec(tm, KV_WIDTH),
                   _row_spec(tm, KV_WIDTH)),
        scratch_shapes=[pltpu.VMEM((FOURIER_WIDTH // LANES, tm, LANES), F32)],
        compiler_params=_cparams(1),
        name="inproj",
    )(x_lat, x_ctx, mod_l, pre_g, w_in, *tables)


def _head_pair_perm():
    cols = []
    for i in range(HEAD_GROUP):
        for h in (i, i + HEAD_GROUP):
            cols.extend(range(h * HEAD_DIM, (h + 1) * HEAD_DIM))
    return np.asarray(cols)


def _dft_tables(length):
    half = length // 2
    k = np.arange(half)[:, None]
    m = np.arange(half)[None, :]

    def tab(n):
        ang = 2.0 * np.pi * ((k * n) % length) / length
        t = np.concatenate([np.cos(ang), -np.sin(ang)], axis=1) / math.sqrt(length)
        return jnp.asarray(t.astype(np.float32)).astype(BF16)

    return tab(2 * m), tab(2 * m + 1)


def _chan_table():
    n = FOURIER_GROUP_DIM
    k = np.arange(n)
    ang = 2.0 * np.pi * ((k[:, None] * k[None, :]) % n) / n
    t = np.concatenate([np.cos(ang), np.sin(ang)], axis=1) / math.sqrt(n)
    return jnp.asarray(t.astype(np.float32)).astype(BF16)


def _fourier_one(f_ref, chan_ref, pos_e_ref, pos_o_ref, o_ref, stk_ref):
    gd = FOURIER_GROUP_DIM
    half = f_ref.shape[0]
    row_chunk = min(half, 512)
    for parity in range(2):
        for g in range(FOURIER_GROUPS):
            lanes = slice(parity * FOURIER_WIDTH + g * gd, parity * FOURIER_WIDTH + (g + 1) * gd)
            z = jnp.dot(f_ref[:, lanes], chan_ref[...], preferred_element_type=F32)
            stk_ref[parity, 0:half, g * gd:(g + 1) * gd] = z[:, :gd].astype(BF16)
            stk_ref[parity, half:2 * half, g * gd:(g + 1) * gd] = z[:, gd:].astype(BF16)
    for r in range(half // row_chunk):
        rows = slice(r * row_chunk, (r + 1) * row_chunk)
        even = jnp.dot(pos_e_ref[rows, :], stk_ref[0], preferred_element_type=F32)
        odd = jnp.dot(pos_o_ref[rows, :], stk_ref[1], preferred_element_type=F32)
        o_ref[rows, :] = (even + odd).astype(BF16)
        o_ref[half + r * row_chunk:half + (r + 1) * row_chunk, :] = (even - odd).astype(BF16)


def _fourier_kernel(fl_ref, fc_ref, chan_ref, ple_ref, plo_ref, pce_ref, pco_ref, ol_ref, oc_ref, stkl_ref, stkc_ref):
    _fourier_one(fl_ref, chan_ref, ple_ref, plo_ref, ol_ref, stkl_ref)
    _fourier_one(fc_ref, chan_ref, pce_ref, pco_ref, oc_ref, stkc_ref)


def _fourier(tok, f_pairs, chan, tabs_lat, tabs_ctx):
    ctx0 = tok.n_lat // tok.n_ctx
    pair_blk = lambda rows, m: pl.BlockSpec((rows // 2, 2 * FOURIER_WIDTH), m)
    lat_blk = pl.BlockSpec((tok.seq, FOURIER_WIDTH), lambda b: (b, 0))
    ctx_blk = pl.BlockSpec((tok.n_ctx, FOURIER_WIDTH), lambda b: (b, 0))
    tabs = (*tabs_lat, *tabs_ctx)
    return pl.pallas_call(
        _fourier_kernel,
        out_shape=(jax.ShapeDtypeStruct((tok.n_lat, FOURIER_WIDTH), BF16),
                   jax.ShapeDtypeStruct((tok.batch * tok.n_ctx, FOURIER_WIDTH), BF16)),
        grid=(tok.batch,),
        in_specs=[pair_blk(tok.seq, lambda b: (b, 0)), pair_blk(tok.n_ctx, lambda b: (ctx0 + b, 0)),
                  _resident(chan.shape), *[_resident(t.shape) for t in tabs]],
        out_specs=(lat_blk, ctx_blk),
        scratch_shapes=[pltpu.VMEM((2, tok.seq, FOURIER_WIDTH), BF16),
                        pltpu.VMEM((2, tok.n_ctx, FOURIER_WIDTH), BF16)],
        compiler_params=_cparams(1),
        name="fourier",
    )(f_pairs, f_pairs, chan, *tabs)


def _attn_kernel(sink_ref, q_ref, kp_ref, kc_ref, kn_ref, kx_ref, vp_ref, vc_ref, vn_ref, vx_ref, o_ref,
                 *, n_qblk):
    j = pl.program_id(1)
    blk = WINDOW
    n_loc = 3 * blk
    rows = HEAD_GROUP * blk
    q = jnp.concatenate([q_ref[:, i * LANES:(i + 1) * LANES] for i in range(HEAD_GROUP)], axis=0)
    k_loc = jnp.concatenate([kp_ref[...], kc_ref[...], kn_ref[...]], axis=0)
    v_loc = jnp.concatenate([vp_ref[...], vc_ref[...], vn_ref[...]], axis=0)
    k_ctx, v_ctx = kx_ref[...], vx_ref[...]
    is_lat = j < n_qblk
    col_lo = jnp.where(is_lat, jnp.where(j >= 1, 0, blk), n_loc)
    col_hi = jnp.where(is_lat, jnp.where(j + 1 < n_qblk, n_loc, 2 * blk), 0)
    qi = lax.broadcasted_iota(jnp.int32, (blk, 1), 0)
    kj = lax.broadcasted_iota(jnp.int32, (blk, n_loc), 1)
    visible = (kj >= jnp.maximum(qi, col_lo)) & (kj <= jnp.minimum(qi + 2 * WINDOW, col_hi - 1))
    bias = jnp.where(visible, 0.0, NEG_INF)
    bias = jnp.concatenate([bias] * HEAD_GROUP, axis=0)
    row_blk = lax.broadcasted_iota(jnp.int32, (rows, 1), 0) // blk
    lane_o = lax.broadcasted_iota(jnp.int32, (rows, LANES), 1)
    nt = (((1,), (1,)), ((), ()))
    outs = []
    for kvh in range(N_KV_HEADS):
        def own_lanes(t):
            lane = lax.broadcasted_iota(jnp.int32, t.shape, 1)
            return jnp.where((lane >= kvh * HEAD_DIM) & (lane < (kvh + 1) * HEAD_DIM), t, jnp.zeros_like(t))
        s_loc = lax.dot_general(q, own_lanes(k_loc), nt, preferred_element_type=F32) + bias
        s_ctx = lax.dot_general(q, own_lanes(k_ctx), nt, preferred_element_type=F32)
        sink = jnp.zeros((rows, 1), F32)
        for i in range(HEAD_GROUP):
            sink = jnp.where(row_blk == i, sink_ref[kvh * HEAD_GROUP + i] * LOG2_E, sink)
        m = jnp.maximum(jnp.maximum(jnp.max(s_loc, axis=-1, keepdims=True), jnp.max(s_ctx, axis=-1, keepdims=True)),
                        sink)
        p_loc = jnp.exp2(s_loc - m)
        p_ctx = jnp.exp2(s_ctx - m)
        denom = (jnp.sum(p_loc, axis=-1, keepdims=True) + jnp.sum(p_ctx, axis=-1, keepdims=True)
                 + jnp.exp2(sink - m))
        o = (jnp.dot(p_loc.astype(BF16), v_loc, preferred_element_type=F32)
             + jnp.dot(p_ctx.astype(BF16), v_ctx, preferred_element_type=F32))
        outs.append(o / denom)
    merged = jnp.where(lane_o < HEAD_DIM, outs[0], outs[1])
    for i in range(HEAD_GROUP):
        o_ref[:, i * LANES:(i + 1) * LANES] = merged[i * blk:(i + 1) * blk, :].astype(BF16)


def _attention(tok, q, k, v, sink):
    blk = WINDOW
    n_qblk = tok.seq // blk
    n_cblk = tok.n_ctx // blk
    lat_blocks = tok.n_lat // blk
    ctx0 = tok.n_lat // tok.n_ctx

    def q_map(b, j):
        return (jnp.where(j < n_qblk, b * n_qblk + j, lat_blocks + b * n_cblk + (j - n_qblk)), 0)

    def k_map(off):
        return lambda b, j: (b * n_qblk + jnp.clip(j + off, 0, n_qblk - 1), 0)

    ctx_map = lambda b, j: (ctx0 + b, 0)
    kv_blk = lambda m: pl.BlockSpec((blk, KV_WIDTH), m)
    ctx_blk = pl.BlockSpec((tok.n_ctx, KV_WIDTH), ctx_map)
    kern = functools.partial(_attn_kernel, n_qblk=n_qblk)
    return pl.pallas_call(
        kern,
        out_shape=jax.ShapeDtypeStruct((tok.n_all, ATTN_WIDTH), BF16),
        grid=(tok.batch, n_qblk + n_cblk),
        in_specs=[pl.BlockSpec(memory_space=pltpu.SMEM),
                  pl.BlockSpec((blk, ATTN_WIDTH), q_map),
                  kv_blk(k_map(-1)), kv_blk(k_map(0)), kv_blk(k_map(1)), ctx_blk,
                  kv_blk(k_map(-1)), kv_blk(k_map(0)), kv_blk(k_map(1)), ctx_blk],
        out_specs=pl.BlockSpec((blk, ATTN_WIDTH), q_map),
        compiler_params=_cparams(2),
        name="window_attention",
    )(sink, q, k, k, k, k, v, v, v, v)


def _piece_perm():
    idx = np.arange(D_MODEL)
    a, b, c = idx // LANES, (idx // SSM_GROUP_DIM) % SSM_PIECES, idx % SSM_GROUP_DIM
    perm = np.zeros((D_MODEL, D_MODEL), np.float32)
    perm[idx, b * LANES + a * SSM_GROUP_DIM + c] = 1.0
    return jnp.asarray(perm).astype(BF16)


def _slab_pitch(blk):
    return blk + SUBLANES


def _s5_pre_kernel(x_ref, mod_ref, pre_ref, perm_ref, u_ref, hs_ref, *, blk, batch, n_blocks):
    p, b = pl.program_id(0), pl.program_id(1)
    pitch = _slab_pitch(blk)

    @pl.when(p < n_blocks)
    def _():
        h = _norm_mod(x_ref[...], pre_ref[...], mod_ref[0, 1:2, :], mod_ref[0, 0:1, :])
        row0 = pl.multiple_of(b * pitch, SUBLANES)
        slot0 = (p % 2) * SSM_SLABS
        for s in range(SSM_SLABS):
            hs_ref[slot0 + s, pl.ds(row0, blk), :] = h[:, s * LANES:(s + 1) * LANES]

    @pl.when(p >= 1)
    def _():
        half = SSM_CHUNK // 2
        cpb = blk // SSM_CHUNK
        rows_h = cpb * batch
        s = b
        src = ((p - 1) % 2) * SSM_SLABS + s
        lhs = []
        for hh in range(2):
            for i in range(cpb):
                t0 = i * SSM_CHUNK + hh * half
                lhs.append(jnp.concatenate(
                    [hs_ref[src, pl.ds(t0 + j, batch, stride=pitch), :] for j in range(half)], axis=1))
        lhs = jnp.concatenate(lhs, axis=0).astype(BF16)
        out = jnp.dot(lhs, perm_ref[...], preferred_element_type=F32).astype(BF16)
        for hh in range(2):
            for g in range(SSM_PIECES):
                u_ref[s * SSM_PIECES + g, :, hh * LANES:(hh + 1) * LANES] = (
                    out[hh * rows_h:(hh + 1) * rows_h, g * LANES:(g + 1) * LANES])


def _s5_pre(tok, xa, mod_l, pre_g, perm, blk):
    assert tok.batch == SSM_SLABS
    (n_blocks, _), row_map, mod_map = tok.pos_grid(blk, True)
    n_lat_blk = tok.seq // blk
    n_ctx_blk = tok.n_ctx // blk
    cpb = blk // SSM_CHUNK
    n_chunks = (tok.seq + tok.n_ctx) // SSM_CHUNK
    last = n_blocks - 1

    def u_map(p, b):
        q = jnp.maximum(p - 1, 0)
        return (0, jnp.where(q < n_lat_blk, n_ctx_blk + q, q - n_lat_blk), 0)

    clamp = lambda m: (lambda p, b: m(jnp.minimum(p, last), b))
    kern = functools.partial(_s5_pre_kernel, blk=blk, batch=tok.batch, n_blocks=n_blocks)
    return pl.pallas_call(
        kern,
        out_shape=jax.ShapeDtypeStruct((SSM_GROUPS, n_chunks * tok.batch, SSM_CW), BF16),
        grid=(n_blocks + 1, tok.batch),
        in_specs=[pl.BlockSpec((blk, D_MODEL), clamp(row_map)), pl.BlockSpec((1, N_MOD, D_MODEL), clamp(mod_map)),
                  _vec_spec(D_MODEL), _resident((D_MODEL, D_MODEL))],
        out_specs=pl.BlockSpec((SSM_GROUPS, cpb * tok.batch, SSM_CW), u_map),
        scratch_shapes=[pltpu.VMEM((2 * SSM_SLABS, tok.batch * _slab_pitch(blk), LANES), F32)],
        compiler_params=_cparams(2),
        name="s5_pre",
    )(xa, mod_l, pre_g, perm)


def _cmul(ar, ai, br, bi):
    return ar * br - ai * bi, ar * bi + ai * br


def _s5_table_plan():
    t_n = SSM_CHUNK
    t = np.arange(t_n)
    return [
        [(0, t_n - 1 - t, 'b', 're'), (1, t, 'b', 're'), (0, t_n - 1 - t, 'b', 'im'), (1, t, 'b', 'im')],
        [(0, -t, 'b', 're'), (0, -t, 'b', 'im'), (1, t, 'b', 're'), (1, t, 'b', 'im')],
        [(0, t, 'c', 're'), (0, t, 'c', '-im'), (1, -t, 'c', 're'), (1, -t, 'c', '-im')],
        [(0, t + 1, 'c', 're'), (1, t_n - t, 'c', 're'), (0, t + 1, 'c', '-im'), (1, t_n - t, 'c', '-im')],
    ]


def _s5_operands(a_re, a_im, log_dt, b_re, b_im, c_re, c_im):
    t_n = SSM_CHUNK
    per_group = lambda x: jnp.swapaxes(x.astype(F32), 1, 2)
    l_re, l_im = per_group(a_re), per_group(a_im)
    dt = jnp.exp(per_group(log_dt))[..., None]
    z_re, z_im = l_re * dt, l_im * dt
    n_all = jnp.arange(1 - t_n, t_n + 1, dtype=F32)[:, None]
    mag = jnp.exp(z_re[..., None, :] * n_all)
    ang = z_im[..., None, :] * n_all
    pows = jnp.stack([mag * jnp.cos(ang), mag * jnp.sin(ang)])
    a1_re, a1_im = pows[0, ..., t_n, :], pows[1, ..., t_n, :]
    den = l_re * l_re + l_im * l_im
    r_re = ((a1_re - 1.0) * l_re + a1_im * l_im) / den
    r_im = (a1_im * l_re - (a1_re - 1.0) * l_im) / den
    bb_re, bb_im = _cmul(r_re[..., None], r_im[..., None], per_group(b_re), per_group(b_im))
    base = jnp.stack([jnp.swapaxes(bb_re, -1, -2), jnp.swapaxes(bb_im, -1, -2), per_group(c_re), per_group(c_im)],
                     axis=3)
    n_l, n_g = base.shape[0], base.shape[1]
    base = base.reshape(n_l, n_g, 8, SSM_GROUP_DIM, SSM_STATE)
    signed = jnp.concatenate([base, -base], axis=2)

    plan = _s5_table_plan()
    p_idx = np.zeros((len(plan), t_n, 4), np.int32)
    w_idx = np.zeros((2, len(plan), 4), np.int32)
    for k, blocks in enumerate(plan):
        for j, (d, expo, w, part) in enumerate(blocks):
            p_idx[k, :, j] = d * 2 * t_n + expo + t_n - 1
            re, im = d * 4 + (0 if w == 'b' else 2), d * 4 + (1 if w == 'b' else 3)
            w_idx[:, k, j] = {'re': (re, im + 8), 'im': (im, re), '-im': (im + 8, re + 8)}[part]
    pows = pows.reshape(2, n_l, n_g, 4 * t_n, SSM_STATE)
    pp = jnp.moveaxis(pows[:, :, :, p_idx, :].reshape(2, n_l, n_g, len(plan), t_n, 4 * SSM_STATE), 0, 2)
    ww = signed[:, :, w_idx]
    ww = jnp.swapaxes(ww, 4, 5).reshape(n_l, n_g, 2, len(plan), SSM_GROUP_DIM, 4 * SSM_STATE)
    last = 2 * t_n - 1
    pw = pows.reshape(2, n_l, n_g, 2, 2 * t_n, SSM_STATE)[..., last, :]
    dec = jnp.concatenate([pw[0, :, :, 0], pw[0, :, :, 1], pw[1, :, :, 0], pw[1, :, :, 1]], axis=-1)[:, :, None, :]
    return pp, ww, dec


def _s5_kernel(u_ref, pp_ref, ww_ref, dec_ref, y_ref, v_ref, xin_ref, *, batch, n_chunks, n_ctx_chunks):
    gps = SSM_GROUPS_PER_STEP
    ns = SSM_STATE
    ns2 = 2 * ns
    cw = SSM_CW
    nt = (((1,), (1,)), ((), ()))

    def table(g, k):
        w_a, w_b = ww_ref[g, 0, k], ww_ref[g, 1, k]
        return jnp.concatenate([pp_ref[g, 0, k, t:t + 1, :] * w_a + pp_ref[g, 1, k, t:t + 1, :] * w_b
                                for t in range(SSM_CHUNK)], axis=0)

    for g in range(gps):
        v_ref[g] = jnp.dot(u_ref[g], table(g, 0).astype(BF16), preferred_element_type=F32)
    a_re = jnp.broadcast_to(dec_ref[:, :, 0:ns2], (gps, batch, ns2))
    a_im = jnp.broadcast_to(dec_ref[:, :, ns2:2 * ns2], (gps, batch, ns2))
    is_fwd = lax.broadcasted_iota(jnp.int32, (gps, batch, ns2), 2) < ns

    def step(k, carry):
        x_re, x_im = carry
        kb = jnp.where(k < n_ctx_chunks, n_ctx_chunks - 1 - k, n_chunks + n_ctx_chunks - 1 - k)
        rf = pl.multiple_of(k * batch, batch)
        rb = pl.multiple_of(kb * batch, batch)
        xin_ref[:, pl.ds(rf, batch), 0:ns] = x_re[:, :, 0:ns]
        xin_ref[:, pl.ds(rb, batch), ns:ns2] = x_re[:, :, ns:ns2]
        xin_ref[:, pl.ds(rf, batch), ns2:ns2 + ns] = x_im[:, :, 0:ns]
        xin_ref[:, pl.ds(rb, batch), ns2 + ns:2 * ns2] = x_im[:, :, ns:ns2]
        v_re = jnp.where(is_fwd, v_ref[:, pl.ds(rf, batch), 0:ns2], v_ref[:, pl.ds(rb, batch), 0:ns2])
        v_im = jnp.where(is_fwd, v_ref[:, pl.ds(rf, batch), ns2:2 * ns2], v_ref[:, pl.ds(rb, batch), ns2:2 * ns2])
        return a_re * x_re - a_im * x_im + v_re, a_re * x_im + a_im * x_re + v_im

    zero = jnp.zeros((gps, batch, ns2), F32)
    lax.fori_loop(0, n_chunks, step, (zero, zero))
    t_in = lax.broadcasted_iota(jnp.int32, (cw, cw), 0) // SSM_GROUP_DIM
    t_out = lax.broadcasted_iota(jnp.int32, (cw, cw), 1) // SSM_GROUP_DIM

    def split(x):
        hi = x.astype(BF16)
        return hi, (x - hi.astype(F32)).astype(BF16)

    def lag_kernel(e, ft, lanes):
        (eh, el), (fh, fl) = split(e[:, lanes]), split(ft[:, lanes])
        return (lax.dot_general(eh, fh, nt, preferred_element_type=F32)
                + lax.dot_general(eh, fl, nt, preferred_element_type=F32)
                + lax.dot_general(el, fh, nt, preferred_element_type=F32))

    for g in range(gps):
        e, ft = table(g, 1), table(g, 2)
        m = (jnp.where(t_out >= t_in, lag_kernel(e, ft, slice(0, ns2)), 0.0)
             + jnp.where(t_in >= t_out, lag_kernel(e, ft, slice(ns2, 2 * ns2)), 0.0)).astype(BF16)
        y_ref[g] = (jnp.dot(u_ref[g], m, preferred_element_type=F32)
                    + lax.dot_general(xin_ref[g].astype(BF16), table(g, 3).astype(BF16), nt,
                                      preferred_element_type=F32)).astype(BF16)


def _s5(u_t, operands, layer, batch, n_chunks, n_ctx_chunks):
    pp, ww, dec = operands
    gps = SSM_GROUPS_PER_STEP
    rows = n_chunks * batch
    kern = functools.partial(_s5_kernel, batch=batch, n_chunks=n_chunks, n_ctx_chunks=n_ctx_chunks)
    gspec = lambda r, c: pl.BlockSpec((gps, r, c), lambda i: (i, 0, 0))
    lspec = lambda shape: pl.BlockSpec((None, gps) + shape, lambda i: (layer, i) + (0,) * len(shape))
    return pl.pallas_call(
        kern,
        out_shape=jax.ShapeDtypeStruct((SSM_GROUPS, rows, SSM_CW), BF16),
        grid=(SSM_GROUPS // gps,),
        in_specs=[gspec(rows, SSM_CW), lspec(pp.shape[2:]), lspec(ww.shape[2:]), lspec(dec.shape[2:])],
        out_specs=gspec(rows, SSM_CW),
        scratch_shapes=[pltpu.VMEM((gps, rows, SSM_CW), F32), pltpu.VMEM((gps, rows, SSM_CW), F32)],
        compiler_params=_cparams(1),
        name="s5_scan",
    )(u_t, pp, ww, dec)


def _s5_unpack_kernel(y_ref, perm_ref, o_ref, ys_ref, tmp_ref, *, blk, batch):
    b = pl.program_id(1)
    pitch = _slab_pitch(blk)

    @pl.when(b == 0)
    def _():
        half = SSM_CHUNK // 2
        cpb = blk // SSM_CHUNK
        rows_h = cpb * batch

        def slab(s, carry):
            lhs = jnp.concatenate(
                [jnp.concatenate([y_ref[s * SSM_PIECES + g, :, hh * LANES:(hh + 1) * LANES]
                                  for g in range(SSM_PIECES)], axis=1) for hh in range(2)], axis=0)
            tmp_ref[...] = jnp.dot(lhs, perm_ref[...], preferred_element_type=F32)
            for hh in range(2):
                for i in range(cpb):
                    r0 = hh * rows_h + i * batch
                    t0 = i * SSM_CHUNK + hh * half
                    for j in range(half):
                        ys_ref[s, pl.ds(t0 + j, batch, stride=pitch), :] = (
                            tmp_ref[r0:r0 + batch, j * LANES:(j + 1) * LANES])
            return carry

        lax.fori_loop(0, SSM_SLABS, slab, 0)

    row0 = pl.multiple_of(b * pitch, SUBLANES)
    for s in range(SSM_SLABS):
        o_ref[:, s * LANES:(s + 1) * LANES] = ys_ref[s, pl.ds(row0, blk), :].astype(BF16)


def _s5_unpack(tok, y_t, perm, blk, with_ctx):
    grid, row_map, _ = tok.pos_grid(blk, with_ctx)
    n_lat_blk = tok.seq // blk
    n_ctx_blk = tok.n_ctx // blk
    cpb = blk // SSM_CHUNK
    y_map = lambda p, b: (0, jnp.where(p < n_lat_blk, n_ctx_blk + p, p - n_lat_blk), 0)
    n_rows = tok.n_all if with_ctx else tok.n_lat
    kern = functools.partial(_s5_unpack_kernel, blk=blk, batch=tok.batch)
    return pl.pallas_call(
        kern,
        out_shape=jax.ShapeDtypeStruct((n_rows, D_MODEL), BF16),
        grid=grid,
        in_specs=[pl.BlockSpec((SSM_GROUPS, cpb * tok.batch, SSM_CW), y_map), _resident((D_MODEL, D_MODEL))],
        out_specs=pl.BlockSpec((blk, D_MODEL), row_map),
        scratch_shapes=[pltpu.VMEM((SSM_SLABS, tok.batch * _slab_pitch(blk), LANES), F32),
                        pltpu.VMEM((2 * cpb * tok.batch, D_MODEL), F32)],
        compiler_params=_cparams(2),
        name="s5_unpack",
    )(y_t, perm)


def _glu_ffn_kernel(x_ref, y_ref, d_ref, mod_ref, mpre_ref, mpost_ref, pre_ref, post_ref, wg_ref, w1_ref, w2_ref,
                    o_ref, acc_ref):
    x = x_ref[...]
    h = _norm_mod(x, mpre_ref[...], mod_ref[0, 1:2, :], mod_ref[0, 0:1, :])
    y = y_ref[...].astype(F32) + d_ref[...] * h
    z = jnp.dot(jax.nn.gelu(y).astype(BF16), wg_ref[...], preferred_element_type=F32)
    out = z[:, :D_MODEL] * jax.nn.sigmoid(z[:, D_MODEL:])
    x = x + mod_ref[0, 2:3, :] * _rms(out, mpost_ref[...])
    _ffn_body(x, mod_ref, pre_ref, post_ref, w1_ref, w2_ref, o_ref, acc_ref)


def _glu_ffn(tok, xa, n_rows, y_tok, d_skip, mod_l, mix_pre_g, mix_post_g, pre_g, post_g, glu_all, w1_all, w2_all,
             i, layer, tm):
    return pl.pallas_call(
        _glu_ffn_kernel,
        out_shape=jax.ShapeDtypeStruct((n_rows, D_MODEL), F32),
        grid=(n_rows // tm,),
        in_specs=[_row_spec(tm, D_MODEL), _row_spec(tm, D_MODEL), _vec_spec(D_MODEL), tok.mod_spec(tm),
                  _vec_spec(D_MODEL), _vec_spec(D_MODEL), _vec_spec(D_MODEL), _vec_spec(D_MODEL),
                  _layer_resident(i, (D_MODEL, 2 * D_MODEL)),
                  _layer_resident(layer, (D_MODEL, D_FF)), _layer_resident(layer, (D_FF, D_MODEL))],
        out_specs=_row_spec(tm, D_MODEL),
        scratch_shapes=[pltpu.VMEM((tm, D_MODEL), F32)],
        compiler_params=_cparams(1),
        name="glu_ffn",
    )(xa, y_tok, d_skip, mod_l, mix_pre_g, mix_post_g, pre_g, post_g, glu_all, w1_all, w2_all)


def _tile(limit, *sizes):
    tm = limit
    while any(s % tm for s in sizes):
        tm //= 2
    return tm


def kernel(x, c, ctx, c_ctx, mod_w, mod_b, mix_pre_g, mix_post_g, ffn_pre_g, ffn_post_g, ffn_w1, ffn_w2,
           even_w_in, even_w_out, even_sink, ssm_a_re, ssm_a_im, ssm_log_dt, ssm_b_re, ssm_b_im, ssm_c_re,
           ssm_c_im, ssm_d, ssm_glu_w):
    batch, seq, _ = x.shape
    n_ctx = ctx.shape[1]
    tok = _Tokens(batch, seq, n_ctx)
    assert seq % WINDOW == 0 and n_ctx % WINDOW == 0 and tok.n_lat % n_ctx == 0
    assert batch == SUBLANES
    tm = _tile(256, seq, n_ctx)
    tm_wide = _tile(512, seq, batch * n_ctx)

    x_lat, x_ctx = x.reshape(tok.n_lat, D_MODEL), ctx.astype(x.dtype).reshape(-1, D_MODEL)

    n_cond = 2 * SUBLANES
    cond = jnp.zeros((n_cond, D_MODEL), F32).at[:batch].set(c).at[batch].set(c_ctx)
    mod = _modulation(cond, mod_w, mod_b).reshape(DEPTH, n_cond, N_MOD, D_MODEL)

    rope = _rope_tables(seq, tm_wide)
    head_perm = _head_pair_perm()
    piece_perm = _piece_perm()
    dft_chan, dft_lat, dft_ctx = _chan_table(), _dft_tables(seq), _dft_tables(n_ctx)
    vec = lambda g: g.reshape(1, D_MODEL)
    w1_all, w2_all, glu_all = ffn_w1.astype(BF16), ffn_w2.astype(BF16), ssm_glu_w.astype(BF16)
    s5_operands = _s5_operands(ssm_a_re, ssm_a_im, ssm_log_dt, ssm_b_re, ssm_b_im, ssm_c_re, ssm_c_im)

    for layer in range(DEPTH):
        need_ctx = layer < DEPTH - 1
        n_rows = tok.n_all if need_ctx else tok.n_lat
        mod_l = mod[layer]
        i = layer // 2
        ffn_pre, ffn_post = vec(ffn_pre_g[layer]), vec(ffn_post_g[layer])
        if layer % 2 == 0:
            w_in = even_w_in[i]
            q0 = FOURIER_WIDTH
            w_in = jnp.concatenate([w_in[:, :q0], w_in[:, q0:q0 + ATTN_WIDTH][:, head_perm],
                                    w_in[:, q0 + ATTN_WIDTH:]], axis=1).astype(BF16)
            f, q, k, v = _inproj(tok, x_lat, x_ctx, mod_l, vec(mix_pre_g[layer]), w_in, rope, tm_wide)
            fm_lat, fm_ctx = _fourier(tok, f, dft_chan, dft_lat, dft_ctx)
            ao = _attention(tok, q, k, v, even_sink[i])
            w_out = even_w_out[i]
            w_f = w_out[:FOURIER_WIDTH].astype(BF16)
            w_a = w_out[FOURIER_WIDTH:][head_perm].astype(BF16)
            xa = _mix_ffn(tok, x_lat, x_ctx, fm_lat, fm_ctx, ao, mod_l, vec(mix_post_g[layer]), ffn_pre, ffn_post,
                          w_f, w_a, w1_all, w2_all, layer, tm_wide)
        else:
            pre_g = vec(mix_pre_g[layer])
            u_t = _s5_pre(tok, xa, mod_l, pre_g, piece_perm, tm)
            y_t = _s5(u_t, s5_operands, i, batch, (seq + n_ctx) // SSM_CHUNK, n_ctx // SSM_CHUNK)
            y_tok = _s5_unpack(tok, y_t, piece_perm, tm, need_ctx)
            xa = _glu_ffn(tok, xa, n_rows, y_tok, vec(ssm_d[i]), mod_l, pre_g, vec(mix_post_g[layer]),
                          ffn_pre, ffn_post, glu_all, w1_all, w2_all, i, layer, tm_wide)
        x_lat = x_ctx = xa
    return xa[:tok.n_lat].reshape(batch, seq, D_MODEL)
```

```python
import functools
import math

import numpy as np
import jax
import jax.numpy as jnp
from jax import lax
from jax.experimental import pallas as pl
from jax.experimental.pallas import tpu as pltpu

D_MODEL = 1024
DEPTH = 4
N_MOD = 6
EPS = 1e-6
NEG_INF = -1e30
GRID_W = 64

FOURIER_GROUPS = 4
FOURIER_GROUP_DIM = 128
FOURIER_WIDTH = FOURIER_GROUPS * FOURIER_GROUP_DIM

N_HEADS = 8
N_KV_HEADS = 2
HEAD_GROUP = N_HEADS // N_KV_HEADS
HEAD_DIM = 64
ATTN_WIDTH = N_HEADS * HEAD_DIM
KV_WIDTH = N_KV_HEADS * HEAD_DIM
WINDOW = 128
ROPE_AXIS_DIM = HEAD_DIM // 2
ROPE_BASE = 10000.0
LOG2_E = math.log2(math.e)
IN_WIDTH = FOURIER_WIDTH + ATTN_WIDTH + 2 * KV_WIDTH

LANES = 128
SUBLANES = 8
VMEM_LIMIT = 56 * 1024 * 1024

SSM_GROUP_DIM = 16
SSM_GROUPS = D_MODEL // SSM_GROUP_DIM
SSM_STATE = 64
SSM_CHUNK = 16
SSM_CW = SSM_CHUNK * SSM_GROUP_DIM
SSM_GROUPS_PER_STEP = 4
SSM_SLABS = D_MODEL // LANES
SSM_PIECES = LANES // SSM_GROUP_DIM

D_FF = 4 * D_MODEL

F32 = jnp.float32
BF16 = jnp.bfloat16


def _cparams(n_axes):
    return pltpu.CompilerParams(dimension_semantics=("arbitrary",) * n_axes, vmem_limit_bytes=VMEM_LIMIT)


def _resident(shape):
    nd = len(shape)
    return pl.BlockSpec(shape, lambda *_: (0,) * nd, pipeline_mode=pl.Buffered(1))


def _rms(x, g):
    return x * lax.rsqrt(jnp.mean(x * x, axis=-1, keepdims=True) + EPS) * g


def _norm_mod(x, g, sc, sh):
    return _rms(x, g) * (1.0 + sc) + sh


def _mod_kernel(cond_ref, w_ref, b_ref, o_ref):
    cond = cond_ref[...]
    s = cond * jax.nn.sigmoid(cond)
    s_hi = s.astype(BF16)
    s_lo = (s - s_hi.astype(F32)).astype(BF16)
    w = w_ref[0].astype(BF16)
    o_ref[0] = (jnp.dot(s_hi, w, preferred_element_type=F32) + jnp.dot(s_lo, w, preferred_element_type=F32)
                + b_ref[0])


def _modulation(cond, mod_w, mod_b):
    rows = cond.shape[0]
    tn = 1024
    n = N_MOD * D_MODEL
    return pl.pallas_call(
        _mod_kernel,
        out_shape=jax.ShapeDtypeStruct((DEPTH, rows, n), F32),
        grid=(DEPTH, n // tn),
        in_specs=[pl.BlockSpec((rows, D_MODEL), lambda l, j: (0, 0)),
                  pl.BlockSpec((1, D_MODEL, tn), lambda l, j: (l, 0, j)),
                  pl.BlockSpec((1, 1, tn), lambda l, j: (l, 0, j))],
        out_specs=pl.BlockSpec((1, rows, tn), lambda l, j: (l, 0, j)),
        compiler_params=_cparams(2),
        name="modulation",
    )(cond, mod_w, mod_b.reshape(DEPTH, 1, n))


class _Tokens:
    def __init__(self, batch, seq, n_ctx):
        self.batch, self.seq, self.n_ctx = batch, seq, n_ctx
        self.n_lat = batch * seq
        self.n_all = self.n_lat + batch * n_ctx

    def mod_spec(self, tm):
        per_batch = self.seq // tm
        return pl.BlockSpec((1, N_MOD, D_MODEL), lambda i: (jnp.minimum(i // per_batch, self.batch), 0, 0))

    def split_specs(self, tm, width, joined):
        nlt = self.n_lat // tm
        ctx_map = (lambda i: (jnp.maximum(i, nlt), 0)) if joined else (lambda i: (jnp.maximum(i - nlt, 0), 0))
        return pl.BlockSpec((tm, width), lambda i: (jnp.minimum(i, nlt - 1), 0)), pl.BlockSpec((tm, width), ctx_map)

    def pos_grid(self, blk, with_ctx):
        n_lat_blk = self.seq // blk
        n_ctx_blk = self.n_ctx // blk
        lat_blocks = self.n_lat // blk
        row_map = lambda p, b: (jnp.where(p < n_lat_blk, b * n_lat_blk + p, lat_blocks + b * n_ctx_blk + (p - n_lat_blk)), 0)
        mod_map = lambda p, b: (jnp.where(p < n_lat_blk, b, self.batch), 0, 0)
        grid = (n_lat_blk + (n_ctx_blk if with_ctx else 0), self.batch)
        return grid, row_map, mod_map


def _pick(n_lat_tiles, lat_ref, ctx_ref):
    return jnp.where(pl.program_id(0) < n_lat_tiles, lat_ref[...], ctx_ref[...])


def _layer_resident(layer, shape):
    nd = len(shape)
    return pl.BlockSpec((None,) + tuple(shape), lambda *_: (layer,) + (0,) * nd, pipeline_mode=pl.Buffered(1))


def _row_spec(tm, width):
    return pl.BlockSpec((tm, width), lambda i: (i, 0))


def _vec_spec(width):
    return pl.BlockSpec((1, width), lambda *_: (0, 0))


FFN_CHUNK = 512


def _ffn_body(x, mod_ref, pre_ref, post_ref, w1_ref, w2_ref, o_ref, acc_ref):
    h = _norm_mod(x, pre_ref[...], mod_ref[0, 4:5, :], mod_ref[0, 3:4, :]).astype(BF16)
    for c in range(D_FF // FFN_CHUNK):
        sl = slice(c * FFN_CHUNK, (c + 1) * FFN_CHUNK)
        a = jnp.maximum(jnp.dot(h, w1_ref[:, sl], preferred_element_type=F32), 0.0)
        part = jnp.dot((a * a).astype(BF16), w2_ref[sl, :], preferred_element_type=F32)
        if c == 0:
            acc_ref[...] = part
        else:
            acc_ref[...] += part
    o_ref[...] = x + mod_ref[0, 5:6, :] * _rms(acc_ref[...], post_ref[...])


def _mix_ffn_kernel(xl_ref, xc_ref, fl_ref, fc_ref, ao_ref, mod_ref, mpost_ref, pre_ref, post_ref, wf_ref, wa_ref,
                    w1_ref, w2_ref, o_ref, acc_ref, *, n_lat_tiles):
    y = (jnp.dot(_pick(n_lat_tiles, fl_ref, fc_ref), wf_ref[...], preferred_element_type=F32)
         + jnp.dot(ao_ref[...], wa_ref[...], preferred_element_type=F32))
    x = _pick(n_lat_tiles, xl_ref, xc_ref) + mod_ref[0, 2:3, :] * _rms(y, mpost_ref[...])
    _ffn_body(x, mod_ref, pre_ref, post_ref, w1_ref, w2_ref, o_ref, acc_ref)


def _mix_ffn(tok, x_lat, x_ctx, fm_lat, fm_ctx, ao, mod_l, mix_post_g, pre_g, post_g, w_f, w_a, w1_all, w2_all,
             layer, tm):
    n = tok.n_all
    kern = functools.partial(_mix_ffn_kernel, n_lat_tiles=tok.n_lat // tm)
    return pl.pallas_call(
        kern,
        out_shape=jax.ShapeDtypeStruct((n, D_MODEL), F32),
        grid=(n // tm,),
        in_specs=[*tok.split_specs(tm, D_MODEL, x_lat is x_ctx), *tok.split_specs(tm, FOURIER_WIDTH, False),
                  _row_spec(tm, ATTN_WIDTH), tok.mod_spec(tm),
                  _vec_spec(D_MODEL), _vec_spec(D_MODEL), _vec_spec(D_MODEL),
                  _resident((FOURIER_WIDTH, D_MODEL)), _resident((ATTN_WIDTH, D_MODEL)),
                  _layer_resident(layer, (D_MODEL, D_FF)), _layer_resident(layer, (D_FF, D_MODEL))],
        out_specs=_row_spec(tm, D_MODEL),
        scratch_shapes=[pltpu.VMEM((tm, D_MODEL), F32)],
        compiler_params=_cparams(1),
        name="mix_ffn",
    )(x_lat, x_ctx, fm_lat, fm_ctx, ao, mod_l, mix_post_g, pre_g, post_g, w_f, w_a, w1_all, w2_all)


def _rope_block(x, cos, sin_hi, sin_lo):
    half = ROPE_AXIS_DIM // 2
    return (x * cos + pltpu.roll(x, half, axis=1) * sin_hi
            + pltpu.roll(x, LANES - half, axis=1) * sin_lo)


def _inproj_kernel(xl_ref, xc_ref, mod_ref, pre_ref, w_ref, cos_ref, shi_ref, slo_ref, f_ref, q_ref, k_ref, v_ref,
                   fs_ref, *, n_lat_tiles):
    x = _pick(n_lat_tiles, xl_ref, xc_ref)
    h = _norm_mod(x, pre_ref[...], mod_ref[0, 1:2, :], mod_ref[0, 0:1, :]).astype(BF16)
    p = jnp.dot(h, w_ref[...], preferred_element_type=F32)
    cos, shi, slo = cos_ref[...], shi_ref[...], slo_ref[...]
    pairs = x.shape[0] // 2
    n_slabs = FOURIER_WIDTH // LANES
    for s in range(n_slabs):
        fs_ref[s] = p[:, s * LANES:(s + 1) * LANES]
    for parity in range(2):
        for s in range(n_slabs):
            lo = parity * FOURIER_WIDTH + s * LANES
            f_ref[:, lo:lo + LANES] = fs_ref[s, pl.ds(parity, pairs, stride=2), :].astype(BF16)
    scale = HEAD_DIM ** -0.5 * LOG2_E
    for j in range(ATTN_WIDTH // LANES):
        lo = FOURIER_WIDTH + j * LANES
        q_ref[:, j * LANES:(j + 1) * LANES] = (_rope_block(p[:, lo:lo + LANES], cos, shi, slo) * scale).astype(BF16)
    k0 = FOURIER_WIDTH + ATTN_WIDTH
    k_ref[...] = _rope_block(p[:, k0:k0 + KV_WIDTH], cos, shi, slo).astype(BF16)
    v_ref[...] = p[:, k0 + KV_WIDTH:].astype(BF16)


def _rope_tables(seq, n_pad):
    pos = np.arange(seq)
    row = (pos // GRID_W).astype(np.float64)
    col = (pos % GRID_W).astype(np.float64)
    lane = np.arange(LANES)
    d = lane % HEAD_DIM
    j = d % (ROPE_AXIS_DIM // 2)
    inv = jnp.asarray(ROPE_BASE, F32) ** (-jnp.asarray(2 * j, F32) / ROPE_AXIS_DIM)
    use_col = jnp.asarray(d >= ROPE_AXIS_DIM)
    posv = jnp.where(use_col[None, :], jnp.asarray(col, F32)[:, None], jnp.asarray(row, F32)[:, None])
    ang = posv * inv[None, :]
    upper = jnp.asarray((d % ROPE_AXIS_DIM) >= ROPE_AXIS_DIM // 2)[None, :]
    cos, sin = jnp.cos(ang), jnp.sin(ang)
    sin_hi = jnp.where(upper, sin, 0.0)
    sin_lo = jnp.where(upper, 0.0, -sin)
    pad = lambda t, v: jnp.concatenate([t, jnp.full((n_pad, LANES), v, F32)], axis=0)
    return pad(cos, 1.0), pad(sin_hi, 0.0), pad(sin_lo, 0.0)


def _inproj(tok, x_lat, x_ctx, mod_l, pre_g, w_in, tables, tm):
    per_batch = tok.seq // tm
    n_lat_tiles = tok.n_lat // tm
    tab_map = lambda i: (jnp.where(i < n_lat_tiles, i % per_batch, per_batch), 0)
    tab_spec = pl.BlockSpec((tm, LANES), tab_map)
    n = tok.n_all
    kern = functools.partial(_inproj_kernel, n_lat_tiles=n_lat_tiles)
    return pl.pallas_call(
        kern,
        out_shape=(jax.ShapeDtypeStruct((n // 2, 2 * FOURIER_WIDTH), BF16),
                   jax.ShapeDtypeStruct((n, ATTN_WIDTH), BF16),
                   jax.ShapeDtypeStruct((n, KV_WIDTH), BF16), jax.ShapeDtypeStruct((n, KV_WIDTH), BF16)),
        grid=(n // tm,),
        in_specs=[*tok.split_specs(tm, D_MODEL, x_lat is x_ctx), tok.mod_spec(tm), _vec_spec(D_MODEL),
                  _resident((D_MODEL, IN_WIDTH)), tab_spec, tab_spec, tab_spec],
        out_specs=(_row_spec(tm // 2, 2 * FOURIER_WIDTH), _row_spec(tm, ATTN_WIDTH), _row_spec(tm, KV_WIDTH),
                   _row_spec(tm, KV_WIDTH)),
        scratch_shapes=[pltpu.VMEM((FOURIER_WIDTH // LANES, tm, LANES), F32)],
        compiler_params=_cparams(1),
        name="inproj",
    )(x_lat, x_ctx, mod_l, pre_g, w_in, *tables)


def _head_pair_perm():
    cols = []
    for i in range(HEAD_GROUP):
        for h in (i, i + HEAD_GROUP):
            cols.extend(range(h * HEAD_DIM, (h + 1) * HEAD_DIM))
    return np.asarray(cols)


def _dft_tables(length):
    half = length // 2
    k = np.arange(half)[:, None]
    m = np.arange(half)[None, :]

    def tab(n):
        ang = 2.0 * np.pi * ((k * n) % length) / length
        t = np.concatenate([np.cos(ang), -np.sin(ang)], axis=1) / math.sqrt(length)
        return jnp.asarray(t.astype(np.float32)).astype(BF16)

    return tab(2 * m), tab(2 * m + 1)


def _chan_table():
    n = FOURIER_GROUP_DIM
    k = np.arange(n)
    ang = 2.0 * np.pi * ((k[:, None] * k[None, :]) % n) / n
    t = np.concatenate([np.cos(ang), np.sin(ang)], axis=1) / math.sqrt(n)
    return jnp.asarray(t.astype(np.float32)).astype(BF16)


def _fourier_one(f_ref, chan_ref, pos_e_ref, pos_o_ref, o_ref, stk_ref):
    gd = FOURIER_GROUP_DIM
    half = f_ref.shape[0]
    row_chunk = min(half, 512)
    for parity in range(2):
        for g in range(FOURIER_GROUPS):
            lanes = slice(parity * FOURIER_WIDTH + g * gd, parity * FOURIER_WIDTH + (g + 1) * gd)
            z = jnp.dot(f_ref[:, lanes], chan_ref[...], preferred_element_type=F32)
            stk_ref[parity, 0:half, g * gd:(g + 1) * gd] = z[:, :gd].astype(BF16)
            stk_ref[parity, half:2 * half, g * gd:(g + 1) * gd] = z[:, gd:].astype(BF16)
    for r in range(half // row_chunk):
        rows = slice(r * row_chunk, (r + 1) * row_chunk)
        even = jnp.dot(pos_e_ref[rows, :], stk_ref[0], preferred_element_type=F32)
        odd = jnp.dot(pos_o_ref[rows, :], stk_ref[1], preferred_element_type=F32)
        o_ref[rows, :] = (even + odd).astype(BF16)
        o_ref[half + r * row_chunk:half + (r + 1) * row_chunk, :] = (even - odd).astype(BF16)


def _fourier_kernel(fl_ref, fc_ref, chan_ref, ple_ref, plo_ref, pce_ref, pco_ref, ol_ref, oc_ref, stkl_ref, stkc_ref):
    _fourier_one(fl_ref, chan_ref, ple_ref, plo_ref, ol_ref, stkl_ref)
    _fourier_one(fc_ref, chan_ref, pce_ref, pco_ref, oc_ref, stkc_ref)


def _fourier(tok, f_pairs, chan, tabs_lat, tabs_ctx):
    ctx0 = tok.n_lat // tok.n_ctx
    pair_blk = lambda rows, m: pl.BlockSpec((rows // 2, 2 * FOURIER_WIDTH), m)
    lat_blk = pl.BlockSpec((tok.seq, FOURIER_WIDTH), lambda b: (b, 0))
    ctx_blk = pl.BlockSpec((tok.n_ctx, FOURIER_WIDTH), lambda b: (b, 0))
    tabs = (*tabs_lat, *tabs_ctx)
    return pl.pallas_call(
        _fourier_kernel,
        out_shape=(jax.ShapeDtypeStruct((tok.n_lat, FOURIER_WIDTH), BF16),
                   jax.ShapeDtypeStruct((tok.batch * tok.n_ctx, FOURIER_WIDTH), BF16)),
        grid=(tok.batch,),
        in_specs=[pair_blk(tok.seq, lambda b: (b, 0)), pair_blk(tok.n_ctx, lambda b: (ctx0 + b, 0)),
                  _resident(chan.shape), *[_resident(t.shape) for t in tabs]],
        out_specs=(lat_blk, ctx_blk),
        scratch_shapes=[pltpu.VMEM((2, tok.seq, FOURIER_WIDTH), BF16),
                        pltpu.VMEM((2, tok.n_ctx, FOURIER_WIDTH), BF16)],
        compiler_params=_cparams(1),
        name="fourier",
    )(f_pairs, f_pairs, chan, *tabs)


def _attn_kernel(sink_ref, q_ref, kp_ref, kc_ref, kn_ref, kx_ref, vp_ref, vc_ref, vn_ref, vx_ref, o_ref,
                 *, n_qblk):
    j = pl.program_id(1)
    blk = WINDOW
    n_loc = 3 * blk
    rows = HEAD_GROUP * blk
    q = jnp.concatenate([q_ref[:, i * LANES:(i + 1) * LANES] for i in range(HEAD_GROUP)], axis=0)
    k_loc = jnp.concatenate([kp_ref[...], kc_ref[...], kn_ref[...]], axis=0)
    v_loc = jnp.concatenate([vp_ref[...], vc_ref[...], vn_ref[...]], axis=0)
    k_ctx, v_ctx = kx_ref[...], vx_ref[...]
    is_lat = j < n_qblk
    col_lo = jnp.where(is_lat, jnp.where(j >= 1, 0, blk), n_loc)
    col_hi = jnp.where(is_lat, jnp.where(j + 1 < n_qblk, n_loc, 2 * blk), 0)
    qi = lax.broadcasted_iota(jnp.int32, (blk, 1), 0)
    kj = lax.broadcasted_iota(jnp.int32, (blk, n_loc), 1)
    visible = (kj >= jnp.maximum(qi, col_lo)) & (kj <= jnp.minimum(qi + 2 * WINDOW, col_hi - 1))
    bias = jnp.where(visible, 0.0, NEG_INF)
    bias = jnp.concatenate([bias] * HEAD_GROUP, axis=0)
    row_blk = lax.broadcasted_iota(jnp.int32, (rows, 1), 0) // blk
    lane_o = lax.broadcasted_iota(jnp.int32, (rows, LANES), 1)
    nt = (((1,), (1,)), ((), ()))
    outs = []
    for kvh in range(N_KV_HEADS):
        def own_lanes(t):
            lane = lax.broadcasted_iota(jnp.int32, t.shape, 1)
            return jnp.where((lane >= kvh * HEAD_DIM) & (lane < (kvh + 1) * HEAD_DIM), t, jnp.zeros_like(t))
        s_loc = lax.dot_general(q, own_lanes(k_loc), nt, preferred_element_type=F32) + bias
        s_ctx = lax.dot_general(q, own_lanes(k_ctx), nt, preferred_element_type=F32)
        sink = jnp.zeros((rows, 1), F32)
        for i in range(HEAD_GROUP):
            sink = jnp.where(row_blk == i, sink_ref[kvh * HEAD_GROUP + i] * LOG2_E, sink)
        m = jnp.maximum(jnp.maximum(jnp.max(s_loc, axis=-1, keepdims=True), jnp.max(s_ctx, axis=-1, keepdims=True)),
                        sink)
        p_loc = jnp.exp2(s_loc - m)
        p_ctx = jnp.exp2(s_ctx - m)
        denom = (jnp.sum(p_loc, axis=-1, keepdims=True) + jnp.sum(p_ctx, axis=-1, keepdims=True)
                 + jnp.exp2(sink - m))
        o = (jnp.dot(p_loc.astype(BF16), v_loc, preferred_element_type=F32)
             + jnp.dot(p_ctx.astype(BF16), v_ctx, preferred_element_type=F32))
        outs.append(o / denom)
    merged = jnp.where(lane_o < HEAD_DIM, outs[0], outs[1])
    for i in range(HEAD_GROUP):
        o_ref[:, i * LANES:(i + 1) * LANES] = merged[i * blk:(i + 1) * blk, :].astype(BF16)


def _attention(tok, q, k, v, sink):
    blk = WINDOW
    n_qblk = tok.seq // blk
    n_cblk = tok.n_ctx // blk
    lat_blocks = tok.n_lat // blk
    ctx0 = tok.n_lat // tok.n_ctx

    def q_map(b, j):
        return (jnp.where(j < n_qblk, b * n_qblk + j, lat_blocks + b * n_cblk + (j - n_qblk)), 0)

    def k_map(off):
        return lambda b, j: (b * n_qblk + jnp.clip(j + off, 0, n_qblk - 1), 0)

    ctx_map = lambda b, j: (ctx0 + b, 0)
    kv_blk = lambda m: pl.BlockSpec((blk, KV_WIDTH), m)
    ctx_blk = pl.BlockSpec((tok.n_ctx, KV_WIDTH), ctx_map)
    kern = functools.partial(_attn_kernel, n_qblk=n_qblk)
    return pl.pallas_call(
        kern,
        out_shape=jax.ShapeDtypeStruct((tok.n_all, ATTN_WIDTH), BF16),
        grid=(tok.batch, n_qblk + n_cblk),
        in_specs=[pl.BlockSpec(memory_space=pltpu.SMEM),
                  pl.BlockSpec((blk, ATTN_WIDTH), q_map),
                  kv_blk(k_map(-1)), kv_blk(k_map(0)), kv_blk(k_map(1)), ctx_blk,
                  kv_blk(k_map(-1)), kv_blk(k_map(0)), kv_blk(k_map(1)), ctx_blk],
        out_specs=pl.BlockSpec((blk, ATTN_WIDTH), q_map),
        compiler_params=_cparams(2),
        name="window_attention",
    )(sink, q, k, k, k, k, v, v, v, v)


def _piece_perm():
    idx = np.arange(D_MODEL)
    a, b, c = idx // LANES, (idx // SSM_GROUP_DIM) % SSM_PIECES, idx % SSM_GROUP_DIM
    perm = np.zeros((D_MODEL, D_MODEL), np.float32)
    perm[idx, b * LANES + a * SSM_GROUP_DIM + c] = 1.0
    return jnp.asarray(perm).astype(BF16)


def _slab_pitch(blk):
    return blk + SUBLANES


def _s5_pre_kernel(x_ref, mod_ref, pre_ref, perm_ref, u_ref, hs_ref, *, blk, batch, n_blocks):
    p, b = pl.program_id(0), pl.program_id(1)
    pitch = _slab_pitch(blk)

    @pl.when(p < n_blocks)
    def _():
        h = _norm_mod(x_ref[...], pre_ref[...], mod_ref[0, 1:2, :], mod_ref[0, 0:1, :])
        row0 = pl.multiple_of(b * pitch, SUBLANES)
        slot0 = (p % 2) * SSM_SLABS
        for s in range(SSM_SLABS):
            hs_ref[slot0 + s, pl.ds(row0, blk), :] = h[:, s * LANES:(s + 1) * LANES]

    @pl.when(p >= 1)
    def _():
        half = SSM_CHUNK // 2
        cpb = blk // SSM_CHUNK
        rows_h = cpb * batch
        s = b
        src = ((p - 1) % 2) * SSM_SLABS + s
        lhs = []
        for hh in range(2):
            for i in range(cpb):
                t0 = i * SSM_CHUNK + hh * half
                lhs.append(jnp.concatenate(
                    [hs_ref[src, pl.ds(t0 + j, batch, stride=pitch), :] for j in range(half)], axis=1))
        lhs = jnp.concatenate(lhs, axis=0).astype(BF16)
        out = jnp.dot(lhs, perm_ref[...], preferred_element_type=F32).astype(BF16)
        for hh in range(2):
            for g in range(SSM_PIECES):
                u_ref[s * SSM_PIECES + g, :, hh * LANES:(hh + 1) * LANES] = (
                    out[hh * rows_h:(hh + 1) * rows_h, g * LANES:(g + 1) * LANES])


def _s5_pre(tok, xa, mod_l, pre_g, perm, blk):
    assert tok.batch == SSM_SLABS
    (n_blocks, _), row_map, mod_map = tok.pos_grid(blk, True)
    n_lat_blk = tok.seq // blk
    n_ctx_blk = tok.n_ctx // blk
    cpb = blk // SSM_CHUNK
    n_chunks = (tok.seq + tok.n_ctx) // SSM_CHUNK
    last = n_blocks - 1

    def u_map(p, b):
        q = jnp.maximum(p - 1, 0)
        return (0, jnp.where(q < n_lat_blk, n_ctx_blk + q, q - n_lat_blk), 0)

    clamp = lambda m: (lambda p, b: m(jnp.minimum(p, last), b))
    kern = functools.partial(_s5_pre_kernel, blk=blk, batch=tok.batch, n_blocks=n_blocks)
    return pl.pallas_call(
        kern,
        out_shape=jax.ShapeDtypeStruct((SSM_GROUPS, n_chunks * tok.batch, SSM_CW), BF16),
        grid=(n_blocks + 1, tok.batch),
        in_specs=[pl.BlockSpec((blk, D_MODEL), clamp(row_map)), pl.BlockSpec((1, N_MOD, D_MODEL), clamp(mod_map)),
                  _vec_spec(D_MODEL), _resident((D_MODEL, D_MODEL))],
        out_specs=pl.BlockSpec((SSM_GROUPS, cpb * tok.batch, SSM_CW), u_map),
        scratch_shapes=[pltpu.VMEM((2 * SSM_SLABS, tok.batch * _slab_pitch(blk), LANES), F32)],
        compiler_params=_cparams(2),
        name="s5_pre",
    )(xa, mod_l, pre_g, perm)


def _cmul(ar, ai, br, bi):
    return ar * br - ai * bi, ar * bi + ai * br


def _s5_table_plan():
    t_n = SSM_CHUNK
    t = np.arange(t_n)
    return [
        [(0, t_n - 1 - t, 'b', 're'), (1, t, 'b', 're'), (0, t_n - 1 - t, 'b', 'im'), (1, t, 'b', 'im')],
        [(0, -t, 'b', 're'), (0, -t, 'b', 'im'), (1, t, 'b', 're'), (1, t, 'b', 'im')],
        [(0, t, 'c', 're'), (0, t, 'c', '-im'), (1, -t, 'c', 're'), (1, -t, 'c', '-im')],
        [(0, t + 1, 'c', 're'), (1, t_n - t, 'c', 're'), (0, t + 1, 'c', '-im'), (1, t_n - t, 'c', '-im')],
    ]


def _s5_operands(a_re, a_im, log_dt, b_re, b_im, c_re, c_im):
    t_n = SSM_CHUNK
    per_group = lambda x: jnp.swapaxes(x.astype(F32), 1, 2)
    l_re, l_im = per_group(a_re), per_group(a_im)
    dt = jnp.exp(per_group(log_dt))[..., None]
    z_re, z_im = l_re * dt, l_im * dt
    n_all = jnp.arange(1 - t_n, t_n + 1, dtype=F32)[:, None]
    mag = jnp.exp(z_re[..., None, :] * n_all)
    ang = z_im[..., None, :] * n_all
    pows = jnp.stack([mag * jnp.cos(ang), mag * jnp.sin(ang)])
    a1_re, a1_im = pows[0, ..., t_n, :], pows[1, ..., t_n, :]
    den = l_re * l_re + l_im * l_im
    r_re = ((a1_re - 1.0) * l_re + a1_im * l_im) / den
    r_im = (a1_im * l_re - (a1_re - 1.0) * l_im) / den
    bb_re, bb_im = _cmul(r_re[..., None], r_im[..., None], per_group(b_re), per_group(b_im))
    base = jnp.stack([jnp.swapaxes(bb_re, -1, -2), jnp.swapaxes(bb_im, -1, -2), per_group(c_re), per_group(c_im)],
                     axis=3)
    n_l, n_g = base.shape[0], base.shape[1]
    base = base.reshape(n_l, n_g, 8, SSM_GROUP_DIM, SSM_STATE)
    signed = jnp.concatenate([base, -base], axis=2)

    plan = _s5_table_plan()
    p_idx = np.zeros((len(plan), t_n, 4), np.int32)
    w_idx = np.zeros((2, len(plan), 4), np.int32)
    for k, blocks in enumerate(plan):
        for j, (d, expo, w, part) in enumerate(blocks):
            p_idx[k, :, j] = d * 2 * t_n + expo + t_n - 1
            re, im = d * 4 + (0 if w == 'b' else 2), d * 4 + (1 if w == 'b' else 3)
            w_idx[:, k, j] = {'re': (re, im + 8), 'im': (im, re), '-im': (im + 8, re + 8)}[part]
    pows = pows.reshape(2, n_l, n_g, 4 * t_n, SSM_STATE)
    pp = pows[:, :, :, p_idx, :].transpose(1, 2, 0, 3, 5, 4, 6)
    ww = signed[:, :, w_idx]
    last = 2 * t_n - 1
    pw = pows.reshape(2, n_l, n_g, 2, 2 * t_n, SSM_STATE)[..., last, :]
    dec = jnp.concatenate([pw[0, :, :, 0], pw[0, :, :, 1], pw[1, :, :, 0], pw[1, :, :, 1]], axis=-1)[:, :, None, :]
    return pp, ww, dec


def _s5_kernel(u_ref, pp_ref, ww_ref, dec_ref, y_ref, v_ref, xin_ref, *, batch, n_chunks, n_ctx_chunks):
    gps = SSM_GROUPS_PER_STEP
    ns = SSM_STATE
    ns2 = 2 * ns
    cw = SSM_CW
    nt = (((1,), (1,)), ((), ()))

    def table(g, k):
        lanes = lambda ref, a: jnp.concatenate([ref[g, a, k, j] for j in range(ref.shape[3])], axis=1)
        p_re, p_im = lanes(pp_ref, 0), lanes(pp_ref, 1)
        w_a, w_b = lanes(ww_ref, 0), lanes(ww_ref, 1)
        return jnp.concatenate([p_re[t:t + 1, :] * w_a + p_im[t:t + 1, :] * w_b for t in range(SSM_CHUNK)],
                               axis=0)

    for g in range(gps):
        v_ref[g] = jnp.dot(u_ref[g], table(g, 0).astype(BF16), preferred_element_type=F32)
    a_re = jnp.broadcast_to(dec_ref[:, :, 0:ns2], (gps, batch, ns2))
    a_im = jnp.broadcast_to(dec_ref[:, :, ns2:2 * ns2], (gps, batch, ns2))
    is_fwd = lax.broadcasted_iota(jnp.int32, (gps, batch, ns2), 2) < ns

    def step(k, carry):
        x_re, x_im = carry
        kb = jnp.where(k < n_ctx_chunks, n_ctx_chunks - 1 - k, n_chunks + n_ctx_chunks - 1 - k)
        rf = pl.multiple_of(k * batch, batch)
        rb = pl.multiple_of(kb * batch, batch)
        xin_ref[:, pl.ds(rf, batch), 0:ns] = x_re[:, :, 0:ns]
        xin_ref[:, pl.ds(rb, batch), ns:ns2] = x_re[:, :, ns:ns2]
        xin_ref[:, pl.ds(rf, batch), ns2:ns2 + ns] = x_im[:, :, 0:ns]
        xin_ref[:, pl.ds(rb, batch), ns2 + ns:2 * ns2] = x_im[:, :, ns:ns2]
        v_re = jnp.where(is_fwd, v_ref[:, pl.ds(rf, batch), 0:ns2], v_ref[:, pl.ds(rb, batch), 0:ns2])
        v_im = jnp.where(is_fwd, v_ref[:, pl.ds(rf, batch), ns2:2 * ns2], v_ref[:, pl.ds(rb, batch), ns2:2 * ns2])
        return a_re * x_re - a_im * x_im + v_re, a_re * x_im + a_im * x_re + v_im

    zero = jnp.zeros((gps, batch, ns2), F32)
    lax.fori_loop(0, n_chunks, step, (zero, zero))
    t_in = lax.broadcasted_iota(jnp.int32, (cw, cw), 0) // SSM_GROUP_DIM
    t_out = lax.broadcasted_iota(jnp.int32, (cw, cw), 1) // SSM_GROUP_DIM

    def split(x):
        hi = x.astype(BF16)
        return hi, (x - hi.astype(F32)).astype(BF16)

    def lag_kernel(e, ft, lanes):
        (eh, el), (fh, fl) = split(e[:, lanes]), split(ft[:, lanes])
        return (lax.dot_general(eh, fh, nt, preferred_element_type=F32)
                + lax.dot_general(eh, fl, nt, preferred_element_type=F32)
                + lax.dot_general(el, fh, nt, preferred_element_type=F32))

    for g in range(gps):
        e, ft = table(g, 1), table(g, 2)
        m = (jnp.where(t_out >= t_in, lag_kernel(e, ft, slice(0, ns2)), 0.0)
             + jnp.where(t_in >= t_out, lag_kernel(e, ft, slice(ns2, 2 * ns2)), 0.0)).astype(BF16)
        y_ref[g] = (jnp.dot(u_ref[g], m, preferred_element_type=F32)
                    + lax.dot_general(xin_ref[g].astype(BF16), table(g, 3).astype(BF16), nt,
                                      preferred_element_type=F32)).astype(BF16)


def _s5(u_t, operands, layer, batch, n_chunks, n_ctx_chunks):
    pp, ww, dec = operands
    gps = SSM_GROUPS_PER_STEP
    rows = n_chunks * batch
    kern = functools.partial(_s5_kernel, batch=batch, n_chunks=n_chunks, n_ctx_chunks=n_ctx_chunks)
    gspec = lambda r, c: pl.BlockSpec((gps, r, c), lambda i: (i, 0, 0))
    lspec = lambda shape: pl.BlockSpec((None, gps) + shape, lambda i: (layer, i) + (0,) * len(shape))
    return pl.pallas_call(
        kern,
        out_shape=jax.ShapeDtypeStruct((SSM_GROUPS, rows, SSM_CW), BF16),
        grid=(SSM_GROUPS // gps,),
        in_specs=[gspec(rows, SSM_CW), lspec(pp.shape[2:]), lspec(ww.shape[2:]), lspec(dec.shape[2:])],
        out_specs=gspec(rows, SSM_CW),
        scratch_shapes=[pltpu.VMEM((gps, rows, SSM_CW), F32), pltpu.VMEM((gps, rows, SSM_CW), F32)],
        compiler_params=_cparams(1),
        name="s5_scan",
    )(u_t, pp, ww, dec)


def _s5_unpack_kernel(y_ref, perm_ref, o_ref, ys_ref, tmp_ref, *, blk, batch):
    b = pl.program_id(1)

    @pl.when(b == 0)
    def _():
        half = SSM_CHUNK // 2
        cpb = blk // SSM_CHUNK
        rows_h = cpb * batch

        def slab(s, carry):
            lhs = jnp.concatenate(
                [jnp.concatenate([y_ref[s * SSM_PIECES + g, :, hh * LANES:(hh + 1) * LANES]
                                  for g in range(SSM_PIECES)], axis=1) for hh in range(2)], axis=0)
            tmp_ref[...] = jnp.dot(lhs, perm_ref[...], preferred_element_type=F32)
            for hh in range(2):
                for i in range(cpb):
                    r0 = hh * rows_h + i * batch
                    t0 = i * SSM_CHUNK + hh * half
                    for j in range(half):
                        ys_ref[s, (t0 + j) * batch:(t0 + j + 1) * batch, :] = (
                            tmp_ref[r0:r0 + batch, j * LANES:(j + 1) * LANES])
            return carry

        lax.fori_loop(0, SSM_SLABS, slab, 0)

    for s in range(SSM_SLABS):
        o_ref[:, s * LANES:(s + 1) * LANES] = ys_ref[s, pl.ds(b, blk, stride=batch), :].astype(BF16)


def _s5_unpack(tok, y_t, perm, blk, with_ctx):
    grid, row_map, _ = tok.pos_grid(blk, with_ctx)
    n_lat_blk = tok.seq // blk
    n_ctx_blk = tok.n_ctx // blk
    cpb = blk // SSM_CHUNK
    y_map = lambda p, b: (0, jnp.where(p < n_lat_blk, n_ctx_blk + p, p - n_lat_blk), 0)
    n_rows = tok.n_all if with_ctx else tok.n_lat
    kern = functools.partial(_s5_unpack_kernel, blk=blk, batch=tok.batch)
    return pl.pallas_call(
        kern,
        out_shape=jax.ShapeDtypeStruct((n_rows, D_MODEL), BF16),
        grid=grid,
        in_specs=[pl.BlockSpec((SSM_GROUPS, cpb * tok.batch, SSM_CW), y_map), _resident((D_MODEL, D_MODEL))],
        out_specs=pl.BlockSpec((blk, D_MODEL), row_map),
        scratch_shapes=[pltpu.VMEM((SSM_SLABS, tok.batch * blk, LANES), F32),
                        pltpu.VMEM((2 * cpb * tok.batch, D_MODEL), F32)],
        compiler_params=_cparams(2),
        name="s5_unpack",
    )(y_t, perm)


def _glu_ffn_kernel(x_ref, y_ref, d_ref, mod_ref, mpre_ref, mpost_ref, pre_ref, post_ref, wg_ref, w1_ref, w2_ref,
                    o_ref, acc_ref):
    x = x_ref[...]
    h = _norm_mod(x, mpre_ref[...], mod_ref[0, 1:2, :], mod_ref[0, 0:1, :])
    y = y_ref[...].astype(F32) + d_ref[...] * h
    z = jnp.dot(jax.nn.gelu(y).astype(BF16), wg_ref[...], preferred_element_type=F32)
    out = z[:, :D_MODEL] * jax.nn.sigmoid(z[:, D_MODEL:])
    x = x + mod_ref[0, 2:3, :] * _rms(out, mpost_ref[...])
    _ffn_body(x, mod_ref, pre_ref, post_ref, w1_ref, w2_ref, o_ref, acc_ref)


def _glu_ffn(tok, xa, n_rows, y_tok, d_skip, mod_l, mix_pre_g, mix_post_g, pre_g, post_g, glu_all, w1_all, w2_all,
             i, layer, tm):
    return pl.pallas_call(
        _glu_ffn_kernel,
        out_shape=jax.ShapeDtypeStruct((n_rows, D_MODEL), F32),
        grid=(n_rows // tm,),
        in_specs=[_row_spec(tm, D_MODEL), _row_spec(tm, D_MODEL), _vec_spec(D_MODEL), tok.mod_spec(tm),
                  _vec_spec(D_MODEL), _vec_spec(D_MODEL), _vec_spec(D_MODEL), _vec_spec(D_MODEL),
                  _layer_resident(i, (D_MODEL, 2 * D_MODEL)),
                  _layer_resident(layer, (D_MODEL, D_FF)), _layer_resident(layer, (D_FF, D_MODEL))],
        out_specs=_row_spec(tm, D_MODEL),
        scratch_shapes=[pltpu.VMEM((tm, D_MODEL), F32)],
        compiler_params=_cparams(1),
        name="glu_ffn",
    )(xa, y_tok, d_skip, mod_l, mix_pre_g, mix_post_g, pre_g, post_g, glu_all, w1_all, w2_all)


def _tile(limit, *sizes):
    tm = limit
    while any(s % tm for s in sizes):
        tm //= 2
    return tm


def kernel(x, c, ctx, c_ctx, mod_w, mod_b, mix_pre_g, mix_post_g, ffn_pre_g, ffn_post_g, ffn_w1, ffn_w2,
           even_w_in, even_w_out, even_sink, ssm_a_re, ssm_a_im, ssm_log_dt, ssm_b_re, ssm_b_im, ssm_c_re,
           ssm_c_im, ssm_d, ssm_glu_w):
    batch, seq, _ = x.shape
    n_ctx = ctx.shape[1]
    tok = _Tokens(batch, seq, n_ctx)
    assert seq % WINDOW == 0 and n_ctx % WINDOW == 0 and tok.n_lat % n_ctx == 0
    assert batch == SUBLANES
    tm = _tile(256, seq, n_ctx)
    tm_wide = _tile(512, seq, batch * n_ctx)

    x_lat, x_ctx = x.reshape(tok.n_lat, D_MODEL), ctx.astype(x.dtype).reshape(-1, D_MODEL)

    n_cond = 2 * SUBLANES
    cond = jnp.zeros((n_cond, D_MODEL), F32).at[:batch].set(c).at[batch].set(c_ctx)
    mod = _modulation(cond, mod_w, mod_b).reshape(DEPTH, n_cond, N_MOD, D_MODEL)

    rope = _rope_tables(seq, tm_wide)
    head_perm = _head_pair_perm()
    piece_perm = _piece_perm()
    dft_chan, dft_lat, dft_ctx = _chan_table(), _dft_tables(seq), _dft_tables(n_ctx)
    vec = lambda g: g.reshape(1, D_MODEL)
    w1_all, w2_all, glu_all = ffn_w1.astype(BF16), ffn_w2.astype(BF16), ssm_glu_w.astype(BF16)
    s5_operands = _s5_operands(ssm_a_re, ssm_a_im, ssm_log_dt, ssm_b_re, ssm_b_im, ssm_c_re, ssm_c_im)

    for layer in range(DEPTH):
        need_ctx = layer < DEPTH - 1
        n_rows = tok.n_all if need_ctx else tok.n_lat
        mod_l = mod[layer]
        i = layer // 2
        ffn_pre, ffn_post = vec(ffn_pre_g[layer]), vec(ffn_post_g[layer])
        if layer % 2 == 0:
            w_in = even_w_in[i]
            q0 = FOURIER_WIDTH
            w_in = jnp.concatenate([w_in[:, :q0], w_in[:, q0:q0 + ATTN_WIDTH][:, head_perm],
                                    w_in[:, q0 + ATTN_WIDTH:]], axis=1).astype(BF16)
            f, q, k, v = _inproj(tok, x_lat, x_ctx, mod_l, vec(mix_pre_g[layer]), w_in, rope, tm_wide)
            fm_lat, fm_ctx = _fourier(tok, f, dft_chan, dft_lat, dft_ctx)
            ao = _attention(tok, q, k, v, even_sink[i])
            w_out = even_w_out[i]
            w_f = w_out[:FOURIER_WIDTH].astype(BF16)
            w_a = w_out[FOURIER_WIDTH:][head_perm].astype(BF16)
            xa = _mix_ffn(tok, x_lat, x_ctx, fm_lat, fm_ctx, ao, mod_l, vec(mix_post_g[layer]), ffn_pre, ffn_post,
                          w_f, w_a, w1_all, w2_all, layer, tm_wide)
        else:
            pre_g = vec(mix_pre_g[layer])
            u_t = _s5_pre(tok, xa, mod_l, pre_g, piece_perm, tm)
            y_t = _s5(u_t, s5_operands, i, batch, (seq + n_ctx) // SSM_CHUNK, n_ctx // SSM_CHUNK)
            y_tok = _s5_unpack(tok, y_t, piece_perm, tm, need_ctx)
            xa = _glu_ffn(tok, xa, n_rows, y_tok, vec(ssm_d[i]), mod_l, pre_g, vec(mix_post_g[layer]),
                          ffn_pre, ffn_post, glu_all, w1_all, w2_all, i, layer, tm_wide)
        x_lat = x_ctx = xa
    return xa[:tok.n_lat].reshape(batch, seq, D_MODEL)
```

```python
import functools
import math

import numpy as np
import jax
import jax.numpy as jnp
from jax import lax
from jax.experimental import pallas as pl
from jax.experimental.pallas import tpu as pltpu

D_MODEL = 1024
DEPTH = 4
N_MOD = 6
EPS = 1e-6
NEG_INF = -1e30
GRID_W = 64

FOURIER_GROUPS = 4
FOURIER_GROUP_DIM = 128
FOURIER_WIDTH = FOURIER_GROUPS * FOURIER_GROUP_DIM

N_HEADS = 8
N_KV_HEADS = 2
HEAD_GROUP = N_HEADS // N_KV_HEADS
HEAD_DIM = 64
ATTN_WIDTH = N_HEADS * HEAD_DIM
KV_WIDTH = N_KV_HEADS * HEAD_DIM
WINDOW = 128
ROPE_AXIS_DIM = HEAD_DIM // 2
ROPE_BASE = 10000.0
LOG2_E = math.log2(math.e)
IN_WIDTH = FOURIER_WIDTH + ATTN_WIDTH + 2 * KV_WIDTH

LANES = 128
SUBLANES = 8
VMEM_LIMIT = 56 * 1024 * 1024

SSM_GROUP_DIM = 16
SSM_GROUPS = D_MODEL // SSM_GROUP_DIM
SSM_STATE = 64
SSM_CHUNK = 16
SSM_CW = SSM_CHUNK * SSM_GROUP_DIM
SSM_GROUPS_PER_STEP = 4
SSM_SLABS = D_MODEL // LANES
SSM_PIECES = LANES // SSM_GROUP_DIM

D_FF = 4 * D_MODEL

F32 = jnp.float32
BF16 = jnp.bfloat16


def _cparams(n_axes):
    return pltpu.CompilerParams(dimension_semantics=("arbitrary",) * n_axes, vmem_limit_bytes=VMEM_LIMIT)


def _resident(shape):
    nd = len(shape)
    return pl.BlockSpec(shape, lambda *_: (0,) * nd, pipeline_mode=pl.Buffered(1))


def _rms(x, g):
    return x * lax.rsqrt(jnp.mean(x * x, axis=-1, keepdims=True) + EPS) * g


def _norm_mod(x, g, sc, sh):
    return _rms(x, g) * (1.0 + sc) + sh


def _mod_kernel(cond_ref, w_ref, b_ref, o_ref):
    cond = cond_ref[...]
    s = cond * jax.nn.sigmoid(cond)
    s_hi = s.astype(BF16)
    s_lo = (s - s_hi.astype(F32)).astype(BF16)
    w = w_ref[0].astype(BF16)
    o_ref[0] = (jnp.dot(s_hi, w, preferred_element_type=F32) + jnp.dot(s_lo, w, preferred_element_type=F32)
                + b_ref[0])


def _modulation(cond, mod_w, mod_b):
    rows = cond.shape[0]
    tn = 2048
    n = N_MOD * D_MODEL
    return pl.pallas_call(
        _mod_kernel,
        out_shape=jax.ShapeDtypeStruct((DEPTH, rows, n), F32),
        grid=(DEPTH, n // tn),
        in_specs=[pl.BlockSpec((rows, D_MODEL), lambda l, j: (0, 0)),
                  pl.BlockSpec((1, D_MODEL, tn), lambda l, j: (l, 0, j)),
                  pl.BlockSpec((1, 1, tn), lambda l, j: (l, 0, j))],
        out_specs=pl.BlockSpec((1, rows, tn), lambda l, j: (l, 0, j)),
        compiler_params=_cparams(2),
        name="modulation",
    )(cond, mod_w, mod_b.reshape(DEPTH, 1, n))


class _Tokens:
    def __init__(self, batch, seq, n_ctx):
        self.batch, self.seq, self.n_ctx = batch, seq, n_ctx
        self.n_lat = batch * seq
        self.n_all = self.n_lat + batch * n_ctx

    def mod_spec(self, tm, layer):
        per_batch = self.seq // tm
        return pl.BlockSpec((None, 1, N_MOD, D_MODEL),
                            lambda i: (layer, jnp.minimum(i // per_batch, self.batch), 0, 0))

    def split_specs(self, tm, width, joined):
        nlt = self.n_lat // tm
        ctx_map = (lambda i: (jnp.maximum(i, nlt), 0)) if joined else (lambda i: (jnp.maximum(i - nlt, 0), 0))
        return pl.BlockSpec((tm, width), lambda i: (jnp.minimum(i, nlt - 1), 0)), pl.BlockSpec((tm, width), ctx_map)

    def pos_grid(self, blk, with_ctx, layer=0):
        n_lat_blk = self.seq // blk
        n_ctx_blk = self.n_ctx // blk
        lat_blocks = self.n_lat // blk
        row_map = lambda p, b: (jnp.where(p < n_lat_blk, b * n_lat_blk + p, lat_blocks + b * n_ctx_blk + (p - n_lat_blk)), 0)
        mod_map = lambda p, b: (layer, jnp.where(p < n_lat_blk, b, self.batch), 0, 0)
        grid = (n_lat_blk + (n_ctx_blk if with_ctx else 0), self.batch)
        return grid, row_map, mod_map


def _pick(n_lat_tiles, lat_ref, ctx_ref):
    return jnp.where(pl.program_id(0) < n_lat_tiles, lat_ref[...], ctx_ref[...])


def _layer_resident(layer, shape):
    nd = len(shape)
    return pl.BlockSpec((None,) + tuple(shape), lambda *_: (layer,) + (0,) * nd, pipeline_mode=pl.Buffered(1))


def _row_spec(tm, width):
    return pl.BlockSpec((tm, width), lambda i: (i, 0))


def _vec_spec(row):
    table, r = row
    return pl.BlockSpec((None, 1, table.shape[2]), lambda *_: (r, 0, 0))


FFN_CHUNK = 512


def _ffn_body(x, mod_ref, pre_ref, post_ref, w1_ref, w2_ref, o_ref, acc_ref):
    h = _norm_mod(x, pre_ref[...], mod_ref[0, 4:5, :], mod_ref[0, 3:4, :]).astype(BF16)
    for c in range(D_FF // FFN_CHUNK):
        sl = slice(c * FFN_CHUNK, (c + 1) * FFN_CHUNK)
        a = jnp.maximum(jnp.dot(h, w1_ref[:, sl], preferred_element_type=F32), 0.0)
        part = jnp.dot((a * a).astype(BF16), w2_ref[sl, :], preferred_element_type=F32)
        if c == 0:
            acc_ref[...] = part
        else:
            acc_ref[...] += part
    o_ref[...] = x + mod_ref[0, 5:6, :] * _rms(acc_ref[...], post_ref[...])


def _mix_ffn_kernel(xl_ref, xc_ref, fl_ref, fc_ref, ao_ref, mod_ref, mpost_ref, pre_ref, post_ref, wf_ref, wa_ref,
                    w1_ref, w2_ref, o_ref, acc_ref, *, n_lat_tiles):
    y = (jnp.dot(_pick(n_lat_tiles, fl_ref, fc_ref), wf_ref[...], preferred_element_type=F32)
         + jnp.dot(ao_ref[...], wa_ref[...], preferred_element_type=F32))
    x = _pick(n_lat_tiles, xl_ref, xc_ref) + mod_ref[0, 2:3, :] * _rms(y, mpost_ref[...])
    _ffn_body(x, mod_ref, pre_ref, post_ref, w1_ref, w2_ref, o_ref, acc_ref)


def _mix_ffn(tok, x_lat, x_ctx, fm_lat, fm_ctx, ao, mod, mix_post_g, pre_g, post_g, wf_all, wa_all, w1_all, w2_all,
             i, layer, tm):
    n = tok.n_all
    kern = functools.partial(_mix_ffn_kernel, n_lat_tiles=tok.n_lat // tm)
    return pl.pallas_call(
        kern,
        out_shape=jax.ShapeDtypeStruct((n, D_MODEL), F32),
        grid=(n // tm,),
        in_specs=[*tok.split_specs(tm, D_MODEL, x_lat is x_ctx), *tok.split_specs(tm, FOURIER_WIDTH, False),
                  _row_spec(tm, ATTN_WIDTH), tok.mod_spec(tm, layer),
                  _vec_spec(mix_post_g), _vec_spec(pre_g), _vec_spec(post_g),
                  _layer_resident(i, (FOURIER_WIDTH, D_MODEL)), _layer_resident(i, (ATTN_WIDTH, D_MODEL)),
                  _layer_resident(layer, (D_MODEL, D_FF)), _layer_resident(layer, (D_FF, D_MODEL))],
        out_specs=_row_spec(tm, D_MODEL),
        scratch_shapes=[pltpu.VMEM((tm, D_MODEL), F32)],
        compiler_params=_cparams(1),
        name="mix_ffn",
    )(x_lat, x_ctx, fm_lat, fm_ctx, ao, mod, mix_post_g[0], pre_g[0], post_g[0], wf_all, wa_all, w1_all, w2_all)


def _rope_block(x, cos, sin_hi, sin_lo):
    half = ROPE_AXIS_DIM // 2
    return (x * cos + pltpu.roll(x, half, axis=1) * sin_hi
            + pltpu.roll(x, LANES - half, axis=1) * sin_lo)


def _inproj_kernel(xl_ref, xc_ref, mod_ref, pre_ref, w_ref, cos_ref, shi_ref, slo_ref, f_ref, q_ref, k_ref, v_ref,
                   fs_ref, *, n_lat_tiles):
    x = _pick(n_lat_tiles, xl_ref, xc_ref)
    h = _norm_mod(x, pre_ref[...], mod_ref[0, 1:2, :], mod_ref[0, 0:1, :]).astype(BF16)
    p = jnp.dot(h, w_ref[...], preferred_element_type=F32)
    cos, shi, slo = cos_ref[...], shi_ref[...], slo_ref[...]
    pairs = x.shape[0] // 2
    n_slabs = FOURIER_WIDTH // LANES
    for s in range(n_slabs):
        fs_ref[s] = p[:, s * LANES:(s + 1) * LANES]
    for parity in range(2):
        for s in range(n_slabs):
            lo = parity * FOURIER_WIDTH + s * LANES
            f_ref[:, lo:lo + LANES] = fs_ref[s, pl.ds(parity, pairs, stride=2), :].astype(BF16)
    scale = HEAD_DIM ** -0.5 * LOG2_E
    for j in range(ATTN_WIDTH // LANES):
        lo = FOURIER_WIDTH + j * LANES
        q_ref[:, j * LANES:(j + 1) * LANES] = (_rope_block(p[:, lo:lo + LANES], cos, shi, slo) * scale).astype(BF16)
    k0 = FOURIER_WIDTH + ATTN_WIDTH
    k_ref[...] = _rope_block(p[:, k0:k0 + KV_WIDTH], cos, shi, slo).astype(BF16)
    v_ref[...] = p[:, k0 + KV_WIDTH:].astype(BF16)


def _rope_tables(seq, n_pad):
    pos = np.arange(seq)
    row = (pos // GRID_W).astype(np.float64)
    col = (pos % GRID_W).astype(np.float64)
    lane = np.arange(LANES)
    d = lane % HEAD_DIM
    j = d % (ROPE_AXIS_DIM // 2)
    inv = jnp.asarray(ROPE_BASE, F32) ** (-jnp.asarray(2 * j, F32) / ROPE_AXIS_DIM)
    use_col = jnp.asarray(d >= ROPE_AXIS_DIM)
    posv = jnp.where(use_col[None, :], jnp.asarray(col, F32)[:, None], jnp.asarray(row, F32)[:, None])
    ang = posv * inv[None, :]
    upper = jnp.asarray((d % ROPE_AXIS_DIM) >= ROPE_AXIS_DIM // 2)[None, :]
    cos, sin = jnp.cos(ang), jnp.sin(ang)
    sin_hi = jnp.where(upper, sin, 0.0)
    sin_lo = jnp.where(upper, 0.0, -sin)
    pad = lambda t, v: jnp.concatenate([t, jnp.full((n_pad, LANES), v, F32)], axis=0)
    return pad(cos, 1.0), pad(sin_hi, 0.0), pad(sin_lo, 0.0)


def _inproj(tok, x_lat, x_ctx, mod, pre_g, w_in_all, tables, i, layer, tm):
    per_batch = tok.seq // tm
    n_lat_tiles = tok.n_lat // tm
    tab_map = lambda i: (jnp.where(i < n_lat_tiles, i % per_batch, per_batch), 0)
    tab_spec = pl.BlockSpec((tm, LANES), tab_map)
    n = tok.n_all
    kern = functools.partial(_inproj_kernel, n_lat_tiles=n_lat_tiles)
    return pl.pallas_call(
        kern,
        out_shape=(jax.ShapeDtypeStruct((n // 2, 2 * FOURIER_WIDTH), BF16),
                   jax.ShapeDtypeStruct((n, ATTN_WIDTH), BF16),
                   jax.ShapeDtypeStruct((n, KV_WIDTH), BF16), jax.ShapeDtypeStruct((n, KV_WIDTH), BF16)),
        grid=(n // tm,),
        in_specs=[*tok.split_specs(tm, D_MODEL, x_lat is x_ctx), tok.mod_spec(tm, layer), _vec_spec(pre_g),
                  _layer_resident(i, (D_MODEL, IN_WIDTH)), tab_spec, tab_spec, tab_spec],
        out_specs=(_row_spec(tm // 2, 2 * FOURIER_WIDTH), _row_spec(tm, ATTN_WIDTH), _row_spec(tm, KV_WIDTH),
                   _row_spec(tm, KV_WIDTH)),
        scratch_shapes=[pltpu.VMEM((FOURIER_WIDTH // LANES, tm, LANES), F32)],
        compiler_params=_cparams(1),
        name="inproj",
    )(x_lat, x_ctx, mod, pre_g[0], w_in_all, *tables)


def _head_pair_perm():
    cols = []
    for i in range(HEAD_GROUP):
        for h in (i, i + HEAD_GROUP):
            cols.extend(range(h * HEAD_DIM, (h + 1) * HEAD_DIM))
    return np.asarray(cols)


def _dft_tables(length):
    half = length // 2
    k = np.arange(half)[:, None]
    m = np.arange(half)[None, :]

    def tab(n):
        ang = 2.0 * np.pi * ((k * n) % length) / length
        t = np.concatenate([np.cos(ang), -np.sin(ang)], axis=1) / math.sqrt(length)
        return jnp.asarray(t.astype(np.float32)).astype(BF16)

    return tab(2 * m), tab(2 * m + 1)


def _chan_table():
    n = FOURIER_GROUP_DIM
    k = np.arange(n)
    ang = 2.0 * np.pi * ((k[:, None] * k[None, :]) % n) / n
    t = np.concatenate([np.cos(ang), np.sin(ang)], axis=1) / math.sqrt(n)
    return jnp.asarray(t.astype(np.float32)).astype(BF16)


def _fourier_one(f_ref, chan_ref, pos_e_ref, pos_o_ref, o_ref, stk_ref):
    gd = FOURIER_GROUP_DIM
    half = f_ref.shape[0]
    row_chunk = min(half, 512)
    for parity in range(2):
        for g in range(FOURIER_GROUPS):
            lanes = slice(parity * FOURIER_WIDTH + g * gd, parity * FOURIER_WIDTH + (g + 1) * gd)
            z = jnp.dot(f_ref[:, lanes], chan_ref[...], preferred_element_type=F32)
            stk_ref[parity, 0:half, g * gd:(g + 1) * gd] = z[:, :gd].astype(BF16)
            stk_ref[parity, half:2 * half, g * gd:(g + 1) * gd] = z[:, gd:].astype(BF16)
    for r in range(half // row_chunk):
        rows = slice(r * row_chunk, (r + 1) * row_chunk)
        even = jnp.dot(pos_e_ref[rows, :], stk_ref[0], preferred_element_type=F32)
        odd = jnp.dot(pos_o_ref[rows, :], stk_ref[1], preferred_element_type=F32)
        o_ref[rows, :] = (even + odd).astype(BF16)
        o_ref[half + r * row_chunk:half + (r + 1) * row_chunk, :] = (even - odd).astype(BF16)


def _fourier_kernel(fl_ref, fc_ref, chan_ref, ple_ref, plo_ref, pce_ref, pco_ref, ol_ref, oc_ref, stkl_ref, stkc_ref):
    _fourier_one(fl_ref, chan_ref, ple_ref, plo_ref, ol_ref, stkl_ref)
    _fourier_one(fc_ref, chan_ref, pce_ref, pco_ref, oc_ref, stkc_ref)


def _fourier(tok, f_pairs, chan, tabs_lat, tabs_ctx):
    ctx0 = tok.n_lat // tok.n_ctx
    pair_blk = lambda rows, m: pl.BlockSpec((rows // 2, 2 * FOURIER_WIDTH), m)
    lat_blk = pl.BlockSpec((tok.seq, FOURIER_WIDTH), lambda b: (b, 0))
    ctx_blk = pl.BlockSpec((tok.n_ctx, FOURIER_WIDTH), lambda b: (b, 0))
    tabs = (*tabs_lat, *tabs_ctx)
    return pl.pallas_call(
        _fourier_kernel,
        out_shape=(jax.ShapeDtypeStruct((tok.n_lat, FOURIER_WIDTH), BF16),
                   jax.ShapeDtypeStruct((tok.batch * tok.n_ctx, FOURIER_WIDTH), BF16)),
        grid=(tok.batch,),
        in_specs=[pair_blk(tok.seq, lambda b: (b, 0)), pair_blk(tok.n_ctx, lambda b: (ctx0 + b, 0)),
                  _resident(chan.shape), *[_resident(t.shape) for t in tabs]],
        out_specs=(lat_blk, ctx_blk),
        scratch_shapes=[pltpu.VMEM((2, tok.seq, FOURIER_WIDTH), BF16),
                        pltpu.VMEM((2, tok.n_ctx, FOURIER_WIDTH), BF16)],
        compiler_params=_cparams(1),
        name="fourier",
    )(f_pairs, f_pairs, chan, *tabs)


def _attn_kernel(sink_ref, q_ref, kp_ref, kc_ref, kn_ref, kx_ref, vp_ref, vc_ref, vn_ref, vx_ref, o_ref,
                 *, n_qblk, layer_idx):
    j = pl.program_id(1)
    blk = WINDOW
    n_loc = 3 * blk
    rows = HEAD_GROUP * blk
    nt = (((1,), (1,)), ((), ()))

    def attend(with_local):
        q = jnp.concatenate([q_ref[:, i * LANES:(i + 1) * LANES] for i in range(HEAD_GROUP)], axis=0)
        k_ctx, v_ctx = kx_ref[...], vx_ref[...]
        if with_local:
            k_loc = jnp.concatenate([kp_ref[...], kc_ref[...], kn_ref[...]], axis=0)
            v_loc = jnp.concatenate([vp_ref[...], vc_ref[...], vn_ref[...]], axis=0)
            col_lo = jnp.where(j >= 1, 0, blk)
            col_hi = jnp.where(j + 1 < n_qblk, n_loc, 2 * blk)
            qi = lax.broadcasted_iota(jnp.int32, (blk, 1), 0)
            kj = lax.broadcasted_iota(jnp.int32, (blk, n_loc), 1)
            visible = (kj >= jnp.maximum(qi, col_lo)) & (kj <= jnp.minimum(qi + 2 * WINDOW, col_hi - 1))
            bias = jnp.concatenate([jnp.where(visible, 0.0, NEG_INF)] * HEAD_GROUP, axis=0)
        row_blk = lax.broadcasted_iota(jnp.int32, (rows, 1), 0) // blk
        lane_o = lax.broadcasted_iota(jnp.int32, (rows, LANES), 1)
        outs = []
        for kvh in range(N_KV_HEADS):
            def own_lanes(t):
                lane = lax.broadcasted_iota(jnp.int32, t.shape, 1)
                return jnp.where((lane >= kvh * HEAD_DIM) & (lane < (kvh + 1) * HEAD_DIM), t, jnp.zeros_like(t))
            s_ctx = lax.dot_general(q, own_lanes(k_ctx), nt, preferred_element_type=F32)
            sink = jnp.zeros((rows, 1), F32)
            for i in range(HEAD_GROUP):
                sink = jnp.where(row_blk == i, sink_ref[layer_idx, kvh * HEAD_GROUP + i] * LOG2_E, sink)
            m = jnp.maximum(jnp.max(s_ctx, axis=-1, keepdims=True), sink)
            if with_local:
                s_loc = lax.dot_general(q, own_lanes(k_loc), nt, preferred_element_type=F32) + bias
                m = jnp.maximum(m, jnp.max(s_loc, axis=-1, keepdims=True))
            p_ctx = jnp.exp2(s_ctx - m)
            denom = jnp.sum(p_ctx, axis=-1, keepdims=True) + jnp.exp2(sink - m)
            o = jnp.dot(p_ctx.astype(BF16), v_ctx, preferred_element_type=F32)
            if with_local:
                p_loc = jnp.exp2(s_loc - m)
                denom = denom + jnp.sum(p_loc, axis=-1, keepdims=True)
                o = o + jnp.dot(p_loc.astype(BF16), v_loc, preferred_element_type=F32)
            outs.append(o / denom)
        merged = jnp.where(lane_o < HEAD_DIM, outs[0], outs[1])
        for i in range(HEAD_GROUP):
            o_ref[:, i * LANES:(i + 1) * LANES] = merged[i * blk:(i + 1) * blk, :].astype(BF16)

    pl.when(j < n_qblk)(lambda: attend(True))
    pl.when(j >= n_qblk)(lambda: attend(False))


def _attention(tok, q, k, v, sink_all, layer_idx):
    blk = WINDOW
    n_qblk = tok.seq // blk
    n_cblk = tok.n_ctx // blk
    lat_blocks = tok.n_lat // blk
    ctx0 = tok.n_lat // tok.n_ctx

    def q_map(b, j):
        return (jnp.where(j < n_qblk, b * n_qblk + j, lat_blocks + b * n_cblk + (j - n_qblk)), 0)

    def k_map(off):
        return lambda b, j: (b * n_qblk + jnp.clip(j + off, 0, n_qblk - 1), 0)

    ctx_map = lambda b, j: (ctx0 + b, 0)
    kv_blk = lambda m: pl.BlockSpec((blk, KV_WIDTH), m)
    ctx_blk = pl.BlockSpec((tok.n_ctx, KV_WIDTH), ctx_map)
    kern = functools.partial(_attn_kernel, n_qblk=n_qblk, layer_idx=layer_idx)
    return pl.pallas_call(
        kern,
        out_shape=jax.ShapeDtypeStruct((tok.n_all, ATTN_WIDTH), BF16),
        grid=(tok.batch, n_qblk + n_cblk),
        in_specs=[pl.BlockSpec(memory_space=pltpu.SMEM),
                  pl.BlockSpec((blk, ATTN_WIDTH), q_map),
                  kv_blk(k_map(-1)), kv_blk(k_map(0)), kv_blk(k_map(1)), ctx_blk,
                  kv_blk(k_map(-1)), kv_blk(k_map(0)), kv_blk(k_map(1)), ctx_blk],
        out_specs=pl.BlockSpec((blk, ATTN_WIDTH), q_map),
        compiler_params=_cparams(2),
        name="window_attention",
    )(sink_all, q, k, k, k, k, v, v, v, v)


def _piece_perm():
    idx = np.arange(D_MODEL)
    a, b, c = idx // LANES, (idx // SSM_GROUP_DIM) % SSM_PIECES, idx % SSM_GROUP_DIM
    perm = np.zeros((D_MODEL, D_MODEL), np.float32)
    perm[idx, b * LANES + a * SSM_GROUP_DIM + c] = 1.0
    return jnp.asarray(perm).astype(BF16)


def _slab_pitch(blk):
    return blk + SUBLANES


def _s5_pre_kernel(x_ref, mod_ref, pre_ref, perm_ref, u_ref, hs_ref, *, blk, batch, n_blocks):
    p, b = pl.program_id(0), pl.program_id(1)
    pitch = _slab_pitch(blk)

    @pl.when(p < n_blocks)
    def _():
        h = _norm_mod(x_ref[...], pre_ref[...], mod_ref[0, 1:2, :], mod_ref[0, 0:1, :])
        row0 = pl.multiple_of(b * pitch, SUBLANES)
        slot0 = (p % 2) * SSM_SLABS
        for s in range(SSM_SLABS):
            hs_ref[slot0 + s, pl.ds(row0, blk), :] = h[:, s * LANES:(s + 1) * LANES]

    @pl.when(p >= 1)
    def _():
        half = SSM_CHUNK // 2
        cpb = blk // SSM_CHUNK
        rows_h = cpb * batch
        s = b
        src = ((p - 1) % 2) * SSM_SLABS + s
        lhs = []
        for hh in range(2):
            for i in range(cpb):
                t0 = i * SSM_CHUNK + hh * half
                lhs.append(jnp.concatenate(
                    [hs_ref[src, pl.ds(t0 + j, batch, stride=pitch), :] for j in range(half)], axis=1))
        lhs = jnp.concatenate(lhs, axis=0).astype(BF16)
        out = jnp.dot(lhs, perm_ref[...], preferred_element_type=F32).astype(BF16)
        for hh in range(2):
            for g in range(SSM_PIECES):
                u_ref[s * SSM_PIECES + g, :, hh * LANES:(hh + 1) * LANES] = (
                    out[hh * rows_h:(hh + 1) * rows_h, g * LANES:(g + 1) * LANES])


def _s5_pre(tok, xa, mod, pre_g, perm, layer, blk):
    assert tok.batch == SSM_SLABS
    (n_blocks, _), row_map, mod_map = tok.pos_grid(blk, True, layer)
    n_lat_blk = tok.seq // blk
    n_ctx_blk = tok.n_ctx // blk
    cpb = blk // SSM_CHUNK
    n_chunks = (tok.seq + tok.n_ctx) // SSM_CHUNK
    last = n_blocks - 1

    def u_map(p, b):
        q = jnp.maximum(p - 1, 0)
        return (0, jnp.where(q < n_lat_blk, n_ctx_blk + q, q - n_lat_blk), 0)

    clamp = lambda m: (lambda p, b: m(jnp.minimum(p, last), b))
    kern = functools.partial(_s5_pre_kernel, blk=blk, batch=tok.batch, n_blocks=n_blocks)
    return pl.pallas_call(
        kern,
        out_shape=jax.ShapeDtypeStruct((SSM_GROUPS, n_chunks * tok.batch, SSM_CW), BF16),
        grid=(n_blocks + 1, tok.batch),
        in_specs=[pl.BlockSpec((blk, D_MODEL), clamp(row_map)),
                  pl.BlockSpec((None, 1, N_MOD, D_MODEL), clamp(mod_map)),
                  _vec_spec(pre_g), _resident((D_MODEL, D_MODEL))],
        out_specs=pl.BlockSpec((SSM_GROUPS, cpb * tok.batch, SSM_CW), u_map),
        scratch_shapes=[pltpu.VMEM((2 * SSM_SLABS, tok.batch * _slab_pitch(blk), LANES), F32)],
        compiler_params=_cparams(2),
        name="s5_pre",
    )(xa, mod, pre_g[0], perm)


def _cmul(ar, ai, br, bi):
    return ar * br - ai * bi, ar * bi + ai * br


def _s5_table_plan():
    t_n = SSM_CHUNK
    t = np.arange(t_n)
    return [
        [(0, t_n - 1 - t, 'b', 're'), (1, t, 'b', 're'), (0, t_n - 1 - t, 'b', 'im'), (1, t, 'b', 'im')],
        [(0, -t, 'b', 're'), (0, -t, 'b', 'im'), (1, t, 'b', 're'), (1, t, 'b', 'im')],
        [(0, t, 'c', 're'), (0, t, 'c', '-im'), (1, -t, 'c', 're'), (1, -t, 'c', '-im')],
        [(0, t + 1, 'c', 're'), (1, t_n - t, 'c', 're'), (0, t + 1, 'c', '-im'), (1, t_n - t, 'c', '-im')],
    ]


def _s5_operands(a_re, a_im, log_dt, b_re, b_im, c_re, c_im):
    t_n = SSM_CHUNK
    per_group = lambda x: jnp.swapaxes(x.astype(F32), 1, 2)
    l_re, l_im = per_group(a_re), per_group(a_im)
    dt = jnp.exp(per_group(log_dt))[..., None]
    z_re, z_im = l_re * dt, l_im * dt
    n_all = jnp.arange(1 - t_n, t_n + 1, dtype=F32)[:, None]
    mag = jnp.exp(z_re[..., None, :] * n_all)
    ang = z_im[..., None, :] * n_all
    pows = jnp.stack([mag * jnp.cos(ang), mag * jnp.sin(ang)], axis=2)
    a1_re, a1_im = pows[:, :, 0, :, t_n, :], pows[:, :, 1, :, t_n, :]
    den = l_re * l_re + l_im * l_im
    r_re = ((a1_re - 1.0) * l_re + a1_im * l_im) / den
    r_im = (a1_im * l_re - (a1_re - 1.0) * l_im) / den
    bb_re, bb_im = _cmul(r_re[..., None], r_im[..., None], per_group(b_re), per_group(b_im))
    base = jnp.stack([jnp.swapaxes(bb_re, -1, -2), jnp.swapaxes(bb_im, -1, -2), per_group(c_re), per_group(c_im)],
                     axis=3)
    n_l, n_g = base.shape[0], base.shape[1]
    base = base.reshape(n_l, n_g, 8, SSM_GROUP_DIM, SSM_STATE)
    signed = jnp.concatenate([base, -base], axis=2)

    plan = _s5_table_plan()
    p_idx = np.zeros((len(plan), 4, t_n), np.int32)
    w_idx = np.zeros((2, len(plan), 4), np.int32)
    for k, blocks in enumerate(plan):
        for j, (d, expo, w, part) in enumerate(blocks):
            p_idx[k, j, :] = d * 2 * t_n + expo + t_n - 1
            re, im = d * 4 + (0 if w == 'b' else 2), d * 4 + (1 if w == 'b' else 3)
            w_idx[:, k, j] = {'re': (re, im + 8), 'im': (im, re), '-im': (im + 8, re + 8)}[part]
    last = 2 * t_n - 1
    pw = pows[:, :, :, :, last, :]
    dec = jnp.concatenate([pw[:, :, 0, 0], pw[:, :, 0, 1], pw[:, :, 1, 0], pw[:, :, 1, 1]], axis=-1)[:, :, None, :]
    pp = pows.reshape(n_l, n_g, 2, 4 * t_n, SSM_STATE)[:, :, :, p_idx, :]
    ww = signed[:, :, w_idx]
    return pp, ww, dec


def _s5_kernel(u_ref, pp_ref, ww_ref, dec_ref, y_ref, v_ref, xin_ref, *, batch, n_chunks, n_ctx_chunks):
    gps = SSM_GROUPS_PER_STEP
    ns = SSM_STATE
    ns2 = 2 * ns
    cw = SSM_CW
    nt = (((1,), (1,)), ((), ()))

    def table(g, k):
        lanes = lambda ref, a: jnp.concatenate([ref[g, a, k, j] for j in range(ref.shape[3])], axis=1)
        p_re, p_im = lanes(pp_ref, 0), lanes(pp_ref, 1)
        w_a, w_b = lanes(ww_ref, 0), lanes(ww_ref, 1)
        return jnp.concatenate([p_re[t:t + 1, :] * w_a + p_im[t:t + 1, :] * w_b for t in range(SSM_CHUNK)],
                               axis=0)

    for g in range(gps):
        v_ref[g] = jnp.dot(u_ref[g], table(g, 0).astype(BF16), preferred_element_type=F32)
    a_re = jnp.broadcast_to(dec_ref[:, :, 0:ns2], (gps, batch, ns2))
    a_im = jnp.broadcast_to(dec_ref[:, :, ns2:2 * ns2], (gps, batch, ns2))
    is_fwd = lax.broadcasted_iota(jnp.int32, (gps, batch, ns2), 2) < ns

    def step(k, carry):
        x_re, x_im = carry
        kb = jnp.where(k < n_ctx_chunks, n_ctx_chunks - 1 - k, n_chunks + n_ctx_chunks - 1 - k)
        rf = pl.multiple_of(k * batch, batch)
        rb = pl.multiple_of(kb * batch, batch)
        xin_ref[:, pl.ds(rf, batch), 0:ns] = x_re[:, :, 0:ns]
        xin_ref[:, pl.ds(rb, batch), ns:ns2] = x_re[:, :, ns:ns2]
        xin_ref[:, pl.ds(rf, batch), ns2:ns2 + ns] = x_im[:, :, 0:ns]
        xin_ref[:, pl.ds(rb, batch), ns2 + ns:2 * ns2] = x_im[:, :, ns:ns2]
        v_re = jnp.where(is_fwd, v_ref[:, pl.ds(rf, batch), 0:ns2], v_ref[:, pl.ds(rb, batch), 0:ns2])
        v_im = jnp.where(is_fwd, v_ref[:, pl.ds(rf, batch), ns2:2 * ns2], v_ref[:, pl.ds(rb, batch), ns2:2 * ns2])
        return a_re * x_re - a_im * x_im + v_re, a_re * x_im + a_im * x_re + v_im

    zero = jnp.zeros((gps, batch, ns2), F32)
    lax.fori_loop(0, n_chunks, step, (zero, zero))
    t_in = lax.broadcasted_iota(jnp.int32, (cw, cw), 0) // SSM_GROUP_DIM
    t_out = lax.broadcasted_iota(jnp.int32, (cw, cw), 1) // SSM_GROUP_DIM

    def split(x):
        hi = x.astype(BF16)
        return hi, (x - hi.astype(F32)).astype(BF16)

    def lag_kernel(e, ft, lanes):
        (eh, el), (fh, fl) = split(e[:, lanes]), split(ft[:, lanes])
        return (lax.dot_general(eh, fh, nt, preferred_element_type=F32)
                + lax.dot_general(eh, fl, nt, preferred_element_type=F32)
                + lax.dot_general(el, fh, nt, preferred_element_type=F32))

    for g in range(gps):
        e, ft = table(g, 1), table(g, 2)
        m = (jnp.where(t_out >= t_in, lag_kernel(e, ft, slice(0, ns2)), 0.0)
             + jnp.where(t_in >= t_out, lag_kernel(e, ft, slice(ns2, 2 * ns2)), 0.0)).astype(BF16)
        y_ref[g] = (jnp.dot(u_ref[g], m, preferred_element_type=F32)
                    + lax.dot_general(xin_ref[g].astype(BF16), table(g, 3).astype(BF16), nt,
                                      preferred_element_type=F32)).astype(BF16)


def _s5(u_t, operands, layer, batch, n_chunks, n_ctx_chunks):
    pp, ww, dec = operands
    gps = SSM_GROUPS_PER_STEP
    rows = n_chunks * batch
    kern = functools.partial(_s5_kernel, batch=batch, n_chunks=n_chunks, n_ctx_chunks=n_ctx_chunks)
    gspec = lambda r, c: pl.BlockSpec((gps, r, c), lambda i: (i, 0, 0))
    lspec = lambda shape: pl.BlockSpec((None, gps) + shape, lambda i: (layer, i) + (0,) * len(shape))
    return pl.pallas_call(
        kern,
        out_shape=jax.ShapeDtypeStruct((SSM_GROUPS, rows, SSM_CW), BF16),
        grid=(SSM_GROUPS // gps,),
        in_specs=[gspec(rows, SSM_CW), lspec(pp.shape[2:]), lspec(ww.shape[2:]), lspec(dec.shape[2:])],
        out_specs=gspec(rows, SSM_CW),
        scratch_shapes=[pltpu.VMEM((gps, rows, SSM_CW), F32), pltpu.VMEM((gps, rows, SSM_CW), F32)],
        compiler_params=_cparams(1),
        name="s5_scan",
    )(u_t, pp, ww, dec)


def _s5_unpack_kernel(y_ref, perm_ref, o_ref, ys_ref, tmp_ref, *, blk, batch):
    b = pl.program_id(1)

    @pl.when(b == 0)
    def _():
        half = SSM_CHUNK // 2
        cpb = blk // SSM_CHUNK
        rows_h = cpb * batch

        def slab(s, carry):
            lhs = jnp.concatenate(
                [jnp.concatenate([y_ref[s * SSM_PIECES + g, :, hh * LANES:(hh + 1) * LANES]
                                  for g in range(SSM_PIECES)], axis=1) for hh in range(2)], axis=0)
            tmp_ref[...] = jnp.dot(lhs, perm_ref[...], preferred_element_type=F32)
            for hh in range(2):
                for i in range(cpb):
                    r0 = hh * rows_h + i * batch
                    t0 = i * SSM_CHUNK + hh * half
                    for j in range(half):
                        ys_ref[s, (t0 + j) * batch:(t0 + j + 1) * batch, :] = (
                            tmp_ref[r0:r0 + batch, j * LANES:(j + 1) * LANES])
            return carry

        lax.fori_loop(0, SSM_SLABS, slab, 0)

    for s in range(SSM_SLABS):
        o_ref[:, s * LANES:(s + 1) * LANES] = ys_ref[s, pl.ds(b, blk, stride=batch), :].astype(BF16)


def _s5_unpack(tok, y_t, perm, blk, with_ctx):
    grid, row_map, _ = tok.pos_grid(blk, with_ctx)
    n_lat_blk = tok.seq // blk
    n_ctx_blk = tok.n_ctx // blk
    cpb = blk // SSM_CHUNK
    y_map = lambda p, b: (0, jnp.where(p < n_lat_blk, n_ctx_blk + p, p - n_lat_blk), 0)
    n_rows = tok.n_all if with_ctx else tok.n_lat
    kern = functools.partial(_s5_unpack_kernel, blk=blk, batch=tok.batch)
    return pl.pallas_call(
        kern,
        out_shape=jax.ShapeDtypeStruct((n_rows, D_MODEL), BF16),
        grid=grid,
        in_specs=[pl.BlockSpec((SSM_GROUPS, cpb * tok.batch, SSM_CW), y_map), _resident((D_MODEL, D_MODEL))],
        out_specs=pl.BlockSpec((blk, D_MODEL), row_map),
        scratch_shapes=[pltpu.VMEM((SSM_SLABS, tok.batch * blk, LANES), F32),
                        pltpu.VMEM((2 * cpb * tok.batch, D_MODEL), F32)],
        compiler_params=_cparams(2),
        name="s5_unpack",
    )(y_t, perm)


def _glu_ffn_kernel(x_ref, y_ref, d_ref, mod_ref, mpre_ref, mpost_ref, pre_ref, post_ref, wg_ref, w1_ref, w2_ref,
                    o_ref, acc_ref):
    x = x_ref[...]
    h = _norm_mod(x, mpre_ref[...], mod_ref[0, 1:2, :], mod_ref[0, 0:1, :])
    y = y_ref[...].astype(F32) + d_ref[...] * h
    z = jnp.dot(jax.nn.gelu(y).astype(BF16), wg_ref[...], preferred_element_type=F32)
    out = z[:, :D_MODEL] * jax.nn.sigmoid(z[:, D_MODEL:])
    x = x + mod_ref[0, 2:3, :] * _rms(out, mpost_ref[...])
    _ffn_body(x, mod_ref, pre_ref, post_ref, w1_ref, w2_ref, o_ref, acc_ref)


def _glu_ffn(tok, xa, n_rows, y_tok, d_skip, mod, mix_pre_g, mix_post_g, pre_g, post_g, glu_all, w1_all, w2_all,
             i, layer, tm):
    return pl.pallas_call(
        _glu_ffn_kernel,
        out_shape=jax.ShapeDtypeStruct((n_rows, D_MODEL), F32),
        grid=(n_rows // tm,),
        in_specs=[_row_spec(tm, D_MODEL), _row_spec(tm, D_MODEL), _vec_spec(d_skip), tok.mod_spec(tm, layer),
                  _vec_spec(mix_pre_g), _vec_spec(mix_post_g), _vec_spec(pre_g), _vec_spec(post_g),
                  _layer_resident(i, (D_MODEL, 2 * D_MODEL)),
                  _layer_resident(layer, (D_MODEL, D_FF)), _layer_resident(layer, (D_FF, D_MODEL))],
        out_specs=_row_spec(tm, D_MODEL),
        scratch_shapes=[pltpu.VMEM((tm, D_MODEL), F32)],
        compiler_params=_cparams(1),
        name="glu_ffn",
    )(xa, y_tok, d_skip[0], mod, mix_pre_g[0], mix_post_g[0], pre_g[0], post_g[0], glu_all, w1_all, w2_all)


def _tile(limit, *sizes):
    tm = limit
    while any(s % tm for s in sizes):
        tm //= 2
    return tm


def kernel(x, c, ctx, c_ctx, mod_w, mod_b, mix_pre_g, mix_post_g, ffn_pre_g, ffn_post_g, ffn_w1, ffn_w2,
           even_w_in, even_w_out, even_sink, ssm_a_re, ssm_a_im, ssm_log_dt, ssm_b_re, ssm_b_im, ssm_c_re,
           ssm_c_im, ssm_d, ssm_glu_w):
    batch, seq, _ = x.shape
    n_ctx = ctx.shape[1]
    tok = _Tokens(batch, seq, n_ctx)
    assert seq % WINDOW == 0 and n_ctx % WINDOW == 0 and tok.n_lat % n_ctx == 0
    assert batch == SUBLANES
    tm = _tile(256, seq, n_ctx)
    tm_wide = _tile(512, seq, batch * n_ctx)

    x_lat, x_ctx = x.reshape(tok.n_lat, D_MODEL), ctx.astype(x.dtype).reshape(-1, D_MODEL)

    n_cond = 2 * SUBLANES
    cond = jnp.zeros((n_cond, D_MODEL), F32).at[:batch].set(c).at[batch].set(c_ctx)
    mod = _modulation(cond, mod_w, mod_b).reshape(DEPTH, n_cond, N_MOD, D_MODEL)

    rope = _rope_tables(seq, tm_wide)
    head_perm = _head_pair_perm()
    piece_perm = _piece_perm()
    dft_chan, dft_lat, dft_ctx = _chan_table(), _dft_tables(seq), _dft_tables(n_ctx)
    rows3 = lambda t: t.reshape(t.shape[0], 1, t.shape[1])
    w1_all, w2_all, glu_all = ffn_w1.astype(BF16), ffn_w2.astype(BF16), ssm_glu_w.astype(BF16)
    q0, k0 = FOURIER_WIDTH, FOURIER_WIDTH + ATTN_WIDTH
    w_in_all = jnp.concatenate([even_w_in[:, :, :q0], even_w_in[:, :, q0:k0][:, :, head_perm],
                                even_w_in[:, :, k0:]], axis=2).astype(BF16)
    wf_all = even_w_out[:, :FOURIER_WIDTH].astype(BF16)
    wa_all = even_w_out[:, FOURIER_WIDTH:][:, head_perm].astype(BF16)
    s5_operands = _s5_operands(ssm_a_re, ssm_a_im, ssm_log_dt, ssm_b_re, ssm_b_im, ssm_c_re, ssm_c_im)

    for layer in range(DEPTH):
        need_ctx = layer < DEPTH - 1
        n_rows = tok.n_all if need_ctx else tok.n_lat
        i = layer // 2
        mix_pre, mix_post = (rows3(mix_pre_g), layer), (rows3(mix_post_g), layer)
        ffn_pre, ffn_post = (rows3(ffn_pre_g), layer), (rows3(ffn_post_g), layer)
        if layer % 2 == 0:
            f, q, k, v = _inproj(tok, x_lat, x_ctx, mod, mix_pre, w_in_all, rope, i, layer, tm_wide)
            fm_lat, fm_ctx = _fourier(tok, f, dft_chan, dft_lat, dft_ctx)
            ao = _attention(tok, q, k, v, even_sink, i)
            xa = _mix_ffn(tok, x_lat, x_ctx, fm_lat, fm_ctx, ao, mod, mix_post, ffn_pre, ffn_post,
                          wf_all, wa_all, w1_all, w2_all, i, layer, tm_wide)
        else:
            u_t = _s5_pre(tok, xa, mod, mix_pre, piece_perm, layer, tm)
            y_t = _s5(u_t, s5_operands, i, batch, (seq + n_ctx) // SSM_CHUNK, n_ctx // SSM_CHUNK)
            y_tok = _s5_unpack(tok, y_t, piece_perm, tm, need_ctx)
            xa = _glu_ffn(tok, xa, n_rows, y_tok, (rows3(ssm_d), i), mod, mix_pre, mix_post, ffn_pre, ffn_post,
                          glu_all, w1_all, w2_all, i, layer, tm_wide)
        x_lat = x_ctx = xa
    return xa[:tok.n_lat].reshape(batch, seq, D_MODEL)
```

```python
import functools
import math

import numpy as np
import jax
import jax.numpy as jnp
from jax import lax
from jax.experimental import pallas as pl
from jax.experimental.pallas import tpu as pltpu

D_MODEL = 1024
DEPTH = 4
N_MOD = 6
EPS = 1e-6
NEG_INF = -1e30
GRID_W = 64

FOURIER_GROUPS = 4
FOURIER_GROUP_DIM = 128
FOURIER_WIDTH = FOURIER_GROUPS * FOURIER_GROUP_DIM

N_HEADS = 8
N_KV_HEADS = 2
HEAD_GROUP = N_HEADS // N_KV_HEADS
HEAD_DIM = 64
ATTN_WIDTH = N_HEADS * HEAD_DIM
KV_WIDTH = N_KV_HEADS * HEAD_DIM
WINDOW = 128
ROPE_AXIS_DIM = HEAD_DIM // 2
ROPE_BASE = 10000.0
LOG2_E = math.log2(math.e)
IN_WIDTH = FOURIER_WIDTH + ATTN_WIDTH + 2 * KV_WIDTH

LANES = 128
SUBLANES = 8
VMEM_LIMIT = 56 * 1024 * 1024

SSM_GROUP_DIM = 16
SSM_GROUPS = D_MODEL // SSM_GROUP_DIM
SSM_STATE = 64
SSM_CHUNK = 16
SSM_CW = SSM_CHUNK * SSM_GROUP_DIM
SSM_GROUPS_PER_STEP = 4
SSM_SLABS = D_MODEL // LANES
SSM_PIECES = LANES // SSM_GROUP_DIM

D_FF = 4 * D_MODEL

F32 = jnp.float32
BF16 = jnp.bfloat16


def _cparams(n_axes):
    return pltpu.CompilerParams(dimension_semantics=("arbitrary",) * n_axes, vmem_limit_bytes=VMEM_LIMIT)


def _resident(shape):
    nd = len(shape)
    return pl.BlockSpec(shape, lambda *_: (0,) * nd, pipeline_mode=pl.Buffered(1))


def _rms(x, g):
    return x * lax.rsqrt(jnp.mean(x * x, axis=-1, keepdims=True) + EPS) * g


def _norm_mod(x, g, sc, sh):
    return _rms(x, g) * (1.0 + sc) + sh


def _mod_kernel(cond_ref, w_ref, b_ref, o_ref):
    cond = cond_ref[...]
    s = cond * jax.nn.sigmoid(cond)
    s_hi = s.astype(BF16)
    s_lo = (s - s_hi.astype(F32)).astype(BF16)
    w = w_ref[0].astype(BF16)
    o_ref[0] = (jnp.dot(s_hi, w, preferred_element_type=F32) + jnp.dot(s_lo, w, preferred_element_type=F32)
                + b_ref[0])


def _modulation(cond, mod_w, mod_b):
    rows = cond.shape[0]
    tn = 2048
    n = N_MOD * D_MODEL
    return pl.pallas_call(
        _mod_kernel,
        out_shape=jax.ShapeDtypeStruct((DEPTH, rows, n), F32),
        grid=(DEPTH, n // tn),
        in_specs=[pl.BlockSpec((rows, D_MODEL), lambda l, j: (0, 0)),
                  pl.BlockSpec((1, D_MODEL, tn), lambda l, j: (l, 0, j)),
                  pl.BlockSpec((1, 1, tn), lambda l, j: (l, 0, j))],
        out_specs=pl.BlockSpec((1, rows, tn), lambda l, j: (l, 0, j)),
        compiler_params=_cparams(2),
        name="modulation",
    )(cond, mod_w, mod_b.reshape(DEPTH, 1, n))


class _Tokens:
    def __init__(self, batch, seq, n_ctx):
        self.batch, self.seq, self.n_ctx = batch, seq, n_ctx
        self.n_lat = batch * seq
        self.n_all = self.n_lat + batch * n_ctx

    def mod_spec(self, tm, layer):
        per_batch = self.seq // tm
        return pl.BlockSpec((None, 1, N_MOD, D_MODEL),
                            lambda i: (layer, jnp.minimum(i // per_batch, self.batch), 0, 0))

    def split_specs(self, tm, width, joined):
        nlt = self.n_lat // tm
        ctx_map = (lambda i: (jnp.maximum(i, nlt), 0)) if joined else (lambda i: (jnp.maximum(i - nlt, 0), 0))
        return pl.BlockSpec((tm, width), lambda i: (jnp.minimum(i, nlt - 1), 0)), pl.BlockSpec((tm, width), ctx_map)

    def pos_grid(self, blk, with_ctx, layer=0):
        n_lat_blk = self.seq // blk
        n_ctx_blk = self.n_ctx // blk
        lat_blocks = self.n_lat // blk
        row_map = lambda p, b: (jnp.where(p < n_lat_blk, b * n_lat_blk + p, lat_blocks + b * n_ctx_blk + (p - n_lat_blk)), 0)
        mod_map = lambda p, b: (layer, jnp.where(p < n_lat_blk, b, self.batch), 0, 0)
        grid = (n_lat_blk + (n_ctx_blk if with_ctx else 0), self.batch)
        return grid, row_map, mod_map


def _pick(n_lat_tiles, lat_ref, ctx_ref):
    return jnp.where(pl.program_id(0) < n_lat_tiles, lat_ref[...], ctx_ref[...])


def _layer_resident(layer, shape):
    nd = len(shape)
    return pl.BlockSpec((None,) + tuple(shape), lambda *_: (layer,) + (0,) * nd, pipeline_mode=pl.Buffered(1))


def _row_spec(tm, width):
    return pl.BlockSpec((tm, width), lambda i: (i, 0))


def _vec_spec(row):
    table, r = row
    return pl.BlockSpec((None, 1, table.shape[2]), lambda *_: (r, 0, 0))


FFN_CHUNK = 512


def _ffn_body(x, mod_ref, pre_ref, post_ref, w1_ref, w2_ref, o_ref, acc_ref):
    h = _norm_mod(x, pre_ref[...], mod_ref[0, 4:5, :], mod_ref[0, 3:4, :]).astype(BF16)
    for c in range(D_FF // FFN_CHUNK):
        sl = slice(c * FFN_CHUNK, (c + 1) * FFN_CHUNK)
        a = jnp.maximum(jnp.dot(h, w1_ref[:, sl], preferred_element_type=F32), 0.0)
        part = jnp.dot((a * a).astype(BF16), w2_ref[sl, :], preferred_element_type=F32)
        if c == 0:
            acc_ref[...] = part
        else:
            acc_ref[...] += part
    o_ref[...] = x + mod_ref[0, 5:6, :] * _rms(acc_ref[...], post_ref[...])


def _mix_ffn_kernel(xl_ref, xc_ref, fl_ref, fc_ref, ao_ref, mod_ref, mpost_ref, pre_ref, post_ref, wf_ref, wa_ref,
                    w1_ref, w2_ref, o_ref, acc_ref, *, n_lat_tiles):
    y = (jnp.dot(_pick(n_lat_tiles, fl_ref, fc_ref), wf_ref[...], preferred_element_type=F32)
         + jnp.dot(ao_ref[...], wa_ref[...], preferred_element_type=F32))
    x = _pick(n_lat_tiles, xl_ref, xc_ref) + mod_ref[0, 2:3, :] * _rms(y, mpost_ref[...])
    _ffn_body(x, mod_ref, pre_ref, post_ref, w1_ref, w2_ref, o_ref, acc_ref)


def _mix_ffn(tok, x_lat, x_ctx, fm_lat, fm_ctx, ao, mod, mix_post_g, pre_g, post_g, wf_all, wa_all, w1_all, w2_all,
             i, layer, tm):
    n = tok.n_all
    kern = functools.partial(_mix_ffn_kernel, n_lat_tiles=tok.n_lat // tm)
    return pl.pallas_call(
        kern,
        out_shape=jax.ShapeDtypeStruct((n, D_MODEL), F32),
        grid=(n // tm,),
        in_specs=[*tok.split_specs(tm, D_MODEL, x_lat is x_ctx), *tok.split_specs(tm, FOURIER_WIDTH, False),
                  _row_spec(tm, ATTN_WIDTH), tok.mod_spec(tm, layer),
                  _vec_spec(mix_post_g), _vec_spec(pre_g), _vec_spec(post_g),
                  _layer_resident(i, (FOURIER_WIDTH, D_MODEL)), _layer_resident(i, (ATTN_WIDTH, D_MODEL)),
                  _layer_resident(layer, (D_MODEL, D_FF)), _layer_resident(layer, (D_FF, D_MODEL))],
        out_specs=_row_spec(tm, D_MODEL),
        scratch_shapes=[pltpu.VMEM((tm, D_MODEL), F32)],
        compiler_params=_cparams(1),
        name="mix_ffn",
    )(x_lat, x_ctx, fm_lat, fm_ctx, ao, mod, mix_post_g[0], pre_g[0], post_g[0], wf_all, wa_all, w1_all, w2_all)


def _rope_block(x, cos, sin_hi, sin_lo):
    half = ROPE_AXIS_DIM // 2
    return (x * cos + pltpu.roll(x, half, axis=1) * sin_hi
            + pltpu.roll(x, LANES - half, axis=1) * sin_lo)


def _inproj_kernel(xl_ref, xc_ref, mod_ref, pre_ref, w_ref, cos_ref, shi_ref, slo_ref, f_ref, q_ref, k_ref, v_ref,
                   fs_ref, *, n_lat_tiles):
    x = _pick(n_lat_tiles, xl_ref, xc_ref)
    h = _norm_mod(x, pre_ref[...], mod_ref[0, 1:2, :], mod_ref[0, 0:1, :]).astype(BF16)
    p = jnp.dot(h, w_ref[...], preferred_element_type=F32)
    cos, shi, slo = cos_ref[...], shi_ref[...], slo_ref[...]
    pairs = x.shape[0] // 2
    n_slabs = FOURIER_WIDTH // LANES
    for s in range(n_slabs):
        fs_ref[s] = p[:, s * LANES:(s + 1) * LANES]
    for parity in range(2):
        for s in range(n_slabs):
            lo = parity * FOURIER_WIDTH + s * LANES
            f_ref[:, lo:lo + LANES] = fs_ref[s, pl.ds(parity, pairs, stride=2), :].astype(BF16)
    scale = HEAD_DIM ** -0.5 * LOG2_E
    for j in range(ATTN_WIDTH // LANES):
        lo = FOURIER_WIDTH + j * LANES
        q_ref[:, j * LANES:(j + 1) * LANES] = (_rope_block(p[:, lo:lo + LANES], cos, shi, slo) * scale).astype(BF16)
    k0 = FOURIER_WIDTH + ATTN_WIDTH
    k_ref[...] = _rope_block(p[:, k0:k0 + KV_WIDTH], cos, shi, slo).astype(BF16)
    v_ref[...] = p[:, k0 + KV_WIDTH:].astype(BF16)


def _rope_tables(seq, n_pad):
    pos = np.arange(seq)
    row = (pos // GRID_W).astype(np.float64)
    col = (pos % GRID_W).astype(np.float64)
    lane = np.arange(LANES)
    d = lane % HEAD_DIM
    j = d % (ROPE_AXIS_DIM // 2)
    inv = jnp.asarray(ROPE_BASE, F32) ** (-jnp.asarray(2 * j, F32) / ROPE_AXIS_DIM)
    use_col = jnp.asarray(d >= ROPE_AXIS_DIM)
    posv = jnp.where(use_col[None, :], jnp.asarray(col, F32)[:, None], jnp.asarray(row, F32)[:, None])
    ang = posv * inv[None, :]
    upper = jnp.asarray((d % ROPE_AXIS_DIM) >= ROPE_AXIS_DIM // 2)[None, :]
    cos, sin = jnp.cos(ang), jnp.sin(ang)
    sin_hi = jnp.where(upper, sin, 0.0)
    sin_lo = jnp.where(upper, 0.0, -sin)
    pad = lambda t, v: jnp.concatenate([t, jnp.full((n_pad, LANES), v, F32)], axis=0)
    return pad(cos, 1.0), pad(sin_hi, 0.0), pad(sin_lo, 0.0)


def _inproj(tok, x_lat, x_ctx, mod, pre_g, w_in_all, tables, i, layer, tm):
    per_batch = tok.seq // tm
    n_lat_tiles = tok.n_lat // tm
    tab_map = lambda i: (jnp.where(i < n_lat_tiles, i % per_batch, per_batch), 0)
    tab_spec = pl.BlockSpec((tm, LANES), tab_map)
    n = tok.n_all
    kern = functools.partial(_inproj_kernel, n_lat_tiles=n_lat_tiles)
    return pl.pallas_call(
        kern,
        out_shape=(jax.ShapeDtypeStruct((n // 2, 2 * FOURIER_WIDTH), BF16),
                   jax.ShapeDtypeStruct((n, ATTN_WIDTH), BF16),
                   jax.ShapeDtypeStruct((n, KV_WIDTH), BF16), jax.ShapeDtypeStruct((n, KV_WIDTH), BF16)),
        grid=(n // tm,),
        in_specs=[*tok.split_specs(tm, D_MODEL, x_lat is x_ctx), tok.mod_spec(tm, layer), _vec_spec(pre_g),
                  _layer_resident(i, (D_MODEL, IN_WIDTH)), tab_spec, tab_spec, tab_spec],
        out_specs=(_row_spec(tm // 2, 2 * FOURIER_WIDTH), _row_spec(tm, ATTN_WIDTH), _row_spec(tm, KV_WIDTH),
                   _row_spec(tm, KV_WIDTH)),
        scratch_shapes=[pltpu.VMEM((FOURIER_WIDTH // LANES, tm, LANES), F32)],
        compiler_params=_cparams(1),
        name="inproj",
    )(x_lat, x_ctx, mod, pre_g[0], w_in_all, *tables)


def _pair_heads(w, axis):
    shape = w.shape
    split = shape[:axis] + (N_KV_HEADS, HEAD_GROUP, HEAD_DIM) + shape[axis + 1:]
    return jnp.swapaxes(w.reshape(split), axis, axis + 1).reshape(shape)


def _dft_tables(length):
    half = length // 2
    k = np.arange(half)[:, None]
    m = np.arange(half)[None, :]

    def tab(n):
        ang = 2.0 * np.pi * ((k * n) % length) / length
        t = np.concatenate([np.cos(ang), -np.sin(ang)], axis=1) / math.sqrt(length)
        return jnp.asarray(t.astype(np.float32)).astype(BF16)

    return tab(2 * m), tab(2 * m + 1)


def _chan_table():
    n = FOURIER_GROUP_DIM
    k = np.arange(n)
    ang = 2.0 * np.pi * ((k[:, None] * k[None, :]) % n) / n
    t = np.concatenate([np.cos(ang), np.sin(ang)], axis=1) / math.sqrt(n)
    return jnp.asarray(t.astype(np.float32)).astype(BF16)


def _fourier_one(f_ref, chan_ref, pos_e_ref, pos_o_ref, o_ref, stk_ref):
    gd = FOURIER_GROUP_DIM
    half = f_ref.shape[0]
    row_chunk = min(half, 512)
    for parity in range(2):
        for g in range(FOURIER_GROUPS):
            lanes = slice(parity * FOURIER_WIDTH + g * gd, parity * FOURIER_WIDTH + (g + 1) * gd)
            z = jnp.dot(f_ref[:, lanes], chan_ref[...], preferred_element_type=F32)
            stk_ref[parity, 0:half, g * gd:(g + 1) * gd] = z[:, :gd].astype(BF16)
            stk_ref[parity, half:2 * half, g * gd:(g + 1) * gd] = z[:, gd:].astype(BF16)
    for r in range(half // row_chunk):
        rows = slice(r * row_chunk, (r + 1) * row_chunk)
        even = jnp.dot(pos_e_ref[rows, :], stk_ref[0], preferred_element_type=F32)
        odd = jnp.dot(pos_o_ref[rows, :], stk_ref[1], preferred_element_type=F32)
        o_ref[rows, :] = (even + odd).astype(BF16)
        o_ref[half + r * row_chunk:half + (r + 1) * row_chunk, :] = (even - odd).astype(BF16)


def _fourier_kernel(fl_ref, fc_ref, chan_ref, ple_ref, plo_ref, pce_ref, pco_ref, ol_ref, oc_ref, stkl_ref, stkc_ref):
    _fourier_one(fl_ref, chan_ref, ple_ref, plo_ref, ol_ref, stkl_ref)
    _fourier_one(fc_ref, chan_ref, pce_ref, pco_ref, oc_ref, stkc_ref)


def _fourier(tok, f_pairs, chan, tabs_lat, tabs_ctx):
    ctx0 = tok.n_lat // tok.n_ctx
    pair_blk = lambda rows, m: pl.BlockSpec((rows // 2, 2 * FOURIER_WIDTH), m)
    lat_blk = pl.BlockSpec((tok.seq, FOURIER_WIDTH), lambda b: (b, 0))
    ctx_blk = pl.BlockSpec((tok.n_ctx, FOURIER_WIDTH), lambda b: (b, 0))
    tabs = (*tabs_lat, *tabs_ctx)
    return pl.pallas_call(
        _fourier_kernel,
        out_shape=(jax.ShapeDtypeStruct((tok.n_lat, FOURIER_WIDTH), BF16),
                   jax.ShapeDtypeStruct((tok.batch * tok.n_ctx, FOURIER_WIDTH), BF16)),
        grid=(tok.batch,),
        in_specs=[pair_blk(tok.seq, lambda b: (b, 0)), pair_blk(tok.n_ctx, lambda b: (ctx0 + b, 0)),
                  _resident(chan.shape), *[_resident(t.shape) for t in tabs]],
        out_specs=(lat_blk, ctx_blk),
        scratch_shapes=[pltpu.VMEM((2, tok.seq, FOURIER_WIDTH), BF16),
                        pltpu.VMEM((2, tok.n_ctx, FOURIER_WIDTH), BF16)],
        compiler_params=_cparams(1),
        name="fourier",
    )(f_pairs, f_pairs, chan, *tabs)


ATTN_QBLOCKS = 2


def _attn_kernel(sink_ref, q_ref, k0_ref, k1_ref, k2_ref, k3_ref, kx_ref, v0_ref, v1_ref, v2_ref, v3_ref, vx_ref,
                 o_ref, *, n_steps_lat, layer_idx):
    j = pl.program_id(1)
    blk = WINDOW
    n_loc = 3 * blk
    rows = HEAD_GROUP * blk
    nt = (((1,), (1,)), ((), ()))
    k_refs, v_refs = (k0_ref, k1_ref, k2_ref, k3_ref), (v0_ref, v1_ref, v2_ref, v3_ref)

    def attend(half, with_local):
        r0 = half * blk
        q = jnp.concatenate([q_ref[r0:r0 + blk, i * LANES:(i + 1) * LANES] for i in range(HEAD_GROUP)], axis=0)
        k_ctx, v_ctx = kx_ref[...], vx_ref[...]
        if with_local:
            k_loc = jnp.concatenate([r[...] for r in k_refs[half:half + 3]], axis=0)
            v_loc = jnp.concatenate([r[...] for r in v_refs[half:half + 3]], axis=0)
            col_lo = jnp.where(j >= 1, 0, blk) if half == 0 else 0
            col_hi = jnp.where(j + 1 < n_steps_lat, n_loc, 2 * blk) if half == ATTN_QBLOCKS - 1 else n_loc
            qi = lax.broadcasted_iota(jnp.int32, (blk, 1), 0)
            kj = lax.broadcasted_iota(jnp.int32, (blk, n_loc), 1)
            visible = (kj >= jnp.maximum(qi, col_lo)) & (kj <= jnp.minimum(qi + 2 * WINDOW, col_hi - 1))
            bias = jnp.concatenate([jnp.where(visible, 0.0, NEG_INF)] * HEAD_GROUP, axis=0)
        row_blk = lax.broadcasted_iota(jnp.int32, (rows, 1), 0) // blk
        lane_o = lax.broadcasted_iota(jnp.int32, (rows, LANES), 1)
        outs = []
        for kvh in range(N_KV_HEADS):
            def own_lanes(t):
                lane = lax.broadcasted_iota(jnp.int32, t.shape, 1)
                return jnp.where((lane >= kvh * HEAD_DIM) & (lane < (kvh + 1) * HEAD_DIM), t, jnp.zeros_like(t))
            s_ctx = lax.dot_general(q, own_lanes(k_ctx), nt, preferred_element_type=F32)
            sink = jnp.zeros((rows, 1), F32)
            for i in range(HEAD_GROUP):
                sink = jnp.where(row_blk == i, sink_ref[layer_idx, kvh * HEAD_GROUP + i] * LOG2_E, sink)
            m = jnp.maximum(jnp.max(s_ctx, axis=-1, keepdims=True), sink)
            if with_local:
                s_loc = lax.dot_general(q, own_lanes(k_loc), nt, preferred_element_type=F32) + bias
                m = jnp.maximum(m, jnp.max(s_loc, axis=-1, keepdims=True))
            p_ctx = jnp.exp2(s_ctx - m)
            denom = jnp.sum(p_ctx, axis=-1, keepdims=True) + jnp.exp2(sink - m)
            o = jnp.dot(p_ctx.astype(BF16), v_ctx, preferred_element_type=F32)
            if with_local:
                p_loc = jnp.exp2(s_loc - m)
                denom = denom + jnp.sum(p_loc, axis=-1, keepdims=True)
                o = o + jnp.dot(p_loc.astype(BF16), v_loc, preferred_element_type=F32)
            outs.append(o / denom)
        merged = jnp.where(lane_o < HEAD_DIM, outs[0], outs[1])
        for i in range(HEAD_GROUP):
            o_ref[r0:r0 + blk, i * LANES:(i + 1) * LANES] = merged[i * blk:(i + 1) * blk, :].astype(BF16)

    def step(with_local):
        for half in range(ATTN_QBLOCKS):
            attend(half, with_local)

    pl.when(j < n_steps_lat)(lambda: step(True))
    pl.when(j >= n_steps_lat)(lambda: step(False))


def _attention(tok, q, k, v, sink_all, layer_idx):
    blk = WINDOW
    qb = ATTN_QBLOCKS
    n_qblk = tok.seq // blk
    assert n_qblk % qb == 0 and tok.n_ctx % (qb * blk) == 0
    n_lat_steps = n_qblk // qb
    n_ctx_steps = tok.n_ctx // (qb * blk)
    ctx0 = tok.n_lat // tok.n_ctx

    def q_map(b, j):
        return (jnp.where(j < n_lat_steps, b * n_lat_steps + j,
                          tok.batch * n_lat_steps + b * n_ctx_steps + (j - n_lat_steps)), 0)

    def k_map(off):
        return lambda b, j: (b * n_qblk + jnp.clip(j * qb + off, 0, n_qblk - 1), 0)

    ctx_map = lambda b, j: (ctx0 + b, 0)
    kv_blks = [pl.BlockSpec((blk, KV_WIDTH), k_map(off)) for off in range(-1, qb + 1)]
    ctx_blk = pl.BlockSpec((tok.n_ctx, KV_WIDTH), ctx_map)
    kern = functools.partial(_attn_kernel, n_steps_lat=n_lat_steps, layer_idx=layer_idx)
    return pl.pallas_call(
        kern,
        out_shape=jax.ShapeDtypeStruct((tok.n_all, ATTN_WIDTH), BF16),
        grid=(tok.batch, n_lat_steps + n_ctx_steps),
        in_specs=[pl.BlockSpec(memory_space=pltpu.SMEM),
                  pl.BlockSpec((qb * blk, ATTN_WIDTH), q_map),
                  *kv_blks, ctx_blk, *kv_blks, ctx_blk],
        out_specs=pl.BlockSpec((qb * blk, ATTN_WIDTH), q_map),
        compiler_params=_cparams(2),
        name="window_attention",
    )(sink_all, q, *([k] * (qb + 2)), k, *([v] * (qb + 2)), v)


def _piece_perm():
    idx = np.arange(D_MODEL)
    a, b, c = idx // LANES, (idx // SSM_GROUP_DIM) % SSM_PIECES, idx % SSM_GROUP_DIM
    perm = np.zeros((D_MODEL, D_MODEL), np.float32)
    perm[idx, b * LANES + a * SSM_GROUP_DIM + c] = 1.0
    return jnp.asarray(perm).astype(BF16)


def _slab_pitch(blk):
    return blk + SUBLANES


def _s5_pre_kernel(x_ref, mod_ref, pre_ref, perm_ref, u_ref, hs_ref, *, blk, batch, n_blocks):
    p, b = pl.program_id(0), pl.program_id(1)
    pitch = _slab_pitch(blk)

    @pl.when(p < n_blocks)
    def _():
        h = _norm_mod(x_ref[...], pre_ref[...], mod_ref[0, 1:2, :], mod_ref[0, 0:1, :])
        row0 = pl.multiple_of(b * pitch, SUBLANES)
        slot0 = (p % 2) * SSM_SLABS
        for s in range(SSM_SLABS):
            hs_ref[slot0 + s, pl.ds(row0, blk), :] = h[:, s * LANES:(s + 1) * LANES]

    @pl.when(p >= 1)
    def _():
        half = SSM_CHUNK // 2
        cpb = blk // SSM_CHUNK
        rows_h = cpb * batch
        s = b
        src = ((p - 1) % 2) * SSM_SLABS + s
        lhs = []
        for hh in range(2):
            for i in range(cpb):
                t0 = i * SSM_CHUNK + hh * half
                lhs.append(jnp.concatenate(
                    [hs_ref[src, pl.ds(t0 + j, batch, stride=pitch), :] for j in range(half)], axis=1))
        lhs = jnp.concatenate(lhs, axis=0).astype(BF16)
        out = jnp.dot(lhs, perm_ref[...], preferred_element_type=F32).astype(BF16)
        for hh in range(2):
            for g in range(SSM_PIECES):
                u_ref[s * SSM_PIECES + g, :, hh * LANES:(hh + 1) * LANES] = (
                    out[hh * rows_h:(hh + 1) * rows_h, g * LANES:(g + 1) * LANES])


def _s5_pre(tok, xa, mod, pre_g, perm, layer, blk):
    assert tok.batch == SSM_SLABS
    (n_blocks, _), row_map, mod_map = tok.pos_grid(blk, True, layer)
    n_lat_blk = tok.seq // blk
    n_ctx_blk = tok.n_ctx // blk
    cpb = blk // SSM_CHUNK
    n_chunks = (tok.seq + tok.n_ctx) // SSM_CHUNK
    last = n_blocks - 1

    def u_map(p, b):
        q = jnp.maximum(p - 1, 0)
        return (0, jnp.where(q < n_lat_blk, n_ctx_blk + q, q - n_lat_blk), 0)

    clamp = lambda m: (lambda p, b: m(jnp.minimum(p, last), b))
    kern = functools.partial(_s5_pre_kernel, blk=blk, batch=tok.batch, n_blocks=n_blocks)
    return pl.pallas_call(
        kern,
        out_shape=jax.ShapeDtypeStruct((SSM_GROUPS, n_chunks * tok.batch, SSM_CW), BF16),
        grid=(n_blocks + 1, tok.batch),
        in_specs=[pl.BlockSpec((blk, D_MODEL), clamp(row_map)),
                  pl.BlockSpec((None, 1, N_MOD, D_MODEL), clamp(mod_map)),
                  _vec_spec(pre_g), _resident((D_MODEL, D_MODEL))],
        out_specs=pl.BlockSpec((SSM_GROUPS, cpb * tok.batch, SSM_CW), u_map),
        scratch_shapes=[pltpu.VMEM((2 * SSM_SLABS, tok.batch * _slab_pitch(blk), LANES), F32)],
        compiler_params=_cparams(2),
        name="s5_pre",
    )(xa, mod, pre_g[0], perm)


def _cmul(ar, ai, br, bi):
    return ar * br - ai * bi, ar * bi + ai * br


def _s5_table_plan():
    t_n = SSM_CHUNK
    t = np.arange(t_n)
    return [
        [(0, t_n - 1 - t, 'b', 're'), (1, t, 'b', 're'), (0, t_n - 1 - t, 'b', 'im'), (1, t, 'b', 'im')],
        [(0, -t, 'b', 're'), (0, -t, 'b', 'im'), (1, t, 'b', 're'), (1, t, 'b', 'im')],
        [(0, t, 'c', 're'), (0, t, 'c', '-im'), (1, -t, 'c', 're'), (1, -t, 'c', '-im')],
        [(0, t + 1, 'c', 're'), (1, t_n - t, 'c', 're'), (0, t + 1, 'c', '-im'), (1, t_n - t, 'c', '-im')],
    ]


def _s5_operands(a_re, a_im, log_dt, b_re, b_im, c_re, c_im):
    t_n = SSM_CHUNK
    per_group = lambda x: jnp.swapaxes(x.astype(F32), 1, 2)
    l_re, l_im = per_group(a_re), per_group(a_im)
    dt = jnp.exp(per_group(log_dt))[..., None]
    z_re, z_im = l_re * dt, l_im * dt
    n_all = jnp.arange(1 - t_n, t_n + 1, dtype=F32)[:, None]
    mag = jnp.exp(z_re[..., None, :] * n_all)
    ang = z_im[..., None, :] * n_all
    pows = jnp.stack([mag * jnp.cos(ang), mag * jnp.sin(ang)], axis=2)
    a1_re, a1_im = pows[:, :, 0, :, t_n, :], pows[:, :, 1, :, t_n, :]
    den = l_re * l_re + l_im * l_im
    r_re = ((a1_re - 1.0) * l_re + a1_im * l_im) / den
    r_im = (a1_im * l_re - (a1_re - 1.0) * l_im) / den
    bb_re, bb_im = _cmul(r_re[..., None], r_im[..., None], per_group(b_re), per_group(b_im))
    base = jnp.stack([jnp.swapaxes(bb_re, -1, -2), jnp.swapaxes(bb_im, -1, -2), per_group(c_re), per_group(c_im)],
                     axis=3)
    n_l, n_g = base.shape[0], base.shape[1]
    base = base.reshape(n_l, n_g, 8, SSM_GROUP_DIM, SSM_STATE)
    signed = jnp.concatenate([base, -base], axis=2)

    plan = _s5_table_plan()
    pows_rev = pows[:, :, :, :, ::-1, :]
    p_blocks, w_a, w_b = [], [], []
    for blocks in plan:
        for d, expo, w, part in blocks:
            rising = expo[1] > expo[0]
            lo = int(expo[0]) + t_n - 1 if rising else t_n - int(expo[0])
            p_blocks.append((pows if rising else pows_rev)[:, :, :, d, lo:lo + t_n, :])
            re, im = d * 4 + (0 if w == 'b' else 2), d * 4 + (1 if w == 'b' else 3)
            k_a, k_b = {'re': (re, im + 8), 'im': (im, re), '-im': (im + 8, re + 8)}[part]
            w_a.append(signed[:, :, k_a])
            w_b.append(signed[:, :, k_b])
    last = 2 * t_n - 1
    pw = pows[:, :, :, :, last, :]
    dec = jnp.concatenate([pw[:, :, 0, 0], pw[:, :, 0, 1], pw[:, :, 1, 0], pw[:, :, 1, 1]], axis=-1)[:, :, None, :]
    tabs = (len(plan), 4)
    pp = jnp.stack(p_blocks, axis=3).reshape(n_l, n_g, 2, *tabs, t_n, SSM_STATE)
    ww = jnp.stack(w_a + w_b, axis=2).reshape(n_l, n_g, 2, *tabs, SSM_GROUP_DIM, SSM_STATE)
    return pp, ww, dec


def _s5_kernel(u_ref, pp_ref, ww_ref, dec_ref, y_ref, v_ref, xin_ref, *, batch, n_chunks, n_ctx_chunks):
    gps = SSM_GROUPS_PER_STEP
    ns = SSM_STATE
    ns2 = 2 * ns
    cw = SSM_CW
    nt = (((1,), (1,)), ((), ()))

    def table(g, k):
        lanes = lambda ref, a: jnp.concatenate([ref[g, a, k, j] for j in range(ref.shape[3])], axis=1)
        p_re, p_im = lanes(pp_ref, 0), lanes(pp_ref, 1)
        w_a, w_b = lanes(ww_ref, 0), lanes(ww_ref, 1)
        return jnp.concatenate([p_re[t:t + 1, :] * w_a + p_im[t:t + 1, :] * w_b for t in range(SSM_CHUNK)],
                               axis=0)

    for g in range(gps):
        v_ref[g] = jnp.dot(u_ref[g], table(g, 0).astype(BF16), preferred_element_type=F32)
    a_re = jnp.broadcast_to(dec_ref[:, :, 0:ns2], (gps, batch, ns2))
    a_im = jnp.broadcast_to(dec_ref[:, :, ns2:2 * ns2], (gps, batch, ns2))
    is_fwd = lax.broadcasted_iota(jnp.int32, (gps, batch, ns2), 2) < ns

    def step(k, carry):
        x_re, x_im = carry
        kb = jnp.where(k < n_ctx_chunks, n_ctx_chunks - 1 - k, n_chunks + n_ctx_chunks - 1 - k)
        rf = pl.multiple_of(k * batch, batch)
        rb = pl.multiple_of(kb * batch, batch)
        xin_ref[:, pl.ds(rf, batch), 0:ns] = x_re[:, :, 0:ns]
        xin_ref[:, pl.ds(rb, batch), ns:ns2] = x_re[:, :, ns:ns2]
        xin_ref[:, pl.ds(rf, batch), ns2:ns2 + ns] = x_im[:, :, 0:ns]
        xin_ref[:, pl.ds(rb, batch), ns2 + ns:2 * ns2] = x_im[:, :, ns:ns2]
        v_re = jnp.where(is_fwd, v_ref[:, pl.ds(rf, batch), 0:ns2], v_ref[:, pl.ds(rb, batch), 0:ns2])
        v_im = jnp.where(is_fwd, v_ref[:, pl.ds(rf, batch), ns2:2 * ns2], v_ref[:, pl.ds(rb, batch), ns2:2 * ns2])
        return a_re * x_re - a_im * x_im + v_re, a_re * x_im + a_im * x_re + v_im

    zero = jnp.zeros((gps, batch, ns2), F32)
    lax.fori_loop(0, n_chunks, step, (zero, zero))
    t_in = lax.broadcasted_iota(jnp.int32, (cw, cw), 0) // SSM_GROUP_DIM
    t_out = lax.broadcasted_iota(jnp.int32, (cw, cw), 1) // SSM_GROUP_DIM

    def split(x):
        hi = x.astype(BF16)
        return hi, (x - hi.astype(F32)).astype(BF16)

    def lag_kernel(e, ft, lanes):
        (eh, el), (fh, fl) = split(e[:, lanes]), split(ft[:, lanes])
        return (lax.dot_general(eh, fh, nt, preferred_element_type=F32)
                + lax.dot_general(eh, fl, nt, preferred_element_type=F32)
                + lax.dot_general(el, fh, nt, preferred_element_type=F32))

    for g in range(gps):
        e, ft = table(g, 1), table(g, 2)
        m = (jnp.where(t_out >= t_in, lag_kernel(e, ft, slice(0, ns2)), 0.0)
             + jnp.where(t_in >= t_out, lag_kernel(e, ft, slice(ns2, 2 * ns2)), 0.0)).astype(BF16)
        y_ref[g] = (jnp.dot(u_ref[g], m, preferred_element_type=F32)
                    + lax.dot_general(xin_ref[g].astype(BF16), table(g, 3).astype(BF16), nt,
                                      preferred_element_type=F32)).astype(BF16)


def _s5(u_t, operands, layer, batch, n_chunks, n_ctx_chunks):
    pp, ww, dec = operands
    gps = SSM_GROUPS_PER_STEP
    rows = n_chunks * batch
    kern = functools.partial(_s5_kernel, batch=batch, n_chunks=n_chunks, n_ctx_chunks=n_ctx_chunks)
    gspec = lambda r, c: pl.BlockSpec((gps, r, c), lambda i: (i, 0, 0))
    lspec = lambda shape: pl.BlockSpec((None, gps) + shape, lambda i: (layer, i) + (0,) * len(shape))
    return pl.pallas_call(
        kern,
        out_shape=jax.ShapeDtypeStruct((SSM_GROUPS, rows, SSM_CW), BF16),
        grid=(SSM_GROUPS // gps,),
        in_specs=[gspec(rows, SSM_CW), lspec(pp.shape[2:]), lspec(ww.shape[2:]), lspec(dec.shape[2:])],
        out_specs=gspec(rows, SSM_CW),
        scratch_shapes=[pltpu.VMEM((gps, rows, SSM_CW), F32), pltpu.VMEM((gps, rows, SSM_CW), F32)],
        compiler_params=_cparams(1),
        name="s5_scan",
    )(u_t, pp, ww, dec)


def _s5_unpack_kernel(y_ref, perm_ref, o_ref, ys_ref, tmp_ref, *, blk, batch):
    b = pl.program_id(1)

    @pl.when(b == 0)
    def _():
        half = SSM_CHUNK // 2
        cpb = blk // SSM_CHUNK
        rows_h = cpb * batch

        lhs = jnp.concatenate(
            [jnp.concatenate([y_ref[s * SSM_PIECES + g, :, hh * LANES:(hh + 1) * LANES]
                              for g in range(SSM_PIECES)], axis=1)
             for s in range(SSM_SLABS) for hh in range(2)], axis=0)
        tmp_ref[...] = jnp.dot(lhs, perm_ref[...], preferred_element_type=F32)

        def slab(s, carry):
            for hh in range(2):
                for i in range(cpb):
                    r0 = pl.multiple_of((s * 2 + hh) * rows_h + i * batch, batch)
                    t0 = i * SSM_CHUNK + hh * half
                    for j in range(half):
                        ys_ref[s, (t0 + j) * batch:(t0 + j + 1) * batch, :] = (
                            tmp_ref[pl.ds(r0, batch), j * LANES:(j + 1) * LANES])
            return carry

        lax.fori_loop(0, SSM_SLABS, slab, 0)

    for s in range(SSM_SLABS):
        o_ref[:, s * LANES:(s + 1) * LANES] = ys_ref[s, pl.ds(b, blk, stride=batch), :].astype(BF16)


def _s5_unpack(tok, y_t, perm, blk, with_ctx):
    grid, row_map, _ = tok.pos_grid(blk, with_ctx)
    n_lat_blk = tok.seq // blk
    n_ctx_blk = tok.n_ctx // blk
    cpb = blk // SSM_CHUNK
    y_map = lambda p, b: (0, jnp.where(p < n_lat_blk, n_ctx_blk + p, p - n_lat_blk), 0)
    n_rows = tok.n_all if with_ctx else tok.n_lat
    kern = functools.partial(_s5_unpack_kernel, blk=blk, batch=tok.batch)
    return pl.pallas_call(
        kern,
        out_shape=jax.ShapeDtypeStruct((n_rows, D_MODEL), BF16),
        grid=grid,
        in_specs=[pl.BlockSpec((SSM_GROUPS, cpb * tok.batch, SSM_CW), y_map), _resident((D_MODEL, D_MODEL))],
        out_specs=pl.BlockSpec((blk, D_MODEL), row_map),
        scratch_shapes=[pltpu.VMEM((SSM_SLABS, tok.batch * blk, LANES), F32),
                        pltpu.VMEM((SSM_SLABS * 2 * cpb * tok.batch, D_MODEL), F32)],
        compiler_params=_cparams(2),
        name="s5_unpack",
    )(y_t, perm)


def _glu_ffn_kernel(x_ref, y_ref, d_ref, mod_ref, mpre_ref, mpost_ref, pre_ref, post_ref, wg_ref, w1_ref, w2_ref,
                    o_ref, acc_ref):
    x = x_ref[...]
    h = _norm_mod(x, mpre_ref[...], mod_ref[0, 1:2, :], mod_ref[0, 0:1, :])
    y = y_ref[...].astype(F32) + d_ref[...] * h
    z = jnp.dot(jax.nn.gelu(y).astype(BF16), wg_ref[...], preferred_element_type=F32)
    out = z[:, :D_MODEL] * jax.nn.sigmoid(z[:, D_MODEL:])
    x = x + mod_ref[0, 2:3, :] * _rms(out, mpost_ref[...])
    _ffn_body(x, mod_ref, pre_ref, post_ref, w1_ref, w2_ref, o_ref, acc_ref)


def _glu_ffn(tok, xa, n_rows, y_tok, d_skip, mod, mix_pre_g, mix_post_g, pre_g, post_g, glu_all, w1_all, w2_all,
             i, layer, tm):
    return pl.pallas_call(
        _glu_ffn_kernel,
        out_shape=jax.ShapeDtypeStruct((n_rows, D_MODEL), F32),
        grid=(n_rows // tm,),
        in_specs=[_row_spec(tm, D_MODEL), _row_spec(tm, D_MODEL), _vec_spec(d_skip), tok.mod_spec(tm, layer),
                  _vec_spec(mix_pre_g), _vec_spec(mix_post_g), _vec_spec(pre_g), _vec_spec(post_g),
                  _layer_resident(i, (D_MODEL, 2 * D_MODEL)),
                  _layer_resident(layer, (D_MODEL, D_FF)), _layer_resident(layer, (D_FF, D_MODEL))],
        out_specs=_row_spec(tm, D_MODEL),
        scratch_shapes=[pltpu.VMEM((tm, D_MODEL), F32)],
        compiler_params=_cparams(1),
        name="glu_ffn",
    )(xa, y_tok, d_skip[0], mod, mix_pre_g[0], mix_post_g[0], pre_g[0], post_g[0], glu_all, w1_all, w2_all)


def _tile(limit, *sizes):
    tm = limit
    while any(s % tm for s in sizes):
        tm //= 2
    return tm


def kernel(x, c, ctx, c_ctx, mod_w, mod_b, mix_pre_g, mix_post_g, ffn_pre_g, ffn_post_g, ffn_w1, ffn_w2,
           even_w_in, even_w_out, even_sink, ssm_a_re, ssm_a_im, ssm_log_dt, ssm_b_re, ssm_b_im, ssm_c_re,
           ssm_c_im, ssm_d, ssm_glu_w):
    batch, seq, _ = x.shape
    n_ctx = ctx.shape[1]
    tok = _Tokens(batch, seq, n_ctx)
    assert seq % WINDOW == 0 and n_ctx % WINDOW == 0 and tok.n_lat % n_ctx == 0
    assert batch == SUBLANES
    tm = _tile(256, seq, n_ctx)
    tm_wide = _tile(512, seq, batch * n_ctx)

    x_lat, x_ctx = x.reshape(tok.n_lat, D_MODEL), ctx.astype(x.dtype).reshape(-1, D_MODEL)

    n_cond = 2 * SUBLANES
    cond = jnp.zeros((n_cond, D_MODEL), F32).at[:batch].set(c).at[batch].set(c_ctx)
    mod = _modulation(cond, mod_w, mod_b).reshape(DEPTH, n_cond, N_MOD, D_MODEL)

    rope = _rope_tables(seq, tm_wide)
    piece_perm = _piece_perm()
    dft_chan, dft_lat, dft_ctx = _chan_table(), _dft_tables(seq), _dft_tables(n_ctx)
    rows3 = lambda t: t.reshape(t.shape[0], 1, t.shape[1])
    w1_all, w2_all, glu_all = ffn_w1.astype(BF16), ffn_w2.astype(BF16), ssm_glu_w.astype(BF16)
    q0, k0 = FOURIER_WIDTH, FOURIER_WIDTH + ATTN_WIDTH
    w_in_all = jnp.concatenate([even_w_in[:, :, :q0], _pair_heads(even_w_in[:, :, q0:k0], 2),
                                even_w_in[:, :, k0:]], axis=2).astype(BF16)
    wf_all = even_w_out[:, :FOURIER_WIDTH].astype(BF16)
    wa_all = _pair_heads(even_w_out[:, FOURIER_WIDTH:], 1).astype(BF16)
    s5_operands = _s5_operands(ssm_a_re, ssm_a_im, ssm_log_dt, ssm_b_re, ssm_b_im, ssm_c_re, ssm_c_im)

    for layer in range(DEPTH):
        need_ctx = layer < DEPTH - 1
        n_rows = tok.n_all if need_ctx else tok.n_lat
        i = layer // 2
        mix_pre, mix_post = (rows3(mix_pre_g), layer), (rows3(mix_post_g), layer)
        ffn_pre, ffn_post = (rows3(ffn_pre_g), layer), (rows3(ffn_post_g), layer)
        if layer % 2 == 0:
            f, q, k, v = _inproj(tok, x_lat, x_ctx, mod, mix_pre, w_in_all, rope, i, layer, tm_wide)
            fm_lat, fm_ctx = _fourier(tok, f, dft_chan, dft_lat, dft_ctx)
            ao = _attention(tok, q, k, v, even_sink, i)
            xa = _mix_ffn(tok, x_lat, x_ctx, fm_lat, fm_ctx, ao, mod, mix_post, ffn_pre, ffn_post,
                          wf_all, wa_all, w1_all, w2_all, i, layer, tm_wide)
        else:
            u_t = _s5_pre(tok, xa, mod, mix_pre, piece_perm, layer, tm)
            y_t = _s5(u_t, s5_operands, i, batch, (seq + n_ctx) // SSM_CHUNK, n_ctx // SSM_CHUNK)
            y_tok = _s5_unpack(tok, y_t, piece_perm, tm, need_ctx)
            xa = _glu_ffn(tok, xa, n_rows, y_tok, (rows3(ssm_d), i), mod, mix_pre, mix_post, ffn_pre, ffn_post,
                          glu_all, w1_all, w2_all, i, layer, tm_wide)
        x_lat = x_ctx = xa
    return xa[:tok.n_lat].reshape(batch, seq, D_MODEL)
```

```python
import functools
import math

import numpy as np
import jax
import jax.numpy as jnp
from jax import lax
from jax.experimental import pallas as pl
from jax.experimental.pallas import tpu as pltpu

D_MODEL = 1024
DEPTH = 4
N_MOD = 6
EPS = 1e-6
NEG_INF = -1e30
GRID_W = 64

FOURIER_GROUPS = 4
FOURIER_GROUP_DIM = 128
FOURIER_WIDTH = FOURIER_GROUPS * FOURIER_GROUP_DIM

N_HEADS = 8
N_KV_HEADS = 2
HEAD_GROUP = N_HEADS // N_KV_HEADS
HEAD_DIM = 64
ATTN_WIDTH = N_HEADS * HEAD_DIM
KV_WIDTH = N_KV_HEADS * HEAD_DIM
WINDOW = 128
ROPE_AXIS_DIM = HEAD_DIM // 2
ROPE_BASE = 10000.0
LOG2_E = math.log2(math.e)
IN_WIDTH = FOURIER_WIDTH + ATTN_WIDTH + 2 * KV_WIDTH

LANES = 128
SUBLANES = 8
VMEM_LIMIT = 56 * 1024 * 1024

SSM_GROUP_DIM = 16
SSM_GROUPS = D_MODEL // SSM_GROUP_DIM
SSM_STATE = 64
SSM_CHUNK = 16
SSM_CW = SSM_CHUNK * SSM_GROUP_DIM
SSM_GROUPS_PER_STEP = 4
SSM_SLABS = D_MODEL // LANES
SSM_PIECES = LANES // SSM_GROUP_DIM

D_FF = 4 * D_MODEL

F32 = jnp.float32
BF16 = jnp.bfloat16


def _cparams(n_axes):
    return pltpu.CompilerParams(dimension_semantics=("arbitrary",) * n_axes, vmem_limit_bytes=VMEM_LIMIT)


def _resident(shape):
    nd = len(shape)
    return pl.BlockSpec(shape, lambda *_: (0,) * nd, pipeline_mode=pl.Buffered(1))


def _rms(x, g):
    return x * lax.rsqrt(jnp.mean(x * x, axis=-1, keepdims=True) + EPS) * g


def _norm_mod(x, g, sc, sh):
    return _rms(x, g) * (1.0 + sc) + sh


def _mod_kernel(cond_ref, w_ref, b_ref, o_ref):
    cond = cond_ref[...]
    s = cond * jax.nn.sigmoid(cond)
    s_hi = s.astype(BF16)
    s_lo = (s - s_hi.astype(F32)).astype(BF16)
    w = w_ref[0].astype(BF16)
    o_ref[0] = (jnp.dot(s_hi, w, preferred_element_type=F32) + jnp.dot(s_lo, w, preferred_element_type=F32)
                + b_ref[0])


def _modulation(cond, mod_w, mod_b):
    rows = cond.shape[0]
    tn = 2048
    n = N_MOD * D_MODEL
    return pl.pallas_call(
        _mod_kernel,
        out_shape=jax.ShapeDtypeStruct((DEPTH, rows, n), F32),
        grid=(DEPTH, n // tn),
        in_specs=[pl.BlockSpec((rows, D_MODEL), lambda l, j: (0, 0)),
                  pl.BlockSpec((1, D_MODEL, tn), lambda l, j: (l, 0, j)),
                  pl.BlockSpec((1, 1, tn), lambda l, j: (l, 0, j))],
        out_specs=pl.BlockSpec((1, rows, tn), lambda l, j: (l, 0, j)),
        compiler_params=_cparams(2),
        name="modulation",
    )(cond, mod_w, mod_b.reshape(DEPTH, 1, n))


class _Tokens:
    def __init__(self, batch, seq, n_ctx):
        self.batch, self.seq, self.n_ctx = batch, seq, n_ctx
        self.n_lat = batch * seq
        self.n_all = self.n_lat + batch * n_ctx

    def mod_spec(self, tm, layer):
        per_batch = self.seq // tm
        return pl.BlockSpec((None, 1, N_MOD, D_MODEL),
                            lambda i: (layer, jnp.minimum(i // per_batch, self.batch), 0, 0))

    def split_specs(self, tm, width, joined):
        nlt = self.n_lat // tm
        ctx_map = (lambda i: (jnp.maximum(i, nlt), 0)) if joined else (lambda i: (jnp.maximum(i - nlt, 0), 0))
        return pl.BlockSpec((tm, width), lambda i: (jnp.minimum(i, nlt - 1), 0)), pl.BlockSpec((tm, width), ctx_map)

    def pos_grid(self, blk, with_ctx, layer=0):
        n_lat_blk = self.seq // blk
        n_ctx_blk = self.n_ctx // blk
        lat_blocks = self.n_lat // blk
        row_map = lambda p, b: (jnp.where(p < n_lat_blk, b * n_lat_blk + p, lat_blocks + b * n_ctx_blk + (p - n_lat_blk)), 0)
        mod_map = lambda p, b: (layer, jnp.where(p < n_lat_blk, b, self.batch), 0, 0)
        grid = (n_lat_blk + (n_ctx_blk if with_ctx else 0), self.batch)
        return grid, row_map, mod_map


def _pick(n_lat_tiles, lat_ref, ctx_ref):
    return jnp.where(pl.program_id(0) < n_lat_tiles, lat_ref[...], ctx_ref[...])


def _layer_resident(layer, shape):
    nd = len(shape)
    return pl.BlockSpec((None,) + tuple(shape), lambda *_: (layer,) + (0,) * nd, pipeline_mode=pl.Buffered(1))


def _row_spec(tm, width):
    return pl.BlockSpec((tm, width), lambda i: (i, 0))


def _vec_spec(row):
    table, r = row
    return pl.BlockSpec((None, 1, table.shape[2]), lambda *_: (r, 0, 0))


FFN_CHUNK = 512


def _ffn_body(x, mod_ref, pre_ref, post_ref, w1_ref, w2_ref, o_ref, acc_ref):
    h = _norm_mod(x, pre_ref[...], mod_ref[0, 4:5, :], mod_ref[0, 3:4, :]).astype(BF16)
    for c in range(D_FF // FFN_CHUNK):
        sl = slice(c * FFN_CHUNK, (c + 1) * FFN_CHUNK)
        a = jnp.maximum(jnp.dot(h, w1_ref[:, sl], preferred_element_type=F32), 0.0)
        part = jnp.dot((a * a).astype(BF16), w2_ref[sl, :], preferred_element_type=F32)
        if c == 0:
            acc_ref[...] = part
        else:
            acc_ref[...] += part
    o_ref[...] = x + mod_ref[0, 5:6, :] * _rms(acc_ref[...], post_ref[...])


def _mix_ffn_kernel(xl_ref, xc_ref, fl_ref, fc_ref, ao_ref, mod_ref, mpost_ref, pre_ref, post_ref, wf_ref, wa_ref,
                    w1_ref, w2_ref, o_ref, acc_ref, *, n_lat_tiles):
    y = (jnp.dot(_pick(n_lat_tiles, fl_ref, fc_ref), wf_ref[...], preferred_element_type=F32)
         + jnp.dot(ao_ref[...], wa_ref[...], preferred_element_type=F32))
    x = _pick(n_lat_tiles, xl_ref, xc_ref) + mod_ref[0, 2:3, :] * _rms(y, mpost_ref[...])
    _ffn_body(x, mod_ref, pre_ref, post_ref, w1_ref, w2_ref, o_ref, acc_ref)


def _mix_ffn(tok, x_lat, x_ctx, fm_lat, fm_ctx, ao, mod, mix_post_g, pre_g, post_g, wf_all, wa_all, w1_all, w2_all,
             i, layer, tm):
    n = tok.n_all
    kern = functools.partial(_mix_ffn_kernel, n_lat_tiles=tok.n_lat // tm)
    return pl.pallas_call(
        kern,
        out_shape=jax.ShapeDtypeStruct((n, D_MODEL), F32),
        grid=(n // tm,),
        in_specs=[*tok.split_specs(tm, D_MODEL, x_lat is x_ctx), *tok.split_specs(tm, FOURIER_WIDTH, False),
                  _row_spec(tm, ATTN_WIDTH), tok.mod_spec(tm, layer),
                  _vec_spec(mix_post_g), _vec_spec(pre_g), _vec_spec(post_g),
                  _layer_resident(i, (FOURIER_WIDTH, D_MODEL)), _layer_resident(i, (ATTN_WIDTH, D_MODEL)),
                  _layer_resident(layer, (D_MODEL, D_FF)), _layer_resident(layer, (D_FF, D_MODEL))],
        out_specs=_row_spec(tm, D_MODEL),
        scratch_shapes=[pltpu.VMEM((tm, D_MODEL), F32)],
        compiler_params=_cparams(1),
        name="mix_ffn",
    )(x_lat, x_ctx, fm_lat, fm_ctx, ao, mod, mix_post_g[0], pre_g[0], post_g[0], wf_all, wa_all, w1_all, w2_all)


def _rope_block(x, cos, sin_hi, sin_lo):
    half = ROPE_AXIS_DIM // 2
    return (x * cos + pltpu.roll(x, half, axis=1) * sin_hi
            + pltpu.roll(x, LANES - half, axis=1) * sin_lo)


def _inproj_kernel(xl_ref, xc_ref, mod_ref, pre_ref, w_ref, cos_ref, shi_ref, slo_ref, f_ref, q_ref, k_ref, v_ref,
                   fs_ref, *, n_lat_tiles):
    x = _pick(n_lat_tiles, xl_ref, xc_ref)
    h = _norm_mod(x, pre_ref[...], mod_ref[0, 1:2, :], mod_ref[0, 0:1, :]).astype(BF16)
    p = jnp.dot(h, w_ref[...], preferred_element_type=F32)
    cos, shi, slo = cos_ref[...], shi_ref[...], slo_ref[...]
    pairs = x.shape[0] // 2
    n_slabs = FOURIER_WIDTH // LANES
    for s in range(n_slabs):
        fs_ref[s] = p[:, s * LANES:(s + 1) * LANES]
    for parity in range(2):
        for s in range(n_slabs):
            lo = parity * FOURIER_WIDTH + s * LANES
            f_ref[:, lo:lo + LANES] = fs_ref[s, pl.ds(parity, pairs, stride=2), :].astype(BF16)
    scale = HEAD_DIM ** -0.5 * LOG2_E
    for j in range(ATTN_WIDTH // LANES):
        lo = FOURIER_WIDTH + j * LANES
        q_ref[:, j * LANES:(j + 1) * LANES] = (_rope_block(p[:, lo:lo + LANES], cos, shi, slo) * scale).astype(BF16)
    k0 = FOURIER_WIDTH + ATTN_WIDTH
    k_ref[...] = _rope_block(p[:, k0:k0 + KV_WIDTH], cos, shi, slo).astype(BF16)
    v_ref[...] = p[:, k0 + KV_WIDTH:].astype(BF16)


def _rope_tables(seq, n_pad):
    pos = np.arange(seq)
    row = (pos // GRID_W).astype(np.float64)
    col = (pos % GRID_W).astype(np.float64)
    lane = np.arange(LANES)
    d = lane % HEAD_DIM
    j = d % (ROPE_AXIS_DIM // 2)
    inv = jnp.asarray(ROPE_BASE, F32) ** (-jnp.asarray(2 * j, F32) / ROPE_AXIS_DIM)
    use_col = jnp.asarray(d >= ROPE_AXIS_DIM)
    posv = jnp.where(use_col[None, :], jnp.asarray(col, F32)[:, None], jnp.asarray(row, F32)[:, None])
    ang = posv * inv[None, :]
    upper = jnp.asarray((d % ROPE_AXIS_DIM) >= ROPE_AXIS_DIM // 2)[None, :]
    cos, sin = jnp.cos(ang), jnp.sin(ang)
    sin_hi = jnp.where(upper, sin, 0.0)
    sin_lo = jnp.where(upper, 0.0, -sin)
    pad = lambda t, v: jnp.concatenate([t, jnp.full((n_pad, LANES), v, F32)], axis=0)
    return pad(cos, 1.0), pad(sin_hi, 0.0), pad(sin_lo, 0.0)


def _inproj(tok, x_lat, x_ctx, mod, pre_g, w_in_all, tables, i, layer, tm):
    per_batch = tok.seq // tm
    n_lat_tiles = tok.n_lat // tm
    tab_map = lambda i: (jnp.where(i < n_lat_tiles, i % per_batch, per_batch), 0)
    tab_spec = pl.BlockSpec((tm, LANES), tab_map)
    n = tok.n_all
    kern = functools.partial(_inproj_kernel, n_lat_tiles=n_lat_tiles)
    return pl.pallas_call(
        kern,
        out_shape=(jax.ShapeDtypeStruct((n // 2, 2 * FOURIER_WIDTH), BF16),
                   jax.ShapeDtypeStruct((n, ATTN_WIDTH), BF16),
                   jax.ShapeDtypeStruct((n, KV_WIDTH), BF16), jax.ShapeDtypeStruct((n, KV_WIDTH), BF16)),
        grid=(n // tm,),
        in_specs=[*tok.split_specs(tm, D_MODEL, x_lat is x_ctx), tok.mod_spec(tm, layer), _vec_spec(pre_g),
                  _layer_resident(i, (D_MODEL, IN_WIDTH)), tab_spec, tab_spec, tab_spec],
        out_specs=(_row_spec(tm // 2, 2 * FOURIER_WIDTH), _row_spec(tm, ATTN_WIDTH), _row_spec(tm, KV_WIDTH),
                   _row_spec(tm, KV_WIDTH)),
        scratch_shapes=[pltpu.VMEM((FOURIER_WIDTH // LANES, tm, LANES), F32)],
        compiler_params=_cparams(1),
        name="inproj",
    )(x_lat, x_ctx, mod, pre_g[0], w_in_all, *tables)


def _pair_heads(w, axis):
    shape = w.shape
    split = shape[:axis] + (N_KV_HEADS, HEAD_GROUP, HEAD_DIM) + shape[axis + 1:]
    return jnp.swapaxes(w.reshape(split), axis, axis + 1).reshape(shape)


def _dft_tables(length):
    half = length // 2
    k = np.arange(half)[:, None]
    m = np.arange(half)[None, :]

    def tab(n):
        ang = 2.0 * np.pi * ((k * n) % length) / length
        t = np.concatenate([np.cos(ang), -np.sin(ang)], axis=1) / math.sqrt(length)
        return jnp.asarray(t.astype(np.float32)).astype(BF16)

    return tab(2 * m), tab(2 * m + 1)


def _chan_table():
    n = FOURIER_GROUP_DIM
    k = np.arange(n)
    ang = 2.0 * np.pi * ((k[:, None] * k[None, :]) % n) / n
    t = np.concatenate([np.cos(ang), np.sin(ang)], axis=1) / math.sqrt(n)
    return jnp.asarray(t.astype(np.float32)).astype(BF16)


def _fourier_one(f_ref, chan_ref, pos_e_ref, pos_o_ref, o_ref, stk_ref):
    gd = FOURIER_GROUP_DIM
    half = f_ref.shape[0]
    row_chunk = min(half, 512)
    for parity in range(2):
        for g in range(FOURIER_GROUPS):
            lanes = slice(parity * FOURIER_WIDTH + g * gd, parity * FOURIER_WIDTH + (g + 1) * gd)
            z = jnp.dot(f_ref[:, lanes], chan_ref[...], preferred_element_type=F32)
            stk_ref[parity, 0:half, g * gd:(g + 1) * gd] = z[:, :gd].astype(BF16)
            stk_ref[parity, half:2 * half, g * gd:(g + 1) * gd] = z[:, gd:].astype(BF16)
    for r in range(half // row_chunk):
        rows = slice(r * row_chunk, (r + 1) * row_chunk)
        even = jnp.dot(pos_e_ref[rows, :], stk_ref[0], preferred_element_type=F32)
        odd = jnp.dot(pos_o_ref[rows, :], stk_ref[1], preferred_element_type=F32)
        o_ref[rows, :] = (even + odd).astype(BF16)
        o_ref[half + r * row_chunk:half + (r + 1) * row_chunk, :] = (even - odd).astype(BF16)


def _fourier_kernel(fl_ref, fc_ref, chan_ref, ple_ref, plo_ref, pce_ref, pco_ref, ol_ref, oc_ref, stkl_ref, stkc_ref):
    _fourier_one(fl_ref, chan_ref, ple_ref, plo_ref, ol_ref, stkl_ref)
    _fourier_one(fc_ref, chan_ref, pce_ref, pco_ref, oc_ref, stkc_ref)


def _fourier(tok, f_pairs, chan, tabs_lat, tabs_ctx):
    ctx0 = tok.n_lat // tok.n_ctx
    pair_blk = lambda rows, m: pl.BlockSpec((rows // 2, 2 * FOURIER_WIDTH), m)
    lat_blk = pl.BlockSpec((tok.seq, FOURIER_WIDTH), lambda b: (b, 0))
    ctx_blk = pl.BlockSpec((tok.n_ctx, FOURIER_WIDTH), lambda b: (b, 0))
    tabs = (*tabs_lat, *tabs_ctx)
    return pl.pallas_call(
        _fourier_kernel,
        out_shape=(jax.ShapeDtypeStruct((tok.n_lat, FOURIER_WIDTH), BF16),
                   jax.ShapeDtypeStruct((tok.batch * tok.n_ctx, FOURIER_WIDTH), BF16)),
        grid=(tok.batch,),
        in_specs=[pair_blk(tok.seq, lambda b: (b, 0)), pair_blk(tok.n_ctx, lambda b: (ctx0 + b, 0)),
                  _resident(chan.shape), *[_resident(t.shape) for t in tabs]],
        out_specs=(lat_blk, ctx_blk),
        scratch_shapes=[pltpu.VMEM((2, tok.seq, FOURIER_WIDTH), BF16),
                        pltpu.VMEM((2, tok.n_ctx, FOURIER_WIDTH), BF16)],
        compiler_params=_cparams(1),
        name="fourier",
    )(f_pairs, f_pairs, chan, *tabs)


ATTN_QBLOCKS = 2


def _attn_kernel(sink_ref, q_ref, k0_ref, k1_ref, k2_ref, k3_ref, kx_ref, v0_ref, v1_ref, v2_ref, v3_ref, vx_ref,
                 o_ref, *, n_steps_lat, layer_idx):
    j = pl.program_id(1)
    blk = WINDOW
    n_loc = 3 * blk
    rows = HEAD_GROUP * blk
    nt = (((1,), (1,)), ((), ()))
    k_refs, v_refs = (k0_ref, k1_ref, k2_ref, k3_ref), (v0_ref, v1_ref, v2_ref, v3_ref)

    def attend(half, with_local):
        r0 = half * blk
        q = jnp.concatenate([q_ref[r0:r0 + blk, i * LANES:(i + 1) * LANES] for i in range(HEAD_GROUP)], axis=0)
        k_ctx, v_ctx = kx_ref[...], vx_ref[...]
        if with_local:
            k_loc = jnp.concatenate([r[...] for r in k_refs[half:half + 3]], axis=0)
            v_loc = jnp.concatenate([r[...] for r in v_refs[half:half + 3]], axis=0)
            col_lo = jnp.where(j >= 1, 0, blk) if half == 0 else 0
            col_hi = jnp.where(j + 1 < n_steps_lat, n_loc, 2 * blk) if half == ATTN_QBLOCKS - 1 else n_loc
            qi = lax.broadcasted_iota(jnp.int32, (blk, 1), 0)
            kj = lax.broadcasted_iota(jnp.int32, (blk, n_loc), 1)
            visible = (kj >= jnp.maximum(qi, col_lo)) & (kj <= jnp.minimum(qi + 2 * WINDOW, col_hi - 1))
            bias = jnp.concatenate([jnp.where(visible, 0.0, NEG_INF)] * HEAD_GROUP, axis=0)
        row_blk = lax.broadcasted_iota(jnp.int32, (rows, 1), 0) // blk
        lane_o = lax.broadcasted_iota(jnp.int32, (rows, LANES), 1)
        outs = []
        for kvh in range(N_KV_HEADS):
            def own_lanes(t):
                lane = lax.broadcasted_iota(jnp.int32, t.shape, 1)
                return jnp.where((lane >= kvh * HEAD_DIM) & (lane < (kvh + 1) * HEAD_DIM), t, jnp.zeros_like(t))
            s_ctx = lax.dot_general(q, own_lanes(k_ctx), nt, preferred_element_type=F32)
            sink = jnp.zeros((rows, 1), F32)
            for i in range(HEAD_GROUP):
                sink = jnp.where(row_blk == i, sink_ref[layer_idx, kvh * HEAD_GROUP + i] * LOG2_E, sink)
            m = jnp.maximum(jnp.max(s_ctx, axis=-1, keepdims=True), sink)
            if with_local:
                s_loc = lax.dot_general(q, own_lanes(k_loc), nt, preferred_element_type=F32) + bias
                m = jnp.maximum(m, jnp.max(s_loc, axis=-1, keepdims=True))
            p_ctx = jnp.exp2(s_ctx - m)
            denom = jnp.sum(p_ctx, axis=-1, keepdims=True) + jnp.exp2(sink - m)
            o = jnp.dot(p_ctx.astype(BF16), v_ctx, preferred_element_type=F32)
            if with_local:
                p_loc = jnp.exp2(s_loc - m)
                denom = denom + jnp.sum(p_loc, axis=-1, keepdims=True)
                o = o + jnp.dot(p_loc.astype(BF16), v_loc, preferred_element_type=F32)
            outs.append(o / denom)
        merged = jnp.where(lane_o < HEAD_DIM, outs[0], outs[1])
        for i in range(HEAD_GROUP):
            o_ref[r0:r0 + blk, i * LANES:(i + 1) * LANES] = merged[i * blk:(i + 1) * blk, :].astype(BF16)

    def step(with_local):
        for half in range(ATTN_QBLOCKS):
            attend(half, with_local)

    pl.when(j < n_steps_lat)(lambda: step(True))
    pl.when(j >= n_steps_lat)(lambda: step(False))


def _attention(tok, q, k, v, sink_all, layer_idx):
    blk = WINDOW
    qb = ATTN_QBLOCKS
    n_qblk = tok.seq // blk
    assert n_qblk % qb == 0 and tok.n_ctx % (qb * blk) == 0
    n_lat_steps = n_qblk // qb
    n_ctx_steps = tok.n_ctx // (qb * blk)
    ctx0 = tok.n_lat // tok.n_ctx

    def q_map(b, j):
        return (jnp.where(j < n_lat_steps, b * n_lat_steps + j,
                          tok.batch * n_lat_steps + b * n_ctx_steps + (j - n_lat_steps)), 0)

    def k_map(off):
        return lambda b, j: (b * n_qblk + jnp.clip(j * qb + off, 0, n_qblk - 1), 0)

    ctx_map = lambda b, j: (ctx0 + b, 0)
    kv_blks = [pl.BlockSpec((blk, KV_WIDTH), k_map(off)) for off in range(-1, qb + 1)]
    ctx_blk = pl.BlockSpec((tok.n_ctx, KV_WIDTH), ctx_map)
    kern = functools.partial(_attn_kernel, n_steps_lat=n_lat_steps, layer_idx=layer_idx)
    return pl.pallas_call(
        kern,
        out_shape=jax.ShapeDtypeStruct((tok.n_all, ATTN_WIDTH), BF16),
        grid=(tok.batch, n_lat_steps + n_ctx_steps),
        in_specs=[pl.BlockSpec(memory_space=pltpu.SMEM),
                  pl.BlockSpec((qb * blk, ATTN_WIDTH), q_map),
                  *kv_blks, ctx_blk, *kv_blks, ctx_blk],
        out_specs=pl.BlockSpec((qb * blk, ATTN_WIDTH), q_map),
        compiler_params=_cparams(2),
        name="window_attention",
    )(sink_all, q, *([k] * (qb + 2)), k, *([v] * (qb + 2)), v)


def _piece_perm():
    idx = np.arange(D_MODEL)
    a, b, c = idx // LANES, (idx // SSM_GROUP_DIM) % SSM_PIECES, idx % SSM_GROUP_DIM
    perm = np.zeros((D_MODEL, D_MODEL), np.float32)
    perm[idx, b * LANES + a * SSM_GROUP_DIM + c] = 1.0
    return jnp.asarray(perm).astype(BF16)


def _slab_pitch(blk):
    return blk + SUBLANES


def _s5_pre_kernel(x_ref, mod_ref, pre_ref, perm_ref, u_ref, hs_ref, *, blk, batch, n_blocks):
    p, b = pl.program_id(0), pl.program_id(1)
    pitch = _slab_pitch(blk)

    @pl.when(p < n_blocks)
    def _():
        h = _norm_mod(x_ref[...], pre_ref[...], mod_ref[0, 1:2, :], mod_ref[0, 0:1, :])
        row0 = pl.multiple_of(b * pitch, SUBLANES)
        slot0 = (p % 2) * SSM_SLABS
        for s in range(SSM_SLABS):
            hs_ref[slot0 + s, pl.ds(row0, blk), :] = h[:, s * LANES:(s + 1) * LANES]

    @pl.when(p >= 1)
    def _():
        half = SSM_CHUNK // 2
        cpb = blk // SSM_CHUNK
        rows_h = cpb * batch
        s = b
        src = ((p - 1) % 2) * SSM_SLABS + s
        lhs = []
        for hh in range(2):
            for i in range(cpb):
                t0 = i * SSM_CHUNK + hh * half
                lhs.append(jnp.concatenate(
                    [hs_ref[src, pl.ds(t0 + j, batch, stride=pitch), :] for j in range(half)], axis=1))
        lhs = jnp.concatenate(lhs, axis=0).astype(BF16)
        out = jnp.dot(lhs, perm_ref[...], preferred_element_type=F32).astype(BF16)
        for hh in range(2):
            for g in range(SSM_PIECES):
                u_ref[s * SSM_PIECES + g, :, hh * LANES:(hh + 1) * LANES] = (
                    out[hh * rows_h:(hh + 1) * rows_h, g * LANES:(g + 1) * LANES])


def _s5_pre(tok, xa, mod, pre_g, perm, layer, blk):
    assert tok.batch == SSM_SLABS
    (n_blocks, _), row_map, mod_map = tok.pos_grid(blk, True, layer)
    n_lat_blk = tok.seq // blk
    n_ctx_blk = tok.n_ctx // blk
    cpb = blk // SSM_CHUNK
    n_chunks = (tok.seq + tok.n_ctx) // SSM_CHUNK
    last = n_blocks - 1

    def u_map(p, b):
        q = jnp.maximum(p - 1, 0)
        return (0, jnp.where(q < n_lat_blk, n_ctx_blk + q, q - n_lat_blk), 0)

    clamp = lambda m: (lambda p, b: m(jnp.minimum(p, last), b))
    kern = functools.partial(_s5_pre_kernel, blk=blk, batch=tok.batch, n_blocks=n_blocks)
    return pl.pallas_call(
        kern,
        out_shape=jax.ShapeDtypeStruct((SSM_GROUPS, n_chunks * tok.batch, SSM_CW), BF16),
        grid=(n_blocks + 1, tok.batch),
        in_specs=[pl.BlockSpec((blk, D_MODEL), clamp(row_map)),
                  pl.BlockSpec((None, 1, N_MOD, D_MODEL), clamp(mod_map)),
                  _vec_spec(pre_g), _resident((D_MODEL, D_MODEL))],
        out_specs=pl.BlockSpec((SSM_GROUPS, cpb * tok.batch, SSM_CW), u_map),
        scratch_shapes=[pltpu.VMEM((2 * SSM_SLABS, tok.batch * _slab_pitch(blk), LANES), F32)],
        compiler_params=_cparams(2),
        name="s5_pre",
    )(xa, mod, pre_g[0], perm)


def _cmul(ar, ai, br, bi):
    return ar * br - ai * bi, ar * bi + ai * br


def _s5_table_plan():
    t_n = SSM_CHUNK
    t = np.arange(t_n)
    return [
        [(0, t_n - 1 - t, 'b', 're'), (1, t, 'b', 're'), (0, t_n - 1 - t, 'b', 'im'), (1, t, 'b', 'im')],
        [(0, -t, 'b', 're'), (0, -t, 'b', 'im'), (1, t, 'b', 're'), (1, t, 'b', 'im')],
        [(0, t, 'c', 're'), (0, t, 'c', '-im'), (1, -t, 'c', 're'), (1, -t, 'c', '-im')],
        [(0, t + 1, 'c', 're'), (1, t_n - t, 'c', 're'), (0, t + 1, 'c', '-im'), (1, t_n - t, 'c', '-im')],
    ]


def _s5_operands(a_re, a_im, log_dt, b_re, b_im, c_re, c_im):
    t_n = SSM_CHUNK
    per_group = lambda x: jnp.swapaxes(x.astype(F32), 1, 2)
    l_re, l_im = per_group(a_re), per_group(a_im)
    dt = jnp.exp(per_group(log_dt))[..., None]
    z_re, z_im = l_re * dt, l_im * dt
    n_all = np.arange(1 - t_n, t_n + 1)
    positive = (n_all > 0)[:, None]
    mag, mag_inv = jnp.exp(z_re), jnp.exp(-z_re)
    up = (mag * jnp.cos(z_im), mag * jnp.sin(z_im))
    down = (mag_inv * jnp.cos(z_im), -mag_inv * jnp.sin(z_im))
    shape = z_re.shape[:-1] + (2 * t_n, z_re.shape[-1])
    acc = (jnp.ones(shape, F32), jnp.zeros(shape, F32))
    for k in range(int(t_n).bit_length()):
        bit = (((np.abs(n_all) >> k) & 1) == 1)[:, None]
        f_re = jnp.where(bit, jnp.where(positive, up[0][..., None, :], down[0][..., None, :]), 1.0)
        f_im = jnp.where(bit, jnp.where(positive, up[1][..., None, :], down[1][..., None, :]), 0.0)
        acc = _cmul(acc[0], acc[1], f_re, f_im)
        up, down = _cmul(*up, *up), _cmul(*down, *down)
    pows = jnp.stack(acc, axis=2)
    a1_re, a1_im = pows[:, :, 0, :, t_n, :], pows[:, :, 1, :, t_n, :]
    den = l_re * l_re + l_im * l_im
    r_re = ((a1_re - 1.0) * l_re + a1_im * l_im) / den
    r_im = (a1_im * l_re - (a1_re - 1.0) * l_im) / den
    bb_re, bb_im = _cmul(r_re[..., None], r_im[..., None], per_group(b_re), per_group(b_im))
    base = jnp.stack([jnp.swapaxes(bb_re, -1, -2), jnp.swapaxes(bb_im, -1, -2), per_group(c_re), per_group(c_im)],
                     axis=3)
    n_l, n_g = base.shape[0], base.shape[1]
    base = base.reshape(n_l, n_g, 8, SSM_GROUP_DIM, SSM_STATE)
    signed = jnp.concatenate([base, -base], axis=2)

    pows_rev = pows[:, :, :, :, ::-1, :]
    last = 2 * t_n - 1
    pw = pows[:, :, :, :, last, :]
    dec = jnp.concatenate([pw[:, :, 0, 0], pw[:, :, 0, 1], pw[:, :, 1, 0], pw[:, :, 1, 1]], axis=-1)[:, :, None, :]
    return pows, pows_rev, signed, dec


def _s5_kernel(u_ref, pw_ref, pr_ref, sg_ref, dec_ref, y_ref, v_ref, xin_ref, *, batch, n_chunks, n_ctx_chunks):
    gps = SSM_GROUPS_PER_STEP
    ns = SSM_STATE
    ns2 = 2 * ns
    cw = SSM_CW
    t_n = SSM_CHUNK
    nt = (((1,), (1,)), ((), ()))
    plan = _s5_table_plan()

    def table(g, k):
        p_re, p_im, w_a, w_b = [], [], [], []
        for d, expo, w, part in plan[k]:
            rising = expo[1] > expo[0]
            ref, lo = (pw_ref, int(expo[0]) + t_n - 1) if rising else (pr_ref, t_n - int(expo[0]))
            p_re.append(ref[g, 0, d, lo:lo + t_n, :])
            p_im.append(ref[g, 1, d, lo:lo + t_n, :])
            re, im = d * 4 + (0 if w == 'b' else 2), d * 4 + (1 if w == 'b' else 3)
            k_a, k_b = {'re': (re, im + 8), 'im': (im, re), '-im': (im + 8, re + 8)}[part]
            w_a.append(sg_ref[g, k_a])
            w_b.append(sg_ref[g, k_b])
        p_re, p_im, w_a, w_b = (jnp.concatenate(x, axis=1) for x in (p_re, p_im, w_a, w_b))
        return jnp.concatenate([p_re[t:t + 1, :] * w_a + p_im[t:t + 1, :] * w_b for t in range(t_n)], axis=0)

    for g in range(gps):
        v_ref[g] = jnp.dot(u_ref[g], table(g, 0).astype(BF16), preferred_element_type=F32)
    a_re = jnp.broadcast_to(dec_ref[:, :, 0:ns2], (gps, batch, ns2))
    a_im = jnp.broadcast_to(dec_ref[:, :, ns2:2 * ns2], (gps, batch, ns2))
    is_fwd = lax.broadcasted_iota(jnp.int32, (gps, batch, ns2), 2) < ns

    def step(k, carry):
        x_re, x_im = carry
        kb = jnp.where(k < n_ctx_chunks, n_ctx_chunks - 1 - k, n_chunks + n_ctx_chunks - 1 - k)
        rf = pl.multiple_of(k * batch, batch)
        rb = pl.multiple_of(kb * batch, batch)
        xin_ref[:, pl.ds(rf, batch), 0:ns] = x_re[:, :, 0:ns]
        xin_ref[:, pl.ds(rb, batch), ns:ns2] = x_re[:, :, ns:ns2]
        xin_ref[:, pl.ds(rf, batch), ns2:ns2 + ns] = x_im[:, :, 0:ns]
        xin_ref[:, pl.ds(rb, batch), ns2 + ns:2 * ns2] = x_im[:, :, ns:ns2]
        v_re = jnp.where(is_fwd, v_ref[:, pl.ds(rf, batch), 0:ns2], v_ref[:, pl.ds(rb, batch), 0:ns2])
        v_im = jnp.where(is_fwd, v_ref[:, pl.ds(rf, batch), ns2:2 * ns2], v_ref[:, pl.ds(rb, batch), ns2:2 * ns2])
        return a_re * x_re - a_im * x_im + v_re, a_re * x_im + a_im * x_re + v_im

    zero = jnp.zeros((gps, batch, ns2), F32)
    lax.fori_loop(0, n_chunks, step, (zero, zero))
    t_in = lax.broadcasted_iota(jnp.int32, (cw, cw), 0) // SSM_GROUP_DIM
    t_out = lax.broadcasted_iota(jnp.int32, (cw, cw), 1) // SSM_GROUP_DIM

    def split(x):
        hi = x.astype(BF16)
        return hi, (x - hi.astype(F32)).astype(BF16)

    def lag_kernel(e, ft, lanes):
        (eh, el), (fh, fl) = split(e[:, lanes]), split(ft[:, lanes])
        return (lax.dot_general(eh, fh, nt, preferred_element_type=F32)
                + lax.dot_general(eh, fl, nt, preferred_element_type=F32)
                + lax.dot_general(el, fh, nt, preferred_element_type=F32))

    for g in range(gps):
        e, ft = table(g, 1), table(g, 2)
        m = (jnp.where(t_out >= t_in, lag_kernel(e, ft, slice(0, ns2)), 0.0)
             + jnp.where(t_in >= t_out, lag_kernel(e, ft, slice(ns2, 2 * ns2)), 0.0)).astype(BF16)
        y_ref[g] = (jnp.dot(u_ref[g], m, preferred_element_type=F32)
                    + lax.dot_general(xin_ref[g].astype(BF16), table(g, 3).astype(BF16), nt,
                                      preferred_element_type=F32)).astype(BF16)


def _s5(u_t, operands, layer, batch, n_chunks, n_ctx_chunks):
    pows, pows_rev, signed, dec = operands
    gps = SSM_GROUPS_PER_STEP
    rows = n_chunks * batch
    kern = functools.partial(_s5_kernel, batch=batch, n_chunks=n_chunks, n_ctx_chunks=n_ctx_chunks)
    gspec = lambda r, c: pl.BlockSpec((gps, r, c), lambda i: (i, 0, 0))
    lspec = lambda shape: pl.BlockSpec((None, gps) + shape, lambda i: (layer, i) + (0,) * len(shape))
    return pl.pallas_call(
        kern,
        out_shape=jax.ShapeDtypeStruct((SSM_GROUPS, rows, SSM_CW), BF16),
        grid=(SSM_GROUPS // gps,),
        in_specs=[gspec(rows, SSM_CW), lspec(pows.shape[2:]), lspec(pows_rev.shape[2:]), lspec(signed.shape[2:]),
                  lspec(dec.shape[2:])],
        out_specs=gspec(rows, SSM_CW),
        scratch_shapes=[pltpu.VMEM((gps, rows, SSM_CW), F32), pltpu.VMEM((gps, rows, SSM_CW), F32)],
        compiler_params=_cparams(1),
        name="s5_scan",
    )(u_t, pows, pows_rev, signed, dec)


def _s5_unpack_kernel(y_ref, perm_ref, o_ref, ys_ref, tmp_ref, *, blk, batch):
    b = pl.program_id(1)

    @pl.when(b == 0)
    def _():
        half = SSM_CHUNK // 2
        cpb = blk // SSM_CHUNK
        rows_h = cpb * batch

        lhs = jnp.concatenate(
            [jnp.concatenate([y_ref[s * SSM_PIECES + g, :, hh * LANES:(hh + 1) * LANES]
                              for g in range(SSM_PIECES)], axis=1)
             for s in range(SSM_SLABS) for hh in range(2)], axis=0)
        tmp_ref[...] = jnp.dot(lhs, perm_ref[...], preferred_element_type=F32)

        def slab(s, carry):
            for hh in range(2):
                for i in range(cpb):
                    r0 = pl.multiple_of((s * 2 + hh) * rows_h + i * batch, batch)
                    t0 = i * SSM_CHUNK + hh * half
                    for j in range(half):
                        ys_ref[s, (t0 + j) * batch:(t0 + j + 1) * batch, :] = (
                            tmp_ref[pl.ds(r0, batch), j * LANES:(j + 1) * LANES])
            return carry

        lax.fori_loop(0, SSM_SLABS, slab, 0)

    for s in range(SSM_SLABS):
        o_ref[:, s * LANES:(s + 1) * LANES] = ys_ref[s, pl.ds(b, blk, stride=batch), :].astype(BF16)


def _s5_unpack(tok, y_t, perm, blk, with_ctx):
    grid, row_map, _ = tok.pos_grid(blk, with_ctx)
    n_lat_blk = tok.seq // blk
    n_ctx_blk = tok.n_ctx // blk
    cpb = blk // SSM_CHUNK
    y_map = lambda p, b: (0, jnp.where(p < n_lat_blk, n_ctx_blk + p, p - n_lat_blk), 0)
    n_rows = tok.n_all if with_ctx else tok.n_lat
    kern = functools.partial(_s5_unpack_kernel, blk=blk, batch=tok.batch)
    return pl.pallas_call(
        kern,
        out_shape=jax.ShapeDtypeStruct((n_rows, D_MODEL), BF16),
        grid=grid,
        in_specs=[pl.BlockSpec((SSM_GROUPS, cpb * tok.batch, SSM_CW), y_map), _resident((D_MODEL, D_MODEL))],
        out_specs=pl.BlockSpec((blk, D_MODEL), row_map),
        scratch_shapes=[pltpu.VMEM((SSM_SLABS, tok.batch * blk, LANES), F32),
                        pltpu.VMEM((SSM_SLABS * 2 * cpb * tok.batch, D_MODEL), F32)],
        compiler_params=_cparams(2),
        name="s5_unpack",
    )(y_t, perm)


def _glu_ffn_kernel(x_ref, y_ref, d_ref, mod_ref, mpre_ref, mpost_ref, pre_ref, post_ref, wg_ref, w1_ref, w2_ref,
                    o_ref, acc_ref):
    x = x_ref[...]
    h = _norm_mod(x, mpre_ref[...], mod_ref[0, 1:2, :], mod_ref[0, 0:1, :])
    y = y_ref[...].astype(F32) + d_ref[...] * h
    z = jnp.dot(jax.nn.gelu(y).astype(BF16), wg_ref[...], preferred_element_type=F32)
    out = z[:, :D_MODEL] * jax.nn.sigmoid(z[:, D_MODEL:])
    x = x + mod_ref[0, 2:3, :] * _rms(out, mpost_ref[...])
    _ffn_body(x, mod_ref, pre_ref, post_ref, w1_ref, w2_ref, o_ref, acc_ref)


def _glu_ffn(tok, xa, n_rows, y_tok, d_skip, mod, mix_pre_g, mix_post_g, pre_g, post_g, glu_all, w1_all, w2_all,
             i, layer, tm):
    return pl.pallas_call(
        _glu_ffn_kernel,
        out_shape=jax.ShapeDtypeStruct((n_rows, D_MODEL), F32),
        grid=(n_rows // tm,),
        in_specs=[_row_spec(tm, D_MODEL), _row_spec(tm, D_MODEL), _vec_spec(d_skip), tok.mod_spec(tm, layer),
                  _vec_spec(mix_pre_g), _vec_spec(mix_post_g), _vec_spec(pre_g), _vec_spec(post_g),
                  _layer_resident(i, (D_MODEL, 2 * D_MODEL)),
                  _layer_resident(layer, (D_MODEL, D_FF)), _layer_resident(layer, (D_FF, D_MODEL))],
        out_specs=_row_spec(tm, D_MODEL),
        scratch_shapes=[pltpu.VMEM((tm, D_MODEL), F32)],
        compiler_params=_cparams(1),
        name="glu_ffn",
    )(xa, y_tok, d_skip[0], mod, mix_pre_g[0], mix_post_g[0], pre_g[0], post_g[0], glu_all, w1_all, w2_all)


def _tile(limit, *sizes):
    tm = limit
    while any(s % tm for s in sizes):
        tm //= 2
    return tm


def kernel(x, c, ctx, c_ctx, mod_w, mod_b, mix_pre_g, mix_post_g, ffn_pre_g, ffn_post_g, ffn_w1, ffn_w2,
           even_w_in, even_w_out, even_sink, ssm_a_re, ssm_a_im, ssm_log_dt, ssm_b_re, ssm_b_im, ssm_c_re,
           ssm_c_im, ssm_d, ssm_glu_w):
    batch, seq, _ = x.shape
    n_ctx = ctx.shape[1]
    tok = _Tokens(batch, seq, n_ctx)
    assert seq % WINDOW == 0 and n_ctx % WINDOW == 0 and tok.n_lat % n_ctx == 0
    assert batch == SUBLANES
    tm = _tile(256, seq, n_ctx)
    tm_wide = _tile(512, seq, batch * n_ctx)
    tm_in = _tile(1024, seq, batch * n_ctx)

    x_lat, x_ctx = x.reshape(tok.n_lat, D_MODEL), ctx.astype(x.dtype).reshape(-1, D_MODEL)

    n_cond = 2 * SUBLANES
    cond = jnp.zeros((n_cond, D_MODEL), F32).at[:batch].set(c).at[batch].set(c_ctx)
    mod = _modulation(cond, mod_w, mod_b).reshape(DEPTH, n_cond, N_MOD, D_MODEL)

    rope = _rope_tables(seq, tm_in)
    piece_perm = _piece_perm()
    dft_chan, dft_lat, dft_ctx = _chan_table(), _dft_tables(seq), _dft_tables(n_ctx)
    rows3 = lambda t: t.reshape(t.shape[0], 1, t.shape[1])
    w1_all, w2_all, glu_all = ffn_w1.astype(BF16), ffn_w2.astype(BF16), ssm_glu_w.astype(BF16)
    q0, k0 = FOURIER_WIDTH, FOURIER_WIDTH + ATTN_WIDTH
    w_in_all = jnp.concatenate([even_w_in[:, :, :q0], _pair_heads(even_w_in[:, :, q0:k0], 2),
                                even_w_in[:, :, k0:]], axis=2).astype(BF16)
    wf_all = even_w_out[:, :FOURIER_WIDTH].astype(BF16)
    wa_all = _pair_heads(even_w_out[:, FOURIER_WIDTH:], 1).astype(BF16)
    s5_operands = _s5_operands(ssm_a_re, ssm_a_im, ssm_log_dt, ssm_b_re, ssm_b_im, ssm_c_re, ssm_c_im)

    for layer in range(DEPTH):
        need_ctx = layer < DEPTH - 1
        n_rows = tok.n_all if need_ctx else tok.n_lat
        i = layer // 2
        mix_pre, mix_post = (rows3(mix_pre_g), layer), (rows3(mix_post_g), layer)
        ffn_pre, ffn_post = (rows3(ffn_pre_g), layer), (rows3(ffn_post_g), layer)
        if layer % 2 == 0:
            f, q, k, v = _inproj(tok, x_lat, x_ctx, mod, mix_pre, w_in_all, rope, i, layer, tm_in)
            fm_lat, fm_ctx = _fourier(tok, f, dft_chan, dft_lat, dft_ctx)
            ao = _attention(tok, q, k, v, even_sink, i)
            xa = _mix_ffn(tok, x_lat, x_ctx, fm_lat, fm_ctx, ao, mod, mix_post, ffn_pre, ffn_post,
                          wf_all, wa_all, w1_all, w2_all, i, layer, tm_wide)
        else:
            u_t = _s5_pre(tok, xa, mod, mix_pre, piece_perm, layer, tm)
            y_t = _s5(u_t, s5_operands, i, batch, (seq + n_ctx) // SSM_CHUNK, n_ctx // SSM_CHUNK)
            y_tok = _s5_unpack(tok, y_t, piece_perm, tm, need_ctx)
            xa = _glu_ffn(tok, xa, n_rows, y_tok, (rows3(ssm_d), i), mod, mix_pre, mix_post, ffn_pre, ffn_post,
                          glu_all, w1_all, w2_all, i, layer, tm_wide)
        x_lat = x_ctx = xa
    return xa[:tok.n_lat].reshape(batch, seq, D_MODEL)
```

```python
import functools
import math

import numpy as np
import jax
import jax.numpy as jnp
from jax import lax
from jax.experimental import pallas as pl
from jax.experimental.pallas import tpu as pltpu

D_MODEL = 1024
DEPTH = 4
N_MOD = 6
EPS = 1e-6
NEG_INF = -1e30
GRID_W = 64

FOURIER_GROUPS = 4
FOURIER_GROUP_DIM = 128
FOURIER_WIDTH = FOURIER_GROUPS * FOURIER_GROUP_DIM

N_HEADS = 8
N_KV_HEADS = 2
HEAD_GROUP = N_HEADS // N_KV_HEADS
HEAD_DIM = 64
ATTN_WIDTH = N_HEADS * HEAD_DIM
KV_WIDTH = N_KV_HEADS * HEAD_DIM
WINDOW = 128
ROPE_AXIS_DIM = HEAD_DIM // 2
ROPE_BASE = 10000.0
LOG2_E = math.log2(math.e)
IN_WIDTH = FOURIER_WIDTH + ATTN_WIDTH + 2 * KV_WIDTH

LANES = 128
SUBLANES = 8
VMEM_LIMIT = 56 * 1024 * 1024

SSM_GROUP_DIM = 16
SSM_GROUPS = D_MODEL // SSM_GROUP_DIM
SSM_STATE = 64
SSM_CHUNK = 16
SSM_CW = SSM_CHUNK * SSM_GROUP_DIM
SSM_GROUPS_PER_STEP = 4
SSM_SLABS = D_MODEL // LANES
SSM_PIECES = LANES // SSM_GROUP_DIM

D_FF = 4 * D_MODEL

F32 = jnp.float32
BF16 = jnp.bfloat16


def _cparams(n_axes):
    return pltpu.CompilerParams(dimension_semantics=("arbitrary",) * n_axes, vmem_limit_bytes=VMEM_LIMIT)


def _resident(shape):
    nd = len(shape)
    return pl.BlockSpec(shape, lambda *_: (0,) * nd, pipeline_mode=pl.Buffered(1))


def _rms(x, g):
    return x * lax.rsqrt(jnp.mean(x * x, axis=-1, keepdims=True) + EPS) * g


def _norm_mod(x, g, sc, sh):
    return _rms(x, g) * (1.0 + sc) + sh


def _mod_kernel(cond_ref, w_ref, b_ref, o_ref):
    cond = cond_ref[...]
    s = cond * jax.nn.sigmoid(cond)
    s_hi = s.astype(BF16)
    s_lo = (s - s_hi.astype(F32)).astype(BF16)
    w = w_ref[0].astype(BF16)
    o_ref[0] = (jnp.dot(s_hi, w, preferred_element_type=F32) + jnp.dot(s_lo, w, preferred_element_type=F32)
                + b_ref[0])


def _modulation(cond, mod_w, mod_b):
    rows = cond.shape[0]
    tn = 2048
    n = N_MOD * D_MODEL
    return pl.pallas_call(
        _mod_kernel,
        out_shape=jax.ShapeDtypeStruct((DEPTH, rows, n), F32),
        grid=(DEPTH, n // tn),
        in_specs=[pl.BlockSpec((rows, D_MODEL), lambda l, j: (0, 0)),
                  pl.BlockSpec((1, D_MODEL, tn), lambda l, j: (l, 0, j)),
                  pl.BlockSpec((1, 1, tn), lambda l, j: (l, 0, j))],
        out_specs=pl.BlockSpec((1, rows, tn), lambda l, j: (l, 0, j)),
        compiler_params=_cparams(2),
        name="modulation",
    )(cond, mod_w, mod_b.reshape(DEPTH, 1, n))


class _Tokens:
    def __init__(self, batch, seq, n_ctx):
        self.batch, self.seq, self.n_ctx = batch, seq, n_ctx
        self.n_lat = batch * seq
        self.n_all = self.n_lat + batch * n_ctx

    def mod_spec(self, tm, layer):
        per_batch = self.seq // tm
        return pl.BlockSpec((None, 1, N_MOD, D_MODEL),
                            lambda i: (layer, jnp.minimum(i // per_batch, self.batch), 0, 0))

    def split_specs(self, tm, width, joined):
        nlt = self.n_lat // tm
        ctx_map = (lambda i: (jnp.maximum(i, nlt), 0)) if joined else (lambda i: (jnp.maximum(i - nlt, 0), 0))
        return pl.BlockSpec((tm, width), lambda i: (jnp.minimum(i, nlt - 1), 0)), pl.BlockSpec((tm, width), ctx_map)

    def pos_grid(self, blk, with_ctx, layer=0):
        n_lat_blk = self.seq // blk
        n_ctx_blk = self.n_ctx // blk
        lat_blocks = self.n_lat // blk
        row_map = lambda p, b: (jnp.where(p < n_lat_blk, b * n_lat_blk + p, lat_blocks + b * n_ctx_blk + (p - n_lat_blk)), 0)
        mod_map = lambda p, b: (layer, jnp.where(p < n_lat_blk, b, self.batch), 0, 0)
        grid = (n_lat_blk + (n_ctx_blk if with_ctx else 0), self.batch)
        return grid, row_map, mod_map


def _pick(n_lat_tiles, lat_ref, ctx_ref):
    return jnp.where(pl.program_id(0) < n_lat_tiles, lat_ref[...], ctx_ref[...])


def _layer_resident(layer, shape):
    nd = len(shape)
    return pl.BlockSpec((None,) + tuple(shape), lambda *_: (layer,) + (0,) * nd, pipeline_mode=pl.Buffered(1))


def _row_spec(tm, width):
    return pl.BlockSpec((tm, width), lambda i: (i, 0))


def _vec_spec(row):
    table, r = row
    return pl.BlockSpec((None, 1, table.shape[2]), lambda *_: (r, 0, 0))


FFN_CHUNK = 512


def _ffn_body(x, mod_ref, pre_ref, post_ref, w1_ref, w2_ref, o_ref, acc_ref):
    h = _norm_mod(x, pre_ref[...], mod_ref[0, 4:5, :], mod_ref[0, 3:4, :]).astype(BF16)
    for c in range(D_FF // FFN_CHUNK):
        sl = slice(c * FFN_CHUNK, (c + 1) * FFN_CHUNK)
        a = jnp.maximum(jnp.dot(h, w1_ref[:, sl], preferred_element_type=F32), 0.0)
        part = jnp.dot((a * a).astype(BF16), w2_ref[sl, :], preferred_element_type=F32)
        if c == 0:
            acc_ref[...] = part
        else:
            acc_ref[...] += part
    o_ref[...] = x + mod_ref[0, 5:6, :] * _rms(acc_ref[...], post_ref[...])


def _mix_ffn_kernel(xl_ref, xc_ref, fl_ref, fc_ref, al_ref, ac_ref, mod_ref, mpost_ref, pre_ref, post_ref,
                    wf_ref, wa_ref, w1_ref, w2_ref, o_ref, acc_ref, *, n_lat_tiles):
    y = (jnp.dot(_pick(n_lat_tiles, fl_ref, fc_ref), wf_ref[...], preferred_element_type=F32)
         + jnp.dot(_pick(n_lat_tiles, al_ref, ac_ref), wa_ref[...], preferred_element_type=F32))
    x = _pick(n_lat_tiles, xl_ref, xc_ref) + mod_ref[0, 2:3, :] * _rms(y, mpost_ref[...])
    _ffn_body(x, mod_ref, pre_ref, post_ref, w1_ref, w2_ref, o_ref, acc_ref)


def _mix_ffn(tok, x_lat, x_ctx, fm_lat, fm_ctx, ao_lat, ao_ctx, mod, mix_post_g, pre_g, post_g, wf_all, wa_all,
             w1_all, w2_all, i, layer, tm):
    n = tok.n_all
    kern = functools.partial(_mix_ffn_kernel, n_lat_tiles=tok.n_lat // tm)
    return pl.pallas_call(
        kern,
        out_shape=jax.ShapeDtypeStruct((n, D_MODEL), F32),
        grid=(n // tm,),
        in_specs=[*tok.split_specs(tm, D_MODEL, x_lat is x_ctx), *tok.split_specs(tm, FOURIER_WIDTH, False),
                  *tok.split_specs(tm, ATTN_WIDTH, False), tok.mod_spec(tm, layer),
                  _vec_spec(mix_post_g), _vec_spec(pre_g), _vec_spec(post_g),
                  _layer_resident(i, (FOURIER_WIDTH, D_MODEL)), _layer_resident(i, (ATTN_WIDTH, D_MODEL)),
                  _layer_resident(layer, (D_MODEL, D_FF)), _layer_resident(layer, (D_FF, D_MODEL))],
        out_specs=_row_spec(tm, D_MODEL),
        scratch_shapes=[pltpu.VMEM((tm, D_MODEL), F32)],
        compiler_params=_cparams(1),
        name="mix_ffn",
    )(x_lat, x_ctx, fm_lat, fm_ctx, ao_lat, ao_ctx, mod, mix_post_g[0], pre_g[0], post_g[0], wf_all, wa_all,
      w1_all, w2_all)


def _rope_block(x, cos, sin_hi, sin_lo):
    half = ROPE_AXIS_DIM // 2
    return (x * cos + pltpu.roll(x, half, axis=1) * sin_hi
            + pltpu.roll(x, LANES - half, axis=1) * sin_lo)


def _inproj_kernel(xl_ref, xc_ref, mod_ref, pre_ref, w_ref, cos_ref, shi_ref, slo_ref, f_ref, q_ref, k_ref, v_ref,
                   fs_ref, *, n_lat_tiles):
    x = _pick(n_lat_tiles, xl_ref, xc_ref)
    h = _norm_mod(x, pre_ref[...], mod_ref[0, 1:2, :], mod_ref[0, 0:1, :]).astype(BF16)
    p = jnp.dot(h, w_ref[...], preferred_element_type=F32)
    cos, shi, slo = cos_ref[...], shi_ref[...], slo_ref[...]
    pairs = x.shape[0] // 2
    n_slabs = FOURIER_WIDTH // LANES
    for s in range(n_slabs):
        fs_ref[s] = p[:, s * LANES:(s + 1) * LANES]
    for parity in range(2):
        for s in range(n_slabs):
            lo = parity * FOURIER_WIDTH + s * LANES
            f_ref[:, lo:lo + LANES] = fs_ref[s, pl.ds(parity, pairs, stride=2), :].astype(BF16)
    scale = HEAD_DIM ** -0.5 * LOG2_E
    for j in range(ATTN_WIDTH // LANES):
        lo = FOURIER_WIDTH + j * LANES
        q_ref[:, j * LANES:(j + 1) * LANES] = (_rope_block(p[:, lo:lo + LANES], cos, shi, slo) * scale).astype(BF16)
    k0 = FOURIER_WIDTH + ATTN_WIDTH
    k_ref[...] = _rope_block(p[:, k0:k0 + KV_WIDTH], cos, shi, slo).astype(BF16)
    v_ref[...] = p[:, k0 + KV_WIDTH:].astype(BF16)


def _rope_tables(seq, n_pad):
    pos = np.arange(seq)
    row = (pos // GRID_W).astype(np.float64)
    col = (pos % GRID_W).astype(np.float64)
    lane = np.arange(LANES)
    d = lane % HEAD_DIM
    j = d % (ROPE_AXIS_DIM // 2)
    inv = jnp.asarray(ROPE_BASE, F32) ** (-jnp.asarray(2 * j, F32) / ROPE_AXIS_DIM)
    use_col = jnp.asarray(d >= ROPE_AXIS_DIM)
    posv = jnp.where(use_col[None, :], jnp.asarray(col, F32)[:, None], jnp.asarray(row, F32)[:, None])
    ang = posv * inv[None, :]
    upper = jnp.asarray((d % ROPE_AXIS_DIM) >= ROPE_AXIS_DIM // 2)[None, :]
    cos, sin = jnp.cos(ang), jnp.sin(ang)
    sin_hi = jnp.where(upper, sin, 0.0)
    sin_lo = jnp.where(upper, 0.0, -sin)
    pad = lambda t, v: jnp.concatenate([t, jnp.full((n_pad, LANES), v, F32)], axis=0)
    return pad(cos, 1.0), pad(sin_hi, 0.0), pad(sin_lo, 0.0)


def _inproj(tok, x_lat, x_ctx, mod, pre_g, w_in_all, tables, i, layer, tm):
    per_batch = tok.seq // tm
    n_lat_tiles = tok.n_lat // tm
    tab_map = lambda i: (jnp.where(i < n_lat_tiles, i % per_batch, per_batch), 0)
    tab_spec = pl.BlockSpec((tm, LANES), tab_map)
    n = tok.n_all
    kern = functools.partial(_inproj_kernel, n_lat_tiles=n_lat_tiles)
    return pl.pallas_call(
        kern,
        out_shape=(jax.ShapeDtypeStruct((n // 2, 2 * FOURIER_WIDTH), BF16),
                   jax.ShapeDtypeStruct((n, ATTN_WIDTH), BF16),
                   jax.ShapeDtypeStruct((n, KV_WIDTH), BF16), jax.ShapeDtypeStruct((n, KV_WIDTH), BF16)),
        grid=(n // tm,),
        in_specs=[*tok.split_specs(tm, D_MODEL, x_lat is x_ctx), tok.mod_spec(tm, layer), _vec_spec(pre_g),
                  _layer_resident(i, (D_MODEL, IN_WIDTH)), tab_spec, tab_spec, tab_spec],
        out_specs=(_row_spec(tm // 2, 2 * FOURIER_WIDTH), _row_spec(tm, ATTN_WIDTH), _row_spec(tm, KV_WIDTH),
                   _row_spec(tm, KV_WIDTH)),
        scratch_shapes=[pltpu.VMEM((FOURIER_WIDTH // LANES, tm, LANES), F32)],
        compiler_params=_cparams(1),
        name="inproj",
    )(x_lat, x_ctx, mod, pre_g[0], w_in_all, *tables)


def _pair_heads(w, axis):
    shape = w.shape
    split = shape[:axis] + (N_KV_HEADS, HEAD_GROUP, HEAD_DIM) + shape[axis + 1:]
    return jnp.swapaxes(w.reshape(split), axis, axis + 1).reshape(shape)


def _dft_tables(length):
    half = length // 2
    k = np.arange(half)[:, None]
    m = np.arange(half)[None, :]

    def tab(n):
        ang = 2.0 * np.pi * ((k * n) % length) / length
        t = np.concatenate([np.cos(ang), -np.sin(ang)], axis=1) / math.sqrt(length)
        return jnp.asarray(t.astype(np.float32)).astype(BF16)

    return tab(2 * m), tab(2 * m + 1)


def _chan_table():
    n = FOURIER_GROUP_DIM
    k = np.arange(n)
    ang = 2.0 * np.pi * ((k[:, None] * k[None, :]) % n) / n
    t = np.concatenate([np.cos(ang), np.sin(ang)], axis=1) / math.sqrt(n)
    return jnp.asarray(t.astype(np.float32)).astype(BF16)


def _fourier_one(f_ref, chan_ref, pos_e_ref, pos_o_ref, o_ref, stk_ref):
    gd = FOURIER_GROUP_DIM
    half = f_ref.shape[0]
    row_chunk = min(half, 512)
    for parity in range(2):
        for g in range(FOURIER_GROUPS):
            lanes = slice(parity * FOURIER_WIDTH + g * gd, parity * FOURIER_WIDTH + (g + 1) * gd)
            z = jnp.dot(f_ref[:, lanes], chan_ref[...], preferred_element_type=F32)
            stk_ref[parity, 0:half, g * gd:(g + 1) * gd] = z[:, :gd].astype(BF16)
            stk_ref[parity, half:2 * half, g * gd:(g + 1) * gd] = z[:, gd:].astype(BF16)
    for r in range(half // row_chunk):
        rows = slice(r * row_chunk, (r + 1) * row_chunk)
        even = jnp.dot(pos_e_ref[rows, :], stk_ref[0], preferred_element_type=F32)
        odd = jnp.dot(pos_o_ref[rows, :], stk_ref[1], preferred_element_type=F32)
        o_ref[rows, :] = (even + odd).astype(BF16)
        o_ref[half + r * row_chunk:half + (r + 1) * row_chunk, :] = (even - odd).astype(BF16)


def _fourier_kernel(fl_ref, fc_ref, chan_ref, ple_ref, plo_ref, pce_ref, pco_ref, ol_ref, oc_ref, stkl_ref, stkc_ref):
    _fourier_one(fl_ref, chan_ref, ple_ref, plo_ref, ol_ref, stkl_ref)
    _fourier_one(fc_ref, chan_ref, pce_ref, pco_ref, oc_ref, stkc_ref)


def _fourier(tok, f_pairs, chan, tabs_lat, tabs_ctx):
    ctx0 = tok.n_lat // tok.n_ctx
    pair_blk = lambda rows, m: pl.BlockSpec((rows // 2, 2 * FOURIER_WIDTH), m)
    lat_blk = pl.BlockSpec((tok.seq, FOURIER_WIDTH), lambda b: (b, 0))
    ctx_blk = pl.BlockSpec((tok.n_ctx, FOURIER_WIDTH), lambda b: (b, 0))
    tabs = (*tabs_lat, *tabs_ctx)
    return pl.pallas_call(
        _fourier_kernel,
        out_shape=(jax.ShapeDtypeStruct((tok.n_lat, FOURIER_WIDTH), BF16),
                   jax.ShapeDtypeStruct((tok.batch * tok.n_ctx, FOURIER_WIDTH), BF16)),
        grid=(tok.batch,),
        in_specs=[pair_blk(tok.seq, lambda b: (b, 0)), pair_blk(tok.n_ctx, lambda b: (ctx0 + b, 0)),
                  _resident(chan.shape), *[_resident(t.shape) for t in tabs]],
        out_specs=(lat_blk, ctx_blk),
        scratch_shapes=[pltpu.VMEM((2, tok.seq, FOURIER_WIDTH), BF16),
                        pltpu.VMEM((2, tok.n_ctx, FOURIER_WIDTH), BF16)],
        compiler_params=_cparams(1),
        name="fourier",
    )(f_pairs, f_pairs, chan, *tabs)


ATTN_QBLOCKS = 4


def _attn_kernel(sink_ref, q_ref, *refs, n_qb, n_steps, layer_idx, with_local):
    j = pl.program_id(1)
    blk = WINDOW
    n_loc = 3 * blk
    rows = HEAD_GROUP * blk
    nt = (((1,), (1,)), ((), ()))
    if with_local:
        kp_ref, km_ref, kn_ref, kx_ref, vp_ref, vm_ref, vn_ref, vx_ref, o_ref = refs
        k_cat = jnp.concatenate([kp_ref[...], km_ref[...], kn_ref[...]], axis=0)
        v_cat = jnp.concatenate([vp_ref[...], vm_ref[...], vn_ref[...]], axis=0)
    else:
        kx_ref, vx_ref, o_ref = refs

    def attend(half):
        r0 = half * blk
        q = jnp.concatenate([q_ref[r0:r0 + blk, i * LANES:(i + 1) * LANES] for i in range(HEAD_GROUP)], axis=0)
        k_ctx, v_ctx = kx_ref[...], vx_ref[...]
        if with_local:
            k_loc, v_loc = k_cat[r0:r0 + n_loc], v_cat[r0:r0 + n_loc]
            col_lo = jnp.where(j >= 1, 0, blk) if half == 0 else 0
            col_hi = jnp.where(j + 1 < n_steps, n_loc, 2 * blk) if half == n_qb - 1 else n_loc
            qi = lax.broadcasted_iota(jnp.int32, (blk, 1), 0)
            kj = lax.broadcasted_iota(jnp.int32, (blk, n_loc), 1)
            visible = (kj >= jnp.maximum(qi, col_lo)) & (kj <= jnp.minimum(qi + 2 * WINDOW, col_hi - 1))
            bias = jnp.concatenate([jnp.where(visible, 0.0, NEG_INF)] * HEAD_GROUP, axis=0)
        row_blk = lax.broadcasted_iota(jnp.int32, (rows, 1), 0) // blk
        lane_o = lax.broadcasted_iota(jnp.int32, (rows, LANES), 1)
        outs = []
        for kvh in range(N_KV_HEADS):
            def own_lanes(t):
                lane = lax.broadcasted_iota(jnp.int32, t.shape, 1)
                return jnp.where((lane >= kvh * HEAD_DIM) & (lane < (kvh + 1) * HEAD_DIM), t, jnp.zeros_like(t))
            s_ctx = lax.dot_general(q, own_lanes(k_ctx), nt, preferred_element_type=F32)
            sink = jnp.zeros((rows, 1), F32)
            for i in range(HEAD_GROUP):
                sink = jnp.where(row_blk == i, sink_ref[layer_idx, kvh * HEAD_GROUP + i] * LOG2_E, sink)
            m = jnp.maximum(jnp.max(s_ctx, axis=-1, keepdims=True), sink)
            if with_local:
                s_loc = lax.dot_general(q, own_lanes(k_loc), nt, preferred_element_type=F32) + bias
                m = jnp.maximum(m, jnp.max(s_loc, axis=-1, keepdims=True))
            p_ctx = jnp.exp2(s_ctx - m)
            denom = jnp.sum(p_ctx, axis=-1, keepdims=True) + jnp.exp2(sink - m)
            o = jnp.dot(p_ctx.astype(BF16), v_ctx, preferred_element_type=F32)
            if with_local:
                p_loc = jnp.exp2(s_loc - m)
                denom = denom + jnp.sum(p_loc, axis=-1, keepdims=True)
                o = o + jnp.dot(p_loc.astype(BF16), v_loc, preferred_element_type=F32)
            outs.append(o / denom)
        merged = jnp.where(lane_o < HEAD_DIM, outs[0], outs[1])
        for i in range(HEAD_GROUP):
            o_ref[r0:r0 + blk, i * LANES:(i + 1) * LANES] = merged[i * blk:(i + 1) * blk, :].astype(BF16)

    for half in range(n_qb):
        attend(half)


def _attention(tok, q, k, v, sink_all, layer_idx):
    blk = WINDOW
    n_qblk = tok.seq // blk
    qb = math.gcd(ATTN_QBLOCKS, n_qblk)
    n_steps = n_qblk // qb
    ctx0 = tok.n_lat // tok.n_ctx
    smem = pl.BlockSpec(memory_space=pltpu.SMEM)
    ctx_blk = pl.BlockSpec((tok.n_ctx, KV_WIDTH), lambda b, j: (ctx0 + b, 0))

    main_map = lambda b, j: (b * n_steps + j, 0)
    prev_map = lambda b, j: (b * n_qblk + jnp.maximum(j * qb - 1, 0), 0)
    next_map = lambda b, j: (b * n_qblk + jnp.minimum(j * qb + qb, n_qblk - 1), 0)
    kv_specs = [pl.BlockSpec((blk, KV_WIDTH), prev_map), pl.BlockSpec((qb * blk, KV_WIDTH), main_map),
                pl.BlockSpec((blk, KV_WIDTH), next_map), ctx_blk]
    lat = pl.pallas_call(
        functools.partial(_attn_kernel, n_qb=qb, n_steps=n_steps, layer_idx=layer_idx, with_local=True),
        out_shape=jax.ShapeDtypeStruct((tok.n_lat, ATTN_WIDTH), BF16),
        grid=(tok.batch, n_steps),
        in_specs=[smem, pl.BlockSpec((qb * blk, ATTN_WIDTH), main_map), *kv_specs, *kv_specs],
        out_specs=pl.BlockSpec((qb * blk, ATTN_WIDTH), main_map),
        compiler_params=_cparams(2),
        name="window_attention",
    )(sink_all, q, k, k, k, k, v, v, v, v)

    n_cb = tok.n_ctx // blk
    ctx = pl.pallas_call(
        functools.partial(_attn_kernel, n_qb=n_cb, n_steps=1, layer_idx=layer_idx, with_local=False),
        out_shape=jax.ShapeDtypeStruct((tok.batch * tok.n_ctx, ATTN_WIDTH), BF16),
        grid=(tok.batch, 1),
        in_specs=[smem, pl.BlockSpec((tok.n_ctx, ATTN_WIDTH), lambda b, j: (ctx0 + b, 0)), ctx_blk, ctx_blk],
        out_specs=pl.BlockSpec((tok.n_ctx, ATTN_WIDTH), lambda b, j: (b, 0)),
        compiler_params=_cparams(2),
        name="context_attention",
    )(sink_all, q, k, v)
    return lat, ctx


def _piece_perm():
    idx = np.arange(D_MODEL)
    a, b, c = idx // LANES, (idx // SSM_GROUP_DIM) % SSM_PIECES, idx % SSM_GROUP_DIM
    perm = np.zeros((D_MODEL, D_MODEL), np.float32)
    perm[idx, b * LANES + a * SSM_GROUP_DIM + c] = 1.0
    return jnp.asarray(perm).astype(BF16)


def _slab_pitch(blk):
    return blk + SUBLANES


def _s5_pre_kernel(x_ref, mod_ref, pre_ref, perm_ref, u_ref, hs_ref, *, blk, batch, n_blocks):
    p, b = pl.program_id(0), pl.program_id(1)
    pitch = _slab_pitch(blk)

    @pl.when(p < n_blocks)
    def _():
        h = _norm_mod(x_ref[...], pre_ref[...], mod_ref[0, 1:2, :], mod_ref[0, 0:1, :])
        row0 = pl.multiple_of(b * pitch, SUBLANES)
        slot0 = (p % 2) * SSM_SLABS
        for s in range(SSM_SLABS):
            hs_ref[slot0 + s, pl.ds(row0, blk), :] = h[:, s * LANES:(s + 1) * LANES]

    @pl.when(p >= 1)
    def _():
        half = SSM_CHUNK // 2
        cpb = blk // SSM_CHUNK
        rows_h = cpb * batch
        s = b
        src = ((p - 1) % 2) * SSM_SLABS + s
        lhs = []
        for hh in range(2):
            for i in range(cpb):
                t0 = i * SSM_CHUNK + hh * half
                lhs.append(jnp.concatenate(
                    [hs_ref[src, pl.ds(t0 + j, batch, stride=pitch), :] for j in range(half)], axis=1))
        lhs = jnp.concatenate(lhs, axis=0).astype(BF16)
        out = jnp.dot(lhs, perm_ref[...], preferred_element_type=F32).astype(BF16)
        for hh in range(2):
            for g in range(SSM_PIECES):
                u_ref[s * SSM_PIECES + g, :, hh * LANES:(hh + 1) * LANES] = (
                    out[hh * rows_h:(hh + 1) * rows_h, g * LANES:(g + 1) * LANES])


def _s5_pre(tok, xa, mod, pre_g, perm, layer, blk):
    assert tok.batch == SSM_SLABS
    (n_blocks, _), row_map, mod_map = tok.pos_grid(blk, True, layer)
    n_lat_blk = tok.seq // blk
    n_ctx_blk = tok.n_ctx // blk
    cpb = blk // SSM_CHUNK
    n_chunks = (tok.seq + tok.n_ctx) // SSM_CHUNK
    last = n_blocks - 1

    def u_map(p, b):
        q = jnp.maximum(p - 1, 0)
        return (0, jnp.where(q < n_lat_blk, n_ctx_blk + q, q - n_lat_blk), 0)

    clamp = lambda m: (lambda p, b: m(jnp.minimum(p, last), b))
    kern = functools.partial(_s5_pre_kernel, blk=blk, batch=tok.batch, n_blocks=n_blocks)
    return pl.pallas_call(
        kern,
        out_shape=jax.ShapeDtypeStruct((SSM_GROUPS, n_chunks * tok.batch, SSM_CW), BF16),
        grid=(n_blocks + 1, tok.batch),
        in_specs=[pl.BlockSpec((blk, D_MODEL), clamp(row_map)),
                  pl.BlockSpec((None, 1, N_MOD, D_MODEL), clamp(mod_map)),
                  _vec_spec(pre_g), _resident((D_MODEL, D_MODEL))],
        out_specs=pl.BlockSpec((SSM_GROUPS, cpb * tok.batch, SSM_CW), u_map),
        scratch_shapes=[pltpu.VMEM((2 * SSM_SLABS, tok.batch * _slab_pitch(blk), LANES), F32)],
        compiler_params=_cparams(2),
        name="s5_pre",
    )(xa, mod, pre_g[0], perm)


def _cmul(ar, ai, br, bi):
    return ar * br - ai * bi, ar * bi + ai * br


def _s5_table_plan():
    t_n = SSM_CHUNK
    t = np.arange(t_n)
    return [
        [(0, t_n - 1 - t, 'b', 're'), (1, t, 'b', 're'), (0, t_n - 1 - t, 'b', 'im'), (1, t, 'b', 'im')],
        [(0, -t, 'b', 're'), (0, -t, 'b', 'im'), (1, t, 'b', 're'), (1, t, 'b', 'im')],
        [(0, t, 'c', 're'), (0, t, 'c', '-im'), (1, -t, 'c', 're'), (1, -t, 'c', '-im')],
        [(0, t + 1, 'c', 're'), (1, t_n - t, 'c', 're'), (0, t + 1, 'c', '-im'), (1, t_n - t, 'c', '-im')],
    ]


def _s5_operands(a_re, a_im, log_dt, b_re, b_im, c_re, c_im):
    t_n = SSM_CHUNK
    per_group = lambda x: jnp.swapaxes(x.astype(F32), 1, 2)
    l_re, l_im = per_group(a_re), per_group(a_im)
    dt = jnp.exp(per_group(log_dt))[..., None]
    z_re, z_im = l_re * dt, l_im * dt
    n_all = np.arange(1 - t_n, t_n + 1)
    positive = (n_all > 0)[:, None]
    mag, mag_inv = jnp.exp(z_re), jnp.exp(-z_re)
    up = (mag * jnp.cos(z_im), mag * jnp.sin(z_im))
    down = (mag_inv * jnp.cos(z_im), -mag_inv * jnp.sin(z_im))
    shape = z_re.shape[:-1] + (2 * t_n, z_re.shape[-1])
    acc = (jnp.ones(shape, F32), jnp.zeros(shape, F32))
    for k in range(int(t_n).bit_length()):
        bit = (((np.abs(n_all) >> k) & 1) == 1)[:, None]
        f_re = jnp.where(bit, jnp.where(positive, up[0][..., None, :], down[0][..., None, :]), 1.0)
        f_im = jnp.where(bit, jnp.where(positive, up[1][..., None, :], down[1][..., None, :]), 0.0)
        acc = _cmul(acc[0], acc[1], f_re, f_im)
        up, down = _cmul(*up, *up), _cmul(*down, *down)
    pows = jnp.stack(acc, axis=2)
    a1_re, a1_im = pows[:, :, 0, :, t_n, :], pows[:, :, 1, :, t_n, :]
    den = l_re * l_re + l_im * l_im
    r_re = ((a1_re - 1.0) * l_re + a1_im * l_im) / den
    r_im = (a1_im * l_re - (a1_re - 1.0) * l_im) / den
    bb_re, bb_im = _cmul(r_re[..., None], r_im[..., None], per_group(b_re), per_group(b_im))
    base = jnp.stack([jnp.swapaxes(bb_re, -1, -2), jnp.swapaxes(bb_im, -1, -2), per_group(c_re), per_group(c_im)],
                     axis=3)
    n_l, n_g = base.shape[0], base.shape[1]
    base = base.reshape(n_l, n_g, 8, SSM_GROUP_DIM, SSM_STATE)
    signed = jnp.concatenate([base, -base], axis=2)

    pows_rev = pows[:, :, :, :, ::-1, :]
    last = 2 * t_n - 1
    pw = pows[:, :, :, :, last, :]
    dec = jnp.concatenate([pw[:, :, 0, 0], pw[:, :, 0, 1], pw[:, :, 1, 0], pw[:, :, 1, 1]], axis=-1)[:, :, None, :]
    return pows, pows_rev, signed, dec


def _s5_kernel(u_ref, pw_ref, pr_ref, sg_ref, dec_ref, y_ref, v_ref, xin_ref, *, batch, n_chunks, n_ctx_chunks):
    gps = SSM_GROUPS_PER_STEP
    ns = SSM_STATE
    ns2 = 2 * ns
    cw = SSM_CW
    t_n = SSM_CHUNK
    nt = (((1,), (1,)), ((), ()))
    plan = _s5_table_plan()

    def table(g, k):
        p_re, p_im, w_a, w_b = [], [], [], []
        for d, expo, w, part in plan[k]:
            rising = expo[1] > expo[0]
            ref, lo = (pw_ref, int(expo[0]) + t_n - 1) if rising else (pr_ref, t_n - int(expo[0]))
            p_re.append(ref[g, 0, d, lo:lo + t_n, :])
            p_im.append(ref[g, 1, d, lo:lo + t_n, :])
            re, im = d * 4 + (0 if w == 'b' else 2), d * 4 + (1 if w == 'b' else 3)
            k_a, k_b = {'re': (re, im + 8), 'im': (im, re), '-im': (im + 8, re + 8)}[part]
            w_a.append(sg_ref[g, k_a])
            w_b.append(sg_ref[g, k_b])
        p_re, p_im, w_a, w_b = (jnp.concatenate(x, axis=1) for x in (p_re, p_im, w_a, w_b))
        return jnp.concatenate([p_re[t:t + 1, :] * w_a + p_im[t:t + 1, :] * w_b for t in range(t_n)], axis=0)

    for g in range(gps):
        v_ref[g] = jnp.dot(u_ref[g], table(g, 0).astype(BF16), preferred_element_type=F32)
    a_re = jnp.broadcast_to(dec_ref[:, :, 0:ns2], (gps, batch, ns2))
    a_im = jnp.broadcast_to(dec_ref[:, :, ns2:2 * ns2], (gps, batch, ns2))
    is_fwd = lax.broadcasted_iota(jnp.int32, (gps, batch, ns2), 2) < ns

    def step(k, carry):
        x_re, x_im = carry
        kb = jnp.where(k < n_ctx_chunks, n_ctx_chunks - 1 - k, n_chunks + n_ctx_chunks - 1 - k)
        rf = pl.multiple_of(k * batch, batch)
        rb = pl.multiple_of(kb * batch, batch)
        xin_ref[:, pl.ds(rf, batch), 0:ns] = x_re[:, :, 0:ns]
        xin_ref[:, pl.ds(rb, batch), ns:ns2] = x_re[:, :, ns:ns2]
        xin_ref[:, pl.ds(rf, batch), ns2:ns2 + ns] = x_im[:, :, 0:ns]
        xin_ref[:, pl.ds(rb, batch), ns2 + ns:2 * ns2] = x_im[:, :, ns:ns2]
        v_re = jnp.where(is_fwd, v_ref[:, pl.ds(rf, batch), 0:ns2], v_ref[:, pl.ds(rb, batch), 0:ns2])
        v_im = jnp.where(is_fwd, v_ref[:, pl.ds(rf, batch), ns2:2 * ns2], v_ref[:, pl.ds(rb, batch), ns2:2 * ns2])
        return a_re * x_re - a_im * x_im + v_re, a_re * x_im + a_im * x_re + v_im

    zero = jnp.zeros((gps, batch, ns2), F32)
    lax.fori_loop(0, n_chunks, step, (zero, zero))
    t_in = lax.broadcasted_iota(jnp.int32, (cw, cw), 0) // SSM_GROUP_DIM
    t_out = lax.broadcasted_iota(jnp.int32, (cw, cw), 1) // SSM_GROUP_DIM

    def split(x):
        hi = x.astype(BF16)
        return hi, (x - hi.astype(F32)).astype(BF16)

    def lag_kernel(e, ft, lanes):
        (eh, el), (fh, fl) = split(e[:, lanes]), split(ft[:, lanes])
        return (lax.dot_general(eh, fh, nt, preferred_element_type=F32)
                + lax.dot_general(eh, fl, nt, preferred_element_type=F32)
                + lax.dot_general(el, fh, nt, preferred_element_type=F32))

    for g in range(gps):
        e, ft = table(g, 1), table(g, 2)
        m = (jnp.where(t_out >= t_in, lag_kernel(e, ft, slice(0, ns2)), 0.0)
             + jnp.where(t_in >= t_out, lag_kernel(e, ft, slice(ns2, 2 * ns2)), 0.0)).astype(BF16)
        y_ref[g] = (jnp.dot(u_ref[g], m, preferred_element_type=F32)
                    + lax.dot_general(xin_ref[g].astype(BF16), table(g, 3).astype(BF16), nt,
                                      preferred_element_type=F32)).astype(BF16)


def _s5(u_t, operands, layer, batch, n_chunks, n_ctx_chunks):
    pows, pows_rev, signed, dec = operands
    gps = SSM_GROUPS_PER_STEP
    rows = n_chunks * batch
    kern = functools.partial(_s5_kernel, batch=batch, n_chunks=n_chunks, n_ctx_chunks=n_ctx_chunks)
    gspec = lambda r, c: pl.BlockSpec((gps, r, c), lambda i: (i, 0, 0))
    lspec = lambda shape: pl.BlockSpec((None, gps) + shape, lambda i: (layer, i) + (0,) * len(shape))
    return pl.pallas_call(
        kern,
        out_shape=jax.ShapeDtypeStruct((SSM_GROUPS, rows, SSM_CW), BF16),
        grid=(SSM_GROUPS // gps,),
        in_specs=[gspec(rows, SSM_CW), lspec(pows.shape[2:]), lspec(pows_rev.shape[2:]), lspec(signed.shape[2:]),
                  lspec(dec.shape[2:])],
        out_specs=gspec(rows, SSM_CW),
        scratch_shapes=[pltpu.VMEM((gps, rows, SSM_CW), F32), pltpu.VMEM((gps, rows, SSM_CW), F32)],
        compiler_params=_cparams(1),
        name="s5_scan",
    )(u_t, pows, pows_rev, signed, dec)


def _s5_unpack_kernel(y_ref, perm_ref, o_ref, ys_ref, tmp_ref, *, blk, batch):
    b = pl.program_id(1)

    @pl.when(b == 0)
    def _():
        half = SSM_CHUNK // 2
        cpb = blk // SSM_CHUNK
        rows_h = cpb * batch

        lhs = jnp.concatenate(
            [jnp.concatenate([y_ref[s * SSM_PIECES + g, :, hh * LANES:(hh + 1) * LANES]
                              for g in range(SSM_PIECES)], axis=1)
             for s in range(SSM_SLABS) for hh in range(2)], axis=0)
        tmp_ref[...] = jnp.dot(lhs, perm_ref[...], preferred_element_type=F32)

        def slab(s, carry):
            for hh in range(2):
                for i in range(cpb):
                    r0 = pl.multiple_of((s * 2 + hh) * rows_h + i * batch, batch)
                    t0 = i * SSM_CHUNK + hh * half
                    for j in range(half):
                        ys_ref[s, (t0 + j) * batch:(t0 + j + 1) * batch, :] = (
                            tmp_ref[pl.ds(r0, batch), j * LANES:(j + 1) * LANES])
            return carry

        lax.fori_loop(0, SSM_SLABS, slab, 0)

    for s in range(SSM_SLABS):
        o_ref[:, s * LANES:(s + 1) * LANES] = ys_ref[s, pl.ds(b, blk, stride=batch), :].astype(BF16)


def _s5_unpack(tok, y_t, perm, blk, with_ctx):
    grid, row_map, _ = tok.pos_grid(blk, with_ctx)
    n_lat_blk = tok.seq // blk
    n_ctx_blk = tok.n_ctx // blk
    cpb = blk // SSM_CHUNK
    y_map = lambda p, b: (0, jnp.where(p < n_lat_blk, n_ctx_blk + p, p - n_lat_blk), 0)
    n_rows = tok.n_all if with_ctx else tok.n_lat
    kern = functools.partial(_s5_unpack_kernel, blk=blk, batch=tok.batch)
    return pl.pallas_call(
        kern,
        out_shape=jax.ShapeDtypeStruct((n_rows, D_MODEL), BF16),
        grid=grid,
        in_specs=[pl.BlockSpec((SSM_GROUPS, cpb * tok.batch, SSM_CW), y_map), _resident((D_MODEL, D_MODEL))],
        out_specs=pl.BlockSpec((blk, D_MODEL), row_map),
        scratch_shapes=[pltpu.VMEM((SSM_SLABS, tok.batch * blk, LANES), F32),
                        pltpu.VMEM((SSM_SLABS * 2 * cpb * tok.batch, D_MODEL), F32)],
        compiler_params=_cparams(2),
        name="s5_unpack",
    )(y_t, perm)


def _glu_ffn_kernel(x_ref, y_ref, d_ref, mod_ref, mpre_ref, mpost_ref, pre_ref, post_ref, wg_ref, w1_ref, w2_ref,
                    o_ref, acc_ref):
    x = x_ref[...]
    h = _norm_mod(x, mpre_ref[...], mod_ref[0, 1:2, :], mod_ref[0, 0:1, :])
    y = y_ref[...].astype(F32) + d_ref[...] * h
    z = jnp.dot(jax.nn.gelu(y).astype(BF16), wg_ref[...], preferred_element_type=F32)
    out = z[:, :D_MODEL] * jax.nn.sigmoid(z[:, D_MODEL:])
    x = x + mod_ref[0, 2:3, :] * _rms(out, mpost_ref[...])
    _ffn_body(x, mod_ref, pre_ref, post_ref, w1_ref, w2_ref, o_ref, acc_ref)


def _glu_ffn(tok, xa, n_rows, y_tok, d_skip, mod, mix_pre_g, mix_post_g, pre_g, post_g, glu_all, w1_all, w2_all,
             i, layer, tm):
    return pl.pallas_call(
        _glu_ffn_kernel,
        out_shape=jax.ShapeDtypeStruct((n_rows, D_MODEL), F32),
        grid=(n_rows // tm,),
        in_specs=[_row_spec(tm, D_MODEL), _row_spec(tm, D_MODEL), _vec_spec(d_skip), tok.mod_spec(tm, layer),
                  _vec_spec(mix_pre_g), _vec_spec(mix_post_g), _vec_spec(pre_g), _vec_spec(post_g),
                  _layer_resident(i, (D_MODEL, 2 * D_MODEL)),
                  _layer_resident(layer, (D_MODEL, D_FF)), _layer_resident(layer, (D_FF, D_MODEL))],
        out_specs=_row_spec(tm, D_MODEL),
        scratch_shapes=[pltpu.VMEM((tm, D_MODEL), F32)],
        compiler_params=_cparams(1),
        name="glu_ffn",
    )(xa, y_tok, d_skip[0], mod, mix_pre_g[0], mix_post_g[0], pre_g[0], post_g[0], glu_all, w1_all, w2_all)


def _tile(limit, *sizes):
    tm = limit
    while any(s % tm for s in sizes):
        tm //= 2
    return tm


def kernel(x, c, ctx, c_ctx, mod_w, mod_b, mix_pre_g, mix_post_g, ffn_pre_g, ffn_post_g, ffn_w1, ffn_w2,
           even_w_in, even_w_out, even_sink, ssm_a_re, ssm_a_im, ssm_log_dt, ssm_b_re, ssm_b_im, ssm_c_re,
           ssm_c_im, ssm_d, ssm_glu_w):
    batch, seq, _ = x.shape
    n_ctx = ctx.shape[1]
    tok = _Tokens(batch, seq, n_ctx)
    assert seq % WINDOW == 0 and n_ctx % WINDOW == 0 and tok.n_lat % n_ctx == 0
    assert batch == SUBLANES
    tm = _tile(256, seq, n_ctx)
    tm_wide = _tile(512, seq, batch * n_ctx)
    tm_in = _tile(1024, seq, batch * n_ctx)

    x_lat, x_ctx = x.reshape(tok.n_lat, D_MODEL), ctx.astype(x.dtype).reshape(-1, D_MODEL)

    n_cond = 2 * SUBLANES
    cond = jnp.zeros((n_cond, D_MODEL), F32).at[:batch].set(c).at[batch].set(c_ctx)
    mod = _modulation(cond, mod_w, mod_b).reshape(DEPTH, n_cond, N_MOD, D_MODEL)

    rope = _rope_tables(seq, tm_in)
    piece_perm = _piece_perm()
    dft_chan, dft_lat, dft_ctx = _chan_table(), _dft_tables(seq), _dft_tables(n_ctx)
    rows3 = lambda t: t.reshape(t.shape[0], 1, t.shape[1])
    w1_all, w2_all, glu_all = ffn_w1.astype(BF16), ffn_w2.astype(BF16), ssm_glu_w.astype(BF16)
    q0, k0 = FOURIER_WIDTH, FOURIER_WIDTH + ATTN_WIDTH
    w_in_all = jnp.concatenate([even_w_in[:, :, :q0], _pair_heads(even_w_in[:, :, q0:k0], 2),
                                even_w_in[:, :, k0:]], axis=2).astype(BF16)
    wf_all = even_w_out[:, :FOURIER_WIDTH].astype(BF16)
    wa_all = _pair_heads(even_w_out[:, FOURIER_WIDTH:], 1).astype(BF16)
    s5_operands = _s5_operands(ssm_a_re, ssm_a_im, ssm_log_dt, ssm_b_re, ssm_b_im, ssm_c_re, ssm_c_im)

    for layer in range(DEPTH):
        need_ctx = layer < DEPTH - 1
        n_rows = tok.n_all if need_ctx else tok.n_lat
        i = layer // 2
        mix_pre, mix_post = (rows3(mix_pre_g), layer), (rows3(mix_post_g), layer)
        ffn_pre, ffn_post = (rows3(ffn_pre_g), layer), (rows3(ffn_post_g), layer)
        if layer % 2 == 0:
            f, q, k, v = _inproj(tok, x_lat, x_ctx, mod, mix_pre, w_in_all, rope, i, layer, tm_in)
            fm_lat, fm_ctx = _fourier(tok, f, dft_chan, dft_lat, dft_ctx)
            ao_lat, ao_ctx = _attention(tok, q, k, v, even_sink, i)
            xa = _mix_ffn(tok, x_lat, x_ctx, fm_lat, fm_ctx, ao_lat, ao_ctx, mod, mix_post, ffn_pre, ffn_post,
                          wf_all, wa_all, w1_all, w2_all, i, layer, tm_wide)
        else:
            u_t = _s5_pre(tok, xa, mod, mix_pre, piece_perm, layer, tm)
            y_t = _s5(u_t, s5_operands, i, batch, (seq + n_ctx) // SSM_CHUNK, n_ctx // SSM_CHUNK)
            y_tok = _s5_unpack(tok, y_t, piece_perm, tm, need_ctx)
            xa = _glu_ffn(tok, xa, n_rows, y_tok, (rows3(ssm_d), i), mod, mix_pre, mix_post, ffn_pre, ffn_post,
                          glu_all, w1_all, w2_all, i, layer, tm_wide)
        x_lat = x_ctx = xa
    return xa[:tok.n_lat].reshape(batch, seq, D_MODEL)
```

```python
import functools
import math

import numpy as np
import jax
import jax.numpy as jnp
from jax import lax
from jax.experimental import pallas as pl
from jax.experimental.pallas import tpu as pltpu

D_MODEL = 1024
DEPTH = 4
N_MOD = 6
EPS = 1e-6
NEG_INF = -1e30
GRID_W = 64

FOURIER_GROUPS = 4
FOURIER_GROUP_DIM = 128
FOURIER_WIDTH = FOURIER_GROUPS * FOURIER_GROUP_DIM

N_HEADS = 8
N_KV_HEADS = 2
HEAD_GROUP = N_HEADS // N_KV_HEADS
HEAD_DIM = 64
ATTN_WIDTH = N_HEADS * HEAD_DIM
KV_WIDTH = N_KV_HEADS * HEAD_DIM
WINDOW = 128
ROPE_AXIS_DIM = HEAD_DIM // 2
ROPE_BASE = 10000.0
LOG2_E = math.log2(math.e)
IN_WIDTH = FOURIER_WIDTH + ATTN_WIDTH + 2 * KV_WIDTH

LANES = 128
SUBLANES = 8
VMEM_LIMIT = 56 * 1024 * 1024

SSM_GROUP_DIM = 16
SSM_GROUPS = D_MODEL // SSM_GROUP_DIM
SSM_STATE = 64
SSM_CHUNK = 16
SSM_CW = SSM_CHUNK * SSM_GROUP_DIM
SSM_GROUPS_PER_STEP = 4
SSM_SLABS = D_MODEL // LANES
SSM_PIECES = LANES // SSM_GROUP_DIM

D_FF = 4 * D_MODEL

F32 = jnp.float32
BF16 = jnp.bfloat16


def _cparams(n_axes):
    return pltpu.CompilerParams(dimension_semantics=("arbitrary",) * n_axes, vmem_limit_bytes=VMEM_LIMIT)


def _resident(shape):
    nd = len(shape)
    return pl.BlockSpec(shape, lambda *_: (0,) * nd, pipeline_mode=pl.Buffered(1))


def _rms(x, g):
    return x * lax.rsqrt(jnp.mean(x * x, axis=-1, keepdims=True) + EPS) * g


def _norm_mod(x, g, sc, sh):
    return _rms(x, g) * (1.0 + sc) + sh


def _mod_kernel(cond_ref, w_ref, b_ref, o_ref):
    cond = cond_ref[...]
    s = cond * jax.nn.sigmoid(cond)
    s_hi = s.astype(BF16)
    s_lo = (s - s_hi.astype(F32)).astype(BF16)
    w = w_ref[0].astype(BF16)
    o_ref[0] = (jnp.dot(s_hi, w, preferred_element_type=F32) + jnp.dot(s_lo, w, preferred_element_type=F32)
                + b_ref[0])


def _modulation(cond, mod_w, mod_b):
    rows = cond.shape[0]
    tn = 2048
    n = N_MOD * D_MODEL
    return pl.pallas_call(
        _mod_kernel,
        out_shape=jax.ShapeDtypeStruct((DEPTH, rows, n), F32),
        grid=(DEPTH, n // tn),
        in_specs=[pl.BlockSpec((rows, D_MODEL), lambda l, j: (0, 0)),
                  pl.BlockSpec((1, D_MODEL, tn), lambda l, j: (l, 0, j)),
                  pl.BlockSpec((1, 1, tn), lambda l, j: (l, 0, j))],
        out_specs=pl.BlockSpec((1, rows, tn), lambda l, j: (l, 0, j)),
        compiler_params=_cparams(2),
        name="modulation",
    )(cond, mod_w, mod_b.reshape(DEPTH, 1, n))


class _Tokens:
    def __init__(self, batch, seq, n_ctx):
        self.batch, self.seq, self.n_ctx = batch, seq, n_ctx
        self.n_lat = batch * seq
        self.n_all = self.n_lat + batch * n_ctx

    def mod_spec(self, tm, layer):
        per_batch = self.seq // tm
        return pl.BlockSpec((None, 1, N_MOD, D_MODEL),
                            lambda i: (layer, jnp.minimum(i // per_batch, self.batch), 0, 0))

    def split_specs(self, tm, width, joined):
        nlt = self.n_lat // tm
        ctx_map = (lambda i: (jnp.maximum(i, nlt), 0)) if joined else (lambda i: (jnp.maximum(i - nlt, 0), 0))
        return pl.BlockSpec((tm, width), lambda i: (jnp.minimum(i, nlt - 1), 0)), pl.BlockSpec((tm, width), ctx_map)

    def pos_grid(self, blk, with_ctx, layer=0):
        n_lat_blk = self.seq // blk
        n_ctx_blk = self.n_ctx // blk
        lat_blocks = self.n_lat // blk
        row_map = lambda p, b: (jnp.where(p < n_lat_blk, b * n_lat_blk + p, lat_blocks + b * n_ctx_blk + (p - n_lat_blk)), 0)
        mod_map = lambda p, b: (layer, jnp.where(p < n_lat_blk, b, self.batch), 0, 0)
        grid = (n_lat_blk + (n_ctx_blk if with_ctx else 0), self.batch)
        return grid, row_map, mod_map


def _pick(n_lat_tiles, lat_ref, ctx_ref):
    return jnp.where(pl.program_id(0) < n_lat_tiles, lat_ref[...], ctx_ref[...])


def _layer_resident(layer, shape):
    nd = len(shape)
    return pl.BlockSpec((None,) + tuple(shape), lambda *_: (layer,) + (0,) * nd, pipeline_mode=pl.Buffered(1))


def _row_spec(tm, width):
    return pl.BlockSpec((tm, width), lambda i: (i, 0))


def _vec_spec(row):
    table, r = row
    return pl.BlockSpec((None, 1, table.shape[2]), lambda *_: (r, 0, 0))


FFN_CHUNK = 512


def _ffn_body(x, mod_ref, pre_ref, post_ref, w1_ref, w2_ref, o_ref, acc_ref):
    h = _norm_mod(x, pre_ref[...], mod_ref[0, 4:5, :], mod_ref[0, 3:4, :]).astype(BF16)
    for c in range(D_FF // FFN_CHUNK):
        sl = slice(c * FFN_CHUNK, (c + 1) * FFN_CHUNK)
        a = jnp.maximum(jnp.dot(h, w1_ref[:, sl], preferred_element_type=F32), 0.0)
        part = jnp.dot((a * a).astype(BF16), w2_ref[sl, :], preferred_element_type=F32)
        if c == 0:
            acc_ref[...] = part
        else:
            acc_ref[...] += part
    o_ref[...] = x + mod_ref[0, 5:6, :] * _rms(acc_ref[...], post_ref[...])


def _mix_ffn_kernel(xl_ref, xc_ref, fl_ref, fc_ref, al_ref, ac_ref, mod_ref, mpost_ref, pre_ref, post_ref,
                    wf_ref, wa_ref, w1_ref, w2_ref, o_ref, acc_ref, *, n_lat_tiles):
    y = (jnp.dot(_pick(n_lat_tiles, fl_ref, fc_ref), wf_ref[...], preferred_element_type=F32)
         + jnp.dot(_pick(n_lat_tiles, al_ref, ac_ref), wa_ref[...], preferred_element_type=F32))
    x = _pick(n_lat_tiles, xl_ref, xc_ref) + mod_ref[0, 2:3, :] * _rms(y, mpost_ref[...])
    _ffn_body(x, mod_ref, pre_ref, post_ref, w1_ref, w2_ref, o_ref, acc_ref)


def _mix_ffn(tok, x_lat, x_ctx, fm_lat, fm_ctx, ao_lat, ao_ctx, mod, mix_post_g, pre_g, post_g, wf_all, wa_all,
             w1_all, w2_all, i, layer, tm):
    n = tok.n_all
    kern = functools.partial(_mix_ffn_kernel, n_lat_tiles=tok.n_lat // tm)
    return pl.pallas_call(
        kern,
        out_shape=jax.ShapeDtypeStruct((n, D_MODEL), F32),
        grid=(n // tm,),
        in_specs=[*tok.split_specs(tm, D_MODEL, x_lat is x_ctx), *tok.split_specs(tm, FOURIER_WIDTH, False),
                  *tok.split_specs(tm, ATTN_WIDTH, False), tok.mod_spec(tm, layer),
                  _vec_spec(mix_post_g), _vec_spec(pre_g), _vec_spec(post_g),
                  _layer_resident(i, (FOURIER_WIDTH, D_MODEL)), _layer_resident(i, (ATTN_WIDTH, D_MODEL)),
                  _layer_resident(layer, (D_MODEL, D_FF)), _layer_resident(layer, (D_FF, D_MODEL))],
        out_specs=_row_spec(tm, D_MODEL),
        scratch_shapes=[pltpu.VMEM((tm, D_MODEL), F32)],
        compiler_params=_cparams(1),
        name="mix_ffn",
    )(x_lat, x_ctx, fm_lat, fm_ctx, ao_lat, ao_ctx, mod, mix_post_g[0], pre_g[0], post_g[0], wf_all, wa_all,
      w1_all, w2_all)


def _rope_block(x, cos, sin_hi, sin_lo):
    half = ROPE_AXIS_DIM // 2
    return (x * cos + pltpu.roll(x, half, axis=1) * sin_hi
            + pltpu.roll(x, LANES - half, axis=1) * sin_lo)


def _inproj_kernel(xl_ref, xc_ref, mod_ref, pre_ref, w_ref, cos_ref, shi_ref, slo_ref, f_ref, q_ref, k_ref, v_ref,
                   fs_ref, *, n_lat_tiles):
    x = _pick(n_lat_tiles, xl_ref, xc_ref)
    h = _norm_mod(x, pre_ref[...], mod_ref[0, 1:2, :], mod_ref[0, 0:1, :]).astype(BF16)
    p = jnp.dot(h, w_ref[...], preferred_element_type=F32)
    cos, shi, slo = cos_ref[...], shi_ref[...], slo_ref[...]
    pairs = x.shape[0] // 2
    n_slabs = FOURIER_WIDTH // LANES
    for s in range(n_slabs):
        fs_ref[s] = p[:, s * LANES:(s + 1) * LANES]
    for parity in range(2):
        for s in range(n_slabs):
            lo = parity * FOURIER_WIDTH + s * LANES
            f_ref[:, lo:lo + LANES] = fs_ref[s, pl.ds(parity, pairs, stride=2), :].astype(BF16)
    scale = HEAD_DIM ** -0.5 * LOG2_E
    for j in range(ATTN_WIDTH // LANES):
        lo = FOURIER_WIDTH + j * LANES
        q_ref[:, j * LANES:(j + 1) * LANES] = (_rope_block(p[:, lo:lo + LANES], cos, shi, slo) * scale).astype(BF16)
    k0 = FOURIER_WIDTH + ATTN_WIDTH
    k_ref[...] = _rope_block(p[:, k0:k0 + KV_WIDTH], cos, shi, slo).astype(BF16)
    v_ref[...] = p[:, k0 + KV_WIDTH:].astype(BF16)


def _rope_tables(seq, n_pad):
    pos = np.arange(seq)
    row = (pos // GRID_W).astype(np.float64)
    col = (pos % GRID_W).astype(np.float64)
    lane = np.arange(LANES)
    d = lane % HEAD_DIM
    j = d % (ROPE_AXIS_DIM // 2)
    inv = jnp.asarray(ROPE_BASE, F32) ** (-jnp.asarray(2 * j, F32) / ROPE_AXIS_DIM)
    use_col = jnp.asarray(d >= ROPE_AXIS_DIM)
    posv = jnp.where(use_col[None, :], jnp.asarray(col, F32)[:, None], jnp.asarray(row, F32)[:, None])
    ang = posv * inv[None, :]
    upper = jnp.asarray((d % ROPE_AXIS_DIM) >= ROPE_AXIS_DIM // 2)[None, :]
    cos, sin = jnp.cos(ang), jnp.sin(ang)
    sin_hi = jnp.where(upper, sin, 0.0)
    sin_lo = jnp.where(upper, 0.0, -sin)
    pad = lambda t, v: jnp.concatenate([t, jnp.full((n_pad, LANES), v, F32)], axis=0)
    return pad(cos, 1.0), pad(sin_hi, 0.0), pad(sin_lo, 0.0)


def _inproj(tok, x_lat, x_ctx, mod, pre_g, w_in_all, tables, i, layer, tm):
    per_batch = tok.seq // tm
    n_lat_tiles = tok.n_lat // tm
    tab_map = lambda i: (jnp.where(i < n_lat_tiles, i % per_batch, per_batch), 0)
    tab_spec = pl.BlockSpec((tm, LANES), tab_map)
    n = tok.n_all
    kern = functools.partial(_inproj_kernel, n_lat_tiles=n_lat_tiles)
    return pl.pallas_call(
        kern,
        out_shape=(jax.ShapeDtypeStruct((n // 2, 2 * FOURIER_WIDTH), BF16),
                   jax.ShapeDtypeStruct((n, ATTN_WIDTH), BF16),
                   jax.ShapeDtypeStruct((n, KV_WIDTH), BF16), jax.ShapeDtypeStruct((n, KV_WIDTH), BF16)),
        grid=(n // tm,),
        in_specs=[*tok.split_specs(tm, D_MODEL, x_lat is x_ctx), tok.mod_spec(tm, layer), _vec_spec(pre_g),
                  _layer_resident(i, (D_MODEL, IN_WIDTH)), tab_spec, tab_spec, tab_spec],
        out_specs=(_row_spec(tm // 2, 2 * FOURIER_WIDTH), _row_spec(tm, ATTN_WIDTH), _row_spec(tm, KV_WIDTH),
                   _row_spec(tm, KV_WIDTH)),
        scratch_shapes=[pltpu.VMEM((FOURIER_WIDTH // LANES, tm, LANES), F32)],
        compiler_params=_cparams(1),
        name="inproj",
    )(x_lat, x_ctx, mod, pre_g[0], w_in_all, *tables)


def _pair_heads(w, axis):
    shape = w.shape
    split = shape[:axis] + (N_KV_HEADS, HEAD_GROUP, HEAD_DIM) + shape[axis + 1:]
    return jnp.swapaxes(w.reshape(split), axis, axis + 1).reshape(shape)


def _dft_tables(length):
    half = length // 2
    k = np.arange(half)[:, None]
    m = np.arange(half)[None, :]

    def tab(n):
        ang = 2.0 * np.pi * ((k * n) % length) / length
        t = np.concatenate([np.cos(ang), -np.sin(ang)], axis=1) / math.sqrt(length)
        return jnp.asarray(t.astype(np.float32)).astype(BF16)

    return tab(2 * m), tab(2 * m + 1)


def _chan_table():
    n = FOURIER_GROUP_DIM
    k = np.arange(n)
    ang = 2.0 * np.pi * ((k[:, None] * k[None, :]) % n) / n
    t = np.concatenate([np.cos(ang), np.sin(ang)], axis=1) / math.sqrt(n)
    return jnp.asarray(t.astype(np.float32)).astype(BF16)


def _fourier_one(f_ref, chan_ref, pos_e_ref, pos_o_ref, o_ref, stk_ref):
    gd = FOURIER_GROUP_DIM
    half = f_ref.shape[0]
    row_chunk = min(half, 512)
    for parity in range(2):
        for g in range(FOURIER_GROUPS):
            lanes = slice(parity * FOURIER_WIDTH + g * gd, parity * FOURIER_WIDTH + (g + 1) * gd)
            z = jnp.dot(f_ref[:, lanes], chan_ref[...], preferred_element_type=F32)
            stk_ref[parity, 0:half, g * gd:(g + 1) * gd] = z[:, :gd].astype(BF16)
            stk_ref[parity, half:2 * half, g * gd:(g + 1) * gd] = z[:, gd:].astype(BF16)
    for r in range(half // row_chunk):
        rows = slice(r * row_chunk, (r + 1) * row_chunk)
        even = jnp.dot(pos_e_ref[rows, :], stk_ref[0], preferred_element_type=F32)
        odd = jnp.dot(pos_o_ref[rows, :], stk_ref[1], preferred_element_type=F32)
        o_ref[rows, :] = (even + odd).astype(BF16)
        o_ref[half + r * row_chunk:half + (r + 1) * row_chunk, :] = (even - odd).astype(BF16)


def _fourier_kernel(fl_ref, fc_ref, chan_ref, ple_ref, plo_ref, pce_ref, pco_ref, ol_ref, oc_ref, stkl_ref, stkc_ref):
    _fourier_one(fl_ref, chan_ref, ple_ref, plo_ref, ol_ref, stkl_ref)
    _fourier_one(fc_ref, chan_ref, pce_ref, pco_ref, oc_ref, stkc_ref)


def _fourier(tok, f_pairs, chan, tabs_lat, tabs_ctx):
    ctx0 = tok.n_lat // tok.n_ctx
    pair_blk = lambda rows, m: pl.BlockSpec((rows // 2, 2 * FOURIER_WIDTH), m)
    lat_blk = pl.BlockSpec((tok.seq, FOURIER_WIDTH), lambda b: (b, 0))
    ctx_blk = pl.BlockSpec((tok.n_ctx, FOURIER_WIDTH), lambda b: (b, 0))
    tabs = (*tabs_lat, *tabs_ctx)
    return pl.pallas_call(
        _fourier_kernel,
        out_shape=(jax.ShapeDtypeStruct((tok.n_lat, FOURIER_WIDTH), BF16),
                   jax.ShapeDtypeStruct((tok.batch * tok.n_ctx, FOURIER_WIDTH), BF16)),
        grid=(tok.batch,),
        in_specs=[pair_blk(tok.seq, lambda b: (b, 0)), pair_blk(tok.n_ctx, lambda b: (ctx0 + b, 0)),
                  _resident(chan.shape), *[_resident(t.shape) for t in tabs]],
        out_specs=(lat_blk, ctx_blk),
        scratch_shapes=[pltpu.VMEM((2, tok.seq, FOURIER_WIDTH), BF16),
                        pltpu.VMEM((2, tok.n_ctx, FOURIER_WIDTH), BF16)],
        compiler_params=_cparams(1),
        name="fourier",
    )(f_pairs, f_pairs, chan, *tabs)


ATTN_QBLOCKS = 8


def _attn_kernel(sink_ref, q_ref, *refs, n_qb, n_steps, layer_idx, with_local):
    j = pl.program_id(1)
    blk = WINDOW
    n_loc = 3 * blk
    rows = HEAD_GROUP * blk
    nt = (((1,), (1,)), ((), ()))
    if with_local:
        kp_ref, km_ref, kn_ref, kx_ref, vp_ref, vm_ref, vn_ref, vx_ref, o_ref = refs
        k_cat = jnp.concatenate([kp_ref[...], km_ref[...], kn_ref[...]], axis=0)
        v_cat = jnp.concatenate([vp_ref[...], vm_ref[...], vn_ref[...]], axis=0)
    else:
        kx_ref, vx_ref, o_ref = refs

    def attend(half):
        r0 = half * blk
        q = jnp.concatenate([q_ref[r0:r0 + blk, i * LANES:(i + 1) * LANES] for i in range(HEAD_GROUP)], axis=0)
        k_ctx, v_ctx = kx_ref[...], vx_ref[...]
        if with_local:
            k_loc, v_loc = k_cat[r0:r0 + n_loc], v_cat[r0:r0 + n_loc]
            col_lo = jnp.where(j >= 1, 0, blk) if half == 0 else 0
            col_hi = jnp.where(j + 1 < n_steps, n_loc, 2 * blk) if half == n_qb - 1 else n_loc
            qi = lax.broadcasted_iota(jnp.int32, (blk, 1), 0)
            kj = lax.broadcasted_iota(jnp.int32, (blk, n_loc), 1)
            visible = (kj >= jnp.maximum(qi, col_lo)) & (kj <= jnp.minimum(qi + 2 * WINDOW, col_hi - 1))
            bias = jnp.concatenate([jnp.where(visible, 0.0, NEG_INF)] * HEAD_GROUP, axis=0)
        row_blk = lax.broadcasted_iota(jnp.int32, (rows, 1), 0) // blk
        lane_o = lax.broadcasted_iota(jnp.int32, (rows, LANES), 1)
        outs = []
        for kvh in range(N_KV_HEADS):
            def own_lanes(t):
                lane = lax.broadcasted_iota(jnp.int32, t.shape, 1)
                return jnp.where((lane >= kvh * HEAD_DIM) & (lane < (kvh + 1) * HEAD_DIM), t, jnp.zeros_like(t))
            s_ctx = lax.dot_general(q, own_lanes(k_ctx), nt, preferred_element_type=F32)
            sink = jnp.zeros((rows, 1), F32)
            for i in range(HEAD_GROUP):
                sink = jnp.where(row_blk == i, sink_ref[layer_idx, kvh * HEAD_GROUP + i] * LOG2_E, sink)
            m = jnp.maximum(jnp.max(s_ctx, axis=-1, keepdims=True), sink)
            if with_local:
                s_loc = lax.dot_general(q, own_lanes(k_loc), nt, preferred_element_type=F32) + bias
                m = jnp.maximum(m, jnp.max(s_loc, axis=-1, keepdims=True))
            p_ctx = jnp.exp2(s_ctx - m)
            denom = jnp.sum(p_ctx, axis=-1, keepdims=True) + jnp.exp2(sink - m)
            o = jnp.dot(p_ctx.astype(BF16), v_ctx, preferred_element_type=F32)
            if with_local:
                p_loc = jnp.exp2(s_loc - m)
                denom = denom + jnp.sum(p_loc, axis=-1, keepdims=True)
                o = o + jnp.dot(p_loc.astype(BF16), v_loc, preferred_element_type=F32)
            outs.append(o / denom)
        merged = jnp.where(lane_o < HEAD_DIM, outs[0], outs[1])
        for i in range(HEAD_GROUP):
            o_ref[r0:r0 + blk, i * LANES:(i + 1) * LANES] = merged[i * blk:(i + 1) * blk, :].astype(BF16)

    for half in range(n_qb):
        attend(half)


def _attention(tok, q, k, v, sink_all, layer_idx):
    blk = WINDOW
    n_qblk = tok.seq // blk
    qb = math.gcd(ATTN_QBLOCKS, n_qblk)
    n_steps = n_qblk // qb
    ctx0 = tok.n_lat // tok.n_ctx
    smem = pl.BlockSpec(memory_space=pltpu.SMEM)
    ctx_blk = pl.BlockSpec((tok.n_ctx, KV_WIDTH), lambda b, j: (ctx0 + b, 0))

    main_map = lambda b, j: (b * n_steps + j, 0)
    prev_map = lambda b, j: (b * n_qblk + jnp.maximum(j * qb - 1, 0), 0)
    next_map = lambda b, j: (b * n_qblk + jnp.minimum(j * qb + qb, n_qblk - 1), 0)
    kv_specs = [pl.BlockSpec((blk, KV_WIDTH), prev_map), pl.BlockSpec((qb * blk, KV_WIDTH), main_map),
                pl.BlockSpec((blk, KV_WIDTH), next_map), ctx_blk]
    lat = pl.pallas_call(
        functools.partial(_attn_kernel, n_qb=qb, n_steps=n_steps, layer_idx=layer_idx, with_local=True),
        out_shape=jax.ShapeDtypeStruct((tok.n_lat, ATTN_WIDTH), BF16),
        grid=(tok.batch, n_steps),
        in_specs=[smem, pl.BlockSpec((qb * blk, ATTN_WIDTH), main_map), *kv_specs, *kv_specs],
        out_specs=pl.BlockSpec((qb * blk, ATTN_WIDTH), main_map),
        compiler_params=_cparams(2),
        name="window_attention",
    )(sink_all, q, k, k, k, k, v, v, v, v)

    n_cb = tok.n_ctx // blk
    ctx = pl.pallas_call(
        functools.partial(_attn_kernel, n_qb=n_cb, n_steps=1, layer_idx=layer_idx, with_local=False),
        out_shape=jax.ShapeDtypeStruct((tok.batch * tok.n_ctx, ATTN_WIDTH), BF16),
        grid=(tok.batch, 1),
        in_specs=[smem, pl.BlockSpec((tok.n_ctx, ATTN_WIDTH), lambda b, j: (ctx0 + b, 0)), ctx_blk, ctx_blk],
        out_specs=pl.BlockSpec((tok.n_ctx, ATTN_WIDTH), lambda b, j: (b, 0)),
        compiler_params=_cparams(2),
        name="context_attention",
    )(sink_all, q, k, v)
    return lat, ctx


def _piece_perm():
    idx = np.arange(D_MODEL)
    a, b, c = idx // LANES, (idx // SSM_GROUP_DIM) % SSM_PIECES, idx % SSM_GROUP_DIM
    perm = np.zeros((D_MODEL, D_MODEL), np.float32)
    perm[idx, b * LANES + a * SSM_GROUP_DIM + c] = 1.0
    return jnp.asarray(perm).astype(BF16)


def _slab_pitch(blk):
    return blk + SUBLANES


S5_PRE_BATCHES = 2


def _s5_pre_kernel(*refs, blk, batch, n_blocks):
    nb = S5_PRE_BATCHES
    x_refs, mod_refs = refs[:nb], refs[nb:2 * nb]
    pre_ref, perm_ref, u_ref, hs_ref = refs[2 * nb:]
    p, bb = pl.program_id(0), pl.program_id(1)
    pitch = _slab_pitch(blk)

    @pl.when(p < n_blocks)
    def _():
        slot0 = (p % 2) * SSM_SLABS
        for k in range(nb):
            h = _norm_mod(x_refs[k][...], pre_ref[...], mod_refs[k][0, 1:2, :], mod_refs[k][0, 0:1, :])
            row0 = pl.multiple_of((bb * nb + k) * pitch, SUBLANES)
            for s in range(SSM_SLABS):
                hs_ref[slot0 + s, pl.ds(row0, blk), :] = h[:, s * LANES:(s + 1) * LANES]

    @pl.when(p >= 1)
    def _():
        half = SSM_CHUNK // 2
        cpb = blk // SSM_CHUNK
        rows_h = cpb * batch
        for k in range(nb):
            s = bb * nb + k
            src = ((p - 1) % 2) * SSM_SLABS + s
            lhs = []
            for hh in range(2):
                for i in range(cpb):
                    t0 = i * SSM_CHUNK + hh * half
                    lhs.append(jnp.concatenate(
                        [hs_ref[src, pl.ds(t0 + j, batch, stride=pitch), :] for j in range(half)], axis=1))
            lhs = jnp.concatenate(lhs, axis=0).astype(BF16)
            out = jnp.dot(lhs, perm_ref[...], preferred_element_type=F32).astype(BF16)
            for hh in range(2):
                for g in range(SSM_PIECES):
                    u_ref[s * SSM_PIECES + g, :, hh * LANES:(hh + 1) * LANES] = (
                        out[hh * rows_h:(hh + 1) * rows_h, g * LANES:(g + 1) * LANES])


def _s5_pre(tok, xa, mod, pre_g, perm, layer, blk):
    assert tok.batch == SSM_SLABS
    (n_blocks, _), row_map, mod_map = tok.pos_grid(blk, True, layer)
    n_lat_blk = tok.seq // blk
    n_ctx_blk = tok.n_ctx // blk
    cpb = blk // SSM_CHUNK
    n_chunks = (tok.seq + tok.n_ctx) // SSM_CHUNK
    last = n_blocks - 1

    def u_map(p, b):
        q = jnp.maximum(p - 1, 0)
        return (0, jnp.where(q < n_lat_blk, n_ctx_blk + q, q - n_lat_blk), 0)

    nb = S5_PRE_BATCHES
    assert tok.batch % nb == 0
    of_batch = lambda m, k: (lambda p, bb: m(jnp.minimum(p, last), bb * nb + k))
    kern = functools.partial(_s5_pre_kernel, blk=blk, batch=tok.batch, n_blocks=n_blocks)
    return pl.pallas_call(
        kern,
        out_shape=jax.ShapeDtypeStruct((SSM_GROUPS, n_chunks * tok.batch, SSM_CW), BF16),
        grid=(n_blocks + 1, tok.batch // nb),
        in_specs=[*[pl.BlockSpec((blk, D_MODEL), of_batch(row_map, k)) for k in range(nb)],
                  *[pl.BlockSpec((None, 1, N_MOD, D_MODEL), of_batch(mod_map, k)) for k in range(nb)],
                  _vec_spec(pre_g), _resident((D_MODEL, D_MODEL))],
        out_specs=pl.BlockSpec((SSM_GROUPS, cpb * tok.batch, SSM_CW), u_map),
        scratch_shapes=[pltpu.VMEM((2 * SSM_SLABS, tok.batch * _slab_pitch(blk), LANES), F32)],
        compiler_params=_cparams(2),
        name="s5_pre",
    )(*([xa] * nb), *([mod] * nb), pre_g[0], perm)


def _cmul(ar, ai, br, bi):
    return ar * br - ai * bi, ar * bi + ai * br


def _s5_table_plan():
    t_n = SSM_CHUNK
    t = np.arange(t_n)
    return [
        [(0, t_n - 1 - t, 'b', 're'), (1, t, 'b', 're'), (0, t_n - 1 - t, 'b', 'im'), (1, t, 'b', 'im')],
        [(0, -t, 'b', 're'), (0, -t, 'b', 'im'), (1, t, 'b', 're'), (1, t, 'b', 'im')],
        [(0, t, 'c', 're'), (0, t, 'c', '-im'), (1, -t, 'c', 're'), (1, -t, 'c', '-im')],
        [(0, t + 1, 'c', 're'), (1, t_n - t, 'c', 're'), (0, t + 1, 'c', '-im'), (1, t_n - t, 'c', '-im')],
    ]


def _s5_operands(a_re, a_im, log_dt, b_re, b_im, c_re, c_im):
    t_n = SSM_CHUNK
    per_group = lambda x: jnp.swapaxes(x.astype(F32), 1, 2)
    l_re, l_im = per_group(a_re), per_group(a_im)
    dt = jnp.exp(per_group(log_dt))[..., None]
    z_re, z_im = l_re * dt, l_im * dt
    n_all = np.arange(1 - t_n, t_n + 1)
    positive = (n_all > 0)[:, None]
    mag, mag_inv = jnp.exp(z_re), jnp.exp(-z_re)
    up = (mag * jnp.cos(z_im), mag * jnp.sin(z_im))
    down = (mag_inv * jnp.cos(z_im), -mag_inv * jnp.sin(z_im))
    shape = z_re.shape[:-1] + (2 * t_n, z_re.shape[-1])
    acc = (jnp.ones(shape, F32), jnp.zeros(shape, F32))
    for k in range(int(t_n).bit_length()):
        bit = (((np.abs(n_all) >> k) & 1) == 1)[:, None]
        f_re = jnp.where(bit, jnp.where(positive, up[0][..., None, :], down[0][..., None, :]), 1.0)
        f_im = jnp.where(bit, jnp.where(positive, up[1][..., None, :], down[1][..., None, :]), 0.0)
        acc = _cmul(acc[0], acc[1], f_re, f_im)
        up, down = _cmul(*up, *up), _cmul(*down, *down)
    pows = jnp.stack(acc, axis=2)
    a1_re, a1_im = pows[:, :, 0, :, t_n, :], pows[:, :, 1, :, t_n, :]
    den = l_re * l_re + l_im * l_im
    r_re = ((a1_re - 1.0) * l_re + a1_im * l_im) / den
    r_im = (a1_im * l_re - (a1_re - 1.0) * l_im) / den
    bb_re, bb_im = _cmul(r_re[..., None], r_im[..., None], per_group(b_re), per_group(b_im))
    base = jnp.stack([jnp.swapaxes(bb_re, -1, -2), jnp.swapaxes(bb_im, -1, -2), per_group(c_re), per_group(c_im)],
                     axis=3)
    n_l, n_g = base.shape[0], base.shape[1]
    base = base.reshape(n_l, n_g, 8, SSM_GROUP_DIM, SSM_STATE)
    signed = jnp.concatenate([base, -base], axis=2)

    pows_rev = pows[:, :, :, :, ::-1, :]
    last = 2 * t_n - 1
    pw = pows[:, :, :, :, last, :]
    dec = jnp.concatenate([pw[:, :, 0, 0], pw[:, :, 0, 1], pw[:, :, 1, 0], pw[:, :, 1, 1]], axis=-1)[:, :, None, :]
    return pows, pows_rev, signed, dec


def _s5_kernel(u_ref, pw_ref, pr_ref, sg_ref, dec_ref, y_ref, v_ref, xin_ref, *, batch, n_chunks, n_ctx_chunks):
    gps = SSM_GROUPS_PER_STEP
    ns = SSM_STATE
    ns2 = 2 * ns
    cw = SSM_CW
    t_n = SSM_CHUNK
    nt = (((1,), (1,)), ((), ()))
    plan = _s5_table_plan()

    def table(g, k):
        p_re, p_im, w_a, w_b = [], [], [], []
        for d, expo, w, part in plan[k]:
            rising = expo[1] > expo[0]
            ref, lo = (pw_ref, int(expo[0]) + t_n - 1) if rising else (pr_ref, t_n - int(expo[0]))
            p_re.append(ref[g, 0, d, lo:lo + t_n, :])
            p_im.append(ref[g, 1, d, lo:lo + t_n, :])
            re, im = d * 4 + (0 if w == 'b' else 2), d * 4 + (1 if w == 'b' else 3)
            k_a, k_b = {'re': (re, im + 8), 'im': (im, re), '-im': (im + 8, re + 8)}[part]
            w_a.append(sg_ref[g, k_a])
            w_b.append(sg_ref[g, k_b])
        p_re, p_im, w_a, w_b = (jnp.concatenate(x, axis=1) for x in (p_re, p_im, w_a, w_b))
        return jnp.concatenate([p_re[t:t + 1, :] * w_a + p_im[t:t + 1, :] * w_b for t in range(t_n)], axis=0)

    for g in range(gps):
        v_ref[g] = jnp.dot(u_ref[g], table(g, 0).astype(BF16), preferred_element_type=F32)
    a_re = jnp.broadcast_to(dec_ref[:, :, 0:ns2], (gps, batch, ns2))
    a_im = jnp.broadcast_to(dec_ref[:, :, ns2:2 * ns2], (gps, batch, ns2))
    is_fwd = lax.broadcasted_iota(jnp.int32, (gps, batch, ns2), 2) < ns

    def step(k, carry):
        x_re, x_im = carry
        kb = jnp.where(k < n_ctx_chunks, n_ctx_chunks - 1 - k, n_chunks + n_ctx_chunks - 1 - k)
        rf = pl.multiple_of(k * batch, batch)
        rb = pl.multiple_of(kb * batch, batch)
        xin_ref[:, pl.ds(rf, batch), 0:ns] = x_re[:, :, 0:ns]
        xin_ref[:, pl.ds(rb, batch), ns:ns2] = x_re[:, :, ns:ns2]
        xin_ref[:, pl.ds(rf, batch), ns2:ns2 + ns] = x_im[:, :, 0:ns]
        xin_ref[:, pl.ds(rb, batch), ns2 + ns:2 * ns2] = x_im[:, :, ns:ns2]
        v_re = jnp.where(is_fwd, v_ref[:, pl.ds(rf, batch), 0:ns2], v_ref[:, pl.ds(rb, batch), 0:ns2])
        v_im = jnp.where(is_fwd, v_ref[:, pl.ds(rf, batch), ns2:2 * ns2], v_ref[:, pl.ds(rb, batch), ns2:2 * ns2])
        return a_re * x_re - a_im * x_im + v_re, a_re * x_im + a_im * x_re + v_im

    zero = jnp.zeros((gps, batch, ns2), F32)
    lax.fori_loop(0, n_chunks, step, (zero, zero))
    t_in = lax.broadcasted_iota(jnp.int32, (cw, cw), 0) // SSM_GROUP_DIM
    t_out = lax.broadcasted_iota(jnp.int32, (cw, cw), 1) // SSM_GROUP_DIM

    def split(x):
        hi = x.astype(BF16)
        return hi, (x - hi.astype(F32)).astype(BF16)

    def lag_kernel(e, ft, lanes):
        (eh, el), (fh, fl) = split(e[:, lanes]), split(ft[:, lanes])
        return (lax.dot_general(eh, fh, nt, preferred_element_type=F32)
                + lax.dot_general(eh, fl, nt, preferred_element_type=F32)
                + lax.dot_general(el, fh, nt, preferred_element_type=F32))

    for g in range(gps):
        e, ft = table(g, 1), table(g, 2)
        m = (jnp.where(t_out >= t_in, lag_kernel(e, ft, slice(0, ns2)), 0.0)
             + jnp.where(t_in >= t_out, lag_kernel(e, ft, slice(ns2, 2 * ns2)), 0.0)).astype(BF16)
        y_ref[g] = (jnp.dot(u_ref[g], m, preferred_element_type=F32)
                    + lax.dot_general(xin_ref[g].astype(BF16), table(g, 3).astype(BF16), nt,
                                      preferred_element_type=F32)).astype(BF16)


def _s5(u_t, operands, layer, batch, n_chunks, n_ctx_chunks):
    pows, pows_rev, signed, dec = operands
    gps = SSM_GROUPS_PER_STEP
    rows = n_chunks * batch
    kern = functools.partial(_s5_kernel, batch=batch, n_chunks=n_chunks, n_ctx_chunks=n_ctx_chunks)
    gspec = lambda r, c: pl.BlockSpec((gps, r, c), lambda i: (i, 0, 0))
    lspec = lambda shape: pl.BlockSpec((None, gps) + shape, lambda i: (layer, i) + (0,) * len(shape))
    return pl.pallas_call(
        kern,
        out_shape=jax.ShapeDtypeStruct((SSM_GROUPS, rows, SSM_CW), BF16),
        grid=(SSM_GROUPS // gps,),
        in_specs=[gspec(rows, SSM_CW), lspec(pows.shape[2:]), lspec(pows_rev.shape[2:]), lspec(signed.shape[2:]),
                  lspec(dec.shape[2:])],
        out_specs=gspec(rows, SSM_CW),
        scratch_shapes=[pltpu.VMEM((gps, rows, SSM_CW), F32), pltpu.VMEM((gps, rows, SSM_CW), F32)],
        compiler_params=_cparams(1),
        name="s5_scan",
    )(u_t, pows, pows_rev, signed, dec)


def _s5_unpack_kernel(y_ref, perm_ref, o_ref, ys_ref, tmp_ref, *, blk, batch):
    b = pl.program_id(1)

    @pl.when(b == 0)
    def _():
        half = SSM_CHUNK // 2
        cpb = blk // SSM_CHUNK
        rows_h = cpb * batch

        lhs = jnp.concatenate(
            [jnp.concatenate([y_ref[s * SSM_PIECES + g, :, hh * LANES:(hh + 1) * LANES]
                              for g in range(SSM_PIECES)], axis=1)
             for s in range(SSM_SLABS) for hh in range(2)], axis=0)
        tmp_ref[...] = jnp.dot(lhs, perm_ref[...], preferred_element_type=F32)

        def slab(s, carry):
            for hh in range(2):
                for i in range(cpb):
                    r0 = pl.multiple_of((s * 2 + hh) * rows_h + i * batch, batch)
                    t0 = i * SSM_CHUNK + hh * half
                    for j in range(half):
                        ys_ref[s, (t0 + j) * batch:(t0 + j + 1) * batch, :] = (
                            tmp_ref[pl.ds(r0, batch), j * LANES:(j + 1) * LANES])
            return carry

        lax.fori_loop(0, SSM_SLABS, slab, 0)

    for s in range(SSM_SLABS):
        o_ref[:, s * LANES:(s + 1) * LANES] = ys_ref[s, pl.ds(b, blk, stride=batch), :].astype(BF16)


def _s5_unpack(tok, y_t, perm, blk, with_ctx):
    grid, row_map, _ = tok.pos_grid(blk, with_ctx)
    n_lat_blk = tok.seq // blk
    n_ctx_blk = tok.n_ctx // blk
    cpb = blk // SSM_CHUNK
    y_map = lambda p, b: (0, jnp.where(p < n_lat_blk, n_ctx_blk + p, p - n_lat_blk), 0)
    n_rows = tok.n_all if with_ctx else tok.n_lat
    kern = functools.partial(_s5_unpack_kernel, blk=blk, batch=tok.batch)
    return pl.pallas_call(
        kern,
        out_shape=jax.ShapeDtypeStruct((n_rows, D_MODEL), BF16),
        grid=grid,
        in_specs=[pl.BlockSpec((SSM_GROUPS, cpb * tok.batch, SSM_CW), y_map), _resident((D_MODEL, D_MODEL))],
        out_specs=pl.BlockSpec((blk, D_MODEL), row_map),
        scratch_shapes=[pltpu.VMEM((SSM_SLABS, tok.batch * blk, LANES), F32),
                        pltpu.VMEM((SSM_SLABS * 2 * cpb * tok.batch, D_MODEL), F32)],
        compiler_params=_cparams(2),
        name="s5_unpack",
    )(y_t, perm)


def _glu_ffn_kernel(x_ref, y_ref, d_ref, mod_ref, mpre_ref, mpost_ref, pre_ref, post_ref, wg_ref, w1_ref, w2_ref,
                    o_ref, acc_ref):
    x = x_ref[...]
    h = _norm_mod(x, mpre_ref[...], mod_ref[0, 1:2, :], mod_ref[0, 0:1, :])
    y = y_ref[...].astype(F32) + d_ref[...] * h
    z = jnp.dot(jax.nn.gelu(y).astype(BF16), wg_ref[...], preferred_element_type=F32)
    out = z[:, :D_MODEL] * jax.nn.sigmoid(z[:, D_MODEL:])
    x = x + mod_ref[0, 2:3, :] * _rms(out, mpost_ref[...])
    _ffn_body(x, mod_ref, pre_ref, post_ref, w1_ref, w2_ref, o_ref, acc_ref)


def _glu_ffn(tok, xa, n_rows, y_tok, d_skip, mod, mix_pre_g, mix_post_g, pre_g, post_g, glu_all, w1_all, w2_all,
             i, layer, tm):
    return pl.pallas_call(
        _glu_ffn_kernel,
        out_shape=jax.ShapeDtypeStruct((n_rows, D_MODEL), F32),
        grid=(n_rows // tm,),
        in_specs=[_row_spec(tm, D_MODEL), _row_spec(tm, D_MODEL), _vec_spec(d_skip), tok.mod_spec(tm, layer),
                  _vec_spec(mix_pre_g), _vec_spec(mix_post_g), _vec_spec(pre_g), _vec_spec(post_g),
                  _layer_resident(i, (D_MODEL, 2 * D_MODEL)),
                  _layer_resident(layer, (D_MODEL, D_FF)), _layer_resident(layer, (D_FF, D_MODEL))],
        out_specs=_row_spec(tm, D_MODEL),
        scratch_shapes=[pltpu.VMEM((tm, D_MODEL), F32)],
        compiler_params=_cparams(1),
        name="glu_ffn",
    )(xa, y_tok, d_skip[0], mod, mix_pre_g[0], mix_post_g[0], pre_g[0], post_g[0], glu_all, w1_all, w2_all)


def _tile(limit, *sizes):
    tm = limit
    while any(s % tm for s in sizes):
        tm //= 2
    return tm


def kernel(x, c, ctx, c_ctx, mod_w, mod_b, mix_pre_g, mix_post_g, ffn_pre_g, ffn_post_g, ffn_w1, ffn_w2,
           even_w_in, even_w_out, even_sink, ssm_a_re, ssm_a_im, ssm_log_dt, ssm_b_re, ssm_b_im, ssm_c_re,
           ssm_c_im, ssm_d, ssm_glu_w):
    batch, seq, _ = x.shape
    n_ctx = ctx.shape[1]
    tok = _Tokens(batch, seq, n_ctx)
    assert seq % WINDOW == 0 and n_ctx % WINDOW == 0 and tok.n_lat % n_ctx == 0
    assert batch == SUBLANES
    tm = _tile(256, seq, n_ctx)
    tm_wide = _tile(512, seq, batch * n_ctx)
    tm_in = _tile(1024, seq, batch * n_ctx)

    x_lat, x_ctx = x.reshape(tok.n_lat, D_MODEL), ctx.astype(x.dtype).reshape(-1, D_MODEL)

    n_cond = 2 * SUBLANES
    cond = jnp.zeros((n_cond, D_MODEL), F32).at[:batch].set(c).at[batch].set(c_ctx)
    mod = _modulation(cond, mod_w, mod_b).reshape(DEPTH, n_cond, N_MOD, D_MODEL)

    rope = _rope_tables(seq, tm_in)
    piece_perm = _piece_perm()
    dft_chan, dft_lat, dft_ctx = _chan_table(), _dft_tables(seq), _dft_tables(n_ctx)
    rows3 = lambda t: t.reshape(t.shape[0], 1, t.shape[1])
    w1_all, w2_all, glu_all = ffn_w1.astype(BF16), ffn_w2.astype(BF16), ssm_glu_w.astype(BF16)
    q0, k0 = FOURIER_WIDTH, FOURIER_WIDTH + ATTN_WIDTH
    w_in_all = jnp.concatenate([even_w_in[:, :, :q0], _pair_heads(even_w_in[:, :, q0:k0], 2),
                                even_w_in[:, :, k0:]], axis=2).astype(BF16)
    wf_all = even_w_out[:, :FOURIER_WIDTH].astype(BF16)
    wa_all = _pair_heads(even_w_out[:, FOURIER_WIDTH:], 1).astype(BF16)
    s5_operands = _s5_operands(ssm_a_re, ssm_a_im, ssm_log_dt, ssm_b_re, ssm_b_im, ssm_c_re, ssm_c_im)

    for layer in range(DEPTH):
        need_ctx = layer < DEPTH - 1
        n_rows = tok.n_all if need_ctx else tok.n_lat
        i = layer // 2
        mix_pre, mix_post = (rows3(mix_pre_g), layer), (rows3(mix_post_g), layer)
        ffn_pre, ffn_post = (rows3(ffn_pre_g), layer), (rows3(ffn_post_g), layer)
        if layer % 2 == 0:
            f, q, k, v = _inproj(tok, x_lat, x_ctx, mod, mix_pre, w_in_all, rope, i, layer, tm_in)
            fm_lat, fm_ctx = _fourier(tok, f, dft_chan, dft_lat, dft_ctx)
            ao_lat, ao_ctx = _attention(tok, q, k, v, even_sink, i)
            xa = _mix_ffn(tok, x_lat, x_ctx, fm_lat, fm_ctx, ao_lat, ao_ctx, mod, mix_post, ffn_pre, ffn_post,
                          wf_all, wa_all, w1_all, w2_all, i, layer, tm_wide)
        else:
            u_t = _s5_pre(tok, xa, mod, mix_pre, piece_perm, layer, tm)
            y_t = _s5(u_t, s5_operands, i, batch, (seq + n_ctx) // SSM_CHUNK, n_ctx // SSM_CHUNK)
            y_tok = _s5_unpack(tok, y_t, piece_perm, tm, need_ctx)
            xa = _glu_ffn(tok, xa, n_rows, y_tok, (rows3(ssm_d), i), mod, mix_pre, mix_post, ffn_pre, ffn_post,
                          glu_all, w1_all, w2_all, i, layer, tm_wide)
        x_lat = x_ctx = xa
    return xa[:tok.n_lat].reshape(batch, seq, D_MODEL)
```

```python
import functools
import math

import numpy as np
import jax
import jax.numpy as jnp
from jax import lax
from jax.experimental import pallas as pl
from jax.experimental.pallas import tpu as pltpu

D_MODEL = 1024
DEPTH = 4
N_MOD = 6
EPS = 1e-6
NEG_INF = -1e30
GRID_W = 64

FOURIER_GROUPS = 4
FOURIER_GROUP_DIM = 128
FOURIER_WIDTH = FOURIER_GROUPS * FOURIER_GROUP_DIM

N_HEADS = 8
N_KV_HEADS = 2
HEAD_GROUP = N_HEADS // N_KV_HEADS
HEAD_DIM = 64
ATTN_WIDTH = N_HEADS * HEAD_DIM
KV_WIDTH = N_KV_HEADS * HEAD_DIM
WINDOW = 128
ROPE_AXIS_DIM = HEAD_DIM // 2
ROPE_BASE = 10000.0
LOG2_E = math.log2(math.e)
IN_WIDTH = FOURIER_WIDTH + ATTN_WIDTH + 2 * KV_WIDTH

LANES = 128
SUBLANES = 8
VMEM_LIMIT = 56 * 1024 * 1024

SSM_GROUP_DIM = 16
SSM_GROUPS = D_MODEL // SSM_GROUP_DIM
SSM_STATE = 64
SSM_CHUNK = 16
SSM_CW = SSM_CHUNK * SSM_GROUP_DIM
SSM_GROUPS_PER_STEP = 4
SSM_SLABS = D_MODEL // LANES
SSM_PIECES = LANES // SSM_GROUP_DIM

D_FF = 4 * D_MODEL

F32 = jnp.float32
BF16 = jnp.bfloat16


def _cparams(n_axes):
    return pltpu.CompilerParams(dimension_semantics=("arbitrary",) * n_axes, vmem_limit_bytes=VMEM_LIMIT)


def _resident(shape):
    nd = len(shape)
    return pl.BlockSpec(shape, lambda *_: (0,) * nd, pipeline_mode=pl.Buffered(1))


def _rms(x, g):
    return x * lax.rsqrt(jnp.mean(x * x, axis=-1, keepdims=True) + EPS) * g


def _norm_mod(x, g, sc, sh):
    return _rms(x, g) * (1.0 + sc) + sh


def _mod_kernel(cond_ref, w_ref, b_ref, o_ref):
    cond = cond_ref[...]
    s = cond * jax.nn.sigmoid(cond)
    s_hi = s.astype(BF16)
    s_lo = (s - s_hi.astype(F32)).astype(BF16)
    w = w_ref[0].astype(BF16)
    o_ref[0] = (jnp.dot(s_hi, w, preferred_element_type=F32) + jnp.dot(s_lo, w, preferred_element_type=F32)
                + b_ref[0])


def _modulation(cond, mod_w, mod_b):
    rows = cond.shape[0]
    tn = 2048
    n = N_MOD * D_MODEL
    return pl.pallas_call(
        _mod_kernel,
        out_shape=jax.ShapeDtypeStruct((DEPTH, rows, n), F32),
        grid=(DEPTH, n // tn),
        in_specs=[pl.BlockSpec((rows, D_MODEL), lambda l, j: (0, 0)),
                  pl.BlockSpec((1, D_MODEL, tn), lambda l, j: (l, 0, j)),
                  pl.BlockSpec((1, 1, tn), lambda l, j: (l, 0, j))],
        out_specs=pl.BlockSpec((1, rows, tn), lambda l, j: (l, 0, j)),
        compiler_params=_cparams(2),
        name="modulation",
    )(cond, mod_w, mod_b.reshape(DEPTH, 1, n))


class _Tokens:
    def __init__(self, batch, seq, n_ctx):
        self.batch, self.seq, self.n_ctx = batch, seq, n_ctx
        self.n_lat = batch * seq
        self.n_all = self.n_lat + batch * n_ctx

    def mod_spec(self, tm, layer):
        per_batch = self.seq // tm
        return pl.BlockSpec((None, 1, N_MOD, D_MODEL),
                            lambda i: (layer, jnp.minimum(i // per_batch, self.batch), 0, 0))

    def split_specs(self, tm, width, joined):
        nlt = self.n_lat // tm
        ctx_map = (lambda i: (jnp.maximum(i, nlt), 0)) if joined else (lambda i: (jnp.maximum(i - nlt, 0), 0))
        return pl.BlockSpec((tm, width), lambda i: (jnp.minimum(i, nlt - 1), 0)), pl.BlockSpec((tm, width), ctx_map)

    def pos_grid(self, blk, with_ctx, layer=0):
        n_lat_blk = self.seq // blk
        n_ctx_blk = self.n_ctx // blk
        lat_blocks = self.n_lat // blk
        row_map = lambda p, b: (jnp.where(p < n_lat_blk, b * n_lat_blk + p, lat_blocks + b * n_ctx_blk + (p - n_lat_blk)), 0)
        mod_map = lambda p, b: (layer, jnp.where(p < n_lat_blk, b, self.batch), 0, 0)
        grid = (n_lat_blk + (n_ctx_blk if with_ctx else 0), self.batch)
        return grid, row_map, mod_map


def _pick(n_lat_tiles, lat_ref, ctx_ref):
    return jnp.where(pl.program_id(0) < n_lat_tiles, lat_ref[...], ctx_ref[...])


def _layer_resident(layer, shape):
    nd = len(shape)
    return pl.BlockSpec((None,) + tuple(shape), lambda *_: (layer,) + (0,) * nd, pipeline_mode=pl.Buffered(1))


def _row_spec(tm, width):
    return pl.BlockSpec((tm, width), lambda i: (i, 0))


def _vec_spec(row):
    table, r = row
    return pl.BlockSpec((None, 1, table.shape[2]), lambda *_: (r, 0, 0))


FFN_CHUNK = 512


def _ffn_body(x, mod_ref, pre_ref, post_ref, w1_ref, w2_ref, o_ref, acc_ref):
    h = _norm_mod(x, pre_ref[...], mod_ref[0, 4:5, :], mod_ref[0, 3:4, :]).astype(BF16)
    for c in range(D_FF // FFN_CHUNK):
        sl = slice(c * FFN_CHUNK, (c + 1) * FFN_CHUNK)
        a = jnp.maximum(jnp.dot(h, w1_ref[:, sl], preferred_element_type=F32), 0.0)
        part = jnp.dot((a * a).astype(BF16), w2_ref[sl, :], preferred_element_type=F32)
        if c == 0:
            acc_ref[...] = part
        else:
            acc_ref[...] += part
    o_ref[...] = x + mod_ref[0, 5:6, :] * _rms(acc_ref[...], post_ref[...])


def _mix_ffn_kernel(xl_ref, xc_ref, fl_ref, fc_ref, al_ref, ac_ref, mod_ref, mpost_ref, pre_ref, post_ref,
                    wf_ref, wa_ref, w1_ref, w2_ref, o_ref, acc_ref, *, n_lat_tiles):
    y = (jnp.dot(_pick(n_lat_tiles, fl_ref, fc_ref), wf_ref[...], preferred_element_type=F32)
         + jnp.dot(_pick(n_lat_tiles, al_ref, ac_ref), wa_ref[...], preferred_element_type=F32))
    x = _pick(n_lat_tiles, xl_ref, xc_ref) + mod_ref[0, 2:3, :] * _rms(y, mpost_ref[...])
    _ffn_body(x, mod_ref, pre_ref, post_ref, w1_ref, w2_ref, o_ref, acc_ref)


def _mix_ffn(tok, x_lat, x_ctx, fm_lat, fm_ctx, ao_lat, ao_ctx, mod, mix_post_g, pre_g, post_g, wf_all, wa_all,
             w1_all, w2_all, i, layer, tm):
    n = tok.n_all
    kern = functools.partial(_mix_ffn_kernel, n_lat_tiles=tok.n_lat // tm)
    return pl.pallas_call(
        kern,
        out_shape=jax.ShapeDtypeStruct((n, D_MODEL), F32),
        grid=(n // tm,),
        in_specs=[*tok.split_specs(tm, D_MODEL, x_lat is x_ctx), *tok.split_specs(tm, FOURIER_WIDTH, False),
                  *tok.split_specs(tm, ATTN_WIDTH, False), tok.mod_spec(tm, layer),
                  _vec_spec(mix_post_g), _vec_spec(pre_g), _vec_spec(post_g),
                  _layer_resident(i, (FOURIER_WIDTH, D_MODEL)), _layer_resident(i, (ATTN_WIDTH, D_MODEL)),
                  _layer_resident(layer, (D_MODEL, D_FF)), _layer_resident(layer, (D_FF, D_MODEL))],
        out_specs=_row_spec(tm, D_MODEL),
        scratch_shapes=[pltpu.VMEM((tm, D_MODEL), F32)],
        compiler_params=_cparams(1),
        name="mix_ffn",
    )(x_lat, x_ctx, fm_lat, fm_ctx, ao_lat, ao_ctx, mod, mix_post_g[0], pre_g[0], post_g[0], wf_all, wa_all,
      w1_all, w2_all)


def _rope_block(x, cos, sin_hi, sin_lo):
    half = ROPE_AXIS_DIM // 2
    return (x * cos + pltpu.roll(x, half, axis=1) * sin_hi
            + pltpu.roll(x, LANES - half, axis=1) * sin_lo)


def _inproj_kernel(xl_ref, xc_ref, mod_ref, pre_ref, w_ref, cos_ref, shi_ref, slo_ref, f_ref, q_ref, k_ref, v_ref,
                   fs_ref, *, n_lat_tiles):
    x = _pick(n_lat_tiles, xl_ref, xc_ref)
    h = _norm_mod(x, pre_ref[...], mod_ref[0, 1:2, :], mod_ref[0, 0:1, :]).astype(BF16)
    p = jnp.dot(h, w_ref[...], preferred_element_type=F32)
    cos, shi, slo = cos_ref[...], shi_ref[...], slo_ref[...]
    pairs = x.shape[0] // 2
    n_slabs = FOURIER_WIDTH // LANES
    for s in range(n_slabs):
        fs_ref[s] = p[:, s * LANES:(s + 1) * LANES]
    for parity in range(2):
        for s in range(n_slabs):
            lo = parity * FOURIER_WIDTH + s * LANES
            f_ref[:, lo:lo + LANES] = fs_ref[s, pl.ds(parity, pairs, stride=2), :].astype(BF16)
    scale = HEAD_DIM ** -0.5 * LOG2_E
    for j in range(ATTN_WIDTH // LANES):
        lo = FOURIER_WIDTH + j * LANES
        q_ref[:, j * LANES:(j + 1) * LANES] = (_rope_block(p[:, lo:lo + LANES], cos, shi, slo) * scale).astype(BF16)
    k0 = FOURIER_WIDTH + ATTN_WIDTH
    k_ref[...] = _rope_block(p[:, k0:k0 + KV_WIDTH], cos, shi, slo).astype(BF16)
    v_ref[...] = p[:, k0 + KV_WIDTH:].astype(BF16)


def _rope_tables(seq, n_pad):
    pos = np.arange(seq)
    row = (pos // GRID_W).astype(np.float64)
    col = (pos % GRID_W).astype(np.float64)
    lane = np.arange(LANES)
    d = lane % HEAD_DIM
    j = d % (ROPE_AXIS_DIM // 2)
    inv = jnp.asarray(ROPE_BASE, F32) ** (-jnp.asarray(2 * j, F32) / ROPE_AXIS_DIM)
    use_col = jnp.asarray(d >= ROPE_AXIS_DIM)
    posv = jnp.where(use_col[None, :], jnp.asarray(col, F32)[:, None], jnp.asarray(row, F32)[:, None])
    ang = posv * inv[None, :]
    upper = jnp.asarray((d % ROPE_AXIS_DIM) >= ROPE_AXIS_DIM // 2)[None, :]
    cos, sin = jnp.cos(ang), jnp.sin(ang)
    sin_hi = jnp.where(upper, sin, 0.0)
    sin_lo = jnp.where(upper, 0.0, -sin)
    pad = lambda t, v: jnp.concatenate([t, jnp.full((n_pad, LANES), v, F32)], axis=0)
    return pad(cos, 1.0), pad(sin_hi, 0.0), pad(sin_lo, 0.0)


def _inproj(tok, x_lat, x_ctx, mod, pre_g, w_in_all, tables, i, layer, tm):
    per_batch = tok.seq // tm
    n_lat_tiles = tok.n_lat // tm
    tab_map = lambda i: (jnp.where(i < n_lat_tiles, i % per_batch, per_batch), 0)
    tab_spec = pl.BlockSpec((tm, LANES), tab_map)
    n = tok.n_all
    kern = functools.partial(_inproj_kernel, n_lat_tiles=n_lat_tiles)
    return pl.pallas_call(
        kern,
        out_shape=(jax.ShapeDtypeStruct((n // 2, 2 * FOURIER_WIDTH), BF16),
                   jax.ShapeDtypeStruct((n, ATTN_WIDTH), BF16),
                   jax.ShapeDtypeStruct((n, KV_WIDTH), BF16), jax.ShapeDtypeStruct((n, KV_WIDTH), BF16)),
        grid=(n // tm,),
        in_specs=[*tok.split_specs(tm, D_MODEL, x_lat is x_ctx), tok.mod_spec(tm, layer), _vec_spec(pre_g),
                  _layer_resident(i, (D_MODEL, IN_WIDTH)), tab_spec, tab_spec, tab_spec],
        out_specs=(_row_spec(tm // 2, 2 * FOURIER_WIDTH), _row_spec(tm, ATTN_WIDTH), _row_spec(tm, KV_WIDTH),
                   _row_spec(tm, KV_WIDTH)),
        scratch_shapes=[pltpu.VMEM((FOURIER_WIDTH // LANES, tm, LANES), F32)],
        compiler_params=_cparams(1),
        name="inproj",
    )(x_lat, x_ctx, mod, pre_g[0], w_in_all, *tables)


def _pair_heads(w, axis):
    shape = w.shape
    split = shape[:axis] + (N_KV_HEADS, HEAD_GROUP, HEAD_DIM) + shape[axis + 1:]
    return jnp.swapaxes(w.reshape(split), axis, axis + 1).reshape(shape)


def _dft_tables(length):
    half = length // 2
    k = np.arange(half)[:, None]
    m = np.arange(half)[None, :]

    def tab(n):
        ang = 2.0 * np.pi * ((k * n) % length) / length
        t = np.concatenate([np.cos(ang), -np.sin(ang)], axis=1) / math.sqrt(length)
        return jnp.asarray(t.astype(np.float32)).astype(BF16)

    return tab(2 * m), tab(2 * m + 1)


def _chan_table():
    n = FOURIER_GROUP_DIM
    k = np.arange(n)
    ang = 2.0 * np.pi * ((k[:, None] * k[None, :]) % n) / n
    t = np.concatenate([np.cos(ang), np.sin(ang)], axis=1) / math.sqrt(n)
    return jnp.asarray(t.astype(np.float32)).astype(BF16)


def _fourier_one(f_ref, chan_ref, pos_e_ref, pos_o_ref, o_ref, stk_ref):
    gd = FOURIER_GROUP_DIM
    half = f_ref.shape[0]
    row_chunk = min(half, 512)
    for parity in range(2):
        for g in range(FOURIER_GROUPS):
            lanes = slice(parity * FOURIER_WIDTH + g * gd, parity * FOURIER_WIDTH + (g + 1) * gd)
            z = jnp.dot(f_ref[:, lanes], chan_ref[...], preferred_element_type=F32)
            stk_ref[parity, 0:half, g * gd:(g + 1) * gd] = z[:, :gd].astype(BF16)
            stk_ref[parity, half:2 * half, g * gd:(g + 1) * gd] = z[:, gd:].astype(BF16)
    for r in range(half // row_chunk):
        rows = slice(r * row_chunk, (r + 1) * row_chunk)
        even = jnp.dot(pos_e_ref[rows, :], stk_ref[0], preferred_element_type=F32)
        odd = jnp.dot(pos_o_ref[rows, :], stk_ref[1], preferred_element_type=F32)
        o_ref[rows, :] = (even + odd).astype(BF16)
        o_ref[half + r * row_chunk:half + (r + 1) * row_chunk, :] = (even - odd).astype(BF16)


def _fourier_kernel(fl_ref, fc_ref, chan_ref, ple_ref, plo_ref, pce_ref, pco_ref, ol_ref, oc_ref, stkl_ref, stkc_ref):
    _fourier_one(fl_ref, chan_ref, ple_ref, plo_ref, ol_ref, stkl_ref)
    _fourier_one(fc_ref, chan_ref, pce_ref, pco_ref, oc_ref, stkc_ref)


def _fourier(tok, f_pairs, chan, tabs_lat, tabs_ctx):
    ctx0 = tok.n_lat // tok.n_ctx
    pair_blk = lambda rows, m: pl.BlockSpec((rows // 2, 2 * FOURIER_WIDTH), m)
    lat_blk = pl.BlockSpec((tok.seq, FOURIER_WIDTH), lambda b: (b, 0))
    ctx_blk = pl.BlockSpec((tok.n_ctx, FOURIER_WIDTH), lambda b: (b, 0))
    tabs = (*tabs_lat, *tabs_ctx)
    return pl.pallas_call(
        _fourier_kernel,
        out_shape=(jax.ShapeDtypeStruct((tok.n_lat, FOURIER_WIDTH), BF16),
                   jax.ShapeDtypeStruct((tok.batch * tok.n_ctx, FOURIER_WIDTH), BF16)),
        grid=(tok.batch,),
        in_specs=[pair_blk(tok.seq, lambda b: (b, 0)), pair_blk(tok.n_ctx, lambda b: (ctx0 + b, 0)),
                  _resident(chan.shape), *[_resident(t.shape) for t in tabs]],
        out_specs=(lat_blk, ctx_blk),
        scratch_shapes=[pltpu.VMEM((2, tok.seq, FOURIER_WIDTH), BF16),
                        pltpu.VMEM((2, tok.n_ctx, FOURIER_WIDTH), BF16)],
        compiler_params=_cparams(1),
        name="fourier",
    )(f_pairs, f_pairs, chan, *tabs)


ATTN_QBLOCKS = 8


def _attn_kernel(sink_ref, q_ref, *refs, n_qb, n_steps, layer_idx, with_local):
    j = pl.program_id(1)
    blk = WINDOW
    n_loc = 3 * blk
    rows = HEAD_GROUP * blk
    nt = (((1,), (1,)), ((), ()))
    if with_local:
        kp_ref, km_ref, kn_ref, kx_ref, vp_ref, vm_ref, vn_ref, vx_ref, o_ref = refs
        k_cat = jnp.concatenate([kp_ref[...], km_ref[...], kn_ref[...]], axis=0)
        v_cat = jnp.concatenate([vp_ref[...], vm_ref[...], vn_ref[...]], axis=0)
    else:
        kx_ref, vx_ref, o_ref = refs

    def attend(half):
        r0 = half * blk
        q = jnp.concatenate([q_ref[r0:r0 + blk, i * LANES:(i + 1) * LANES] for i in range(HEAD_GROUP)], axis=0)
        k_ctx, v_ctx = kx_ref[...], vx_ref[...]
        if with_local:
            k_loc, v_loc = k_cat[r0:r0 + n_loc], v_cat[r0:r0 + n_loc]
            col_lo = jnp.where(j >= 1, 0, blk) if half == 0 else 0
            col_hi = jnp.where(j + 1 < n_steps, n_loc, 2 * blk) if half == n_qb - 1 else n_loc
            qi = lax.broadcasted_iota(jnp.int32, (blk, 1), 0)
            kj = lax.broadcasted_iota(jnp.int32, (blk, n_loc), 1)
            visible = (kj >= jnp.maximum(qi, col_lo)) & (kj <= jnp.minimum(qi + 2 * WINDOW, col_hi - 1))
            bias = jnp.concatenate([jnp.where(visible, 0.0, NEG_INF)] * HEAD_GROUP, axis=0)
        row_blk = lax.broadcasted_iota(jnp.int32, (rows, 1), 0) // blk
        lane_o = lax.broadcasted_iota(jnp.int32, (rows, LANES), 1)
        outs = []
        for kvh in range(N_KV_HEADS):
            def own_lanes(t):
                lane = lax.broadcasted_iota(jnp.int32, t.shape, 1)
                return jnp.where((lane >= kvh * HEAD_DIM) & (lane < (kvh + 1) * HEAD_DIM), t, jnp.zeros_like(t))
            s_ctx = lax.dot_general(q, own_lanes(k_ctx), nt, preferred_element_type=F32)
            sink = jnp.zeros((rows, 1), F32)
            for i in range(HEAD_GROUP):
                sink = jnp.where(row_blk == i, sink_ref[layer_idx, kvh * HEAD_GROUP + i] * LOG2_E, sink)
            m = jnp.maximum(jnp.max(s_ctx, axis=-1, keepdims=True), sink)
            if with_local:
                s_loc = lax.dot_general(q, own_lanes(k_loc), nt, preferred_element_type=F32) + bias
                m = jnp.maximum(m, jnp.max(s_loc, axis=-1, keepdims=True))
            p_ctx = jnp.exp2(s_ctx - m)
            denom = jnp.sum(p_ctx, axis=-1, keepdims=True) + jnp.exp2(sink - m)
            o = jnp.dot(p_ctx.astype(BF16), v_ctx, preferred_element_type=F32)
            if with_local:
                p_loc = jnp.exp2(s_loc - m)
                denom = denom + jnp.sum(p_loc, axis=-1, keepdims=True)
                o = o + jnp.dot(p_loc.astype(BF16), v_loc, preferred_element_type=F32)
            outs.append(o / denom)
        merged = jnp.where(lane_o < HEAD_DIM, outs[0], outs[1])
        for i in range(HEAD_GROUP):
            o_ref[r0:r0 + blk, i * LANES:(i + 1) * LANES] = merged[i * blk:(i + 1) * blk, :].astype(BF16)

    for half in range(n_qb):
        attend(half)


def _attention(tok, q, k, v, sink_all, layer_idx):
    blk = WINDOW
    n_qblk = tok.seq // blk
    qb = math.gcd(ATTN_QBLOCKS, n_qblk)
    n_steps = n_qblk // qb
    ctx0 = tok.n_lat // tok.n_ctx
    smem = pl.BlockSpec(memory_space=pltpu.SMEM)
    ctx_blk = pl.BlockSpec((tok.n_ctx, KV_WIDTH), lambda b, j: (ctx0 + b, 0))

    main_map = lambda b, j: (b * n_steps + j, 0)
    prev_map = lambda b, j: (b * n_qblk + jnp.maximum(j * qb - 1, 0), 0)
    next_map = lambda b, j: (b * n_qblk + jnp.minimum(j * qb + qb, n_qblk - 1), 0)
    kv_specs = [pl.BlockSpec((blk, KV_WIDTH), prev_map), pl.BlockSpec((qb * blk, KV_WIDTH), main_map),
                pl.BlockSpec((blk, KV_WIDTH), next_map), ctx_blk]
    lat = pl.pallas_call(
        functools.partial(_attn_kernel, n_qb=qb, n_steps=n_steps, layer_idx=layer_idx, with_local=True),
        out_shape=jax.ShapeDtypeStruct((tok.n_lat, ATTN_WIDTH), BF16),
        grid=(tok.batch, n_steps),
        in_specs=[smem, pl.BlockSpec((qb * blk, ATTN_WIDTH), main_map), *kv_specs, *kv_specs],
        out_specs=pl.BlockSpec((qb * blk, ATTN_WIDTH), main_map),
        compiler_params=_cparams(2),
        name="window_attention",
    )(sink_all, q, k, k, k, k, v, v, v, v)

    n_cb = tok.n_ctx // blk
    ctx = pl.pallas_call(
        functools.partial(_attn_kernel, n_qb=n_cb, n_steps=1, layer_idx=layer_idx, with_local=False),
        out_shape=jax.ShapeDtypeStruct((tok.batch * tok.n_ctx, ATTN_WIDTH), BF16),
        grid=(tok.batch, 1),
        in_specs=[smem, pl.BlockSpec((tok.n_ctx, ATTN_WIDTH), lambda b, j: (ctx0 + b, 0)), ctx_blk, ctx_blk],
        out_specs=pl.BlockSpec((tok.n_ctx, ATTN_WIDTH), lambda b, j: (b, 0)),
        compiler_params=_cparams(2),
        name="context_attention",
    )(sink_all, q, k, v)
    return lat, ctx


def _piece_perm():
    idx = np.arange(D_MODEL)
    a, b, c = idx // LANES, (idx // SSM_GROUP_DIM) % SSM_PIECES, idx % SSM_GROUP_DIM
    perm = np.zeros((D_MODEL, D_MODEL), np.float32)
    perm[idx, b * LANES + a * SSM_GROUP_DIM + c] = 1.0
    return jnp.asarray(perm).astype(BF16)


def _slab_pitch(blk):
    return blk + SUBLANES


S5_PRE_BATCHES = 4


def _s5_pre_kernel(*refs, blk, batch, n_blocks):
    nb = S5_PRE_BATCHES
    x_refs, mod_refs = refs[:nb], refs[nb:2 * nb]
    pre_ref, perm_ref, u_ref, hs_ref = refs[2 * nb:]
    p, bb = pl.program_id(0), pl.program_id(1)
    pitch = _slab_pitch(blk)

    @pl.when(p < n_blocks)
    def _():
        slot0 = (p % 2) * SSM_SLABS
        for k in range(nb):
            h = _norm_mod(x_refs[k][...], pre_ref[...], mod_refs[k][0, 1:2, :], mod_refs[k][0, 0:1, :])
            row0 = pl.multiple_of((bb * nb + k) * pitch, SUBLANES)
            for s in range(SSM_SLABS):
                hs_ref[slot0 + s, pl.ds(row0, blk), :] = h[:, s * LANES:(s + 1) * LANES]

    @pl.when(p >= 1)
    def _():
        half = SSM_CHUNK // 2
        cpb = blk // SSM_CHUNK
        rows_h = cpb * batch
        for k in range(nb):
            s = bb * nb + k
            src = ((p - 1) % 2) * SSM_SLABS + s
            lhs = []
            for hh in range(2):
                for i in range(cpb):
                    t0 = i * SSM_CHUNK + hh * half
                    lhs.append(jnp.concatenate(
                        [hs_ref[src, pl.ds(t0 + j, batch, stride=pitch), :] for j in range(half)], axis=1))
            lhs = jnp.concatenate(lhs, axis=0).astype(BF16)
            out = jnp.dot(lhs, perm_ref[...], preferred_element_type=F32).astype(BF16)
            for hh in range(2):
                for g in range(SSM_PIECES):
                    u_ref[s * SSM_PIECES + g, :, hh * LANES:(hh + 1) * LANES] = (
                        out[hh * rows_h:(hh + 1) * rows_h, g * LANES:(g + 1) * LANES])


def _s5_pre(tok, xa, mod, pre_g, perm, layer, blk):
    assert tok.batch == SSM_SLABS
    (n_blocks, _), row_map, mod_map = tok.pos_grid(blk, True, layer)
    n_lat_blk = tok.seq // blk
    n_ctx_blk = tok.n_ctx // blk
    cpb = blk // SSM_CHUNK
    n_chunks = (tok.seq + tok.n_ctx) // SSM_CHUNK
    last = n_blocks - 1

    def u_map(p, b):
        q = jnp.maximum(p - 1, 0)
        return (0, jnp.where(q < n_lat_blk, n_ctx_blk + q, q - n_lat_blk), 0)

    nb = S5_PRE_BATCHES
    assert tok.batch % nb == 0
    of_batch = lambda m, k: (lambda p, bb: m(jnp.minimum(p, last), bb * nb + k))
    kern = functools.partial(_s5_pre_kernel, blk=blk, batch=tok.batch, n_blocks=n_blocks)
    return pl.pallas_call(
        kern,
        out_shape=jax.ShapeDtypeStruct((SSM_GROUPS, n_chunks * tok.batch, SSM_CW), BF16),
        grid=(n_blocks + 1, tok.batch // nb),
        in_specs=[*[pl.BlockSpec((blk, D_MODEL), of_batch(row_map, k)) for k in range(nb)],
                  *[pl.BlockSpec((None, 1, N_MOD, D_MODEL), of_batch(mod_map, k)) for k in range(nb)],
                  _vec_spec(pre_g), _resident((D_MODEL, D_MODEL))],
        out_specs=pl.BlockSpec((SSM_GROUPS, cpb * tok.batch, SSM_CW), u_map),
        scratch_shapes=[pltpu.VMEM((2 * SSM_SLABS, tok.batch * _slab_pitch(blk), LANES), F32)],
        compiler_params=_cparams(2),
        name="s5_pre",
    )(*([xa] * nb), *([mod] * nb), pre_g[0], perm)


def _cmul(ar, ai, br, bi):
    return ar * br - ai * bi, ar * bi + ai * br


def _s5_table_plan():
    t_n = SSM_CHUNK
    t = np.arange(t_n)
    return [
        [(0, t_n - 1 - t, 'b', 're'), (1, t, 'b', 're'), (0, t_n - 1 - t, 'b', 'im'), (1, t, 'b', 'im')],
        [(0, -t, 'b', 're'), (0, -t, 'b', 'im'), (1, t, 'b', 're'), (1, t, 'b', 'im')],
        [(0, t, 'c', 're'), (0, t, 'c', '-im'), (1, -t, 'c', 're'), (1, -t, 'c', '-im')],
        [(0, t + 1, 'c', 're'), (1, t_n - t, 'c', 're'), (0, t + 1, 'c', '-im'), (1, t_n - t, 'c', '-im')],
    ]


def _s5_operands(a_re, a_im, log_dt, b_re, b_im, c_re, c_im):
    t_n = SSM_CHUNK
    per_group = lambda x: jnp.swapaxes(x.astype(F32), 1, 2)
    l_re, l_im = per_group(a_re), per_group(a_im)
    dt = jnp.exp(per_group(log_dt))[..., None]
    z_re, z_im = l_re * dt, l_im * dt
    n_all = np.arange(1 - t_n, t_n + 1)
    positive = (n_all > 0)[:, None]
    mag, mag_inv = jnp.exp(z_re), jnp.exp(-z_re)
    up = (mag * jnp.cos(z_im), mag * jnp.sin(z_im))
    down = (mag_inv * jnp.cos(z_im), -mag_inv * jnp.sin(z_im))
    shape = z_re.shape[:-1] + (2 * t_n, z_re.shape[-1])
    acc = (jnp.ones(shape, F32), jnp.zeros(shape, F32))
    for k in range(int(t_n).bit_length()):
        bit = (((np.abs(n_all) >> k) & 1) == 1)[:, None]
        f_re = jnp.where(bit, jnp.where(positive, up[0][..., None, :], down[0][..., None, :]), 1.0)
        f_im = jnp.where(bit, jnp.where(positive, up[1][..., None, :], down[1][..., None, :]), 0.0)
        acc = _cmul(acc[0], acc[1], f_re, f_im)
        up, down = _cmul(*up, *up), _cmul(*down, *down)
    pows = jnp.stack(acc, axis=2)
    a1_re, a1_im = pows[:, :, 0, :, t_n, :], pows[:, :, 1, :, t_n, :]
    den = l_re * l_re + l_im * l_im
    r_re = ((a1_re - 1.0) * l_re + a1_im * l_im) / den
    r_im = (a1_im * l_re - (a1_re - 1.0) * l_im) / den
    bb_re, bb_im = _cmul(r_re[..., None], r_im[..., None], per_group(b_re), per_group(b_im))
    base = jnp.stack([jnp.swapaxes(bb_re, -1, -2), jnp.swapaxes(bb_im, -1, -2), per_group(c_re), per_group(c_im)],
                     axis=3)
    n_l, n_g = base.shape[0], base.shape[1]
    base = base.reshape(n_l, n_g, 8, SSM_GROUP_DIM, SSM_STATE)
    signed = jnp.concatenate([base, -base], axis=2)

    pows_rev = pows[:, :, :, :, ::-1, :]
    last = 2 * t_n - 1
    pw = pows[:, :, :, :, last, :]
    dec = jnp.concatenate([pw[:, :, 0, 0], pw[:, :, 0, 1], pw[:, :, 1, 0], pw[:, :, 1, 1]], axis=-1)[:, :, None, :]
    return pows, pows_rev, signed, dec


def _s5_kernel(u_ref, pw_ref, pr_ref, sg_ref, dec_ref, y_ref, v_ref, xin_ref, *, batch, n_chunks, n_ctx_chunks):
    gps = SSM_GROUPS_PER_STEP
    ns = SSM_STATE
    ns2 = 2 * ns
    cw = SSM_CW
    t_n = SSM_CHUNK
    nt = (((1,), (1,)), ((), ()))
    plan = _s5_table_plan()

    def table(g, k):
        p_re, p_im, w_a, w_b = [], [], [], []
        for d, expo, w, part in plan[k]:
            rising = expo[1] > expo[0]
            ref, lo = (pw_ref, int(expo[0]) + t_n - 1) if rising else (pr_ref, t_n - int(expo[0]))
            p_re.append(ref[g, 0, d, lo:lo + t_n, :])
            p_im.append(ref[g, 1, d, lo:lo + t_n, :])
            re, im = d * 4 + (0 if w == 'b' else 2), d * 4 + (1 if w == 'b' else 3)
            k_a, k_b = {'re': (re, im + 8), 'im': (im, re), '-im': (im + 8, re + 8)}[part]
            w_a.append(sg_ref[g, k_a])
            w_b.append(sg_ref[g, k_b])
        p_re, p_im, w_a, w_b = (jnp.concatenate(x, axis=1) for x in (p_re, p_im, w_a, w_b))
        return jnp.concatenate([p_re[t:t + 1, :] * w_a + p_im[t:t + 1, :] * w_b for t in range(t_n)], axis=0)

    for g in range(gps):
        v_ref[g] = jnp.dot(u_ref[g], table(g, 0).astype(BF16), preferred_element_type=F32)
    a_re = jnp.broadcast_to(dec_ref[:, :, 0:ns2], (gps, batch, ns2))
    a_im = jnp.broadcast_to(dec_ref[:, :, ns2:2 * ns2], (gps, batch, ns2))
    is_fwd = lax.broadcasted_iota(jnp.int32, (gps, batch, ns2), 2) < ns

    def step(k, carry):
        x_re, x_im = carry
        kb = jnp.where(k < n_ctx_chunks, n_ctx_chunks - 1 - k, n_chunks + n_ctx_chunks - 1 - k)
        rf = pl.multiple_of(k * batch, batch)
        rb = pl.multiple_of(kb * batch, batch)
        xin_ref[:, pl.ds(rf, batch), 0:ns] = x_re[:, :, 0:ns]
        xin_ref[:, pl.ds(rb, batch), ns:ns2] = x_re[:, :, ns:ns2]
        xin_ref[:, pl.ds(rf, batch), ns2:ns2 + ns] = x_im[:, :, 0:ns]
        xin_ref[:, pl.ds(rb, batch), ns2 + ns:2 * ns2] = x_im[:, :, ns:ns2]
        v_re = jnp.where(is_fwd, v_ref[:, pl.ds(rf, batch), 0:ns2], v_ref[:, pl.ds(rb, batch), 0:ns2])
        v_im = jnp.where(is_fwd, v_ref[:, pl.ds(rf, batch), ns2:2 * ns2], v_ref[:, pl.ds(rb, batch), ns2:2 * ns2])
        return a_re * x_re - a_im * x_im + v_re, a_re * x_im + a_im * x_re + v_im

    zero = jnp.zeros((gps, batch, ns2), F32)
    lax.fori_loop(0, n_chunks, step, (zero, zero))
    t_in = lax.broadcasted_iota(jnp.int32, (cw, cw), 0) // SSM_GROUP_DIM
    t_out = lax.broadcasted_iota(jnp.int32, (cw, cw), 1) // SSM_GROUP_DIM

    def split(x):
        hi = x.astype(BF16)
        return hi, (x - hi.astype(F32)).astype(BF16)

    def lag_kernel(e, ft, lanes):
        (eh, el), (fh, fl) = split(e[:, lanes]), split(ft[:, lanes])
        return (lax.dot_general(eh, fh, nt, preferred_element_type=F32)
                + lax.dot_general(eh, fl, nt, preferred_element_type=F32)
                + lax.dot_general(el, fh, nt, preferred_element_type=F32))

    for g in range(gps):
        e, ft = table(g, 1), table(g, 2)
        m = (jnp.where(t_out >= t_in, lag_kernel(e, ft, slice(0, ns2)), 0.0)
             + jnp.where(t_in >= t_out, lag_kernel(e, ft, slice(ns2, 2 * ns2)), 0.0)).astype(BF16)
        y_ref[g] = (jnp.dot(u_ref[g], m, preferred_element_type=F32)
                    + lax.dot_general(xin_ref[g].astype(BF16), table(g, 3).astype(BF16), nt,
                                      preferred_element_type=F32)).astype(BF16)


def _s5(u_t, operands, layer, batch, n_chunks, n_ctx_chunks):
    pows, pows_rev, signed, dec = operands
    gps = SSM_GROUPS_PER_STEP
    rows = n_chunks * batch
    kern = functools.partial(_s5_kernel, batch=batch, n_chunks=n_chunks, n_ctx_chunks=n_ctx_chunks)
    gspec = lambda r, c: pl.BlockSpec((gps, r, c), lambda i: (i, 0, 0))
    lspec = lambda shape: pl.BlockSpec((None, gps) + shape, lambda i: (layer, i) + (0,) * len(shape))
    return pl.pallas_call(
        kern,
        out_shape=jax.ShapeDtypeStruct((SSM_GROUPS, rows, SSM_CW), BF16),
        grid=(SSM_GROUPS // gps,),
        in_specs=[gspec(rows, SSM_CW), lspec(pows.shape[2:]), lspec(pows_rev.shape[2:]), lspec(signed.shape[2:]),
                  lspec(dec.shape[2:])],
        out_specs=gspec(rows, SSM_CW),
        scratch_shapes=[pltpu.VMEM((gps, rows, SSM_CW), F32), pltpu.VMEM((gps, rows, SSM_CW), F32)],
        compiler_params=_cparams(1),
        name="s5_scan",
    )(u_t, pows, pows_rev, signed, dec)


def _s5_unpack_kernel(y_ref, perm_ref, *refs, blk, batch, n_lat_blk):
    outs, (ys_ref, tmp_ref) = refs[:-2], refs[-2:]
    p = pl.program_id(0)
    half = SSM_CHUNK // 2
    cpb = blk // SSM_CHUNK
    rows_h = cpb * batch

    lhs = jnp.concatenate(
        [jnp.concatenate([y_ref[s * SSM_PIECES + g, :, hh * LANES:(hh + 1) * LANES]
                          for g in range(SSM_PIECES)], axis=1)
         for s in range(SSM_SLABS) for hh in range(2)], axis=0)
    tmp_ref[...] = jnp.dot(lhs, perm_ref[...], preferred_element_type=F32)

    def slab(s, carry):
        for hh in range(2):
            for i in range(cpb):
                r0 = pl.multiple_of((s * 2 + hh) * rows_h + i * batch, batch)
                t0 = i * SSM_CHUNK + hh * half
                for j in range(half):
                    ys_ref[s, (t0 + j) * batch:(t0 + j + 1) * batch, :] = (
                        tmp_ref[pl.ds(r0, batch), j * LANES:(j + 1) * LANES])
        return carry

    lax.fori_loop(0, SSM_SLABS, slab, 0)

    def write(o_ref):
        for b in range(batch):
            for s in range(SSM_SLABS):
                o_ref[b, :, s * LANES:(s + 1) * LANES] = ys_ref[s, pl.ds(b, blk, stride=batch), :].astype(BF16)

    if len(outs) == 1:
        write(outs[0])
    else:
        pl.when(p < n_lat_blk)(lambda: write(outs[0]))
        pl.when(p >= n_lat_blk)(lambda: write(outs[1]))


def _s5_unpack(tok, y_t, perm, blk, with_ctx):
    n_lat_blk = tok.seq // blk
    n_ctx_blk = tok.n_ctx // blk
    cpb = blk // SSM_CHUNK
    y_map = lambda p: (0, jnp.where(p < n_lat_blk, n_ctx_blk + p, p - n_lat_blk), 0)
    out_blk = lambda m: pl.BlockSpec((tok.batch, blk, D_MODEL), m)
    out_shape = [jax.ShapeDtypeStruct((tok.batch, tok.seq, D_MODEL), BF16)]
    out_specs = [out_blk(lambda p: (0, jnp.minimum(p, n_lat_blk - 1), 0))]
    if with_ctx:
        out_shape.append(jax.ShapeDtypeStruct((tok.batch, tok.n_ctx, D_MODEL), BF16))
        out_specs.append(out_blk(lambda p: (0, jnp.maximum(p - n_lat_blk, 0), 0)))
    kern = functools.partial(_s5_unpack_kernel, blk=blk, batch=tok.batch, n_lat_blk=n_lat_blk)
    outs = pl.pallas_call(
        kern,
        out_shape=out_shape,
        grid=(n_lat_blk + (n_ctx_blk if with_ctx else 0),),
        in_specs=[pl.BlockSpec((SSM_GROUPS, cpb * tok.batch, SSM_CW), y_map), _resident((D_MODEL, D_MODEL))],
        out_specs=out_specs,
        scratch_shapes=[pltpu.VMEM((SSM_SLABS, tok.batch * blk, LANES), F32),
                        pltpu.VMEM((SSM_SLABS * 2 * cpb * tok.batch, D_MODEL), F32)],
        compiler_params=_cparams(1),
        name="s5_unpack",
    )(y_t, perm)
    y_lat = outs[0].reshape(tok.n_lat, D_MODEL)
    return y_lat, (outs[1].reshape(tok.batch * tok.n_ctx, D_MODEL) if with_ctx else None)


def _glu_ffn_kernel(x_ref, yl_ref, yc_ref, d_ref, mod_ref, mpre_ref, mpost_ref, pre_ref, post_ref, wg_ref, w1_ref,
                    w2_ref, o_ref, acc_ref, *, n_lat_tiles):
    x = x_ref[...]
    h = _norm_mod(x, mpre_ref[...], mod_ref[0, 1:2, :], mod_ref[0, 0:1, :])
    y = _pick(n_lat_tiles, yl_ref, yc_ref).astype(F32) + d_ref[...] * h
    z = jnp.dot(jax.nn.gelu(y).astype(BF16), wg_ref[...], preferred_element_type=F32)
    out = z[:, :D_MODEL] * jax.nn.sigmoid(z[:, D_MODEL:])
    x = x + mod_ref[0, 2:3, :] * _rms(out, mpost_ref[...])
    _ffn_body(x, mod_ref, pre_ref, post_ref, w1_ref, w2_ref, o_ref, acc_ref)


def _glu_ffn(tok, xa, n_rows, y_lat, y_ctx, d_skip, mod, mix_pre_g, mix_post_g, pre_g, post_g, glu_all, w1_all,
             w2_all, i, layer, tm):
    if y_ctx is None:
        y_ctx = y_lat
    kern = functools.partial(_glu_ffn_kernel, n_lat_tiles=tok.n_lat // tm)
    return pl.pallas_call(
        kern,
        out_shape=jax.ShapeDtypeStruct((n_rows, D_MODEL), F32),
        grid=(n_rows // tm,),
        in_specs=[_row_spec(tm, D_MODEL), *tok.split_specs(tm, D_MODEL, False), _vec_spec(d_skip),
                  tok.mod_spec(tm, layer),
                  _vec_spec(mix_pre_g), _vec_spec(mix_post_g), _vec_spec(pre_g), _vec_spec(post_g),
                  _layer_resident(i, (D_MODEL, 2 * D_MODEL)),
                  _layer_resident(layer, (D_MODEL, D_FF)), _layer_resident(layer, (D_FF, D_MODEL))],
        out_specs=_row_spec(tm, D_MODEL),
        scratch_shapes=[pltpu.VMEM((tm, D_MODEL), F32)],
        compiler_params=_cparams(1),
        name="glu_ffn",
    )(xa, y_lat, y_ctx, d_skip[0], mod, mix_pre_g[0], mix_post_g[0], pre_g[0], post_g[0], glu_all, w1_all, w2_all)


def _tile(limit, *sizes):
    tm = limit
    while any(s % tm for s in sizes):
        tm //= 2
    return tm


def kernel(x, c, ctx, c_ctx, mod_w, mod_b, mix_pre_g, mix_post_g, ffn_pre_g, ffn_post_g, ffn_w1, ffn_w2,
           even_w_in, even_w_out, even_sink, ssm_a_re, ssm_a_im, ssm_log_dt, ssm_b_re, ssm_b_im, ssm_c_re,
           ssm_c_im, ssm_d, ssm_glu_w):
    batch, seq, _ = x.shape
    n_ctx = ctx.shape[1]
    tok = _Tokens(batch, seq, n_ctx)
    assert seq % WINDOW == 0 and n_ctx % WINDOW == 0 and tok.n_lat % n_ctx == 0
    assert batch == SUBLANES
    tm = _tile(256, seq, n_ctx)
    tm_wide = _tile(512, seq, batch * n_ctx)
    tm_in = _tile(1024, seq, batch * n_ctx)

    x_lat, x_ctx = x.reshape(tok.n_lat, D_MODEL), ctx.astype(x.dtype).reshape(-1, D_MODEL)

    n_cond = 2 * SUBLANES
    cond = jnp.zeros((n_cond, D_MODEL), F32).at[:batch].set(c).at[batch].set(c_ctx)
    mod = _modulation(cond, mod_w, mod_b).reshape(DEPTH, n_cond, N_MOD, D_MODEL)

    rope = _rope_tables(seq, tm_in)
    piece_perm = _piece_perm()
    dft_chan, dft_lat, dft_ctx = _chan_table(), _dft_tables(seq), _dft_tables(n_ctx)
    rows3 = lambda t: t.reshape(t.shape[0], 1, t.shape[1])
    w1_all, w2_all, glu_all = ffn_w1.astype(BF16), ffn_w2.astype(BF16), ssm_glu_w.astype(BF16)
    q0, k0 = FOURIER_WIDTH, FOURIER_WIDTH + ATTN_WIDTH
    w_in_all = jnp.concatenate([even_w_in[:, :, :q0], _pair_heads(even_w_in[:, :, q0:k0], 2),
                                even_w_in[:, :, k0:]], axis=2).astype(BF16)
    wf_all = even_w_out[:, :FOURIER_WIDTH].astype(BF16)
    wa_all = _pair_heads(even_w_out[:, FOURIER_WIDTH:], 1).astype(BF16)
    s5_operands = _s5_operands(ssm_a_re, ssm_a_im, ssm_log_dt, ssm_b_re, ssm_b_im, ssm_c_re, ssm_c_im)

    for layer in range(DEPTH):
        need_ctx = layer < DEPTH - 1
        n_rows = tok.n_all if need_ctx else tok.n_lat
        i = layer // 2
        mix_pre, mix_post = (rows3(mix_pre_g), layer), (rows3(mix_post_g), layer)
        ffn_pre, ffn_post = (rows3(ffn_pre_g), layer), (rows3(ffn_post_g), layer)
        if layer % 2 == 0:
            f, q, k, v = _inproj(tok, x_lat, x_ctx, mod, mix_pre, w_in_all, rope, i, layer, tm_in)
            fm_lat, fm_ctx = _fourier(tok, f, dft_chan, dft_lat, dft_ctx)
            ao_lat, ao_ctx = _attention(tok, q, k, v, even_sink, i)
            xa = _mix_ffn(tok, x_lat, x_ctx, fm_lat, fm_ctx, ao_lat, ao_ctx, mod, mix_post, ffn_pre, ffn_post,
                          wf_all, wa_all, w1_all, w2_all, i, layer, tm_wide)
        else:
            u_t = _s5_pre(tok, xa, mod, mix_pre, piece_perm, layer, tm)
            y_t = _s5(u_t, s5_operands, i, batch, (seq + n_ctx) // SSM_CHUNK, n_ctx // SSM_CHUNK)
            y_lat, y_ctx = _s5_unpack(tok, y_t, piece_perm, tm, need_ctx)
            xa = _glu_ffn(tok, xa, n_rows, y_lat, y_ctx, (rows3(ssm_d), i), mod, mix_pre, mix_post, ffn_pre, ffn_post,
                          glu_all, w1_all, w2_all, i, layer, tm_wide)
        x_lat = x_ctx = xa
    return xa[:tok.n_lat].reshape(batch, seq, D_MODEL)
```

```python
import functools
import math

import numpy as np
import jax
import jax.numpy as jnp
from jax import lax
from jax.experimental import pallas as pl
from jax.experimental.pallas import tpu as pltpu

D_MODEL = 1024
DEPTH = 4
N_MOD = 6
EPS = 1e-6
NEG_INF = -1e30
GRID_W = 64

FOURIER_GROUPS = 4
FOURIER_GROUP_DIM = 128
FOURIER_WIDTH = FOURIER_GROUPS * FOURIER_GROUP_DIM

N_HEADS = 8
N_KV_HEADS = 2
HEAD_GROUP = N_HEADS // N_KV_HEADS
HEAD_DIM = 64
ATTN_WIDTH = N_HEADS * HEAD_DIM
KV_WIDTH = N_KV_HEADS * HEAD_DIM
WINDOW = 128
ROPE_AXIS_DIM = HEAD_DIM // 2
ROPE_BASE = 10000.0
LOG2_E = math.log2(math.e)
IN_WIDTH = FOURIER_WIDTH + ATTN_WIDTH + 2 * KV_WIDTH

LANES = 128
SUBLANES = 8
VMEM_LIMIT = 56 * 1024 * 1024

SSM_GROUP_DIM = 16
SSM_GROUPS = D_MODEL // SSM_GROUP_DIM
SSM_STATE = 64
SSM_CHUNK = 16
SSM_CW = SSM_CHUNK * SSM_GROUP_DIM
SSM_GROUPS_PER_STEP = 4
SSM_SLABS = D_MODEL // LANES
SSM_PIECES = LANES // SSM_GROUP_DIM

D_FF = 4 * D_MODEL

F32 = jnp.float32
BF16 = jnp.bfloat16


def _cparams(n_axes):
    return pltpu.CompilerParams(dimension_semantics=("arbitrary",) * n_axes, vmem_limit_bytes=VMEM_LIMIT)


def _resident(shape):
    nd = len(shape)
    return pl.BlockSpec(shape, lambda *_: (0,) * nd, pipeline_mode=pl.Buffered(1))


def _rms(x, g):
    return x * lax.rsqrt(jnp.mean(x * x, axis=-1, keepdims=True) + EPS) * g


def _norm_mod(x, g, sc, sh):
    return _rms(x, g) * (1.0 + sc) + sh


def _mod_kernel(cond_ref, w_ref, b_ref, o_ref):
    cond = cond_ref[...]
    s = cond * jax.nn.sigmoid(cond)
    s_hi = s.astype(BF16)
    s_lo = (s - s_hi.astype(F32)).astype(BF16)
    w = w_ref[0].astype(BF16)
    o_ref[0] = (jnp.dot(s_hi, w, preferred_element_type=F32) + jnp.dot(s_lo, w, preferred_element_type=F32)
                + b_ref[0])


def _modulation(cond, mod_w, mod_b):
    rows = cond.shape[0]
    tn = 2048
    n = N_MOD * D_MODEL
    return pl.pallas_call(
        _mod_kernel,
        out_shape=jax.ShapeDtypeStruct((DEPTH, rows, n), F32),
        grid=(DEPTH, n // tn),
        in_specs=[pl.BlockSpec((rows, D_MODEL), lambda l, j: (0, 0)),
                  pl.BlockSpec((1, D_MODEL, tn), lambda l, j: (l, 0, j)),
                  pl.BlockSpec((1, 1, tn), lambda l, j: (l, 0, j))],
        out_specs=pl.BlockSpec((1, rows, tn), lambda l, j: (l, 0, j)),
        compiler_params=_cparams(2),
        name="modulation",
    )(cond, mod_w, mod_b.reshape(DEPTH, 1, n))


class _Tokens:
    def __init__(self, batch, seq, n_ctx):
        self.batch, self.seq, self.n_ctx = batch, seq, n_ctx
        self.n_lat = batch * seq
        self.n_all = self.n_lat + batch * n_ctx

    def mod_spec(self, tm, layer):
        per_batch = self.seq // tm
        return pl.BlockSpec((None, 1, N_MOD, D_MODEL),
                            lambda i: (layer, jnp.minimum(i // per_batch, self.batch), 0, 0))

    def split_specs(self, tm, width, joined):
        nlt = self.n_lat // tm
        ctx_map = (lambda i: (jnp.maximum(i, nlt), 0)) if joined else (lambda i: (jnp.maximum(i - nlt, 0), 0))
        return pl.BlockSpec((tm, width), lambda i: (jnp.minimum(i, nlt - 1), 0)), pl.BlockSpec((tm, width), ctx_map)

    def pos_grid(self, blk, with_ctx, layer=0):
        n_lat_blk = self.seq // blk
        n_ctx_blk = self.n_ctx // blk
        lat_blocks = self.n_lat // blk
        row_map = lambda p, b: (jnp.where(p < n_lat_blk, b * n_lat_blk + p, lat_blocks + b * n_ctx_blk + (p - n_lat_blk)), 0)
        mod_map = lambda p, b: (layer, jnp.where(p < n_lat_blk, b, self.batch), 0, 0)
        grid = (n_lat_blk + (n_ctx_blk if with_ctx else 0), self.batch)
        return grid, row_map, mod_map


def _pick(n_lat_tiles, lat_ref, ctx_ref):
    return jnp.where(pl.program_id(0) < n_lat_tiles, lat_ref[...], ctx_ref[...])


def _layer_resident(layer, shape):
    nd = len(shape)
    return pl.BlockSpec((None,) + tuple(shape), lambda *_: (layer,) + (0,) * nd, pipeline_mode=pl.Buffered(1))


def _row_spec(tm, width):
    return pl.BlockSpec((tm, width), lambda i: (i, 0))


def _vec_spec(row):
    table, r = row
    return pl.BlockSpec((None, 1, table.shape[2]), lambda *_: (r, 0, 0))


FFN_CHUNK = 512


def _ffn_body(x, mod_ref, pre_ref, post_ref, w1_ref, w2_ref, o_ref, acc_ref):
    h = _norm_mod(x, pre_ref[...], mod_ref[0, 4:5, :], mod_ref[0, 3:4, :]).astype(BF16)
    for c in range(D_FF // FFN_CHUNK):
        sl = slice(c * FFN_CHUNK, (c + 1) * FFN_CHUNK)
        a = jnp.maximum(jnp.dot(h, w1_ref[:, sl], preferred_element_type=F32), 0.0)
        part = jnp.dot((a * a).astype(BF16), w2_ref[sl, :], preferred_element_type=F32)
        if c == 0:
            acc_ref[...] = part
        else:
            acc_ref[...] += part
    o_ref[...] = x + mod_ref[0, 5:6, :] * _rms(acc_ref[...], post_ref[...])


def _mix_ffn_kernel(xl_ref, xc_ref, fl_ref, fc_ref, al_ref, ac_ref, mod_ref, mpost_ref, pre_ref, post_ref,
                    wf_ref, wa_ref, w1_ref, w2_ref, o_ref, acc_ref, *, n_lat_tiles):
    y = (jnp.dot(_pick(n_lat_tiles, fl_ref, fc_ref), wf_ref[...], preferred_element_type=F32)
         + jnp.dot(_pick(n_lat_tiles, al_ref, ac_ref), wa_ref[...], preferred_element_type=F32))
    x = _pick(n_lat_tiles, xl_ref, xc_ref) + mod_ref[0, 2:3, :] * _rms(y, mpost_ref[...])
    _ffn_body(x, mod_ref, pre_ref, post_ref, w1_ref, w2_ref, o_ref, acc_ref)


def _mix_ffn(tok, x_lat, x_ctx, fm_lat, fm_ctx, ao_lat, ao_ctx, mod, mix_post_g, pre_g, post_g, wf_all, wa_all,
             w1_all, w2_all, i, layer, tm):
    n = tok.n_all
    kern = functools.partial(_mix_ffn_kernel, n_lat_tiles=tok.n_lat // tm)
    return pl.pallas_call(
        kern,
        out_shape=jax.ShapeDtypeStruct((n, D_MODEL), F32),
        grid=(n // tm,),
        in_specs=[*tok.split_specs(tm, D_MODEL, x_lat is x_ctx), *tok.split_specs(tm, FOURIER_WIDTH, False),
                  *tok.split_specs(tm, ATTN_WIDTH, False), tok.mod_spec(tm, layer),
                  _vec_spec(mix_post_g), _vec_spec(pre_g), _vec_spec(post_g),
                  _layer_resident(i, (FOURIER_WIDTH, D_MODEL)), _layer_resident(i, (ATTN_WIDTH, D_MODEL)),
                  _layer_resident(layer, (D_MODEL, D_FF)), _layer_resident(layer, (D_FF, D_MODEL))],
        out_specs=_row_spec(tm, D_MODEL),
        scratch_shapes=[pltpu.VMEM((tm, D_MODEL), F32)],
        compiler_params=_cparams(1),
        name="mix_ffn",
    )(x_lat, x_ctx, fm_lat, fm_ctx, ao_lat, ao_ctx, mod, mix_post_g[0], pre_g[0], post_g[0], wf_all, wa_all,
      w1_all, w2_all)


def _rope_block(x, cos, sin_hi, sin_lo):
    half = ROPE_AXIS_DIM // 2
    return (x * cos + pltpu.roll(x, half, axis=1) * sin_hi
            + pltpu.roll(x, LANES - half, axis=1) * sin_lo)


def _inproj_kernel(xl_ref, xc_ref, mod_ref, pre_ref, w_ref, cos_ref, shi_ref, slo_ref, f_ref, q_ref, k_ref, v_ref,
                   fs_ref, *, n_lat_tiles):
    x = _pick(n_lat_tiles, xl_ref, xc_ref)
    h = _norm_mod(x, pre_ref[...], mod_ref[0, 1:2, :], mod_ref[0, 0:1, :]).astype(BF16)
    p = jnp.dot(h, w_ref[...], preferred_element_type=F32)
    cos, shi, slo = cos_ref[...], shi_ref[...], slo_ref[...]
    pairs = x.shape[0] // 2
    n_slabs = FOURIER_WIDTH // LANES
    for s in range(n_slabs):
        fs_ref[s] = p[:, s * LANES:(s + 1) * LANES]
    for parity in range(2):
        for s in range(n_slabs):
            lo = parity * FOURIER_WIDTH + s * LANES
            f_ref[:, lo:lo + LANES] = fs_ref[s, pl.ds(parity, pairs, stride=2), :].astype(BF16)
    scale = HEAD_DIM ** -0.5 * LOG2_E
    for j in range(ATTN_WIDTH // LANES):
        lo = FOURIER_WIDTH + j * LANES
        q_ref[:, j * LANES:(j + 1) * LANES] = (_rope_block(p[:, lo:lo + LANES], cos, shi, slo) * scale).astype(BF16)
    k0 = FOURIER_WIDTH + ATTN_WIDTH
    k_ref[...] = _rope_block(p[:, k0:k0 + KV_WIDTH], cos, shi, slo).astype(BF16)
    v_ref[...] = p[:, k0 + KV_WIDTH:].astype(BF16)


def _rope_tables(seq, n_pad):
    pos = np.arange(seq)
    row = (pos // GRID_W).astype(np.float64)
    col = (pos % GRID_W).astype(np.float64)
    lane = np.arange(LANES)
    d = lane % HEAD_DIM
    j = d % (ROPE_AXIS_DIM // 2)
    inv = jnp.asarray(ROPE_BASE, F32) ** (-jnp.asarray(2 * j, F32) / ROPE_AXIS_DIM)
    use_col = jnp.asarray(d >= ROPE_AXIS_DIM)
    posv = jnp.where(use_col[None, :], jnp.asarray(col, F32)[:, None], jnp.asarray(row, F32)[:, None])
    ang = posv * inv[None, :]
    upper = jnp.asarray((d % ROPE_AXIS_DIM) >= ROPE_AXIS_DIM // 2)[None, :]
    cos, sin = jnp.cos(ang), jnp.sin(ang)
    sin_hi = jnp.where(upper, sin, 0.0)
    sin_lo = jnp.where(upper, 0.0, -sin)
    pad = lambda t, v: jnp.concatenate([t, jnp.full((n_pad, LANES), v, F32)], axis=0)
    return pad(cos, 1.0), pad(sin_hi, 0.0), pad(sin_lo, 0.0)


def _inproj(tok, x_lat, x_ctx, mod, pre_g, w_in_all, tables, i, layer, tm):
    per_batch = tok.seq // tm
    n_lat_tiles = tok.n_lat // tm
    tab_map = lambda i: (jnp.where(i < n_lat_tiles, i % per_batch, per_batch), 0)
    tab_spec = pl.BlockSpec((tm, LANES), tab_map)
    n = tok.n_all
    kern = functools.partial(_inproj_kernel, n_lat_tiles=n_lat_tiles)
    return pl.pallas_call(
        kern,
        out_shape=(jax.ShapeDtypeStruct((n // 2, 2 * FOURIER_WIDTH), BF16),
                   jax.ShapeDtypeStruct((n, ATTN_WIDTH), BF16),
                   jax.ShapeDtypeStruct((n, KV_WIDTH), BF16), jax.ShapeDtypeStruct((n, KV_WIDTH), BF16)),
        grid=(n // tm,),
        in_specs=[*tok.split_specs(tm, D_MODEL, x_lat is x_ctx), tok.mod_spec(tm, layer), _vec_spec(pre_g),
                  _layer_resident(i, (D_MODEL, IN_WIDTH)), tab_spec, tab_spec, tab_spec],
        out_specs=(_row_spec(tm // 2, 2 * FOURIER_WIDTH), _row_spec(tm, ATTN_WIDTH), _row_spec(tm, KV_WIDTH),
                   _row_spec(tm, KV_WIDTH)),
        scratch_shapes=[pltpu.VMEM((FOURIER_WIDTH // LANES, tm, LANES), F32)],
        compiler_params=_cparams(1),
        name="inproj",
    )(x_lat, x_ctx, mod, pre_g[0], w_in_all, *tables)


def _pair_heads(w, axis):
    shape = w.shape
    split = shape[:axis] + (N_KV_HEADS, HEAD_GROUP, HEAD_DIM) + shape[axis + 1:]
    return jnp.swapaxes(w.reshape(split), axis, axis + 1).reshape(shape)


def _dft_tables(length):
    half = length // 2
    k = np.arange(half)[:, None]
    m = np.arange(half)[None, :]

    def tab(n):
        ang = 2.0 * np.pi * ((k * n) % length) / length
        t = np.concatenate([np.cos(ang), -np.sin(ang)], axis=1) / math.sqrt(length)
        return jnp.asarray(t.astype(np.float32)).astype(BF16)

    return tab(2 * m), tab(2 * m + 1)


def _chan_table():
    n = FOURIER_GROUP_DIM
    k = np.arange(n)
    ang = 2.0 * np.pi * ((k[:, None] * k[None, :]) % n) / n
    t = np.concatenate([np.cos(ang), np.sin(ang)], axis=1) / math.sqrt(n)
    return jnp.asarray(t.astype(np.float32)).astype(BF16)


def _fourier_one(f_ref, chan_ref, pos_e_ref, pos_o_ref, o_ref, stk_ref):
    gd = FOURIER_GROUP_DIM
    half = f_ref.shape[0]
    row_chunk = min(half, 512)
    for parity in range(2):
        for g in range(FOURIER_GROUPS):
            lanes = slice(parity * FOURIER_WIDTH + g * gd, parity * FOURIER_WIDTH + (g + 1) * gd)
            z = jnp.dot(f_ref[:, lanes], chan_ref[...], preferred_element_type=F32)
            stk_ref[parity, 0:half, g * gd:(g + 1) * gd] = z[:, :gd].astype(BF16)
            stk_ref[parity, half:2 * half, g * gd:(g + 1) * gd] = z[:, gd:].astype(BF16)
    for r in range(half // row_chunk):
        rows = slice(r * row_chunk, (r + 1) * row_chunk)
        even = jnp.dot(pos_e_ref[rows, :], stk_ref[0], preferred_element_type=F32)
        odd = jnp.dot(pos_o_ref[rows, :], stk_ref[1], preferred_element_type=F32)
        o_ref[rows, :] = (even + odd).astype(BF16)
        o_ref[half + r * row_chunk:half + (r + 1) * row_chunk, :] = (even - odd).astype(BF16)


def _fourier_kernel(fl_ref, fc_ref, chan_ref, ple_ref, plo_ref, pce_ref, pco_ref, ol_ref, oc_ref, stkl_ref, stkc_ref):
    _fourier_one(fl_ref, chan_ref, ple_ref, plo_ref, ol_ref, stkl_ref)
    _fourier_one(fc_ref, chan_ref, pce_ref, pco_ref, oc_ref, stkc_ref)


def _fourier(tok, f_pairs, chan, tabs_lat, tabs_ctx):
    ctx0 = tok.n_lat // tok.n_ctx
    pair_blk = lambda rows, m: pl.BlockSpec((rows // 2, 2 * FOURIER_WIDTH), m)
    lat_blk = pl.BlockSpec((tok.seq, FOURIER_WIDTH), lambda b: (b, 0))
    ctx_blk = pl.BlockSpec((tok.n_ctx, FOURIER_WIDTH), lambda b: (b, 0))
    tabs = (*tabs_lat, *tabs_ctx)
    return pl.pallas_call(
        _fourier_kernel,
        out_shape=(jax.ShapeDtypeStruct((tok.n_lat, FOURIER_WIDTH), BF16),
                   jax.ShapeDtypeStruct((tok.batch * tok.n_ctx, FOURIER_WIDTH), BF16)),
        grid=(tok.batch,),
        in_specs=[pair_blk(tok.seq, lambda b: (b, 0)), pair_blk(tok.n_ctx, lambda b: (ctx0 + b, 0)),
                  _resident(chan.shape), *[_resident(t.shape) for t in tabs]],
        out_specs=(lat_blk, ctx_blk),
        scratch_shapes=[pltpu.VMEM((2, tok.seq, FOURIER_WIDTH), BF16),
                        pltpu.VMEM((2, tok.n_ctx, FOURIER_WIDTH), BF16)],
        compiler_params=_cparams(1),
        name="fourier",
    )(f_pairs, f_pairs, chan, *tabs)


ATTN_QBLOCKS = 8


def _attn_kernel(sink_ref, q_ref, *refs, n_qb, n_steps, layer_idx, with_local):
    j = pl.program_id(1)
    blk = WINDOW
    n_loc = 3 * blk
    rows = HEAD_GROUP * blk
    nt = (((1,), (1,)), ((), ()))
    if with_local:
        kp_ref, km_ref, kn_ref, kx_ref, vp_ref, vm_ref, vn_ref, vx_ref, o_ref = refs
        k_cat = jnp.concatenate([kp_ref[...], km_ref[...], kn_ref[...]], axis=0)
        v_cat = jnp.concatenate([vp_ref[...], vm_ref[...], vn_ref[...]], axis=0)
    else:
        kx_ref, vx_ref, o_ref = refs

    def attend(half):
        r0 = half * blk
        q = jnp.concatenate([q_ref[r0:r0 + blk, i * LANES:(i + 1) * LANES] for i in range(HEAD_GROUP)], axis=0)
        k_ctx, v_ctx = kx_ref[...], vx_ref[...]
        if with_local:
            k_loc, v_loc = k_cat[r0:r0 + n_loc], v_cat[r0:r0 + n_loc]
            col_lo = jnp.where(j >= 1, 0, blk) if half == 0 else 0
            col_hi = jnp.where(j + 1 < n_steps, n_loc, 2 * blk) if half == n_qb - 1 else n_loc
            qi = lax.broadcasted_iota(jnp.int32, (blk, 1), 0)
            kj = lax.broadcasted_iota(jnp.int32, (blk, n_loc), 1)
            visible = (kj >= jnp.maximum(qi, col_lo)) & (kj <= jnp.minimum(qi + 2 * WINDOW, col_hi - 1))
            bias = jnp.concatenate([jnp.where(visible, 0.0, NEG_INF)] * HEAD_GROUP, axis=0)
        row_blk = lax.broadcasted_iota(jnp.int32, (rows, 1), 0) // blk
        lane_o = lax.broadcasted_iota(jnp.int32, (rows, LANES), 1)
        outs = []
        for kvh in range(N_KV_HEADS):
            def own_lanes(t):
                lane = lax.broadcasted_iota(jnp.int32, t.shape, 1)
                return jnp.where((lane >= kvh * HEAD_DIM) & (lane < (kvh + 1) * HEAD_DIM), t, jnp.zeros_like(t))
            s_ctx = lax.dot_general(q, own_lanes(k_ctx), nt, preferred_element_type=F32)
            sink = jnp.zeros((rows, 1), F32)
            for i in range(HEAD_GROUP):
                sink = jnp.where(row_blk == i, sink_ref[layer_idx, kvh * HEAD_GROUP + i] * LOG2_E, sink)
            m = jnp.maximum(jnp.max(s_ctx, axis=-1, keepdims=True), sink)
            if with_local:
                s_loc = lax.dot_general(q, own_lanes(k_loc), nt, preferred_element_type=F32) + bias
                m = jnp.maximum(m, jnp.max(s_loc, axis=-1, keepdims=True))
            p_ctx = jnp.exp2(s_ctx - m)
            denom = jnp.sum(p_ctx, axis=-1, keepdims=True) + jnp.exp2(sink - m)
            o = jnp.dot(p_ctx.astype(BF16), v_ctx, preferred_element_type=F32)
            if with_local:
                p_loc = jnp.exp2(s_loc - m)
                denom = denom + jnp.sum(p_loc, axis=-1, keepdims=True)
                o = o + jnp.dot(p_loc.astype(BF16), v_loc, preferred_element_type=F32)
            outs.append(o / denom)
        merged = jnp.where(lane_o < HEAD_DIM, outs[0], outs[1])
        for i in range(HEAD_GROUP):
            o_ref[r0:r0 + blk, i * LANES:(i + 1) * LANES] = merged[i * blk:(i + 1) * blk, :].astype(BF16)

    for half in range(n_qb):
        attend(half)


def _attention(tok, q, k, v, sink_all, layer_idx):
    blk = WINDOW
    n_qblk = tok.seq // blk
    qb = math.gcd(ATTN_QBLOCKS, n_qblk)
    n_steps = n_qblk // qb
    ctx0 = tok.n_lat // tok.n_ctx
    smem = pl.BlockSpec(memory_space=pltpu.SMEM)
    ctx_blk = pl.BlockSpec((tok.n_ctx, KV_WIDTH), lambda b, j: (ctx0 + b, 0))

    main_map = lambda b, j: (b * n_steps + j, 0)
    prev_map = lambda b, j: (b * n_qblk + jnp.maximum(j * qb - 1, 0), 0)
    next_map = lambda b, j: (b * n_qblk + jnp.minimum(j * qb + qb, n_qblk - 1), 0)
    kv_specs = [pl.BlockSpec((blk, KV_WIDTH), prev_map), pl.BlockSpec((qb * blk, KV_WIDTH), main_map),
                pl.BlockSpec((blk, KV_WIDTH), next_map), ctx_blk]
    lat = pl.pallas_call(
        functools.partial(_attn_kernel, n_qb=qb, n_steps=n_steps, layer_idx=layer_idx, with_local=True),
        out_shape=jax.ShapeDtypeStruct((tok.n_lat, ATTN_WIDTH), BF16),
        grid=(tok.batch, n_steps),
        in_specs=[smem, pl.BlockSpec((qb * blk, ATTN_WIDTH), main_map), *kv_specs, *kv_specs],
        out_specs=pl.BlockSpec((qb * blk, ATTN_WIDTH), main_map),
        compiler_params=_cparams(2),
        name="window_attention",
    )(sink_all, q, k, k, k, k, v, v, v, v)

    n_cb = tok.n_ctx // blk
    ctx = pl.pallas_call(
        functools.partial(_attn_kernel, n_qb=n_cb, n_steps=1, layer_idx=layer_idx, with_local=False),
        out_shape=jax.ShapeDtypeStruct((tok.batch * tok.n_ctx, ATTN_WIDTH), BF16),
        grid=(tok.batch, 1),
        in_specs=[smem, pl.BlockSpec((tok.n_ctx, ATTN_WIDTH), lambda b, j: (ctx0 + b, 0)), ctx_blk, ctx_blk],
        out_specs=pl.BlockSpec((tok.n_ctx, ATTN_WIDTH), lambda b, j: (b, 0)),
        compiler_params=_cparams(2),
        name="context_attention",
    )(sink_all, q, k, v)
    return lat, ctx


def _piece_perm():
    idx = np.arange(D_MODEL)
    a, b, c = idx // LANES, (idx // SSM_GROUP_DIM) % SSM_PIECES, idx % SSM_GROUP_DIM
    perm = np.zeros((D_MODEL, D_MODEL), np.float32)
    perm[idx, b * LANES + a * SSM_GROUP_DIM + c] = 1.0
    return jnp.asarray(perm).astype(BF16)


def _slab_pitch(blk):
    return blk + SUBLANES


S5_PRE_BATCHES = 8


def _s5_pre_kernel(*refs, blk, batch, n_blocks):
    nb = S5_PRE_BATCHES
    x_refs, mod_refs = refs[:nb], refs[nb:2 * nb]
    pre_ref, perm_ref, u_ref, hs_ref = refs[2 * nb:]
    p, bb = pl.program_id(0), pl.program_id(1)
    pitch = _slab_pitch(blk)

    @pl.when(p < n_blocks)
    def _():
        slot0 = (p % 2) * SSM_SLABS
        for k in range(nb):
            h = _norm_mod(x_refs[k][...], pre_ref[...], mod_refs[k][0, 1:2, :], mod_refs[k][0, 0:1, :])
            row0 = pl.multiple_of((bb * nb + k) * pitch, SUBLANES)
            for s in range(SSM_SLABS):
                hs_ref[slot0 + s, pl.ds(row0, blk), :] = h[:, s * LANES:(s + 1) * LANES]

    @pl.when(p >= 1)
    def _():
        half = SSM_CHUNK // 2
        cpb = blk // SSM_CHUNK
        rows_h = cpb * batch
        for k in range(nb):
            s = bb * nb + k
            src = ((p - 1) % 2) * SSM_SLABS + s
            lhs = []
            for hh in range(2):
                for i in range(cpb):
                    t0 = i * SSM_CHUNK + hh * half
                    lhs.append(jnp.concatenate(
                        [hs_ref[src, pl.ds(t0 + j, batch, stride=pitch), :] for j in range(half)], axis=1))
            lhs = jnp.concatenate(lhs, axis=0).astype(BF16)
            out = jnp.dot(lhs, perm_ref[...], preferred_element_type=F32).astype(BF16)
            for hh in range(2):
                for g in range(SSM_PIECES):
                    u_ref[s * SSM_PIECES + g, :, hh * LANES:(hh + 1) * LANES] = (
                        out[hh * rows_h:(hh + 1) * rows_h, g * LANES:(g + 1) * LANES])


def _s5_pre(tok, xa, mod, pre_g, perm, layer, blk):
    assert tok.batch == SSM_SLABS
    (n_blocks, _), row_map, mod_map = tok.pos_grid(blk, True, layer)
    n_lat_blk = tok.seq // blk
    n_ctx_blk = tok.n_ctx // blk
    cpb = blk // SSM_CHUNK
    n_chunks = (tok.seq + tok.n_ctx) // SSM_CHUNK
    last = n_blocks - 1

    def u_map(p, b):
        q = jnp.maximum(p - 1, 0)
        return (0, jnp.where(q < n_lat_blk, n_ctx_blk + q, q - n_lat_blk), 0)

    nb = S5_PRE_BATCHES
    assert tok.batch % nb == 0
    of_batch = lambda m, k: (lambda p, bb: m(jnp.minimum(p, last), bb * nb + k))
    kern = functools.partial(_s5_pre_kernel, blk=blk, batch=tok.batch, n_blocks=n_blocks)
    return pl.pallas_call(
        kern,
        out_shape=jax.ShapeDtypeStruct((SSM_GROUPS, n_chunks * tok.batch, SSM_CW), BF16),
        grid=(n_blocks + 1, tok.batch // nb),
        in_specs=[*[pl.BlockSpec((blk, D_MODEL), of_batch(row_map, k)) for k in range(nb)],
                  *[pl.BlockSpec((None, 1, N_MOD, D_MODEL), of_batch(mod_map, k)) for k in range(nb)],
                  _vec_spec(pre_g), _resident((D_MODEL, D_MODEL))],
        out_specs=pl.BlockSpec((SSM_GROUPS, cpb * tok.batch, SSM_CW), u_map),
        scratch_shapes=[pltpu.VMEM((2 * SSM_SLABS, tok.batch * _slab_pitch(blk), LANES), F32)],
        compiler_params=_cparams(2),
        name="s5_pre",
    )(*([xa] * nb), *([mod] * nb), pre_g[0], perm)


def _cmul(ar, ai, br, bi):
    return ar * br - ai * bi, ar * bi + ai * br


def _s5_table_plan():
    t_n = SSM_CHUNK
    t = np.arange(t_n)
    return [
        [(0, t_n - 1 - t, 'b', 're'), (1, t, 'b', 're'), (0, t_n - 1 - t, 'b', 'im'), (1, t, 'b', 'im')],
        [(0, -t, 'b', 're'), (0, -t, 'b', 'im'), (1, t, 'b', 're'), (1, t, 'b', 'im')],
        [(0, t, 'c', 're'), (0, t, 'c', '-im'), (1, -t, 'c', 're'), (1, -t, 'c', '-im')],
        [(0, t + 1, 'c', 're'), (1, t_n - t, 'c', 're'), (0, t + 1, 'c', '-im'), (1, t_n - t, 'c', '-im')],
    ]


def _s5_operands(a_re, a_im, log_dt, b_re, b_im, c_re, c_im):
    t_n = SSM_CHUNK
    per_group = lambda x: jnp.swapaxes(x.astype(F32), 1, 2)
    l_re, l_im = per_group(a_re), per_group(a_im)
    dt = jnp.exp(per_group(log_dt))[..., None]
    z_re, z_im = l_re * dt, l_im * dt
    n_all = np.concatenate([np.arange(1 - t_n, t_n + 1), np.arange(t_n, -t_n, -1)])
    positive = (n_all > 0)[:, None]
    mag, mag_inv = jnp.exp(z_re), jnp.exp(-z_re)
    up = (mag * jnp.cos(z_im), mag * jnp.sin(z_im))
    down = (mag_inv * jnp.cos(z_im), -mag_inv * jnp.sin(z_im))
    shape = z_re.shape[:-1] + (len(n_all), z_re.shape[-1])
    acc = (jnp.ones(shape, F32), jnp.zeros(shape, F32))
    for k in range(int(t_n).bit_length()):
        bit = (((np.abs(n_all) >> k) & 1) == 1)[:, None]
        f_re = jnp.where(bit, jnp.where(positive, up[0][..., None, :], down[0][..., None, :]), 1.0)
        f_im = jnp.where(bit, jnp.where(positive, up[1][..., None, :], down[1][..., None, :]), 0.0)
        acc = _cmul(acc[0], acc[1], f_re, f_im)
        up, down = _cmul(*up, *up), _cmul(*down, *down)
    pow_re, pow_im = acc
    a1_re, a1_im = pow_re[:, :, :, t_n, :], pow_im[:, :, :, t_n, :]
    den = l_re * l_re + l_im * l_im
    r_re = ((a1_re - 1.0) * l_re + a1_im * l_im) / den
    r_im = (a1_im * l_re - (a1_re - 1.0) * l_im) / den
    bb_re, bb_im = _cmul(r_re[..., None], r_im[..., None], per_group(b_re), per_group(b_im))
    base = jnp.stack([jnp.swapaxes(bb_re, -1, -2), jnp.swapaxes(bb_im, -1, -2), per_group(c_re), per_group(c_im)],
                     axis=3)
    n_l, n_g = base.shape[0], base.shape[1]
    base = base.reshape(n_l, n_g, 8, SSM_GROUP_DIM, SSM_STATE)
    signed = jnp.concatenate([base, -base], axis=2)

    last = 2 * t_n - 1
    dec = jnp.concatenate([pow_re[:, :, 0, last], pow_re[:, :, 1, last], pow_im[:, :, 0, last], pow_im[:, :, 1, last]],
                          axis=-1)[:, :, None, :]
    return pow_re, pow_im, signed, dec


def _s5_kernel(u_ref, pre_ref, pim_ref, sg_ref, dec_ref, y_ref, v_ref, xin_ref, *, batch, n_chunks, n_ctx_chunks):
    gps = SSM_GROUPS_PER_STEP
    ns = SSM_STATE
    ns2 = 2 * ns
    cw = SSM_CW
    t_n = SSM_CHUNK
    nt = (((1,), (1,)), ((), ()))
    plan = _s5_table_plan()

    def table(g, k):
        p_re, p_im, w_a, w_b = [], [], [], []
        for d, expo, w, part in plan[k]:
            rising = expo[1] > expo[0]
            lo = int(expo[0]) + t_n - 1 if rising else 3 * t_n - int(expo[0])
            p_re.append(pre_ref[g, d, lo:lo + t_n, :])
            p_im.append(pim_ref[g, d, lo:lo + t_n, :])
            re, im = d * 4 + (0 if w == 'b' else 2), d * 4 + (1 if w == 'b' else 3)
            k_a, k_b = {'re': (re, im + 8), 'im': (im, re), '-im': (im + 8, re + 8)}[part]
            w_a.append(sg_ref[g, k_a])
            w_b.append(sg_ref[g, k_b])
        p_re, p_im, w_a, w_b = (jnp.concatenate(x, axis=1) for x in (p_re, p_im, w_a, w_b))
        return jnp.concatenate([p_re[t:t + 1, :] * w_a + p_im[t:t + 1, :] * w_b for t in range(t_n)], axis=0)

    for g in range(gps):
        v_ref[g] = jnp.dot(u_ref[g], table(g, 0).astype(BF16), preferred_element_type=F32)
    a_re = jnp.broadcast_to(dec_ref[:, :, 0:ns2], (gps, batch, ns2))
    a_im = jnp.broadcast_to(dec_ref[:, :, ns2:2 * ns2], (gps, batch, ns2))
    is_fwd = lax.broadcasted_iota(jnp.int32, (gps, batch, ns2), 2) < ns

    def step(k, carry):
        x_re, x_im = carry
        kb = jnp.where(k < n_ctx_chunks, n_ctx_chunks - 1 - k, n_chunks + n_ctx_chunks - 1 - k)
        rf = pl.multiple_of(k * batch, batch)
        rb = pl.multiple_of(kb * batch, batch)
        xin_ref[:, pl.ds(rf, batch), 0:ns] = x_re[:, :, 0:ns]
        xin_ref[:, pl.ds(rb, batch), ns:ns2] = x_re[:, :, ns:ns2]
        xin_ref[:, pl.ds(rf, batch), ns2:ns2 + ns] = x_im[:, :, 0:ns]
        xin_ref[:, pl.ds(rb, batch), ns2 + ns:2 * ns2] = x_im[:, :, ns:ns2]
        v_re = jnp.where(is_fwd, v_ref[:, pl.ds(rf, batch), 0:ns2], v_ref[:, pl.ds(rb, batch), 0:ns2])
        v_im = jnp.where(is_fwd, v_ref[:, pl.ds(rf, batch), ns2:2 * ns2], v_ref[:, pl.ds(rb, batch), ns2:2 * ns2])
        return a_re * x_re - a_im * x_im + v_re, a_re * x_im + a_im * x_re + v_im

    zero = jnp.zeros((gps, batch, ns2), F32)
    lax.fori_loop(0, n_chunks, step, (zero, zero))
    t_in = lax.broadcasted_iota(jnp.int32, (cw, cw), 0) // SSM_GROUP_DIM
    t_out = lax.broadcasted_iota(jnp.int32, (cw, cw), 1) // SSM_GROUP_DIM

    def split(x):
        hi = x.astype(BF16)
        return hi, (x - hi.astype(F32)).astype(BF16)

    def lag_kernel(e, ft, lanes):
        (eh, el), (fh, fl) = split(e[:, lanes]), split(ft[:, lanes])
        return (lax.dot_general(eh, fh, nt, preferred_element_type=F32)
                + lax.dot_general(eh, fl, nt, preferred_element_type=F32)
                + lax.dot_general(el, fh, nt, preferred_element_type=F32))

    for g in range(gps):
        e, ft = table(g, 1), table(g, 2)
        m = (jnp.where(t_out >= t_in, lag_kernel(e, ft, slice(0, ns2)), 0.0)
             + jnp.where(t_in >= t_out, lag_kernel(e, ft, slice(ns2, 2 * ns2)), 0.0)).astype(BF16)
        y_ref[g] = (jnp.dot(u_ref[g], m, preferred_element_type=F32)
                    + lax.dot_general(xin_ref[g].astype(BF16), table(g, 3).astype(BF16), nt,
                                      preferred_element_type=F32)).astype(BF16)


def _s5(u_t, operands, layer, batch, n_chunks, n_ctx_chunks):
    pow_re, pow_im, signed, dec = operands
    gps = SSM_GROUPS_PER_STEP
    rows = n_chunks * batch
    kern = functools.partial(_s5_kernel, batch=batch, n_chunks=n_chunks, n_ctx_chunks=n_ctx_chunks)
    gspec = lambda r, c: pl.BlockSpec((gps, r, c), lambda i: (i, 0, 0))
    lspec = lambda shape: pl.BlockSpec((None, gps) + shape, lambda i: (layer, i) + (0,) * len(shape))
    return pl.pallas_call(
        kern,
        out_shape=jax.ShapeDtypeStruct((SSM_GROUPS, rows, SSM_CW), BF16),
        grid=(SSM_GROUPS // gps,),
        in_specs=[gspec(rows, SSM_CW), lspec(pow_re.shape[2:]), lspec(pow_im.shape[2:]), lspec(signed.shape[2:]),
                  lspec(dec.shape[2:])],
        out_specs=gspec(rows, SSM_CW),
        scratch_shapes=[pltpu.VMEM((gps, rows, SSM_CW), F32), pltpu.VMEM((gps, rows, SSM_CW), F32)],
        compiler_params=_cparams(1),
        name="s5_scan",
    )(u_t, pow_re, pow_im, signed, dec)


def _s5_unpack_kernel(y_ref, perm_ref, *refs, blk, batch, n_lat_blk):
    outs, (ys_ref, tmp_ref) = refs[:-2], refs[-2:]
    p = pl.program_id(0)
    half = SSM_CHUNK // 2
    cpb = blk // SSM_CHUNK
    rows_h = cpb * batch

    lhs = jnp.concatenate(
        [jnp.concatenate([y_ref[s * SSM_PIECES + g, :, hh * LANES:(hh + 1) * LANES]
                          for g in range(SSM_PIECES)], axis=1)
         for s in range(SSM_SLABS) for hh in range(2)], axis=0)
    tmp_ref[...] = jnp.dot(lhs, perm_ref[...], preferred_element_type=F32)

    def slab(s, carry):
        for hh in range(2):
            for i in range(cpb):
                r0 = pl.multiple_of((s * 2 + hh) * rows_h + i * batch, batch)
                t0 = i * SSM_CHUNK + hh * half
                for j in range(half):
                    ys_ref[s, (t0 + j) * batch:(t0 + j + 1) * batch, :] = (
                        tmp_ref[pl.ds(r0, batch), j * LANES:(j + 1) * LANES])
        return carry

    lax.fori_loop(0, SSM_SLABS, slab, 0)

    def write(o_ref):
        for b in range(batch):
            for s in range(SSM_SLABS):
                o_ref[b, :, s * LANES:(s + 1) * LANES] = ys_ref[s, pl.ds(b, blk, stride=batch), :].astype(BF16)

    if len(outs) == 1:
        write(outs[0])
    else:
        pl.when(p < n_lat_blk)(lambda: write(outs[0]))
        pl.when(p >= n_lat_blk)(lambda: write(outs[1]))


def _s5_unpack(tok, y_t, perm, blk, with_ctx):
    n_lat_blk = tok.seq // blk
    n_ctx_blk = tok.n_ctx // blk
    cpb = blk // SSM_CHUNK
    y_map = lambda p: (0, jnp.where(p < n_lat_blk, n_ctx_blk + p, p - n_lat_blk), 0)
    out_blk = lambda m: pl.BlockSpec((tok.batch, blk, D_MODEL), m)
    out_shape = [jax.ShapeDtypeStruct((tok.batch, tok.seq, D_MODEL), BF16)]
    out_specs = [out_blk(lambda p: (0, jnp.minimum(p, n_lat_blk - 1), 0))]
    if with_ctx:
        out_shape.append(jax.ShapeDtypeStruct((tok.batch, tok.n_ctx, D_MODEL), BF16))
        out_specs.append(out_blk(lambda p: (0, jnp.maximum(p - n_lat_blk, 0), 0)))
    kern = functools.partial(_s5_unpack_kernel, blk=blk, batch=tok.batch, n_lat_blk=n_lat_blk)
    outs = pl.pallas_call(
        kern,
        out_shape=out_shape,
        grid=(n_lat_blk + (n_ctx_blk if with_ctx else 0),),
        in_specs=[pl.BlockSpec((SSM_GROUPS, cpb * tok.batch, SSM_CW), y_map), _resident((D_MODEL, D_MODEL))],
        out_specs=out_specs,
        scratch_shapes=[pltpu.VMEM((SSM_SLABS, tok.batch * blk, LANES), F32),
                        pltpu.VMEM((SSM_SLABS * 2 * cpb * tok.batch, D_MODEL), F32)],
        compiler_params=_cparams(1),
        name="s5_unpack",
    )(y_t, perm)
    y_lat = outs[0].reshape(tok.n_lat, D_MODEL)
    return y_lat, (outs[1].reshape(tok.batch * tok.n_ctx, D_MODEL) if with_ctx else None)


def _glu_ffn_kernel(x_ref, yl_ref, yc_ref, d_ref, mod_ref, mpre_ref, mpost_ref, pre_ref, post_ref, wg_ref, w1_ref,
                    w2_ref, o_ref, acc_ref, *, n_lat_tiles):
    x = x_ref[...]
    h = _norm_mod(x, mpre_ref[...], mod_ref[0, 1:2, :], mod_ref[0, 0:1, :])
    y = _pick(n_lat_tiles, yl_ref, yc_ref).astype(F32) + d_ref[...] * h
    z = jnp.dot(jax.nn.gelu(y).astype(BF16), wg_ref[...], preferred_element_type=F32)
    out = z[:, :D_MODEL] * jax.nn.sigmoid(z[:, D_MODEL:])
    x = x + mod_ref[0, 2:3, :] * _rms(out, mpost_ref[...])
    _ffn_body(x, mod_ref, pre_ref, post_ref, w1_ref, w2_ref, o_ref, acc_ref)


def _glu_ffn(tok, xa, n_rows, y_lat, y_ctx, d_skip, mod, mix_pre_g, mix_post_g, pre_g, post_g, glu_all, w1_all,
             w2_all, i, layer, tm):
    if y_ctx is None:
        y_ctx = y_lat
    kern = functools.partial(_glu_ffn_kernel, n_lat_tiles=tok.n_lat // tm)
    return pl.pallas_call(
        kern,
        out_shape=jax.ShapeDtypeStruct((n_rows, D_MODEL), F32),
        grid=(n_rows // tm,),
        in_specs=[_row_spec(tm, D_MODEL), *tok.split_specs(tm, D_MODEL, False), _vec_spec(d_skip),
                  tok.mod_spec(tm, layer),
                  _vec_spec(mix_pre_g), _vec_spec(mix_post_g), _vec_spec(pre_g), _vec_spec(post_g),
                  _layer_resident(i, (D_MODEL, 2 * D_MODEL)),
                  _layer_resident(layer, (D_MODEL, D_FF)), _layer_resident(layer, (D_FF, D_MODEL))],
        out_specs=_row_spec(tm, D_MODEL),
        scratch_shapes=[pltpu.VMEM((tm, D_MODEL), F32)],
        compiler_params=_cparams(1),
        name="glu_ffn",
    )(xa, y_lat, y_ctx, d_skip[0], mod, mix_pre_g[0], mix_post_g[0], pre_g[0], post_g[0], glu_all, w1_all, w2_all)


def _tile(limit, *sizes):
    tm = limit
    while any(s % tm for s in sizes):
        tm //= 2
    return tm


def kernel(x, c, ctx, c_ctx, mod_w, mod_b, mix_pre_g, mix_post_g, ffn_pre_g, ffn_post_g, ffn_w1, ffn_w2,
           even_w_in, even_w_out, even_sink, ssm_a_re, ssm_a_im, ssm_log_dt, ssm_b_re, ssm_b_im, ssm_c_re,
           ssm_c_im, ssm_d, ssm_glu_w):
    batch, seq, _ = x.shape
    n_ctx = ctx.shape[1]
    tok = _Tokens(batch, seq, n_ctx)
    assert seq % WINDOW == 0 and n_ctx % WINDOW == 0 and tok.n_lat % n_ctx == 0
    assert batch == SUBLANES
    tm = _tile(256, seq, n_ctx)
    tm_wide = _tile(512, seq, batch * n_ctx)
    tm_in = _tile(1024, seq, batch * n_ctx)

    x_lat, x_ctx = x.reshape(tok.n_lat, D_MODEL), ctx.astype(x.dtype).reshape(-1, D_MODEL)

    n_cond = 2 * SUBLANES
    cond = jnp.zeros((n_cond, D_MODEL), F32).at[:batch].set(c).at[batch].set(c_ctx)
    mod = _modulation(cond, mod_w, mod_b).reshape(DEPTH, n_cond, N_MOD, D_MODEL)

    rope = _rope_tables(seq, tm_in)
    piece_perm = _piece_perm()
    dft_chan, dft_lat, dft_ctx = _chan_table(), _dft_tables(seq), _dft_tables(n_ctx)
    rows3 = lambda t: t.reshape(t.shape[0], 1, t.shape[1])
    w1_all, w2_all, glu_all = ffn_w1.astype(BF16), ffn_w2.astype(BF16), ssm_glu_w.astype(BF16)
    q0, k0 = FOURIER_WIDTH, FOURIER_WIDTH + ATTN_WIDTH
    w_in_all = jnp.concatenate([even_w_in[:, :, :q0], _pair_heads(even_w_in[:, :, q0:k0], 2),
                                even_w_in[:, :, k0:]], axis=2).astype(BF16)
    wf_all = even_w_out[:, :FOURIER_WIDTH].astype(BF16)
    wa_all = _pair_heads(even_w_out[:, FOURIER_WIDTH:], 1).astype(BF16)
    s5_operands = _s5_operands(ssm_a_re, ssm_a_im, ssm_log_dt, ssm_b_re, ssm_b_im, ssm_c_re, ssm_c_im)

    for layer in range(DEPTH):
        need_ctx = layer < DEPTH - 1
        n_rows = tok.n_all if need_ctx else tok.n_lat
        i = layer // 2
        mix_pre, mix_post = (rows3(mix_pre_g), layer), (rows3(mix_post_g), layer)
        ffn_pre, ffn_post = (rows3(ffn_pre_g), layer), (rows3(ffn_post_g), layer)
        if layer % 2 == 0:
            f, q, k, v = _inproj(tok, x_lat, x_ctx, mod, mix_pre, w_in_all, rope, i, layer, tm_in)
            fm_lat, fm_ctx = _fourier(tok, f, dft_chan, dft_lat, dft_ctx)
            ao_lat, ao_ctx = _attention(tok, q, k, v, even_sink, i)
            xa = _mix_ffn(tok, x_lat, x_ctx, fm_lat, fm_ctx, ao_lat, ao_ctx, mod, mix_post, ffn_pre, ffn_post,
                          wf_all, wa_all, w1_all, w2_all, i, layer, tm_wide)
        else:
            u_t = _s5_pre(tok, xa, mod, mix_pre, piece_perm, layer, tm)
            y_t = _s5(u_t, s5_operands, i, batch, (seq + n_ctx) // SSM_CHUNK, n_ctx // SSM_CHUNK)
            y_lat, y_ctx = _s5_unpack(tok, y_t, piece_perm, tm, need_ctx)
            xa = _glu_ffn(tok, xa, n_rows, y_lat, y_ctx, (rows3(ssm_d), i), mod, mix_pre, mix_post, ffn_pre, ffn_post,
                          glu_all, w1_all, w2_all, i, layer, tm_wide)
        x_lat = x_ctx = xa
    return xa[:tok.n_lat].reshape(batch, seq, D_MODEL)
```

```python
import functools
import math

import numpy as np
import jax
import jax.numpy as jnp
from jax import lax
from jax.experimental import pallas as pl
from jax.experimental.pallas import tpu as pltpu

D_MODEL = 1024
DEPTH = 4
N_MOD = 6
EPS = 1e-6
NEG_INF = -1e30
GRID_W = 64

FOURIER_GROUPS = 4
FOURIER_GROUP_DIM = 128
FOURIER_WIDTH = FOURIER_GROUPS * FOURIER_GROUP_DIM

N_HEADS = 8
N_KV_HEADS = 2
HEAD_GROUP = N_HEADS // N_KV_HEADS
HEAD_DIM = 64
ATTN_WIDTH = N_HEADS * HEAD_DIM
KV_WIDTH = N_KV_HEADS * HEAD_DIM
WINDOW = 128
ROPE_AXIS_DIM = HEAD_DIM // 2
ROPE_BASE = 10000.0
LOG2_E = math.log2(math.e)
IN_WIDTH = FOURIER_WIDTH + ATTN_WIDTH + 2 * KV_WIDTH

LANES = 128
SUBLANES = 8
VMEM_LIMIT = 56 * 1024 * 1024

SSM_GROUP_DIM = 16
SSM_GROUPS = D_MODEL // SSM_GROUP_DIM
SSM_STATE = 64
SSM_CHUNK = 16
SSM_CW = SSM_CHUNK * SSM_GROUP_DIM
SSM_GROUPS_PER_STEP = 4
SSM_SLABS = D_MODEL // LANES
SSM_PIECES = LANES // SSM_GROUP_DIM

D_FF = 4 * D_MODEL

F32 = jnp.float32
BF16 = jnp.bfloat16


def _cparams(n_axes):
    return pltpu.CompilerParams(dimension_semantics=("arbitrary",) * n_axes, vmem_limit_bytes=VMEM_LIMIT)


def _resident(shape):
    nd = len(shape)
    return pl.BlockSpec(shape, lambda *_: (0,) * nd, pipeline_mode=pl.Buffered(1))


def _rms(x, g):
    return x * lax.rsqrt(jnp.mean(x * x, axis=-1, keepdims=True) + EPS) * g


def _norm_mod(x, g, sc, sh):
    return _rms(x, g) * (1.0 + sc) + sh


def _mod_kernel(cond_ref, w_ref, b_ref, o_ref):
    cond = cond_ref[...]
    s = cond * jax.nn.sigmoid(cond)
    s_hi = s.astype(BF16)
    s_lo = (s - s_hi.astype(F32)).astype(BF16)
    w = w_ref[0].astype(BF16)
    o_ref[0] = (jnp.dot(s_hi, w, preferred_element_type=F32) + jnp.dot(s_lo, w, preferred_element_type=F32)
                + b_ref[0])


def _modulation(cond, mod_w, mod_b):
    rows = cond.shape[0]
    tn = 2048
    n = N_MOD * D_MODEL
    return pl.pallas_call(
        _mod_kernel,
        out_shape=jax.ShapeDtypeStruct((DEPTH, rows, n), F32),
        grid=(DEPTH, n // tn),
        in_specs=[pl.BlockSpec((rows, D_MODEL), lambda l, j: (0, 0)),
                  pl.BlockSpec((1, D_MODEL, tn), lambda l, j: (l, 0, j)),
                  pl.BlockSpec((1, 1, tn), lambda l, j: (l, 0, j))],
        out_specs=pl.BlockSpec((1, rows, tn), lambda l, j: (l, 0, j)),
        compiler_params=_cparams(2),
        name="modulation",
    )(cond, mod_w, mod_b.reshape(DEPTH, 1, n))


class _Tokens:
    def __init__(self, batch, seq, n_ctx):
        self.batch, self.seq, self.n_ctx = batch, seq, n_ctx
        self.n_lat = batch * seq
        self.n_all = self.n_lat + batch * n_ctx

    def mod_spec(self, tm, layer):
        per_batch = self.seq // tm
        return pl.BlockSpec((None, 1, N_MOD, D_MODEL),
                            lambda i: (layer, jnp.minimum(i // per_batch, self.batch), 0, 0))

    def split_specs(self, tm, width, joined):
        nlt = self.n_lat // tm
        ctx_map = (lambda i: (jnp.maximum(i, nlt), 0)) if joined else (lambda i: (jnp.maximum(i - nlt, 0), 0))
        return pl.BlockSpec((tm, width), lambda i: (jnp.minimum(i, nlt - 1), 0)), pl.BlockSpec((tm, width), ctx_map)

    def pos_grid(self, blk, with_ctx, layer=0):
        n_lat_blk = self.seq // blk
        n_ctx_blk = self.n_ctx // blk
        lat_blocks = self.n_lat // blk
        row_map = lambda p, b: (jnp.where(p < n_lat_blk, b * n_lat_blk + p, lat_blocks + b * n_ctx_blk + (p - n_lat_blk)), 0)
        mod_map = lambda p, b: (layer, jnp.where(p < n_lat_blk, b, self.batch), 0, 0)
        grid = (n_lat_blk + (n_ctx_blk if with_ctx else 0), self.batch)
        return grid, row_map, mod_map


def _pick(n_lat_tiles, lat_ref, ctx_ref):
    return jnp.where(pl.program_id(0) < n_lat_tiles, lat_ref[...], ctx_ref[...])


def _layer_resident(layer, shape):
    nd = len(shape)
    return pl.BlockSpec((None,) + tuple(shape), lambda *_: (layer,) + (0,) * nd, pipeline_mode=pl.Buffered(1))


def _row_spec(tm, width):
    return pl.BlockSpec((tm, width), lambda i: (i, 0))


def _vec_spec(row):
    table, r = row
    return pl.BlockSpec((None, 1, table.shape[2]), lambda *_: (r, 0, 0))


FFN_CHUNK = 512


def _ffn_body(x, mod_ref, pre_ref, post_ref, w1_ref, w2_ref, o_ref, acc_ref):
    h = _norm_mod(x, pre_ref[...], mod_ref[0, 4:5, :], mod_ref[0, 3:4, :]).astype(BF16)
    for c in range(D_FF // FFN_CHUNK):
        sl = slice(c * FFN_CHUNK, (c + 1) * FFN_CHUNK)
        a = jnp.maximum(jnp.dot(h, w1_ref[:, sl], preferred_element_type=F32), 0.0)
        part = jnp.dot((a * a).astype(BF16), w2_ref[sl, :], preferred_element_type=F32)
        if c == 0:
            acc_ref[...] = part
        else:
            acc_ref[...] += part
    o_ref[...] = x + mod_ref[0, 5:6, :] * _rms(acc_ref[...], post_ref[...])


def _mix_ffn_kernel(xl_ref, xc_ref, fl_ref, fc_ref, al_ref, ac_ref, mod_ref, mpost_ref, pre_ref, post_ref,
                    wf_ref, wa_ref, w1_ref, w2_ref, o_ref, acc_ref, *, n_lat_tiles):
    y = (jnp.dot(_pick(n_lat_tiles, fl_ref, fc_ref), wf_ref[...], preferred_element_type=F32)
         + jnp.dot(_pick(n_lat_tiles, al_ref, ac_ref), wa_ref[...], preferred_element_type=F32))
    x = _pick(n_lat_tiles, xl_ref, xc_ref) + mod_ref[0, 2:3, :] * _rms(y, mpost_ref[...])
    _ffn_body(x, mod_ref, pre_ref, post_ref, w1_ref, w2_ref, o_ref, acc_ref)


def _mix_ffn(tok, x_lat, x_ctx, fm_lat, fm_ctx, ao_lat, ao_ctx, mod, mix_post_g, pre_g, post_g, wf_all, wa_all,
             w1_all, w2_all, i, layer, tm):
    n = tok.n_all
    kern = functools.partial(_mix_ffn_kernel, n_lat_tiles=tok.n_lat // tm)
    return pl.pallas_call(
        kern,
        out_shape=jax.ShapeDtypeStruct((n, D_MODEL), F32),
        grid=(n // tm,),
        in_specs=[*tok.split_specs(tm, D_MODEL, x_lat is x_ctx), *tok.split_specs(tm, FOURIER_WIDTH, False),
                  *tok.split_specs(tm, ATTN_WIDTH, False), tok.mod_spec(tm, layer),
                  _vec_spec(mix_post_g), _vec_spec(pre_g), _vec_spec(post_g),
                  _layer_resident(i, (FOURIER_WIDTH, D_MODEL)), _layer_resident(i, (ATTN_WIDTH, D_MODEL)),
                  _layer_resident(layer, (D_MODEL, D_FF)), _layer_resident(layer, (D_FF, D_MODEL))],
        out_specs=_row_spec(tm, D_MODEL),
        scratch_shapes=[pltpu.VMEM((tm, D_MODEL), F32)],
        compiler_params=_cparams(1),
        name="mix_ffn",
    )(x_lat, x_ctx, fm_lat, fm_ctx, ao_lat, ao_ctx, mod, mix_post_g[0], pre_g[0], post_g[0], wf_all, wa_all,
      w1_all, w2_all)


def _rope_block(x, cos, sin_hi, sin_lo):
    half = ROPE_AXIS_DIM // 2
    return (x * cos + pltpu.roll(x, half, axis=1) * sin_hi
            + pltpu.roll(x, LANES - half, axis=1) * sin_lo)


def _inproj_kernel(xl_ref, xc_ref, mod_ref, pre_ref, w_ref, cos_ref, shi_ref, slo_ref, f_ref, q_ref, k_ref, v_ref,
                   fs_ref, *, n_lat_tiles):
    x = _pick(n_lat_tiles, xl_ref, xc_ref)
    h = _norm_mod(x, pre_ref[...], mod_ref[0, 1:2, :], mod_ref[0, 0:1, :]).astype(BF16)
    p = jnp.dot(h, w_ref[...], preferred_element_type=F32)
    cos, shi, slo = cos_ref[...], shi_ref[...], slo_ref[...]
    pairs = x.shape[0] // 2
    n_slabs = FOURIER_WIDTH // LANES
    for s in range(n_slabs):
        fs_ref[s] = p[:, s * LANES:(s + 1) * LANES]
    for parity in range(2):
        for s in range(n_slabs):
            lo = parity * FOURIER_WIDTH + s * LANES
            f_ref[:, lo:lo + LANES] = fs_ref[s, pl.ds(parity, pairs, stride=2), :].astype(BF16)
    scale = HEAD_DIM ** -0.5 * LOG2_E
    for j in range(ATTN_WIDTH // LANES):
        lo = FOURIER_WIDTH + j * LANES
        q_ref[:, j * LANES:(j + 1) * LANES] = (_rope_block(p[:, lo:lo + LANES], cos, shi, slo) * scale).astype(BF16)
    k0 = FOURIER_WIDTH + ATTN_WIDTH
    k_ref[...] = _rope_block(p[:, k0:k0 + KV_WIDTH], cos, shi, slo).astype(BF16)
    v_ref[...] = p[:, k0 + KV_WIDTH:].astype(BF16)


def _rope_tables(seq, n_pad):
    pos = np.arange(seq)
    row = (pos // GRID_W).astype(np.float64)
    col = (pos % GRID_W).astype(np.float64)
    lane = np.arange(LANES)
    d = lane % HEAD_DIM
    j = d % (ROPE_AXIS_DIM // 2)
    inv = jnp.asarray(ROPE_BASE, F32) ** (-jnp.asarray(2 * j, F32) / ROPE_AXIS_DIM)
    use_col = jnp.asarray(d >= ROPE_AXIS_DIM)
    posv = jnp.where(use_col[None, :], jnp.asarray(col, F32)[:, None], jnp.asarray(row, F32)[:, None])
    ang = posv * inv[None, :]
    upper = jnp.asarray((d % ROPE_AXIS_DIM) >= ROPE_AXIS_DIM // 2)[None, :]
    cos, sin = jnp.cos(ang), jnp.sin(ang)
    sin_hi = jnp.where(upper, sin, 0.0)
    sin_lo = jnp.where(upper, 0.0, -sin)
    pad = lambda t, v: jnp.concatenate([t, jnp.full((n_pad, LANES), v, F32)], axis=0)
    return pad(cos, 1.0), pad(sin_hi, 0.0), pad(sin_lo, 0.0)


def _inproj(tok, x_lat, x_ctx, mod, pre_g, w_in_all, tables, i, layer, tm):
    per_batch = tok.seq // tm
    n_lat_tiles = tok.n_lat // tm
    tab_map = lambda i: (jnp.where(i < n_lat_tiles, i % per_batch, per_batch), 0)
    tab_spec = pl.BlockSpec((tm, LANES), tab_map)
    n = tok.n_all
    kern = functools.partial(_inproj_kernel, n_lat_tiles=n_lat_tiles)
    return pl.pallas_call(
        kern,
        out_shape=(jax.ShapeDtypeStruct((n // 2, 2 * FOURIER_WIDTH), BF16),
                   jax.ShapeDtypeStruct((n, ATTN_WIDTH), BF16),
                   jax.ShapeDtypeStruct((n, KV_WIDTH), BF16), jax.ShapeDtypeStruct((n, KV_WIDTH), BF16)),
        grid=(n // tm,),
        in_specs=[*tok.split_specs(tm, D_MODEL, x_lat is x_ctx), tok.mod_spec(tm, layer), _vec_spec(pre_g),
                  _layer_resident(i, (D_MODEL, IN_WIDTH)), tab_spec, tab_spec, tab_spec],
        out_specs=(_row_spec(tm // 2, 2 * FOURIER_WIDTH), _row_spec(tm, ATTN_WIDTH), _row_spec(tm, KV_WIDTH),
                   _row_spec(tm, KV_WIDTH)),
        scratch_shapes=[pltpu.VMEM((FOURIER_WIDTH // LANES, tm, LANES), F32)],
        compiler_params=_cparams(1),
        name="inproj",
    )(x_lat, x_ctx, mod, pre_g[0], w_in_all, *tables)


def _pair_heads(w, axis):
    shape = w.shape
    split = shape[:axis] + (N_KV_HEADS, HEAD_GROUP, HEAD_DIM) + shape[axis + 1:]
    return jnp.swapaxes(w.reshape(split), axis, axis + 1).reshape(shape)


def _dft_tables(length):
    half = length // 2
    k = np.arange(half)[:, None]
    m = np.arange(half)[None, :]

    def tab(n):
        ang = 2.0 * np.pi * ((k * n) % length) / length
        t = np.concatenate([np.cos(ang), -np.sin(ang)], axis=1) / math.sqrt(length)
        return jnp.asarray(t.astype(np.float32)).astype(BF16)

    return tab(2 * m), tab(2 * m + 1)


def _chan_table():
    n = FOURIER_GROUP_DIM
    k = np.arange(n)
    ang = 2.0 * np.pi * ((k[:, None] * k[None, :]) % n) / n
    t = np.concatenate([np.cos(ang), np.sin(ang)], axis=1) / math.sqrt(n)
    return jnp.asarray(t.astype(np.float32)).astype(BF16)


def _fourier_one(f_ref, chan_ref, pos_e_ref, pos_o_ref, o_ref, stk_ref):
    gd = FOURIER_GROUP_DIM
    half = f_ref.shape[0]
    row_chunk = min(half, 512)
    for parity in range(2):
        for g in range(FOURIER_GROUPS):
            lanes = slice(parity * FOURIER_WIDTH + g * gd, parity * FOURIER_WIDTH + (g + 1) * gd)
            z = jnp.dot(f_ref[:, lanes], chan_ref[...], preferred_element_type=F32)
            stk_ref[parity, 0:half, g * gd:(g + 1) * gd] = z[:, :gd].astype(BF16)
            stk_ref[parity, half:2 * half, g * gd:(g + 1) * gd] = z[:, gd:].astype(BF16)
    for r in range(half // row_chunk):
        rows = slice(r * row_chunk, (r + 1) * row_chunk)
        even = jnp.dot(pos_e_ref[rows, :], stk_ref[0], preferred_element_type=F32)
        odd = jnp.dot(pos_o_ref[rows, :], stk_ref[1], preferred_element_type=F32)
        o_ref[rows, :] = (even + odd).astype(BF16)
        o_ref[half + r * row_chunk:half + (r + 1) * row_chunk, :] = (even - odd).astype(BF16)


def _fourier_kernel(fl_ref, fc_ref, chan_ref, ple_ref, plo_ref, pce_ref, pco_ref, ol_ref, oc_ref, stkl_ref, stkc_ref):
    _fourier_one(fl_ref, chan_ref, ple_ref, plo_ref, ol_ref, stkl_ref)
    _fourier_one(fc_ref, chan_ref, pce_ref, pco_ref, oc_ref, stkc_ref)


def _fourier(tok, f_pairs, chan, tabs_lat, tabs_ctx):
    ctx0 = tok.n_lat // tok.n_ctx
    pair_blk = lambda rows, m: pl.BlockSpec((rows // 2, 2 * FOURIER_WIDTH), m)
    lat_blk = pl.BlockSpec((tok.seq, FOURIER_WIDTH), lambda b: (b, 0))
    ctx_blk = pl.BlockSpec((tok.n_ctx, FOURIER_WIDTH), lambda b: (b, 0))
    tabs = (*tabs_lat, *tabs_ctx)
    return pl.pallas_call(
        _fourier_kernel,
        out_shape=(jax.ShapeDtypeStruct((tok.n_lat, FOURIER_WIDTH), BF16),
                   jax.ShapeDtypeStruct((tok.batch * tok.n_ctx, FOURIER_WIDTH), BF16)),
        grid=(tok.batch,),
        in_specs=[pair_blk(tok.seq, lambda b: (b, 0)), pair_blk(tok.n_ctx, lambda b: (ctx0 + b, 0)),
                  _resident(chan.shape), *[_resident(t.shape) for t in tabs]],
        out_specs=(lat_blk, ctx_blk),
        scratch_shapes=[pltpu.VMEM((2, tok.seq, FOURIER_WIDTH), BF16),
                        pltpu.VMEM((2, tok.n_ctx, FOURIER_WIDTH), BF16)],
        compiler_params=_cparams(1),
        name="fourier",
    )(f_pairs, f_pairs, chan, *tabs)


ATTN_QBLOCKS = 8


def _attn_kernel(sink_ref, q_ref, *refs, n_qb, n_steps, layer_idx, with_local):
    j = pl.program_id(1)
    blk = WINDOW
    n_loc = 3 * blk
    rows = HEAD_GROUP * blk
    nt = (((1,), (1,)), ((), ()))
    if with_local:
        kp_ref, km_ref, kn_ref, kx_ref, vp_ref, vm_ref, vn_ref, vx_ref, o_ref = refs
        k_cat = jnp.concatenate([kp_ref[...], km_ref[...], kn_ref[...]], axis=0)
        v_cat = jnp.concatenate([vp_ref[...], vm_ref[...], vn_ref[...]], axis=0)
    else:
        kx_ref, vx_ref, o_ref = refs

    def attend(half):
        r0 = half * blk
        q = jnp.concatenate([q_ref[r0:r0 + blk, i * LANES:(i + 1) * LANES] for i in range(HEAD_GROUP)], axis=0)
        k_ctx, v_ctx = kx_ref[...], vx_ref[...]
        if with_local:
            k_loc, v_loc = k_cat[r0:r0 + n_loc], v_cat[r0:r0 + n_loc]
            col_lo = jnp.where(j >= 1, 0, blk) if half == 0 else 0
            col_hi = jnp.where(j + 1 < n_steps, n_loc, 2 * blk) if half == n_qb - 1 else n_loc
            qi = lax.broadcasted_iota(jnp.int32, (blk, 1), 0)
            kj = lax.broadcasted_iota(jnp.int32, (blk, n_loc), 1)
            visible = (kj >= jnp.maximum(qi, col_lo)) & (kj <= jnp.minimum(qi + 2 * WINDOW, col_hi - 1))
            bias = jnp.concatenate([jnp.where(visible, 0.0, NEG_INF)] * HEAD_GROUP, axis=0)
        row_blk = lax.broadcasted_iota(jnp.int32, (rows, 1), 0) // blk
        lane_o = lax.broadcasted_iota(jnp.int32, (rows, LANES), 1)
        outs = []
        for kvh in range(N_KV_HEADS):
            def own_lanes(t):
                lane = lax.broadcasted_iota(jnp.int32, t.shape, 1)
                return jnp.where((lane >= kvh * HEAD_DIM) & (lane < (kvh + 1) * HEAD_DIM), t, jnp.zeros_like(t))
            s_ctx = lax.dot_general(q, own_lanes(k_ctx), nt, preferred_element_type=F32)
            sink = jnp.zeros((rows, 1), F32)
            for i in range(HEAD_GROUP):
                sink = jnp.where(row_blk == i, sink_ref[layer_idx, kvh * HEAD_GROUP + i] * LOG2_E, sink)
            m = jnp.maximum(jnp.max(s_ctx, axis=-1, keepdims=True), sink)
            if with_local:
                s_loc = lax.dot_general(q, own_lanes(k_loc), nt, preferred_element_type=F32) + bias
                m = jnp.maximum(m, jnp.max(s_loc, axis=-1, keepdims=True))
            p_ctx = jnp.exp2(s_ctx - m)
            denom = jnp.sum(p_ctx, axis=-1, keepdims=True) + jnp.exp2(sink - m)
            o = jnp.dot(p_ctx.astype(BF16), v_ctx, preferred_element_type=F32)
            if with_local:
                p_loc = jnp.exp2(s_loc - m)
                denom = denom + jnp.sum(p_loc, axis=-1, keepdims=True)
                o = o + jnp.dot(p_loc.astype(BF16), v_loc, preferred_element_type=F32)
            outs.append(o / denom)
        merged = jnp.where(lane_o < HEAD_DIM, outs[0], outs[1])
        for i in range(HEAD_GROUP):
            o_ref[r0:r0 + blk, i * LANES:(i + 1) * LANES] = merged[i * blk:(i + 1) * blk, :].astype(BF16)

    for half in range(n_qb):
        attend(half)


def _attention(tok, q, k, v, sink_all, layer_idx):
    blk = WINDOW
    n_qblk = tok.seq // blk
    qb = math.gcd(ATTN_QBLOCKS, n_qblk)
    n_steps = n_qblk // qb
    ctx0 = tok.n_lat // tok.n_ctx
    smem = pl.BlockSpec(memory_space=pltpu.SMEM)
    ctx_blk = pl.BlockSpec((tok.n_ctx, KV_WIDTH), lambda b, j: (ctx0 + b, 0))

    main_map = lambda b, j: (b * n_steps + j, 0)
    prev_map = lambda b, j: (b * n_qblk + jnp.maximum(j * qb - 1, 0), 0)
    next_map = lambda b, j: (b * n_qblk + jnp.minimum(j * qb + qb, n_qblk - 1), 0)
    kv_specs = [pl.BlockSpec((blk, KV_WIDTH), prev_map), pl.BlockSpec((qb * blk, KV_WIDTH), main_map),
                pl.BlockSpec((blk, KV_WIDTH), next_map), ctx_blk]
    lat = pl.pallas_call(
        functools.partial(_attn_kernel, n_qb=qb, n_steps=n_steps, layer_idx=layer_idx, with_local=True),
        out_shape=jax.ShapeDtypeStruct((tok.n_lat, ATTN_WIDTH), BF16),
        grid=(tok.batch, n_steps),
        in_specs=[smem, pl.BlockSpec((qb * blk, ATTN_WIDTH), main_map), *kv_specs, *kv_specs],
        out_specs=pl.BlockSpec((qb * blk, ATTN_WIDTH), main_map),
        compiler_params=_cparams(2),
        name="window_attention",
    )(sink_all, q, k, k, k, k, v, v, v, v)

    n_cb = tok.n_ctx // blk
    ctx = pl.pallas_call(
        functools.partial(_attn_kernel, n_qb=n_cb, n_steps=1, layer_idx=layer_idx, with_local=False),
        out_shape=jax.ShapeDtypeStruct((tok.batch * tok.n_ctx, ATTN_WIDTH), BF16),
        grid=(tok.batch, 1),
        in_specs=[smem, pl.BlockSpec((tok.n_ctx, ATTN_WIDTH), lambda b, j: (ctx0 + b, 0)), ctx_blk, ctx_blk],
        out_specs=pl.BlockSpec((tok.n_ctx, ATTN_WIDTH), lambda b, j: (b, 0)),
        compiler_params=_cparams(2),
        name="context_attention",
    )(sink_all, q, k, v)
    return lat, ctx


def _piece_perm():
    idx = np.arange(D_MODEL)
    a, b, c = idx // LANES, (idx // SSM_GROUP_DIM) % SSM_PIECES, idx % SSM_GROUP_DIM
    perm = np.zeros((D_MODEL, D_MODEL), np.float32)
    perm[idx, b * LANES + a * SSM_GROUP_DIM + c] = 1.0
    return jnp.asarray(perm).astype(BF16)


def _slab_pitch(blk):
    return blk + SUBLANES


S5_PRE_BATCHES = 8


def _s5_pre_kernel(*refs, blk, batch, n_blocks):
    nb = S5_PRE_BATCHES
    x_refs, mod_refs = refs[:nb], refs[nb:2 * nb]
    pre_ref, perm_ref, u_ref, hs_ref = refs[2 * nb:]
    p, bb = pl.program_id(0), pl.program_id(1)
    pitch = _slab_pitch(blk)

    @pl.when(p < n_blocks)
    def _():
        slot0 = (p % 2) * SSM_SLABS
        for k in range(nb):
            h = _norm_mod(x_refs[k][...], pre_ref[...], mod_refs[k][0, 1:2, :], mod_refs[k][0, 0:1, :])
            row0 = pl.multiple_of((bb * nb + k) * pitch, SUBLANES)
            for s in range(SSM_SLABS):
                hs_ref[slot0 + s, pl.ds(row0, blk), :] = h[:, s * LANES:(s + 1) * LANES]

    @pl.when(p >= 1)
    def _():
        half = SSM_CHUNK // 2
        cpb = blk // SSM_CHUNK
        rows_h = cpb * batch
        for k in range(nb):
            s = bb * nb + k
            src = ((p - 1) % 2) * SSM_SLABS + s
            lhs = []
            for hh in range(2):
                for i in range(cpb):
                    t0 = i * SSM_CHUNK + hh * half
                    lhs.append(jnp.concatenate(
                        [hs_ref[src, pl.ds(t0 + j, batch, stride=pitch), :] for j in range(half)], axis=1))
            lhs = jnp.concatenate(lhs, axis=0).astype(BF16)
            out = jnp.dot(lhs, perm_ref[...], preferred_element_type=F32).astype(BF16)
            for hh in range(2):
                for g in range(SSM_PIECES):
                    u_ref[s * SSM_PIECES + g, :, hh * LANES:(hh + 1) * LANES] = (
                        out[hh * rows_h:(hh + 1) * rows_h, g * LANES:(g + 1) * LANES])


def _s5_pre(tok, xa, mod, pre_g, perm, layer, blk):
    assert tok.batch == SSM_SLABS
    (n_blocks, _), row_map, mod_map = tok.pos_grid(blk, True, layer)
    n_lat_blk = tok.seq // blk
    n_ctx_blk = tok.n_ctx // blk
    cpb = blk // SSM_CHUNK
    n_chunks = (tok.seq + tok.n_ctx) // SSM_CHUNK
    last = n_blocks - 1

    def u_map(p, b):
        q = jnp.maximum(p - 1, 0)
        return (0, jnp.where(q < n_lat_blk, n_ctx_blk + q, q - n_lat_blk), 0)

    nb = S5_PRE_BATCHES
    assert tok.batch % nb == 0
    of_batch = lambda m, k: (lambda p, bb: m(jnp.minimum(p, last), bb * nb + k))
    kern = functools.partial(_s5_pre_kernel, blk=blk, batch=tok.batch, n_blocks=n_blocks)
    return pl.pallas_call(
        kern,
        out_shape=jax.ShapeDtypeStruct((SSM_GROUPS, n_chunks * tok.batch, SSM_CW), BF16),
        grid=(n_blocks + 1, tok.batch // nb),
        in_specs=[*[pl.BlockSpec((blk, D_MODEL), of_batch(row_map, k)) for k in range(nb)],
                  *[pl.BlockSpec((None, 1, N_MOD, D_MODEL), of_batch(mod_map, k)) for k in range(nb)],
                  _vec_spec(pre_g), _resident((D_MODEL, D_MODEL))],
        out_specs=pl.BlockSpec((SSM_GROUPS, cpb * tok.batch, SSM_CW), u_map),
        scratch_shapes=[pltpu.VMEM((2 * SSM_SLABS, tok.batch * _slab_pitch(blk), LANES), F32)],
        compiler_params=_cparams(2),
        name="s5_pre",
    )(*([xa] * nb), *([mod] * nb), pre_g[0], perm)


def _cmul(ar, ai, br, bi):
    return ar * br - ai * bi, ar * bi + ai * br


def _s5_table_plan():
    t_n = SSM_CHUNK
    t = np.arange(t_n)
    return [
        [(0, t_n - 1 - t, 'b', 're'), (1, t, 'b', 're'), (0, t_n - 1 - t, 'b', 'im'), (1, t, 'b', 'im')],
        [(0, -t, 'b', 're'), (0, -t, 'b', 'im'), (1, t, 'b', 're'), (1, t, 'b', 'im')],
        [(0, t, 'c', 're'), (0, t, 'c', '-im'), (1, -t, 'c', 're'), (1, -t, 'c', '-im')],
        [(0, t + 1, 'c', 're'), (1, t_n - t, 'c', 're'), (0, t + 1, 'c', '-im'), (1, t_n - t, 'c', '-im')],
    ]


def _s5_operands(a_re, a_im, log_dt, b_re, b_im, c_re, c_im):
    t_n = SSM_CHUNK
    per_group = lambda x: jnp.swapaxes(x.astype(F32), 1, 2)
    l_re, l_im = per_group(a_re), per_group(a_im)
    dt = jnp.exp(per_group(log_dt))[..., None]
    z_re, z_im = l_re * dt, l_im * dt
    n_all = np.arange(1 - t_n, t_n + 1)
    positive = (n_all > 0)[:, None]
    mag, mag_inv = jnp.exp(z_re), jnp.exp(-z_re)
    up = (mag * jnp.cos(z_im), mag * jnp.sin(z_im))
    down = (mag_inv * jnp.cos(z_im), -mag_inv * jnp.sin(z_im))
    shape = z_re.shape[:-1] + (len(n_all), z_re.shape[-1])
    acc = (jnp.ones(shape, F32), jnp.zeros(shape, F32))
    for k in range(int(t_n).bit_length()):
        bit = (((np.abs(n_all) >> k) & 1) == 1)[:, None]
        f_re = jnp.where(bit, jnp.where(positive, up[0][..., None, :], down[0][..., None, :]), 1.0)
        f_im = jnp.where(bit, jnp.where(positive, up[1][..., None, :], down[1][..., None, :]), 0.0)
        acc = _cmul(acc[0], acc[1], f_re, f_im)
        up, down = _cmul(*up, *up), _cmul(*down, *down)
    pow_re, pow_im = acc
    a1_re, a1_im = pow_re[:, :, :, t_n, :], pow_im[:, :, :, t_n, :]
    den = l_re * l_re + l_im * l_im
    r_re = ((a1_re - 1.0) * l_re + a1_im * l_im) / den
    r_im = (a1_im * l_re - (a1_re - 1.0) * l_im) / den
    bb_re, bb_im = _cmul(r_re[..., None], r_im[..., None], per_group(b_re), per_group(b_im))
    base = jnp.stack([jnp.swapaxes(bb_re, -1, -2), jnp.swapaxes(bb_im, -1, -2), per_group(c_re), per_group(c_im)],
                     axis=3)
    n_l, n_g = base.shape[0], base.shape[1]
    base = base.reshape(n_l, n_g, 8, SSM_GROUP_DIM, SSM_STATE)
    signed = jnp.concatenate([base, -base], axis=2)

    last = 2 * t_n - 1
    dec = jnp.concatenate([pow_re[:, :, 0, last], pow_re[:, :, 1, last], pow_im[:, :, 0, last], pow_im[:, :, 1, last]],
                          axis=-1)[:, :, None, :]
    return pow_re, pow_im, signed, dec


def _s5_kernel(u_ref, pre_ref, pim_ref, sg_ref, dec_ref, y_ref, v_ref, xin_ref, *, batch, n_chunks, n_ctx_chunks):
    gps = SSM_GROUPS_PER_STEP
    ns = SSM_STATE
    ns2 = 2 * ns
    cw = SSM_CW
    t_n = SSM_CHUNK
    nt = (((1,), (1,)), ((), ()))
    plan = _s5_table_plan()

    def table(g, k):
        p_re, p_im, w_a, w_b = [], [], [], []
        for d, expo, w, part in plan[k]:
            lo = int(min(expo)) + t_n - 1
            run_re, run_im = pre_ref[g, d, lo:lo + t_n, :], pim_ref[g, d, lo:lo + t_n, :]
            if expo[1] < expo[0]:
                flip = lambda r: jnp.concatenate([r[t_n - 1 - t:t_n - t, :] for t in range(t_n)], axis=0)
                run_re, run_im = flip(run_re), flip(run_im)
            p_re.append(run_re)
            p_im.append(run_im)
            re, im = d * 4 + (0 if w == 'b' else 2), d * 4 + (1 if w == 'b' else 3)
            k_a, k_b = {'re': (re, im + 8), 'im': (im, re), '-im': (im + 8, re + 8)}[part]
            w_a.append(sg_ref[g, k_a])
            w_b.append(sg_ref[g, k_b])
        p_re, p_im, w_a, w_b = (jnp.concatenate(x, axis=1) for x in (p_re, p_im, w_a, w_b))
        return jnp.concatenate([p_re[t:t + 1, :] * w_a + p_im[t:t + 1, :] * w_b for t in range(t_n)], axis=0)

    for g in range(gps):
        v_ref[g] = jnp.dot(u_ref[g], table(g, 0).astype(BF16), preferred_element_type=F32)
    a_re = jnp.broadcast_to(dec_ref[:, :, 0:ns2], (gps, batch, ns2))
    a_im = jnp.broadcast_to(dec_ref[:, :, ns2:2 * ns2], (gps, batch, ns2))
    is_fwd = lax.broadcasted_iota(jnp.int32, (gps, batch, ns2), 2) < ns

    def step(k, carry):
        x_re, x_im = carry
        kb = jnp.where(k < n_ctx_chunks, n_ctx_chunks - 1 - k, n_chunks + n_ctx_chunks - 1 - k)
        rf = pl.multiple_of(k * batch, batch)
        rb = pl.multiple_of(kb * batch, batch)
        xin_ref[:, pl.ds(rf, batch), 0:ns] = x_re[:, :, 0:ns]
        xin_ref[:, pl.ds(rb, batch), ns:ns2] = x_re[:, :, ns:ns2]
        xin_ref[:, pl.ds(rf, batch), ns2:ns2 + ns] = x_im[:, :, 0:ns]
        xin_ref[:, pl.ds(rb, batch), ns2 + ns:2 * ns2] = x_im[:, :, ns:ns2]
        v_re = jnp.where(is_fwd, v_ref[:, pl.ds(rf, batch), 0:ns2], v_ref[:, pl.ds(rb, batch), 0:ns2])
        v_im = jnp.where(is_fwd, v_ref[:, pl.ds(rf, batch), ns2:2 * ns2], v_ref[:, pl.ds(rb, batch), ns2:2 * ns2])
        return a_re * x_re - a_im * x_im + v_re, a_re * x_im + a_im * x_re + v_im

    zero = jnp.zeros((gps, batch, ns2), F32)
    lax.fori_loop(0, n_chunks, step, (zero, zero))
    t_in = lax.broadcasted_iota(jnp.int32, (cw, cw), 0) // SSM_GROUP_DIM
    t_out = lax.broadcasted_iota(jnp.int32, (cw, cw), 1) // SSM_GROUP_DIM

    def split(x):
        hi = x.astype(BF16)
        return hi, (x - hi.astype(F32)).astype(BF16)

    def lag_kernel(e, ft, lanes):
        (eh, el), (fh, fl) = split(e[:, lanes]), split(ft[:, lanes])
        return (lax.dot_general(eh, fh, nt, preferred_element_type=F32)
                + lax.dot_general(eh, fl, nt, preferred_element_type=F32)
                + lax.dot_general(el, fh, nt, preferred_element_type=F32))

    for g in range(gps):
        e, ft = table(g, 1), table(g, 2)
        m = (jnp.where(t_out >= t_in, lag_kernel(e, ft, slice(0, ns2)), 0.0)
             + jnp.where(t_in >= t_out, lag_kernel(e, ft, slice(ns2, 2 * ns2)), 0.0)).astype(BF16)
        y_ref[g] = (jnp.dot(u_ref[g], m, preferred_element_type=F32)
                    + lax.dot_general(xin_ref[g].astype(BF16), table(g, 3).astype(BF16), nt,
                                      preferred_element_type=F32)).astype(BF16)


def _s5(u_t, operands, layer, batch, n_chunks, n_ctx_chunks):
    pow_re, pow_im, signed, dec = operands
    gps = SSM_GROUPS_PER_STEP
    rows = n_chunks * batch
    kern = functools.partial(_s5_kernel, batch=batch, n_chunks=n_chunks, n_ctx_chunks=n_ctx_chunks)
    gspec = lambda r, c: pl.BlockSpec((gps, r, c), lambda i: (i, 0, 0))
    lspec = lambda shape: pl.BlockSpec((None, gps) + shape, lambda i: (layer, i) + (0,) * len(shape))
    return pl.pallas_call(
        kern,
        out_shape=jax.ShapeDtypeStruct((SSM_GROUPS, rows, SSM_CW), BF16),
        grid=(SSM_GROUPS // gps,),
        in_specs=[gspec(rows, SSM_CW), lspec(pow_re.shape[2:]), lspec(pow_im.shape[2:]), lspec(signed.shape[2:]),
                  lspec(dec.shape[2:])],
        out_specs=gspec(rows, SSM_CW),
        scratch_shapes=[pltpu.VMEM((gps, rows, SSM_CW), F32), pltpu.VMEM((gps, rows, SSM_CW), F32)],
        compiler_params=_cparams(1),
        name="s5_scan",
    )(u_t, pow_re, pow_im, signed, dec)


def _s5_unpack_kernel(y_ref, perm_ref, *refs, blk, batch, n_lat_blk):
    outs, (ys_ref, tmp_ref) = refs[:-2], refs[-2:]
    p = pl.program_id(0)
    half = SSM_CHUNK // 2
    cpb = blk // SSM_CHUNK
    rows_h = cpb * batch

    lhs = jnp.concatenate(
        [jnp.concatenate([y_ref[s * SSM_PIECES + g, :, hh * LANES:(hh + 1) * LANES]
                          for g in range(SSM_PIECES)], axis=1)
         for s in range(SSM_SLABS) for hh in range(2)], axis=0)
    tmp_ref[...] = jnp.dot(lhs, perm_ref[...], preferred_element_type=F32)

    def slab(s, carry):
        for hh in range(2):
            for i in range(cpb):
                r0 = pl.multiple_of((s * 2 + hh) * rows_h + i * batch, batch)
                t0 = i * SSM_CHUNK + hh * half
                for j in range(half):
                    ys_ref[s, (t0 + j) * batch:(t0 + j + 1) * batch, :] = (
                        tmp_ref[pl.ds(r0, batch), j * LANES:(j + 1) * LANES])
        return carry

    lax.fori_loop(0, SSM_SLABS, slab, 0)

    def write(o_ref):
        for b in range(batch):
            for s in range(SSM_SLABS):
                o_ref[b, :, s * LANES:(s + 1) * LANES] = ys_ref[s, pl.ds(b, blk, stride=batch), :].astype(BF16)

    if len(outs) == 1:
        write(outs[0])
    else:
        pl.when(p < n_lat_blk)(lambda: write(outs[0]))
        pl.when(p >= n_lat_blk)(lambda: write(outs[1]))


def _s5_unpack(tok, y_t, perm, blk, with_ctx):
    n_lat_blk = tok.seq // blk
    n_ctx_blk = tok.n_ctx // blk
    cpb = blk // SSM_CHUNK
    y_map = lambda p: (0, jnp.where(p < n_lat_blk, n_ctx_blk + p, p - n_lat_blk), 0)
    out_blk = lambda m: pl.BlockSpec((tok.batch, blk, D_MODEL), m)
    out_shape = [jax.ShapeDtypeStruct((tok.batch, tok.seq, D_MODEL), BF16)]
    out_specs = [out_blk(lambda p: (0, jnp.minimum(p, n_lat_blk - 1), 0))]
    if with_ctx:
        out_shape.append(jax.ShapeDtypeStruct((tok.batch, tok.n_ctx, D_MODEL), BF16))
        out_specs.append(out_blk(lambda p: (0, jnp.maximum(p - n_lat_blk, 0), 0)))
    kern = functools.partial(_s5_unpack_kernel, blk=blk, batch=tok.batch, n_lat_blk=n_lat_blk)
    outs = pl.pallas_call(
        kern,
        out_shape=out_shape,
        grid=(n_lat_blk + (n_ctx_blk if with_ctx else 0),),
        in_specs=[pl.BlockSpec((SSM_GROUPS, cpb * tok.batch, SSM_CW), y_map), _resident((D_MODEL, D_MODEL))],
        out_specs=out_specs,
        scratch_shapes=[pltpu.VMEM((SSM_SLABS, tok.batch * blk, LANES), F32),
                        pltpu.VMEM((SSM_SLABS * 2 * cpb * tok.batch, D_MODEL), F32)],
        compiler_params=_cparams(1),
        name="s5_unpack",
    )(y_t, perm)
    y_lat = outs[0].reshape(tok.n_lat, D_MODEL)
    return y_lat, (outs[1].reshape(tok.batch * tok.n_ctx, D_MODEL) if with_ctx else None)


def _glu_ffn_kernel(x_ref, yl_ref, yc_ref, d_ref, mod_ref, mpre_ref, mpost_ref, pre_ref, post_ref, wg_ref, w1_ref,
                    w2_ref, o_ref, acc_ref, *, n_lat_tiles):
    x = x_ref[...]
    h = _norm_mod(x, mpre_ref[...], mod_ref[0, 1:2, :], mod_ref[0, 0:1, :])
    y = _pick(n_lat_tiles, yl_ref, yc_ref).astype(F32) + d_ref[...] * h
    z = jnp.dot(jax.nn.gelu(y).astype(BF16), wg_ref[...], preferred_element_type=F32)
    out = z[:, :D_MODEL] * jax.nn.sigmoid(z[:, D_MODEL:])
    x = x + mod_ref[0, 2:3, :] * _rms(out, mpost_ref[...])
    _ffn_body(x, mod_ref, pre_ref, post_ref, w1_ref, w2_ref, o_ref, acc_ref)


def _glu_ffn(tok, xa, n_rows, y_lat, y_ctx, d_skip, mod, mix_pre_g, mix_post_g, pre_g, post_g, glu_all, w1_all,
             w2_all, i, layer, tm):
    if y_ctx is None:
        y_ctx = y_lat
    kern = functools.partial(_glu_ffn_kernel, n_lat_tiles=tok.n_lat // tm)
    return pl.pallas_call(
        kern,
        out_shape=jax.ShapeDtypeStruct((n_rows, D_MODEL), F32),
        grid=(n_rows // tm,),
        in_specs=[_row_spec(tm, D_MODEL), *tok.split_specs(tm, D_MODEL, False), _vec_spec(d_skip),
                  tok.mod_spec(tm, layer),
                  _vec_spec(mix_pre_g), _vec_spec(mix_post_g), _vec_spec(pre_g), _vec_spec(post_g),
                  _layer_resident(i, (D_MODEL, 2 * D_MODEL)),
                  _layer_resident(layer, (D_MODEL, D_FF)), _layer_resident(layer, (D_FF, D_MODEL))],
        out_specs=_row_spec(tm, D_MODEL),
        scratch_shapes=[pltpu.VMEM((tm, D_MODEL), F32)],
        compiler_params=_cparams(1),
        name="glu_ffn",
    )(xa, y_lat, y_ctx, d_skip[0], mod, mix_pre_g[0], mix_post_g[0], pre_g[0], post_g[0], glu_all, w1_all, w2_all)


def _tile(limit, *sizes):
    tm = limit
    while any(s % tm for s in sizes):
        tm //= 2
    return tm


def kernel(x, c, ctx, c_ctx, mod_w, mod_b, mix_pre_g, mix_post_g, ffn_pre_g, ffn_post_g, ffn_w1, ffn_w2,
           even_w_in, even_w_out, even_sink, ssm_a_re, ssm_a_im, ssm_log_dt, ssm_b_re, ssm_b_im, ssm_c_re,
           ssm_c_im, ssm_d, ssm_glu_w):
    batch, seq, _ = x.shape
    n_ctx = ctx.shape[1]
    tok = _Tokens(batch, seq, n_ctx)
    assert seq % WINDOW == 0 and n_ctx % WINDOW == 0 and tok.n_lat % n_ctx == 0
    assert batch == SUBLANES
    tm = _tile(256, seq, n_ctx)
    tm_wide = _tile(512, seq, batch * n_ctx)
    tm_in = _tile(1024, seq, batch * n_ctx)

    x_lat, x_ctx = x.reshape(tok.n_lat, D_MODEL), ctx.astype(x.dtype).reshape(-1, D_MODEL)

    n_cond = 2 * SUBLANES
    cond = jnp.zeros((n_cond, D_MODEL), F32).at[:batch].set(c).at[batch].set(c_ctx)
    mod = _modulation(cond, mod_w, mod_b).reshape(DEPTH, n_cond, N_MOD, D_MODEL)

    rope = _rope_tables(seq, tm_in)
    piece_perm = _piece_perm()
    dft_chan, dft_lat, dft_ctx = _chan_table(), _dft_tables(seq), _dft_tables(n_ctx)
    rows3 = lambda t: t.reshape(t.shape[0], 1, t.shape[1])
    w1_all, w2_all, glu_all = ffn_w1.astype(BF16), ffn_w2.astype(BF16), ssm_glu_w.astype(BF16)
    q0, k0 = FOURIER_WIDTH, FOURIER_WIDTH + ATTN_WIDTH
    w_in_all = jnp.concatenate([even_w_in[:, :, :q0], _pair_heads(even_w_in[:, :, q0:k0], 2),
                                even_w_in[:, :, k0:]], axis=2).astype(BF16)
    wf_all = even_w_out[:, :FOURIER_WIDTH].astype(BF16)
    wa_all = _pair_heads(even_w_out[:, FOURIER_WIDTH:], 1).astype(BF16)
    s5_operands = _s5_operands(ssm_a_re, ssm_a_im, ssm_log_dt, ssm_b_re, ssm_b_im, ssm_c_re, ssm_c_im)

    for layer in range(DEPTH):
        need_ctx = layer < DEPTH - 1
        n_rows = tok.n_all if need_ctx else tok.n_lat
        i = layer // 2
        mix_pre, mix_post = (rows3(mix_pre_g), layer), (rows3(mix_post_g), layer)
        ffn_pre, ffn_post = (rows3(ffn_pre_g), layer), (rows3(ffn_post_g), layer)
        if layer % 2 == 0:
            f, q, k, v = _inproj(tok, x_lat, x_ctx, mod, mix_pre, w_in_all, rope, i, layer, tm_in)
            fm_lat, fm_ctx = _fourier(tok, f, dft_chan, dft_lat, dft_ctx)
            ao_lat, ao_ctx = _attention(tok, q, k, v, even_sink, i)
            xa = _mix_ffn(tok, x_lat, x_ctx, fm_lat, fm_ctx, ao_lat, ao_ctx, mod, mix_post, ffn_pre, ffn_post,
                          wf_all, wa_all, w1_all, w2_all, i, layer, tm_wide)
        else:
            u_t = _s5_pre(tok, xa, mod, mix_pre, piece_perm, layer, tm)
            y_t = _s5(u_t, s5_operands, i, batch, (seq + n_ctx) // SSM_CHUNK, n_ctx // SSM_CHUNK)
            y_lat, y_ctx = _s5_unpack(tok, y_t, piece_perm, tm, need_ctx)
            xa = _glu_ffn(tok, xa, n_rows, y_lat, y_ctx, (rows3(ssm_d), i), mod, mix_pre, mix_post, ffn_pre, ffn_post,
                          glu_all, w1_all, w2_all, i, layer, tm_wide)
        x_lat = x_ctx = xa
    return xa[:tok.n_lat].reshape(batch, seq, D_MODEL)
```

```python
import functools
import math

import numpy as np
import jax
import jax.numpy as jnp
from jax import lax
from jax.experimental import pallas as pl
from jax.experimental.pallas import tpu as pltpu

D_MODEL = 1024
DEPTH = 4
N_MOD = 6
EPS = 1e-6
NEG_INF = -1e30
GRID_W = 64

FOURIER_GROUPS = 4
FOURIER_GROUP_DIM = 128
FOURIER_WIDTH = FOURIER_GROUPS * FOURIER_GROUP_DIM

N_HEADS = 8
N_KV_HEADS = 2
HEAD_GROUP = N_HEADS // N_KV_HEADS
HEAD_DIM = 64
ATTN_WIDTH = N_HEADS * HEAD_DIM
KV_WIDTH = N_KV_HEADS * HEAD_DIM
WINDOW = 128
ROPE_AXIS_DIM = HEAD_DIM // 2
ROPE_BASE = 10000.0
LOG2_E = math.log2(math.e)
IN_WIDTH = FOURIER_WIDTH + ATTN_WIDTH + 2 * KV_WIDTH

LANES = 128
SUBLANES = 8
VMEM_LIMIT = 56 * 1024 * 1024

SSM_GROUP_DIM = 16
SSM_GROUPS = D_MODEL // SSM_GROUP_DIM
SSM_STATE = 64
SSM_CHUNK = 16
SSM_CW = SSM_CHUNK * SSM_GROUP_DIM
SSM_GROUPS_PER_STEP = 8
SSM_SLABS = D_MODEL // LANES
SSM_PIECES = LANES // SSM_GROUP_DIM

D_FF = 4 * D_MODEL

F32 = jnp.float32
BF16 = jnp.bfloat16


def _cparams(n_axes):
    return pltpu.CompilerParams(dimension_semantics=("arbitrary",) * n_axes, vmem_limit_bytes=VMEM_LIMIT)


def _resident(shape):
    nd = len(shape)
    return pl.BlockSpec(shape, lambda *_: (0,) * nd, pipeline_mode=pl.Buffered(1))


def _rms(x, g):
    return x * lax.rsqrt(jnp.mean(x * x, axis=-1, keepdims=True) + EPS) * g


def _norm_mod(x, g, sc, sh):
    return _rms(x, g) * (1.0 + sc) + sh


def _mod_kernel(cond_ref, w_ref, b_ref, o_ref):
    cond = cond_ref[...]
    s = cond * jax.nn.sigmoid(cond)
    s_hi = s.astype(BF16)
    s_lo = (s - s_hi.astype(F32)).astype(BF16)
    w = w_ref[0].astype(BF16)
    o_ref[0] = (jnp.dot(s_hi, w, preferred_element_type=F32) + jnp.dot(s_lo, w, preferred_element_type=F32)
                + b_ref[0])


def _modulation(cond, mod_w, mod_b):
    rows = cond.shape[0]
    tn = 2048
    n = N_MOD * D_MODEL
    return pl.pallas_call(
        _mod_kernel,
        out_shape=jax.ShapeDtypeStruct((DEPTH, rows, n), F32),
        grid=(DEPTH, n // tn),
        in_specs=[pl.BlockSpec((rows, D_MODEL), lambda l, j: (0, 0)),
                  pl.BlockSpec((1, D_MODEL, tn), lambda l, j: (l, 0, j)),
                  pl.BlockSpec((1, 1, tn), lambda l, j: (l, 0, j))],
        out_specs=pl.BlockSpec((1, rows, tn), lambda l, j: (l, 0, j)),
        compiler_params=_cparams(2),
        name="modulation",
    )(cond, mod_w, mod_b.reshape(DEPTH, 1, n))


class _Tokens:
    def __init__(self, batch, seq, n_ctx):
        self.batch, self.seq, self.n_ctx = batch, seq, n_ctx
        self.n_lat = batch * seq
        self.n_all = self.n_lat + batch * n_ctx

    def mod_spec(self, tm, layer):
        per_batch = self.seq // tm
        return pl.BlockSpec((None, 1, N_MOD, D_MODEL),
                            lambda i: (layer, jnp.minimum(i // per_batch, self.batch), 0, 0))

    def split_specs(self, tm, width, joined):
        nlt = self.n_lat // tm
        ctx_map = (lambda i: (jnp.maximum(i, nlt), 0)) if joined else (lambda i: (jnp.maximum(i - nlt, 0), 0))
        return pl.BlockSpec((tm, width), lambda i: (jnp.minimum(i, nlt - 1), 0)), pl.BlockSpec((tm, width), ctx_map)

    def pos_grid(self, blk, with_ctx, layer=0):
        n_lat_blk = self.seq // blk
        n_ctx_blk = self.n_ctx // blk
        lat_blocks = self.n_lat // blk
        row_map = lambda p, b: (jnp.where(p < n_lat_blk, b * n_lat_blk + p, lat_blocks + b * n_ctx_blk + (p - n_lat_blk)), 0)
        mod_map = lambda p, b: (layer, jnp.where(p < n_lat_blk, b, self.batch), 0, 0)
        grid = (n_lat_blk + (n_ctx_blk if with_ctx else 0), self.batch)
        return grid, row_map, mod_map


def _pick(n_lat_tiles, lat_ref, ctx_ref):
    return jnp.where(pl.program_id(0) < n_lat_tiles, lat_ref[...], ctx_ref[...])


def _layer_resident(layer, shape):
    nd = len(shape)
    return pl.BlockSpec((None,) + tuple(shape), lambda *_: (layer,) + (0,) * nd, pipeline_mode=pl.Buffered(1))


def _row_spec(tm, width):
    return pl.BlockSpec((tm, width), lambda i: (i, 0))


def _vec_spec(row):
    table, r = row
    return pl.BlockSpec((None, 1, table.shape[2]), lambda *_: (r, 0, 0))


FFN_CHUNK = 512


def _ffn_body(x, mod_ref, pre_ref, post_ref, w1_ref, w2_ref, o_ref, acc_ref):
    h = _norm_mod(x, pre_ref[...], mod_ref[0, 4:5, :], mod_ref[0, 3:4, :]).astype(BF16)
    for c in range(D_FF // FFN_CHUNK):
        sl = slice(c * FFN_CHUNK, (c + 1) * FFN_CHUNK)
        a = jnp.maximum(jnp.dot(h, w1_ref[:, sl], preferred_element_type=F32), 0.0)
        part = jnp.dot((a * a).astype(BF16), w2_ref[sl, :], preferred_element_type=F32)
        if c == 0:
            acc_ref[...] = part
        else:
            acc_ref[...] += part
    o_ref[...] = x + mod_ref[0, 5:6, :] * _rms(acc_ref[...], post_ref[...])


def _mix_ffn_kernel(xl_ref, xc_ref, fl_ref, fc_ref, al_ref, ac_ref, mod_ref, mpost_ref, pre_ref, post_ref,
                    wf_ref, wa_ref, w1_ref, w2_ref, o_ref, acc_ref, *, n_lat_tiles):
    y = (jnp.dot(_pick(n_lat_tiles, fl_ref, fc_ref), wf_ref[...], preferred_element_type=F32)
         + jnp.dot(_pick(n_lat_tiles, al_ref, ac_ref), wa_ref[...], preferred_element_type=F32))
    x = _pick(n_lat_tiles, xl_ref, xc_ref) + mod_ref[0, 2:3, :] * _rms(y, mpost_ref[...])
    _ffn_body(x, mod_ref, pre_ref, post_ref, w1_ref, w2_ref, o_ref, acc_ref)


def _mix_ffn(tok, x_lat, x_ctx, fm_lat, fm_ctx, ao_lat, ao_ctx, mod, mix_post_g, pre_g, post_g, wf_all, wa_all,
             w1_all, w2_all, i, layer, tm):
    n = tok.n_all
    kern = functools.partial(_mix_ffn_kernel, n_lat_tiles=tok.n_lat // tm)
    return pl.pallas_call(
        kern,
        out_shape=jax.ShapeDtypeStruct((n, D_MODEL), F32),
        grid=(n // tm,),
        in_specs=[*tok.split_specs(tm, D_MODEL, x_lat is x_ctx), *tok.split_specs(tm, FOURIER_WIDTH, False),
                  *tok.split_specs(tm, ATTN_WIDTH, False), tok.mod_spec(tm, layer),
                  _vec_spec(mix_post_g), _vec_spec(pre_g), _vec_spec(post_g),
                  _layer_resident(i, (FOURIER_WIDTH, D_MODEL)), _layer_resident(i, (ATTN_WIDTH, D_MODEL)),
                  _layer_resident(layer, (D_MODEL, D_FF)), _layer_resident(layer, (D_FF, D_MODEL))],
        out_specs=_row_spec(tm, D_MODEL),
        scratch_shapes=[pltpu.VMEM((tm, D_MODEL), F32)],
        compiler_params=_cparams(1),
        name="mix_ffn",
    )(x_lat, x_ctx, fm_lat, fm_ctx, ao_lat, ao_ctx, mod, mix_post_g[0], pre_g[0], post_g[0], wf_all, wa_all,
      w1_all, w2_all)


def _rope_block(x, cos, sin_hi, sin_lo):
    half = ROPE_AXIS_DIM // 2
    return (x * cos + pltpu.roll(x, half, axis=1) * sin_hi
            + pltpu.roll(x, LANES - half, axis=1) * sin_lo)


def _inproj_kernel(xl_ref, xc_ref, mod_ref, pre_ref, w_ref, cos_ref, shi_ref, slo_ref, f_ref, q_ref, k_ref, v_ref,
                   fs_ref, *, n_lat_tiles):
    x = _pick(n_lat_tiles, xl_ref, xc_ref)
    h = _norm_mod(x, pre_ref[...], mod_ref[0, 1:2, :], mod_ref[0, 0:1, :]).astype(BF16)
    p = jnp.dot(h, w_ref[...], preferred_element_type=F32)
    cos, shi, slo = cos_ref[...], shi_ref[...], slo_ref[...]
    pairs = x.shape[0] // 2
    n_slabs = FOURIER_WIDTH // LANES
    for s in range(n_slabs):
        fs_ref[s] = p[:, s * LANES:(s + 1) * LANES]
    for parity in range(2):
        for s in range(n_slabs):
            lo = parity * FOURIER_WIDTH + s * LANES
            f_ref[:, lo:lo + LANES] = fs_ref[s, pl.ds(parity, pairs, stride=2), :].astype(BF16)
    scale = HEAD_DIM ** -0.5 * LOG2_E
    for j in range(ATTN_WIDTH // LANES):
        lo = FOURIER_WIDTH + j * LANES
        q_ref[:, j * LANES:(j + 1) * LANES] = (_rope_block(p[:, lo:lo + LANES], cos, shi, slo) * scale).astype(BF16)
    k0 = FOURIER_WIDTH + ATTN_WIDTH
    k_ref[...] = _rope_block(p[:, k0:k0 + KV_WIDTH], cos, shi, slo).astype(BF16)
    v_ref[...] = p[:, k0 + KV_WIDTH:].astype(BF16)


def _rope_tables(seq, n_pad):
    pos = np.arange(seq)
    row = (pos // GRID_W).astype(np.float64)
    col = (pos % GRID_W).astype(np.float64)
    lane = np.arange(LANES)
    d = lane % HEAD_DIM
    j = d % (ROPE_AXIS_DIM // 2)
    inv = jnp.asarray(ROPE_BASE, F32) ** (-jnp.asarray(2 * j, F32) / ROPE_AXIS_DIM)
    use_col = jnp.asarray(d >= ROPE_AXIS_DIM)
    posv = jnp.where(use_col[None, :], jnp.asarray(col, F32)[:, None], jnp.asarray(row, F32)[:, None])
    ang = posv * inv[None, :]
    upper = jnp.asarray((d % ROPE_AXIS_DIM) >= ROPE_AXIS_DIM // 2)[None, :]
    cos, sin = jnp.cos(ang), jnp.sin(ang)
    sin_hi = jnp.where(upper, sin, 0.0)
    sin_lo = jnp.where(upper, 0.0, -sin)
    pad = lambda t, v: jnp.concatenate([t, jnp.full((n_pad, LANES), v, F32)], axis=0)
    return pad(cos, 1.0), pad(sin_hi, 0.0), pad(sin_lo, 0.0)


def _inproj(tok, x_lat, x_ctx, mod, pre_g, w_in_all, tables, i, layer, tm):
    per_batch = tok.seq // tm
    n_lat_tiles = tok.n_lat // tm
    tab_map = lambda i: (jnp.where(i < n_lat_tiles, i % per_batch, per_batch), 0)
    tab_spec = pl.BlockSpec((tm, LANES), tab_map)
    n = tok.n_all
    kern = functools.partial(_inproj_kernel, n_lat_tiles=n_lat_tiles)
    return pl.pallas_call(
        kern,
        out_shape=(jax.ShapeDtypeStruct((n // 2, 2 * FOURIER_WIDTH), BF16),
                   jax.ShapeDtypeStruct((n, ATTN_WIDTH), BF16),
                   jax.ShapeDtypeStruct((n, KV_WIDTH), BF16), jax.ShapeDtypeStruct((n, KV_WIDTH), BF16)),
        grid=(n // tm,),
        in_specs=[*tok.split_specs(tm, D_MODEL, x_lat is x_ctx), tok.mod_spec(tm, layer), _vec_spec(pre_g),
                  _layer_resident(i, (D_MODEL, IN_WIDTH)), tab_spec, tab_spec, tab_spec],
        out_specs=(_row_spec(tm // 2, 2 * FOURIER_WIDTH), _row_spec(tm, ATTN_WIDTH), _row_spec(tm, KV_WIDTH),
                   _row_spec(tm, KV_WIDTH)),
        scratch_shapes=[pltpu.VMEM((FOURIER_WIDTH // LANES, tm, LANES), F32)],
        compiler_params=_cparams(1),
        name="inproj",
    )(x_lat, x_ctx, mod, pre_g[0], w_in_all, *tables)


def _pair_heads(w, axis):
    shape = w.shape
    split = shape[:axis] + (N_KV_HEADS, HEAD_GROUP, HEAD_DIM) + shape[axis + 1:]
    return jnp.swapaxes(w.reshape(split), axis, axis + 1).reshape(shape)


def _dft_tables(length):
    half = length // 2
    k = np.arange(half)[:, None]
    m = np.arange(half)[None, :]

    def tab(n):
        ang = 2.0 * np.pi * ((k * n) % length) / length
        t = np.concatenate([np.cos(ang), -np.sin(ang)], axis=1) / math.sqrt(length)
        return jnp.asarray(t.astype(np.float32)).astype(BF16)

    return tab(2 * m), tab(2 * m + 1)


def _chan_table():
    n = FOURIER_GROUP_DIM
    k = np.arange(n)
    ang = 2.0 * np.pi * ((k[:, None] * k[None, :]) % n) / n
    t = np.concatenate([np.cos(ang), np.sin(ang)], axis=1) / math.sqrt(n)
    return jnp.asarray(t.astype(np.float32)).astype(BF16)


def _fourier_one(f_ref, chan_ref, pos_e_ref, pos_o_ref, o_ref, stk_ref):
    gd = FOURIER_GROUP_DIM
    half = f_ref.shape[0]
    row_chunk = min(half, 512)
    for parity in range(2):
        for g in range(FOURIER_GROUPS):
            lanes = slice(parity * FOURIER_WIDTH + g * gd, parity * FOURIER_WIDTH + (g + 1) * gd)
            z = jnp.dot(f_ref[:, lanes], chan_ref[...], preferred_element_type=F32)
            stk_ref[parity, 0:half, g * gd:(g + 1) * gd] = z[:, :gd].astype(BF16)
            stk_ref[parity, half:2 * half, g * gd:(g + 1) * gd] = z[:, gd:].astype(BF16)
    for r in range(half // row_chunk):
        rows = slice(r * row_chunk, (r + 1) * row_chunk)
        even = jnp.dot(pos_e_ref[rows, :], stk_ref[0], preferred_element_type=F32)
        odd = jnp.dot(pos_o_ref[rows, :], stk_ref[1], preferred_element_type=F32)
        o_ref[rows, :] = (even + odd).astype(BF16)
        o_ref[half + r * row_chunk:half + (r + 1) * row_chunk, :] = (even - odd).astype(BF16)


def _fourier_kernel(fl_ref, fc_ref, chan_ref, ple_ref, plo_ref, pce_ref, pco_ref, ol_ref, oc_ref, stkl_ref, stkc_ref):
    _fourier_one(fl_ref, chan_ref, ple_ref, plo_ref, ol_ref, stkl_ref)
    _fourier_one(fc_ref, chan_ref, pce_ref, pco_ref, oc_ref, stkc_ref)


def _fourier(tok, f_pairs, chan, tabs_lat, tabs_ctx):
    ctx0 = tok.n_lat // tok.n_ctx
    pair_blk = lambda rows, m: pl.BlockSpec((rows // 2, 2 * FOURIER_WIDTH), m)
    lat_blk = pl.BlockSpec((tok.seq, FOURIER_WIDTH), lambda b: (b, 0))
    ctx_blk = pl.BlockSpec((tok.n_ctx, FOURIER_WIDTH), lambda b: (b, 0))
    tabs = (*tabs_lat, *tabs_ctx)
    return pl.pallas_call(
        _fourier_kernel,
        out_shape=(jax.ShapeDtypeStruct((tok.n_lat, FOURIER_WIDTH), BF16),
                   jax.ShapeDtypeStruct((tok.batch * tok.n_ctx, FOURIER_WIDTH), BF16)),
        grid=(tok.batch,),
        in_specs=[pair_blk(tok.seq, lambda b: (b, 0)), pair_blk(tok.n_ctx, lambda b: (ctx0 + b, 0)),
                  _resident(chan.shape), *[_resident(t.shape) for t in tabs]],
        out_specs=(lat_blk, ctx_blk),
        scratch_shapes=[pltpu.VMEM((2, tok.seq, FOURIER_WIDTH), BF16),
                        pltpu.VMEM((2, tok.n_ctx, FOURIER_WIDTH), BF16)],
        compiler_params=_cparams(1),
        name="fourier",
    )(f_pairs, f_pairs, chan, *tabs)


ATTN_QBLOCKS = 8


def _attn_kernel(sink_ref, q_ref, *refs, n_qb, n_steps, layer_idx, with_local):
    j = pl.program_id(1)
    blk = WINDOW
    n_loc = 3 * blk
    rows = HEAD_GROUP * blk
    nt = (((1,), (1,)), ((), ()))
    if with_local:
        kp_ref, km_ref, kn_ref, kx_ref, vp_ref, vm_ref, vn_ref, vx_ref, o_ref = refs
        k_cat = jnp.concatenate([kp_ref[...], km_ref[...], kn_ref[...]], axis=0)
        v_cat = jnp.concatenate([vp_ref[...], vm_ref[...], vn_ref[...]], axis=0)
    else:
        kx_ref, vx_ref, o_ref = refs

    def attend(half):
        r0 = half * blk
        q = jnp.concatenate([q_ref[r0:r0 + blk, i * LANES:(i + 1) * LANES] for i in range(HEAD_GROUP)], axis=0)
        k_ctx, v_ctx = kx_ref[...], vx_ref[...]
        if with_local:
            k_loc, v_loc = k_cat[r0:r0 + n_loc], v_cat[r0:r0 + n_loc]
            col_lo = jnp.where(j >= 1, 0, blk) if half == 0 else 0
            col_hi = jnp.where(j + 1 < n_steps, n_loc, 2 * blk) if half == n_qb - 1 else n_loc
            qi = lax.broadcasted_iota(jnp.int32, (blk, 1), 0)
            kj = lax.broadcasted_iota(jnp.int32, (blk, n_loc), 1)
            visible = (kj >= jnp.maximum(qi, col_lo)) & (kj <= jnp.minimum(qi + 2 * WINDOW, col_hi - 1))
            bias = jnp.concatenate([jnp.where(visible, 0.0, NEG_INF)] * HEAD_GROUP, axis=0)
        row_blk = lax.broadcasted_iota(jnp.int32, (rows, 1), 0) // blk
        lane_o = lax.broadcasted_iota(jnp.int32, (rows, LANES), 1)
        outs = []
        for kvh in range(N_KV_HEADS):
            def own_lanes(t):
                lane = lax.broadcasted_iota(jnp.int32, t.shape, 1)
                return jnp.where((lane >= kvh * HEAD_DIM) & (lane < (kvh + 1) * HEAD_DIM), t, jnp.zeros_like(t))
            s_ctx = lax.dot_general(q, own_lanes(k_ctx), nt, preferred_element_type=F32)
            sink = jnp.zeros((rows, 1), F32)
            for i in range(HEAD_GROUP):
                sink = jnp.where(row_blk == i, sink_ref[layer_idx, kvh * HEAD_GROUP + i] * LOG2_E, sink)
            m = jnp.maximum(jnp.max(s_ctx, axis=-1, keepdims=True), sink)
            if with_local:
                s_loc = lax.dot_general(q, own_lanes(k_loc), nt, preferred_element_type=F32) + bias
                m = jnp.maximum(m, jnp.max(s_loc, axis=-1, keepdims=True))
            p_ctx = jnp.exp2(s_ctx - m)
            denom = jnp.sum(p_ctx, axis=-1, keepdims=True) + jnp.exp2(sink - m)
            o = jnp.dot(p_ctx.astype(BF16), v_ctx, preferred_element_type=F32)
            if with_local:
                p_loc = jnp.exp2(s_loc - m)
                denom = denom + jnp.sum(p_loc, axis=-1, keepdims=True)
                o = o + jnp.dot(p_loc.astype(BF16), v_loc, preferred_element_type=F32)
            outs.append(o / denom)
        merged = jnp.where(lane_o < HEAD_DIM, outs[0], outs[1])
        for i in range(HEAD_GROUP):
            o_ref[r0:r0 + blk, i * LANES:(i + 1) * LANES] = merged[i * blk:(i + 1) * blk, :].astype(BF16)

    for half in range(n_qb):
        attend(half)


def _attention(tok, q, k, v, sink_all, layer_idx):
    blk = WINDOW
    n_qblk = tok.seq // blk
    qb = math.gcd(ATTN_QBLOCKS, n_qblk)
    n_steps = n_qblk // qb
    ctx0 = tok.n_lat // tok.n_ctx
    smem = pl.BlockSpec(memory_space=pltpu.SMEM)
    ctx_blk = pl.BlockSpec((tok.n_ctx, KV_WIDTH), lambda b, j: (ctx0 + b, 0))

    main_map = lambda b, j: (b * n_steps + j, 0)
    prev_map = lambda b, j: (b * n_qblk + jnp.maximum(j * qb - 1, 0), 0)
    next_map = lambda b, j: (b * n_qblk + jnp.minimum(j * qb + qb, n_qblk - 1), 0)
    kv_specs = [pl.BlockSpec((blk, KV_WIDTH), prev_map), pl.BlockSpec((qb * blk, KV_WIDTH), main_map),
                pl.BlockSpec((blk, KV_WIDTH), next_map), ctx_blk]
    lat = pl.pallas_call(
        functools.partial(_attn_kernel, n_qb=qb, n_steps=n_steps, layer_idx=layer_idx, with_local=True),
        out_shape=jax.ShapeDtypeStruct((tok.n_lat, ATTN_WIDTH), BF16),
        grid=(tok.batch, n_steps),
        in_specs=[smem, pl.BlockSpec((qb * blk, ATTN_WIDTH), main_map), *kv_specs, *kv_specs],
        out_specs=pl.BlockSpec((qb * blk, ATTN_WIDTH), main_map),
        compiler_params=_cparams(2),
        name="window_attention",
    )(sink_all, q, k, k, k, k, v, v, v, v)

    n_cb = tok.n_ctx // blk
    ctx = pl.pallas_call(
        functools.partial(_attn_kernel, n_qb=n_cb, n_steps=1, layer_idx=layer_idx, with_local=False),
        out_shape=jax.ShapeDtypeStruct((tok.batch * tok.n_ctx, ATTN_WIDTH), BF16),
        grid=(tok.batch, 1),
        in_specs=[smem, pl.BlockSpec((tok.n_ctx, ATTN_WIDTH), lambda b, j: (ctx0 + b, 0)), ctx_blk, ctx_blk],
        out_specs=pl.BlockSpec((tok.n_ctx, ATTN_WIDTH), lambda b, j: (b, 0)),
        compiler_params=_cparams(2),
        name="context_attention",
    )(sink_all, q, k, v)
    return lat, ctx


def _piece_perm():
    idx = np.arange(D_MODEL)
    a, b, c = idx // LANES, (idx // SSM_GROUP_DIM) % SSM_PIECES, idx % SSM_GROUP_DIM
    perm = np.zeros((D_MODEL, D_MODEL), np.float32)
    perm[idx, b * LANES + a * SSM_GROUP_DIM + c] = 1.0
    return jnp.asarray(perm).astype(BF16)


def _slab_pitch(blk):
    return blk + SUBLANES


S5_PRE_BATCHES = 8


def _s5_pre_kernel(*refs, blk, batch, n_blocks):
    nb = S5_PRE_BATCHES
    x_refs, mod_refs = refs[:nb], refs[nb:2 * nb]
    pre_ref, perm_ref, u_ref, hs_ref = refs[2 * nb:]
    p, bb = pl.program_id(0), pl.program_id(1)
    pitch = _slab_pitch(blk)

    @pl.when(p < n_blocks)
    def _():
        slot0 = (p % 2) * SSM_SLABS
        for k in range(nb):
            h = _norm_mod(x_refs[k][...], pre_ref[...], mod_refs[k][0, 1:2, :], mod_refs[k][0, 0:1, :])
            row0 = pl.multiple_of((bb * nb + k) * pitch, SUBLANES)
            for s in range(SSM_SLABS):
                hs_ref[slot0 + s, pl.ds(row0, blk), :] = h[:, s * LANES:(s + 1) * LANES]

    @pl.when(p >= 1)
    def _():
        half = SSM_CHUNK // 2
        cpb = blk // SSM_CHUNK
        rows_h = cpb * batch
        for k in range(nb):
            s = bb * nb + k
            src = ((p - 1) % 2) * SSM_SLABS + s
            lhs = []
            for hh in range(2):
                for i in range(cpb):
                    t0 = i * SSM_CHUNK + hh * half
                    lhs.append(jnp.concatenate(
                        [hs_ref[src, pl.ds(t0 + j, batch, stride=pitch), :] for j in range(half)], axis=1))
            lhs = jnp.concatenate(lhs, axis=0).astype(BF16)
            out = jnp.dot(lhs, perm_ref[...], preferred_element_type=F32).astype(BF16)
            for hh in range(2):
                for g in range(SSM_PIECES):
                    u_ref[s * SSM_PIECES + g, :, hh * LANES:(hh + 1) * LANES] = (
                        out[hh * rows_h:(hh + 1) * rows_h, g * LANES:(g + 1) * LANES])


def _s5_pre(tok, xa, mod, pre_g, perm, layer, blk):
    assert tok.batch == SSM_SLABS
    (n_blocks, _), row_map, mod_map = tok.pos_grid(blk, True, layer)
    n_lat_blk = tok.seq // blk
    n_ctx_blk = tok.n_ctx // blk
    cpb = blk // SSM_CHUNK
    n_chunks = (tok.seq + tok.n_ctx) // SSM_CHUNK
    last = n_blocks - 1

    def u_map(p, b):
        q = jnp.maximum(p - 1, 0)
        return (0, jnp.where(q < n_lat_blk, n_ctx_blk + q, q - n_lat_blk), 0)

    nb = S5_PRE_BATCHES
    assert tok.batch % nb == 0
    of_batch = lambda m, k: (lambda p, bb: m(jnp.minimum(p, last), bb * nb + k))
    kern = functools.partial(_s5_pre_kernel, blk=blk, batch=tok.batch, n_blocks=n_blocks)
    return pl.pallas_call(
        kern,
        out_shape=jax.ShapeDtypeStruct((SSM_GROUPS, n_chunks * tok.batch, SSM_CW), BF16),
        grid=(n_blocks + 1, tok.batch // nb),
        in_specs=[*[pl.BlockSpec((blk, D_MODEL), of_batch(row_map, k)) for k in range(nb)],
                  *[pl.BlockSpec((None, 1, N_MOD, D_MODEL), of_batch(mod_map, k)) for k in range(nb)],
                  _vec_spec(pre_g), _resident((D_MODEL, D_MODEL))],
        out_specs=pl.BlockSpec((SSM_GROUPS, cpb * tok.batch, SSM_CW), u_map),
        scratch_shapes=[pltpu.VMEM((2 * SSM_SLABS, tok.batch * _slab_pitch(blk), LANES), F32)],
        compiler_params=_cparams(2),
        name="s5_pre",
    )(*([xa] * nb), *([mod] * nb), pre_g[0], perm)


def _cmul(ar, ai, br, bi):
    return ar * br - ai * bi, ar * bi + ai * br


def _s5_table_plan():
    t_n = SSM_CHUNK
    t = np.arange(t_n)
    return [
        [(0, t_n - 1 - t, 'b', 're'), (1, t, 'b', 're'), (0, t_n - 1 - t, 'b', 'im'), (1, t, 'b', 'im')],
        [(0, -t, 'b', 're'), (0, -t, 'b', 'im'), (1, t, 'b', 're'), (1, t, 'b', 'im')],
        [(0, t, 'c', 're'), (0, t, 'c', '-im'), (1, -t, 'c', 're'), (1, -t, 'c', '-im')],
        [(0, t + 1, 'c', 're'), (1, t_n - t, 'c', 're'), (0, t + 1, 'c', '-im'), (1, t_n - t, 'c', '-im')],
    ]


def _s5_operands(a_re, a_im, log_dt, b_re, b_im, c_re, c_im):
    t_n = SSM_CHUNK
    per_group = lambda x: jnp.swapaxes(x.astype(F32), 1, 2)
    l_re, l_im = per_group(a_re), per_group(a_im)
    dt = jnp.exp(per_group(log_dt))[..., None]
    z_re, z_im = l_re * dt, l_im * dt
    n_all = np.arange(1 - t_n, t_n + 1)
    positive = (n_all > 0)[:, None]
    mag, mag_inv = jnp.exp(z_re), jnp.exp(-z_re)
    up = (mag * jnp.cos(z_im), mag * jnp.sin(z_im))
    down = (mag_inv * jnp.cos(z_im), -mag_inv * jnp.sin(z_im))
    shape = z_re.shape[:-1] + (len(n_all), z_re.shape[-1])
    acc = (jnp.ones(shape, F32), jnp.zeros(shape, F32))
    for k in range(int(t_n).bit_length()):
        bit = (((np.abs(n_all) >> k) & 1) == 1)[:, None]
        f_re = jnp.where(bit, jnp.where(positive, up[0][..., None, :], down[0][..., None, :]), 1.0)
        f_im = jnp.where(bit, jnp.where(positive, up[1][..., None, :], down[1][..., None, :]), 0.0)
        acc = _cmul(acc[0], acc[1], f_re, f_im)
        up, down = _cmul(*up, *up), _cmul(*down, *down)
    pow_re, pow_im = acc
    a1_re, a1_im = pow_re[:, :, :, t_n, :], pow_im[:, :, :, t_n, :]
    den = l_re * l_re + l_im * l_im
    r_re = ((a1_re - 1.0) * l_re + a1_im * l_im) / den
    r_im = (a1_im * l_re - (a1_re - 1.0) * l_im) / den
    bb_re, bb_im = _cmul(r_re[..., None], r_im[..., None], per_group(b_re), per_group(b_im))
    base = jnp.stack([jnp.swapaxes(bb_re, -1, -2), jnp.swapaxes(bb_im, -1, -2), per_group(c_re), per_group(c_im)],
                     axis=3)
    n_l, n_g = base.shape[0], base.shape[1]
    base = base.reshape(n_l, n_g, 8, SSM_GROUP_DIM, SSM_STATE)
    signed = jnp.concatenate([base, -base], axis=2)

    last = 2 * t_n - 1
    dec = jnp.concatenate([pow_re[:, :, 0, last], pow_re[:, :, 1, last], pow_im[:, :, 0, last], pow_im[:, :, 1, last]],
                          axis=-1)[:, :, None, :]
    return pow_re, pow_im, signed, dec


def _s5_kernel(u_ref, pre_ref, pim_ref, sg_ref, dec_ref, y_ref, v_ref, xin_ref, *, batch, n_chunks, n_ctx_chunks):
    gps = SSM_GROUPS_PER_STEP
    ns = SSM_STATE
    ns2 = 2 * ns
    cw = SSM_CW
    t_n = SSM_CHUNK
    nt = (((1,), (1,)), ((), ()))
    plan = _s5_table_plan()

    def table(g, k):
        p_re, p_im, w_a, w_b = [], [], [], []
        for d, expo, w, part in plan[k]:
            lo = int(min(expo)) + t_n - 1
            run_re, run_im = pre_ref[g, d, lo:lo + t_n, :], pim_ref[g, d, lo:lo + t_n, :]
            if expo[1] < expo[0]:
                flip = lambda r: jnp.concatenate([r[t_n - 1 - t:t_n - t, :] for t in range(t_n)], axis=0)
                run_re, run_im = flip(run_re), flip(run_im)
            p_re.append(run_re)
            p_im.append(run_im)
            re, im = d * 4 + (0 if w == 'b' else 2), d * 4 + (1 if w == 'b' else 3)
            k_a, k_b = {'re': (re, im + 8), 'im': (im, re), '-im': (im + 8, re + 8)}[part]
            w_a.append(sg_ref[g, k_a])
            w_b.append(sg_ref[g, k_b])
        p_re, p_im, w_a, w_b = (jnp.concatenate(x, axis=1) for x in (p_re, p_im, w_a, w_b))
        return jnp.concatenate([p_re[t:t + 1, :] * w_a + p_im[t:t + 1, :] * w_b for t in range(t_n)], axis=0)

    for g in range(gps):
        v_ref[g] = jnp.dot(u_ref[g], table(g, 0).astype(BF16), preferred_element_type=F32)
    a_re = jnp.broadcast_to(dec_ref[:, :, 0:ns2], (gps, batch, ns2))
    a_im = jnp.broadcast_to(dec_ref[:, :, ns2:2 * ns2], (gps, batch, ns2))
    is_fwd = lax.broadcasted_iota(jnp.int32, (gps, batch, ns2), 2) < ns

    def step(k, carry):
        x_re, x_im = carry
        kb = jnp.where(k < n_ctx_chunks, n_ctx_chunks - 1 - k, n_chunks + n_ctx_chunks - 1 - k)
        rf = pl.multiple_of(k * batch, batch)
        rb = pl.multiple_of(kb * batch, batch)
        xin_ref[:, pl.ds(rf, batch), 0:ns] = x_re[:, :, 0:ns]
        xin_ref[:, pl.ds(rb, batch), ns:ns2] = x_re[:, :, ns:ns2]
        xin_ref[:, pl.ds(rf, batch), ns2:ns2 + ns] = x_im[:, :, 0:ns]
        xin_ref[:, pl.ds(rb, batch), ns2 + ns:2 * ns2] = x_im[:, :, ns:ns2]
        v_re = jnp.where(is_fwd, v_ref[:, pl.ds(rf, batch), 0:ns2], v_ref[:, pl.ds(rb, batch), 0:ns2])
        v_im = jnp.where(is_fwd, v_ref[:, pl.ds(rf, batch), ns2:2 * ns2], v_ref[:, pl.ds(rb, batch), ns2:2 * ns2])
        return a_re * x_re - a_im * x_im + v_re, a_re * x_im + a_im * x_re + v_im

    zero = jnp.zeros((gps, batch, ns2), F32)
    lax.fori_loop(0, n_chunks, step, (zero, zero))
    t_in = lax.broadcasted_iota(jnp.int32, (cw, cw), 0) // SSM_GROUP_DIM
    t_out = lax.broadcasted_iota(jnp.int32, (cw, cw), 1) // SSM_GROUP_DIM

    def split(x):
        hi = x.astype(BF16)
        return hi, (x - hi.astype(F32)).astype(BF16)

    def lag_kernel(e, ft, lanes):
        (eh, el), (fh, fl) = split(e[:, lanes]), split(ft[:, lanes])
        return (lax.dot_general(eh, fh, nt, preferred_element_type=F32)
                + lax.dot_general(eh, fl, nt, preferred_element_type=F32)
                + lax.dot_general(el, fh, nt, preferred_element_type=F32))

    for g in range(gps):
        e, ft = table(g, 1), table(g, 2)
        m = (jnp.where(t_out >= t_in, lag_kernel(e, ft, slice(0, ns2)), 0.0)
             + jnp.where(t_in >= t_out, lag_kernel(e, ft, slice(ns2, 2 * ns2)), 0.0)).astype(BF16)
        y_ref[g] = (jnp.dot(u_ref[g], m, preferred_element_type=F32)
                    + lax.dot_general(xin_ref[g].astype(BF16), table(g, 3).astype(BF16), nt,
                                      preferred_element_type=F32)).astype(BF16)


def _s5(u_t, operands, layer, batch, n_chunks, n_ctx_chunks):
    pow_re, pow_im, signed, dec = operands
    gps = SSM_GROUPS_PER_STEP
    rows = n_chunks * batch
    kern = functools.partial(_s5_kernel, batch=batch, n_chunks=n_chunks, n_ctx_chunks=n_ctx_chunks)
    gspec = lambda r, c: pl.BlockSpec((gps, r, c), lambda i: (i, 0, 0))
    lspec = lambda shape: pl.BlockSpec((None, gps) + shape, lambda i: (layer, i) + (0,) * len(shape))
    return pl.pallas_call(
        kern,
        out_shape=jax.ShapeDtypeStruct((SSM_GROUPS, rows, SSM_CW), BF16),
        grid=(SSM_GROUPS // gps,),
        in_specs=[gspec(rows, SSM_CW), lspec(pow_re.shape[2:]), lspec(pow_im.shape[2:]), lspec(signed.shape[2:]),
                  lspec(dec.shape[2:])],
        out_specs=gspec(rows, SSM_CW),
        scratch_shapes=[pltpu.VMEM((gps, rows, SSM_CW), F32), pltpu.VMEM((gps, rows, SSM_CW), F32)],
        compiler_params=_cparams(1),
        name="s5_scan",
    )(u_t, pow_re, pow_im, signed, dec)


def _s5_unpack_kernel(y_ref, perm_ref, *refs, blk, batch, n_lat_blk):
    outs, (ys_ref, tmp_ref) = refs[:-2], refs[-2:]
    p = pl.program_id(0)
    half = SSM_CHUNK // 2
    cpb = blk // SSM_CHUNK
    rows_h = cpb * batch

    lhs = jnp.concatenate(
        [jnp.concatenate([y_ref[s * SSM_PIECES + g, :, hh * LANES:(hh + 1) * LANES]
                          for g in range(SSM_PIECES)], axis=1)
         for s in range(SSM_SLABS) for hh in range(2)], axis=0)
    tmp_ref[...] = jnp.dot(lhs, perm_ref[...], preferred_element_type=F32)

    def slab(s, carry):
        for hh in range(2):
            for i in range(cpb):
                r0 = pl.multiple_of((s * 2 + hh) * rows_h + i * batch, batch)
                t0 = i * SSM_CHUNK + hh * half
                for j in range(half):
                    ys_ref[s, (t0 + j) * batch:(t0 + j + 1) * batch, :] = (
                        tmp_ref[pl.ds(r0, batch), j * LANES:(j + 1) * LANES])
        return carry

    lax.fori_loop(0, SSM_SLABS, slab, 0)

    def write(o_ref):
        for b in range(batch):
            for s in range(SSM_SLABS):
                o_ref[b, :, s * LANES:(s + 1) * LANES] = ys_ref[s, pl.ds(b, blk, stride=batch), :].astype(BF16)

    if len(outs) == 1:
        write(outs[0])
    else:
        pl.when(p < n_lat_blk)(lambda: write(outs[0]))
        pl.when(p >= n_lat_blk)(lambda: write(outs[1]))


def _s5_unpack(tok, y_t, perm, blk, with_ctx):
    n_lat_blk = tok.seq // blk
    n_ctx_blk = tok.n_ctx // blk
    cpb = blk // SSM_CHUNK
    y_map = lambda p: (0, jnp.where(p < n_lat_blk, n_ctx_blk + p, p - n_lat_blk), 0)
    out_blk = lambda m: pl.BlockSpec((tok.batch, blk, D_MODEL), m)
    out_shape = [jax.ShapeDtypeStruct((tok.batch, tok.seq, D_MODEL), BF16)]
    out_specs = [out_blk(lambda p: (0, jnp.minimum(p, n_lat_blk - 1), 0))]
    if with_ctx:
        out_shape.append(jax.ShapeDtypeStruct((tok.batch, tok.n_ctx, D_MODEL), BF16))
        out_specs.append(out_blk(lambda p: (0, jnp.maximum(p - n_lat_blk, 0), 0)))
    kern = functools.partial(_s5_unpack_kernel, blk=blk, batch=tok.batch, n_lat_blk=n_lat_blk)
    outs = pl.pallas_call(
        kern,
        out_shape=out_shape,
        grid=(n_lat_blk + (n_ctx_blk if with_ctx else 0),),
        in_specs=[pl.BlockSpec((SSM_GROUPS, cpb * tok.batch, SSM_CW), y_map), _resident((D_MODEL, D_MODEL))],
        out_specs=out_specs,
        scratch_shapes=[pltpu.VMEM((SSM_SLABS, tok.batch * blk, LANES), F32),
                        pltpu.VMEM((SSM_SLABS * 2 * cpb * tok.batch, D_MODEL), F32)],
        compiler_params=_cparams(1),
        name="s5_unpack",
    )(y_t, perm)
    y_lat = outs[0].reshape(tok.n_lat, D_MODEL)
    return y_lat, (outs[1].reshape(tok.batch * tok.n_ctx, D_MODEL) if with_ctx else None)


def _glu_ffn_kernel(x_ref, yl_ref, yc_ref, d_ref, mod_ref, mpre_ref, mpost_ref, pre_ref, post_ref, wg_ref, w1_ref,
                    w2_ref, o_ref, acc_ref, *, n_lat_tiles):
    x = x_ref[...]
    h = _norm_mod(x, mpre_ref[...], mod_ref[0, 1:2, :], mod_ref[0, 0:1, :])
    y = _pick(n_lat_tiles, yl_ref, yc_ref).astype(F32) + d_ref[...] * h
    z = jnp.dot(jax.nn.gelu(y).astype(BF16), wg_ref[...], preferred_element_type=F32)
    out = z[:, :D_MODEL] * jax.nn.sigmoid(z[:, D_MODEL:])
    x = x + mod_ref[0, 2:3, :] * _rms(out, mpost_ref[...])
    _ffn_body(x, mod_ref, pre_ref, post_ref, w1_ref, w2_ref, o_ref, acc_ref)


def _glu_ffn(tok, xa, n_rows, y_lat, y_ctx, d_skip, mod, mix_pre_g, mix_post_g, pre_g, post_g, glu_all, w1_all,
             w2_all, i, layer, tm):
    if y_ctx is None:
        y_ctx = y_lat
    kern = functools.partial(_glu_ffn_kernel, n_lat_tiles=tok.n_lat // tm)
    return pl.pallas_call(
        kern,
        out_shape=jax.ShapeDtypeStruct((n_rows, D_MODEL), F32),
        grid=(n_rows // tm,),
        in_specs=[_row_spec(tm, D_MODEL), *tok.split_specs(tm, D_MODEL, False), _vec_spec(d_skip),
                  tok.mod_spec(tm, layer),
                  _vec_spec(mix_pre_g), _vec_spec(mix_post_g), _vec_spec(pre_g), _vec_spec(post_g),
                  _layer_resident(i, (D_MODEL, 2 * D_MODEL)),
                  _layer_resident(layer, (D_MODEL, D_FF)), _layer_resident(layer, (D_FF, D_MODEL))],
        out_specs=_row_spec(tm, D_MODEL),
        scratch_shapes=[pltpu.VMEM((tm, D_MODEL), F32)],
        compiler_params=_cparams(1),
        name="glu_ffn",
    )(xa, y_lat, y_ctx, d_skip[0], mod, mix_pre_g[0], mix_post_g[0], pre_g[0], post_g[0], glu_all, w1_all, w2_all)


def _tile(limit, *sizes):
    tm = limit
    while any(s % tm for s in sizes):
        tm //= 2
    return tm


def kernel(x, c, ctx, c_ctx, mod_w, mod_b, mix_pre_g, mix_post_g, ffn_pre_g, ffn_post_g, ffn_w1, ffn_w2,
           even_w_in, even_w_out, even_sink, ssm_a_re, ssm_a_im, ssm_log_dt, ssm_b_re, ssm_b_im, ssm_c_re,
           ssm_c_im, ssm_d, ssm_glu_w):
    batch, seq, _ = x.shape
    n_ctx = ctx.shape[1]
    tok = _Tokens(batch, seq, n_ctx)
    assert seq % WINDOW == 0 and n_ctx % WINDOW == 0 and tok.n_lat % n_ctx == 0
    assert batch == SUBLANES
    tm = _tile(256, seq, n_ctx)
    tm_wide = _tile(512, seq, batch * n_ctx)
    tm_in = _tile(1024, seq, batch * n_ctx)

    x_lat, x_ctx = x.reshape(tok.n_lat, D_MODEL), ctx.astype(x.dtype).reshape(-1, D_MODEL)

    n_cond = 2 * SUBLANES
    cond = jnp.zeros((n_cond, D_MODEL), F32).at[:batch].set(c).at[batch].set(c_ctx)
    mod = _modulation(cond, mod_w, mod_b).reshape(DEPTH, n_cond, N_MOD, D_MODEL)

    rope = _rope_tables(seq, tm_in)
    piece_perm = _piece_perm()
    dft_chan, dft_lat, dft_ctx = _chan_table(), _dft_tables(seq), _dft_tables(n_ctx)
    rows3 = lambda t: t.reshape(t.shape[0], 1, t.shape[1])
    w1_all, w2_all, glu_all = ffn_w1.astype(BF16), ffn_w2.astype(BF16), ssm_glu_w.astype(BF16)
    q0, k0 = FOURIER_WIDTH, FOURIER_WIDTH + ATTN_WIDTH
    w_in_all = jnp.concatenate([even_w_in[:, :, :q0], _pair_heads(even_w_in[:, :, q0:k0], 2),
                                even_w_in[:, :, k0:]], axis=2).astype(BF16)
    wf_all = even_w_out[:, :FOURIER_WIDTH].astype(BF16)
    wa_all = _pair_heads(even_w_out[:, FOURIER_WIDTH:], 1).astype(BF16)
    s5_operands = _s5_operands(ssm_a_re, ssm_a_im, ssm_log_dt, ssm_b_re, ssm_b_im, ssm_c_re, ssm_c_im)

    for layer in range(DEPTH):
        need_ctx = layer < DEPTH - 1
        n_rows = tok.n_all if need_ctx else tok.n_lat
        i = layer // 2
        mix_pre, mix_post = (rows3(mix_pre_g), layer), (rows3(mix_post_g), layer)
        ffn_pre, ffn_post = (rows3(ffn_pre_g), layer), (rows3(ffn_post_g), layer)
        if layer % 2 == 0:
            f, q, k, v = _inproj(tok, x_lat, x_ctx, mod, mix_pre, w_in_all, rope, i, layer, tm_in)
            fm_lat, fm_ctx = _fourier(tok, f, dft_chan, dft_lat, dft_ctx)
            ao_lat, ao_ctx = _attention(tok, q, k, v, even_sink, i)
            xa = _mix_ffn(tok, x_lat, x_ctx, fm_lat, fm_ctx, ao_lat, ao_ctx, mod, mix_post, ffn_pre, ffn_post,
                          wf_all, wa_all, w1_all, w2_all, i, layer, tm_wide)
        else:
            u_t = _s5_pre(tok, xa, mod, mix_pre, piece_perm, layer, tm)
            y_t = _s5(u_t, s5_operands, i, batch, (seq + n_ctx) // SSM_CHUNK, n_ctx // SSM_CHUNK)
            y_lat, y_ctx = _s5_unpack(tok, y_t, piece_perm, tm, need_ctx)
            xa = _glu_ffn(tok, xa, n_rows, y_lat, y_ctx, (rows3(ssm_d), i), mod, mix_pre, mix_post, ffn_pre, ffn_post,
                          glu_all, w1_all, w2_all, i, layer, tm_wide)
        x_lat = x_ctx = xa
    return xa[:tok.n_lat].reshape(batch, seq, D_MODEL)
```

```python
import functools
import math

import numpy as np
import jax
import jax.numpy as jnp
from jax import lax
from jax.experimental import pallas as pl
from jax.experimental.pallas import tpu as pltpu

D_MODEL = 1024
DEPTH = 4
N_MOD = 6
EPS = 1e-6
NEG_INF = -1e30
GRID_W = 64

FOURIER_GROUPS = 4
FOURIER_GROUP_DIM = 128
FOURIER_WIDTH = FOURIER_GROUPS * FOURIER_GROUP_DIM

N_HEADS = 8
N_KV_HEADS = 2
HEAD_GROUP = N_HEADS // N_KV_HEADS
HEAD_DIM = 64
ATTN_WIDTH = N_HEADS * HEAD_DIM
KV_WIDTH = N_KV_HEADS * HEAD_DIM
WINDOW = 128
ROPE_AXIS_DIM = HEAD_DIM // 2
ROPE_BASE = 10000.0
LOG2_E = math.log2(math.e)
IN_WIDTH = FOURIER_WIDTH + ATTN_WIDTH + 2 * KV_WIDTH

LANES = 128
SUBLANES = 8
VMEM_LIMIT = 56 * 1024 * 1024

SSM_GROUP_DIM = 16
SSM_GROUPS = D_MODEL // SSM_GROUP_DIM
SSM_STATE = 64
SSM_CHUNK = 16
SSM_CW = SSM_CHUNK * SSM_GROUP_DIM
SSM_GROUPS_PER_STEP = 8
SSM_SLABS = D_MODEL // LANES
SSM_PIECES = LANES // SSM_GROUP_DIM

D_FF = 4 * D_MODEL

F32 = jnp.float32
BF16 = jnp.bfloat16


def _cparams(n_axes):
    return pltpu.CompilerParams(dimension_semantics=("arbitrary",) * n_axes, vmem_limit_bytes=VMEM_LIMIT)


def _resident(shape):
    nd = len(shape)
    return pl.BlockSpec(shape, lambda *_: (0,) * nd, pipeline_mode=pl.Buffered(1))


def _rms(x, g):
    return x * lax.rsqrt(jnp.mean(x * x, axis=-1, keepdims=True) + EPS) * g


def _norm_mod(x, g, sc, sh):
    return _rms(x, g) * (1.0 + sc) + sh


def _mod_kernel(cond_ref, w_ref, b_ref, o_ref):
    cond = cond_ref[...]
    s = cond * jax.nn.sigmoid(cond)
    s_hi = s.astype(BF16)
    s_lo = (s - s_hi.astype(F32)).astype(BF16)
    w = w_ref[0].astype(BF16)
    o_ref[0] = (jnp.dot(s_hi, w, preferred_element_type=F32) + jnp.dot(s_lo, w, preferred_element_type=F32)
                + b_ref[0])


def _modulation(cond, mod_w, mod_b):
    rows = cond.shape[0]
    tn = 2048
    n = N_MOD * D_MODEL
    return pl.pallas_call(
        _mod_kernel,
        out_shape=jax.ShapeDtypeStruct((DEPTH, rows, n), F32),
        grid=(DEPTH, n // tn),
        in_specs=[pl.BlockSpec((rows, D_MODEL), lambda l, j: (0, 0)),
                  pl.BlockSpec((1, D_MODEL, tn), lambda l, j: (l, 0, j)),
                  pl.BlockSpec((1, 1, tn), lambda l, j: (l, 0, j))],
        out_specs=pl.BlockSpec((1, rows, tn), lambda l, j: (l, 0, j)),
        compiler_params=_cparams(2),
        name="modulation",
    )(cond, mod_w, mod_b.reshape(DEPTH, 1, n))


class _Tokens:
    def __init__(self, batch, seq, n_ctx):
        self.batch, self.seq, self.n_ctx = batch, seq, n_ctx
        self.n_lat = batch * seq
        self.n_all = self.n_lat + batch * n_ctx

    def mod_spec(self, tm, layer):
        per_batch = self.seq // tm
        return pl.BlockSpec((None, 1, N_MOD, D_MODEL),
                            lambda i: (layer, jnp.minimum(i // per_batch, self.batch), 0, 0))

    def split_specs(self, tm, width, joined):
        nlt = self.n_lat // tm
        ctx_map = (lambda i: (jnp.maximum(i, nlt), 0)) if joined else (lambda i: (jnp.maximum(i - nlt, 0), 0))
        return pl.BlockSpec((tm, width), lambda i: (jnp.minimum(i, nlt - 1), 0)), pl.BlockSpec((tm, width), ctx_map)

    def pos_grid(self, blk, with_ctx, layer=0):
        n_lat_blk = self.seq // blk
        n_ctx_blk = self.n_ctx // blk
        lat_blocks = self.n_lat // blk
        row_map = lambda p, b: (jnp.where(p < n_lat_blk, b * n_lat_blk + p, lat_blocks + b * n_ctx_blk + (p - n_lat_blk)), 0)
        mod_map = lambda p, b: (layer, jnp.where(p < n_lat_blk, b, self.batch), 0, 0)
        grid = (n_lat_blk + (n_ctx_blk if with_ctx else 0), self.batch)
        return grid, row_map, mod_map


def _pick(n_lat_tiles, lat_ref, ctx_ref):
    return jnp.where(pl.program_id(0) < n_lat_tiles, lat_ref[...], ctx_ref[...])


def _layer_resident(layer, shape):
    nd = len(shape)
    return pl.BlockSpec((None,) + tuple(shape), lambda *_: (layer,) + (0,) * nd, pipeline_mode=pl.Buffered(1))


def _row_spec(tm, width):
    return pl.BlockSpec((tm, width), lambda i: (i, 0))


def _vec_spec(row):
    table, r = row
    return pl.BlockSpec((None, 1, table.shape[2]), lambda *_: (r, 0, 0))


FFN_CHUNK = 512


def _ffn_body(x, mod_ref, pre_ref, post_ref, w1_ref, w2_ref, o_ref, acc_ref):
    h = _norm_mod(x, pre_ref[...], mod_ref[0, 4:5, :], mod_ref[0, 3:4, :]).astype(BF16)
    for c in range(D_FF // FFN_CHUNK):
        sl = slice(c * FFN_CHUNK, (c + 1) * FFN_CHUNK)
        a = jnp.maximum(jnp.dot(h, w1_ref[:, sl], preferred_element_type=F32), 0.0)
        part = jnp.dot((a * a).astype(BF16), w2_ref[sl, :], preferred_element_type=F32)
        if c == 0:
            acc_ref[...] = part
        else:
            acc_ref[...] += part
    o_ref[...] = x + mod_ref[0, 5:6, :] * _rms(acc_ref[...], post_ref[...])


def _mix_ffn_kernel(xl_ref, xc_ref, fl_ref, fc_ref, al_ref, ac_ref, mod_ref, mpost_ref, pre_ref, post_ref,
                    wf_ref, wa_ref, w1_ref, w2_ref, o_ref, acc_ref, *, n_lat_tiles):
    y = (jnp.dot(_pick(n_lat_tiles, fl_ref, fc_ref), wf_ref[...], preferred_element_type=F32)
         + jnp.dot(_pick(n_lat_tiles, al_ref, ac_ref), wa_ref[...], preferred_element_type=F32))
    x = _pick(n_lat_tiles, xl_ref, xc_ref) + mod_ref[0, 2:3, :] * _rms(y, mpost_ref[...])
    _ffn_body(x, mod_ref, pre_ref, post_ref, w1_ref, w2_ref, o_ref, acc_ref)


def _mix_ffn(tok, x_lat, x_ctx, fm_lat, fm_ctx, ao_lat, ao_ctx, mod, mix_post_g, pre_g, post_g, wf_all, wa_all,
             w1_all, w2_all, i, layer, tm):
    n = tok.n_all
    kern = functools.partial(_mix_ffn_kernel, n_lat_tiles=tok.n_lat // tm)
    return pl.pallas_call(
        kern,
        out_shape=jax.ShapeDtypeStruct((n, D_MODEL), F32),
        grid=(n // tm,),
        in_specs=[*tok.split_specs(tm, D_MODEL, x_lat is x_ctx), *tok.split_specs(tm, FOURIER_WIDTH, False),
                  *tok.split_specs(tm, ATTN_WIDTH, False), tok.mod_spec(tm, layer),
                  _vec_spec(mix_post_g), _vec_spec(pre_g), _vec_spec(post_g),
                  _layer_resident(i, (FOURIER_WIDTH, D_MODEL)), _layer_resident(i, (ATTN_WIDTH, D_MODEL)),
                  _layer_resident(layer, (D_MODEL, D_FF)), _layer_resident(layer, (D_FF, D_MODEL))],
        out_specs=_row_spec(tm, D_MODEL),
        scratch_shapes=[pltpu.VMEM((tm, D_MODEL), F32)],
        compiler_params=_cparams(1),
        name="mix_ffn",
    )(x_lat, x_ctx, fm_lat, fm_ctx, ao_lat, ao_ctx, mod, mix_post_g[0], pre_g[0], post_g[0], wf_all, wa_all,
      w1_all, w2_all)


def _rotate_half_columns(w):
    half = ROPE_AXIS_DIM // 2
    pairs = w.reshape(w.shape[:-1] + (w.shape[-1] // ROPE_AXIS_DIM, 2, half))
    return jnp.stack([-pairs[..., 1, :], pairs[..., 0, :]], axis=-2).reshape(w.shape)


def _inproj_kernel(xl_ref, xc_ref, mod_ref, pre_ref, w_ref, cos_ref, shi_ref, slo_ref, f_ref, q_ref, k_ref, v_ref,
                   fs_ref, *, n_lat_tiles):
    x = _pick(n_lat_tiles, xl_ref, xc_ref)
    h = _norm_mod(x, pre_ref[...], mod_ref[0, 1:2, :], mod_ref[0, 0:1, :]).astype(BF16)
    p = jnp.dot(h, w_ref[...], preferred_element_type=F32)
    cos, sin = cos_ref[...], shi_ref[...] - slo_ref[...]
    pairs = x.shape[0] // 2
    n_slabs = FOURIER_WIDTH // LANES
    for s in range(n_slabs):
        fs_ref[s] = p[:, s * LANES:(s + 1) * LANES]
    for parity in range(2):
        for s in range(n_slabs):
            lo = parity * FOURIER_WIDTH + s * LANES
            f_ref[:, lo:lo + LANES] = fs_ref[s, pl.ds(parity, pairs, stride=2), :].astype(BF16)
    scale = HEAD_DIM ** -0.5 * LOG2_E
    k0 = FOURIER_WIDTH + ATTN_WIDTH
    rot0 = IN_WIDTH
    for j in range(ATTN_WIDTH // LANES):
        lo, lo_r = FOURIER_WIDTH + j * LANES, rot0 + j * LANES
        q_ref[:, j * LANES:(j + 1) * LANES] = (
            (p[:, lo:lo + LANES] * cos + p[:, lo_r:lo_r + LANES] * sin) * scale).astype(BF16)
    k_r = rot0 + ATTN_WIDTH
    k_ref[...] = (p[:, k0:k0 + KV_WIDTH] * cos + p[:, k_r:k_r + KV_WIDTH] * sin).astype(BF16)
    v_ref[...] = p[:, k0 + KV_WIDTH:IN_WIDTH].astype(BF16)


def _rope_tables(seq, n_pad):
    pos = np.arange(seq)
    row = (pos // GRID_W).astype(np.float64)
    col = (pos % GRID_W).astype(np.float64)
    lane = np.arange(LANES)
    d = lane % HEAD_DIM
    j = d % (ROPE_AXIS_DIM // 2)
    inv = jnp.asarray(ROPE_BASE, F32) ** (-jnp.asarray(2 * j, F32) / ROPE_AXIS_DIM)
    use_col = jnp.asarray(d >= ROPE_AXIS_DIM)
    posv = jnp.where(use_col[None, :], jnp.asarray(col, F32)[:, None], jnp.asarray(row, F32)[:, None])
    ang = posv * inv[None, :]
    upper = jnp.asarray((d % ROPE_AXIS_DIM) >= ROPE_AXIS_DIM // 2)[None, :]
    cos, sin = jnp.cos(ang), jnp.sin(ang)
    sin_hi = jnp.where(upper, sin, 0.0)
    sin_lo = jnp.where(upper, 0.0, -sin)
    pad = lambda t, v: jnp.concatenate([t, jnp.full((n_pad, LANES), v, F32)], axis=0)
    return pad(cos, 1.0), pad(sin_hi, 0.0), pad(sin_lo, 0.0)


def _inproj(tok, x_lat, x_ctx, mod, pre_g, w_in_all, tables, i, layer, tm):
    per_batch = tok.seq // tm
    n_lat_tiles = tok.n_lat // tm
    tab_map = lambda i: (jnp.where(i < n_lat_tiles, i % per_batch, per_batch), 0)
    tab_spec = pl.BlockSpec((tm, LANES), tab_map)
    n = tok.n_all
    kern = functools.partial(_inproj_kernel, n_lat_tiles=n_lat_tiles)
    return pl.pallas_call(
        kern,
        out_shape=(jax.ShapeDtypeStruct((n // 2, 2 * FOURIER_WIDTH), BF16),
                   jax.ShapeDtypeStruct((n, ATTN_WIDTH), BF16),
                   jax.ShapeDtypeStruct((n, KV_WIDTH), BF16), jax.ShapeDtypeStruct((n, KV_WIDTH), BF16)),
        grid=(n // tm,),
        in_specs=[*tok.split_specs(tm, D_MODEL, x_lat is x_ctx), tok.mod_spec(tm, layer), _vec_spec(pre_g),
                  _layer_resident(i, w_in_all.shape[1:]), tab_spec, tab_spec, tab_spec],
        out_specs=(_row_spec(tm // 2, 2 * FOURIER_WIDTH), _row_spec(tm, ATTN_WIDTH), _row_spec(tm, KV_WIDTH),
                   _row_spec(tm, KV_WIDTH)),
        scratch_shapes=[pltpu.VMEM((FOURIER_WIDTH // LANES, tm, LANES), F32)],
        compiler_params=_cparams(1),
        name="inproj",
    )(x_lat, x_ctx, mod, pre_g[0], w_in_all, *tables)


def _pair_heads(w, axis):
    shape = w.shape
    split = shape[:axis] + (N_KV_HEADS, HEAD_GROUP, HEAD_DIM) + shape[axis + 1:]
    return jnp.swapaxes(w.reshape(split), axis, axis + 1).reshape(shape)


def _dft_tables(length):
    half = length // 2
    k = np.arange(half)[:, None]
    m = np.arange(half)[None, :]

    def tab(n):
        ang = 2.0 * np.pi * ((k * n) % length) / length
        t = np.concatenate([np.cos(ang), -np.sin(ang)], axis=1) / math.sqrt(length)
        return jnp.asarray(t.astype(np.float32)).astype(BF16)

    return tab(2 * m), tab(2 * m + 1)


def _chan_table():
    n = FOURIER_GROUP_DIM
    k = np.arange(n)
    ang = 2.0 * np.pi * ((k[:, None] * k[None, :]) % n) / n
    t = np.concatenate([np.cos(ang), np.sin(ang)], axis=1) / math.sqrt(n)
    return jnp.asarray(t.astype(np.float32)).astype(BF16)


def _fourier_one(f_ref, chan_ref, pos_e_ref, pos_o_ref, o_ref, stk_ref):
    gd = FOURIER_GROUP_DIM
    half = f_ref.shape[0]
    row_chunk = min(half, 512)
    for parity in range(2):
        for g in range(FOURIER_GROUPS):
            lanes = slice(parity * FOURIER_WIDTH + g * gd, parity * FOURIER_WIDTH + (g + 1) * gd)
            z = jnp.dot(f_ref[:, lanes], chan_ref[...], preferred_element_type=F32)
            stk_ref[parity, 0:half, g * gd:(g + 1) * gd] = z[:, :gd].astype(BF16)
            stk_ref[parity, half:2 * half, g * gd:(g + 1) * gd] = z[:, gd:].astype(BF16)
    for r in range(half // row_chunk):
        rows = slice(r * row_chunk, (r + 1) * row_chunk)
        even = jnp.dot(pos_e_ref[rows, :], stk_ref[0], preferred_element_type=F32)
        odd = jnp.dot(pos_o_ref[rows, :], stk_ref[1], preferred_element_type=F32)
        o_ref[rows, :] = (even + odd).astype(BF16)
        o_ref[half + r * row_chunk:half + (r + 1) * row_chunk, :] = (even - odd).astype(BF16)


def _fourier_kernel(fl_ref, fc_ref, chan_ref, ple_ref, plo_ref, pce_ref, pco_ref, ol_ref, oc_ref, stkl_ref, stkc_ref):
    _fourier_one(fl_ref, chan_ref, ple_ref, plo_ref, ol_ref, stkl_ref)
    _fourier_one(fc_ref, chan_ref, pce_ref, pco_ref, oc_ref, stkc_ref)


def _fourier(tok, f_pairs, chan, tabs_lat, tabs_ctx):
    ctx0 = tok.n_lat // tok.n_ctx
    pair_blk = lambda rows, m: pl.BlockSpec((rows // 2, 2 * FOURIER_WIDTH), m)
    lat_blk = pl.BlockSpec((tok.seq, FOURIER_WIDTH), lambda b: (b, 0))
    ctx_blk = pl.BlockSpec((tok.n_ctx, FOURIER_WIDTH), lambda b: (b, 0))
    tabs = (*tabs_lat, *tabs_ctx)
    return pl.pallas_call(
        _fourier_kernel,
        out_shape=(jax.ShapeDtypeStruct((tok.n_lat, FOURIER_WIDTH), BF16),
                   jax.ShapeDtypeStruct((tok.batch * tok.n_ctx, FOURIER_WIDTH), BF16)),
        grid=(tok.batch,),
        in_specs=[pair_blk(tok.seq, lambda b: (b, 0)), pair_blk(tok.n_ctx, lambda b: (ctx0 + b, 0)),
                  _resident(chan.shape), *[_resident(t.shape) for t in tabs]],
        out_specs=(lat_blk, ctx_blk),
        scratch_shapes=[pltpu.VMEM((2, tok.seq, FOURIER_WIDTH), BF16),
                        pltpu.VMEM((2, tok.n_ctx, FOURIER_WIDTH), BF16)],
        compiler_params=_cparams(1),
        name="fourier",
    )(f_pairs, f_pairs, chan, *tabs)


ATTN_QBLOCKS = 8


def _attn_kernel(sink_ref, q_ref, *refs, n_qb, n_steps, layer_idx, with_local):
    j = pl.program_id(1)
    blk = WINDOW
    n_loc = 3 * blk
    rows = HEAD_GROUP * blk
    nt = (((1,), (1,)), ((), ()))
    if with_local:
        kp_ref, km_ref, kn_ref, kx_ref, vp_ref, vm_ref, vn_ref, vx_ref, o_ref = refs
        k_cat = jnp.concatenate([kp_ref[...], km_ref[...], kn_ref[...]], axis=0)
        v_cat = jnp.concatenate([vp_ref[...], vm_ref[...], vn_ref[...]], axis=0)
    else:
        kx_ref, vx_ref, o_ref = refs

    def attend(half):
        r0 = half * blk
        q = jnp.concatenate([q_ref[r0:r0 + blk, i * LANES:(i + 1) * LANES] for i in range(HEAD_GROUP)], axis=0)
        k_ctx, v_ctx = kx_ref[...], vx_ref[...]
        if with_local:
            k_loc, v_loc = k_cat[r0:r0 + n_loc], v_cat[r0:r0 + n_loc]
            col_lo = jnp.where(j >= 1, 0, blk) if half == 0 else 0
            col_hi = jnp.where(j + 1 < n_steps, n_loc, 2 * blk) if half == n_qb - 1 else n_loc
            qi = lax.broadcasted_iota(jnp.int32, (blk, 1), 0)
            kj = lax.broadcasted_iota(jnp.int32, (blk, n_loc), 1)
            visible = (kj >= jnp.maximum(qi, col_lo)) & (kj <= jnp.minimum(qi + 2 * WINDOW, col_hi - 1))
            bias = jnp.concatenate([jnp.where(visible, 0.0, NEG_INF)] * HEAD_GROUP, axis=0)
        row_blk = lax.broadcasted_iota(jnp.int32, (rows, 1), 0) // blk
        lane_o = lax.broadcasted_iota(jnp.int32, (rows, LANES), 1)
        outs = []
        for kvh in range(N_KV_HEADS):
            def own_lanes(t):
                lane = lax.broadcasted_iota(jnp.int32, t.shape, 1)
                return jnp.where((lane >= kvh * HEAD_DIM) & (lane < (kvh + 1) * HEAD_DIM), t, jnp.zeros_like(t))
            s_ctx = lax.dot_general(q, own_lanes(k_ctx), nt, preferred_element_type=F32)
            sink = jnp.zeros((rows, 1), F32)
            for i in range(HEAD_GROUP):
                sink = jnp.where(row_blk == i, sink_ref[layer_idx, kvh * HEAD_GROUP + i] * LOG2_E, sink)
            m = jnp.maximum(jnp.max(s_ctx, axis=-1, keepdims=True), sink)
            if with_local:
                s_loc = lax.dot_general(q, own_lanes(k_loc), nt, preferred_element_type=F32) + bias
                m = jnp.maximum(m, jnp.max(s_loc, axis=-1, keepdims=True))
            p_ctx = jnp.exp2(s_ctx - m)
            denom = jnp.sum(p_ctx, axis=-1, keepdims=True) + jnp.exp2(sink - m)
            o = jnp.dot(p_ctx.astype(BF16), v_ctx, preferred_element_type=F32)
            if with_local:
                p_loc = jnp.exp2(s_loc - m)
                denom = denom + jnp.sum(p_loc, axis=-1, keepdims=True)
                o = o + jnp.dot(p_loc.astype(BF16), v_loc, preferred_element_type=F32)
            outs.append(o / denom)
        merged = jnp.where(lane_o < HEAD_DIM, outs[0], outs[1])
        for i in range(HEAD_GROUP):
            o_ref[r0:r0 + blk, i * LANES:(i + 1) * LANES] = merged[i * blk:(i + 1) * blk, :].astype(BF16)

    for half in range(n_qb):
        attend(half)


def _attention(tok, q, k, v, sink_all, layer_idx):
    blk = WINDOW
    n_qblk = tok.seq // blk
    qb = math.gcd(ATTN_QBLOCKS, n_qblk)
    n_steps = n_qblk // qb
    ctx0 = tok.n_lat // tok.n_ctx
    smem = pl.BlockSpec(memory_space=pltpu.SMEM)
    ctx_blk = pl.BlockSpec((tok.n_ctx, KV_WIDTH), lambda b, j: (ctx0 + b, 0))

    main_map = lambda b, j: (b * n_steps + j, 0)
    prev_map = lambda b, j: (b * n_qblk + jnp.maximum(j * qb - 1, 0), 0)
    next_map = lambda b, j: (b * n_qblk + jnp.minimum(j * qb + qb, n_qblk - 1), 0)
    kv_specs = [pl.BlockSpec((blk, KV_WIDTH), prev_map), pl.BlockSpec((qb * blk, KV_WIDTH), main_map),
                pl.BlockSpec((blk, KV_WIDTH), next_map), ctx_blk]
    lat = pl.pallas_call(
        functools.partial(_attn_kernel, n_qb=qb, n_steps=n_steps, layer_idx=layer_idx, with_local=True),
        out_shape=jax.ShapeDtypeStruct((tok.n_lat, ATTN_WIDTH), BF16),
        grid=(tok.batch, n_steps),
        in_specs=[smem, pl.BlockSpec((qb * blk, ATTN_WIDTH), main_map), *kv_specs, *kv_specs],
        out_specs=pl.BlockSpec((qb * blk, ATTN_WIDTH), main_map),
        compiler_params=_cparams(2),
        name="window_attention",
    )(sink_all, q, k, k, k, k, v, v, v, v)

    n_cb = tok.n_ctx // blk
    ctx = pl.pallas_call(
        functools.partial(_attn_kernel, n_qb=n_cb, n_steps=1, layer_idx=layer_idx, with_local=False),
        out_shape=jax.ShapeDtypeStruct((tok.batch * tok.n_ctx, ATTN_WIDTH), BF16),
        grid=(tok.batch, 1),
        in_specs=[smem, pl.BlockSpec((tok.n_ctx, ATTN_WIDTH), lambda b, j: (ctx0 + b, 0)), ctx_blk, ctx_blk],
        out_specs=pl.BlockSpec((tok.n_ctx, ATTN_WIDTH), lambda b, j: (b, 0)),
        compiler_params=_cparams(2),
        name="context_attention",
    )(sink_all, q, k, v)
    return lat, ctx


def _piece_perm():
    idx = np.arange(D_MODEL)
    a, b, c = idx // LANES, (idx // SSM_GROUP_DIM) % SSM_PIECES, idx % SSM_GROUP_DIM
    perm = np.zeros((D_MODEL, D_MODEL), np.float32)
    perm[idx, b * LANES + a * SSM_GROUP_DIM + c] = 1.0
    return jnp.asarray(perm).astype(BF16)


def _slab_pitch(blk):
    return blk + SUBLANES


S5_PRE_BATCHES = 8


def _s5_pre_kernel(*refs, blk, batch, n_blocks):
    nb = S5_PRE_BATCHES
    x_refs, mod_refs = refs[:nb], refs[nb:2 * nb]
    pre_ref, perm_ref, u_ref, hs_ref = refs[2 * nb:]
    p, bb = pl.program_id(0), pl.program_id(1)
    pitch = _slab_pitch(blk)

    @pl.when(p < n_blocks)
    def _():
        slot0 = (p % 2) * SSM_SLABS
        for k in range(nb):
            h = _norm_mod(x_refs[k][...], pre_ref[...], mod_refs[k][0, 1:2, :], mod_refs[k][0, 0:1, :])
            row0 = pl.multiple_of((bb * nb + k) * pitch, SUBLANES)
            for s in range(SSM_SLABS):
                hs_ref[slot0 + s, pl.ds(row0, blk), :] = h[:, s * LANES:(s + 1) * LANES]

    @pl.when(p >= 1)
    def _():
        half = SSM_CHUNK // 2
        cpb = blk // SSM_CHUNK
        rows_h = cpb * batch
        for k in range(nb):
            s = bb * nb + k
            src = ((p - 1) % 2) * SSM_SLABS + s
            lhs = []
            for hh in range(2):
                for i in range(cpb):
                    t0 = i * SSM_CHUNK + hh * half
                    lhs.append(jnp.concatenate(
                        [hs_ref[src, pl.ds(t0 + j, batch, stride=pitch), :] for j in range(half)], axis=1))
            lhs = jnp.concatenate(lhs, axis=0).astype(BF16)
            out = jnp.dot(lhs, perm_ref[...], preferred_element_type=F32).astype(BF16)
            for hh in range(2):
                for g in range(SSM_PIECES):
                    u_ref[s * SSM_PIECES + g, :, hh * LANES:(hh + 1) * LANES] = (
                        out[hh * rows_h:(hh + 1) * rows_h, g * LANES:(g + 1) * LANES])


def _s5_pre(tok, xa, mod, pre_g, perm, layer, blk):
    assert tok.batch == SSM_SLABS
    (n_blocks, _), row_map, mod_map = tok.pos_grid(blk, True, layer)
    n_lat_blk = tok.seq // blk
    n_ctx_blk = tok.n_ctx // blk
    cpb = blk // SSM_CHUNK
    n_chunks = (tok.seq + tok.n_ctx) // SSM_CHUNK
    last = n_blocks - 1

    def u_map(p, b):
        q = jnp.maximum(p - 1, 0)
        return (0, jnp.where(q < n_lat_blk, n_ctx_blk + q, q - n_lat_blk), 0)

    nb = S5_PRE_BATCHES
    assert tok.batch % nb == 0
    of_batch = lambda m, k: (lambda p, bb: m(jnp.minimum(p, last), bb * nb + k))
    kern = functools.partial(_s5_pre_kernel, blk=blk, batch=tok.batch, n_blocks=n_blocks)
    return pl.pallas_call(
        kern,
        out_shape=jax.ShapeDtypeStruct((SSM_GROUPS, n_chunks * tok.batch, SSM_CW), BF16),
        grid=(n_blocks + 1, tok.batch // nb),
        in_specs=[*[pl.BlockSpec((blk, D_MODEL), of_batch(row_map, k)) for k in range(nb)],
                  *[pl.BlockSpec((None, 1, N_MOD, D_MODEL), of_batch(mod_map, k)) for k in range(nb)],
                  _vec_spec(pre_g), _resident((D_MODEL, D_MODEL))],
        out_specs=pl.BlockSpec((SSM_GROUPS, cpb * tok.batch, SSM_CW), u_map),
        scratch_shapes=[pltpu.VMEM((2 * SSM_SLABS, tok.batch * _slab_pitch(blk), LANES), F32)],
        compiler_params=_cparams(2),
        name="s5_pre",
    )(*([xa] * nb), *([mod] * nb), pre_g[0], perm)


def _cmul(ar, ai, br, bi):
    return ar * br - ai * bi, ar * bi + ai * br


def _s5_table_plan():
    t_n = SSM_CHUNK
    t = np.arange(t_n)
    return [
        [(0, t_n - 1 - t, 'b', 're'), (1, t, 'b', 're'), (0, t_n - 1 - t, 'b', 'im'), (1, t, 'b', 'im')],
        [(0, -t, 'b', 're'), (0, -t, 'b', 'im'), (1, t, 'b', 're'), (1, t, 'b', 'im')],
        [(0, t, 'c', 're'), (0, t, 'c', '-im'), (1, -t, 'c', 're'), (1, -t, 'c', '-im')],
        [(0, t + 1, 'c', 're'), (1, t_n - t, 'c', 're'), (0, t + 1, 'c', '-im'), (1, t_n - t, 'c', '-im')],
    ]


def _s5_operands(a_re, a_im, log_dt, b_re, b_im, c_re, c_im):
    t_n = SSM_CHUNK
    per_group = lambda x: jnp.swapaxes(x.astype(F32), 1, 2)
    l_re, l_im = per_group(a_re), per_group(a_im)
    dt = jnp.exp(per_group(log_dt))[..., None]
    z_re, z_im = l_re * dt, l_im * dt
    n_all = np.arange(1 - t_n, t_n + 1)
    positive = (n_all > 0)[:, None]
    mag, mag_inv = jnp.exp(z_re), jnp.exp(-z_re)
    up = (mag * jnp.cos(z_im), mag * jnp.sin(z_im))
    down = (mag_inv * jnp.cos(z_im), -mag_inv * jnp.sin(z_im))
    shape = z_re.shape[:-1] + (len(n_all), z_re.shape[-1])
    acc = (jnp.ones(shape, F32), jnp.zeros(shape, F32))
    for k in range(int(t_n).bit_length()):
        bit = (((np.abs(n_all) >> k) & 1) == 1)[:, None]
        f_re = jnp.where(bit, jnp.where(positive, up[0][..., None, :], down[0][..., None, :]), 1.0)
        f_im = jnp.where(bit, jnp.where(positive, up[1][..., None, :], down[1][..., None, :]), 0.0)
        acc = _cmul(acc[0], acc[1], f_re, f_im)
        up, down = _cmul(*up, *up), _cmul(*down, *down)
    pow_re, pow_im = acc
    a1_re, a1_im = pow_re[:, :, :, t_n, :], pow_im[:, :, :, t_n, :]
    den = l_re * l_re + l_im * l_im
    r_re = ((a1_re - 1.0) * l_re + a1_im * l_im) / den
    r_im = (a1_im * l_re - (a1_re - 1.0) * l_im) / den
    bb_re, bb_im = _cmul(r_re[..., None], r_im[..., None], per_group(b_re), per_group(b_im))
    base = jnp.stack([jnp.swapaxes(bb_re, -1, -2), jnp.swapaxes(bb_im, -1, -2), per_group(c_re), per_group(c_im)],
                     axis=3)
    n_l, n_g = base.shape[0], base.shape[1]
    base = base.reshape(n_l, n_g, 8, SSM_GROUP_DIM, SSM_STATE)
    signed = jnp.concatenate([base, -base], axis=2)

    last = 2 * t_n - 1
    dec = jnp.concatenate([pow_re[:, :, 0, last], pow_re[:, :, 1, last], pow_im[:, :, 0, last], pow_im[:, :, 1, last]],
                          axis=-1)[:, :, None, :]
    return pow_re, pow_im, signed, dec


def _s5_kernel(u_ref, pre_ref, pim_ref, sg_ref, dec_ref, y_ref, v_ref, xin_ref, *, batch, n_chunks, n_ctx_chunks):
    gps = SSM_GROUPS_PER_STEP
    ns = SSM_STATE
    ns2 = 2 * ns
    cw = SSM_CW
    t_n = SSM_CHUNK
    nt = (((1,), (1,)), ((), ()))
    plan = _s5_table_plan()

    def table(g, k):
        p_re, p_im, w_a, w_b = [], [], [], []
        for d, expo, w, part in plan[k]:
            lo = int(min(expo)) + t_n - 1
            run_re, run_im = pre_ref[g, d, lo:lo + t_n, :], pim_ref[g, d, lo:lo + t_n, :]
            if expo[1] < expo[0]:
                flip = lambda r: jnp.concatenate([r[t_n - 1 - t:t_n - t, :] for t in range(t_n)], axis=0)
                run_re, run_im = flip(run_re), flip(run_im)
            p_re.append(run_re)
            p_im.append(run_im)
            re, im = d * 4 + (0 if w == 'b' else 2), d * 4 + (1 if w == 'b' else 3)
            k_a, k_b = {'re': (re, im + 8), 'im': (im, re), '-im': (im + 8, re + 8)}[part]
            w_a.append(sg_ref[g, k_a])
            w_b.append(sg_ref[g, k_b])
        p_re, p_im, w_a, w_b = (jnp.concatenate(x, axis=1) for x in (p_re, p_im, w_a, w_b))
        return jnp.concatenate([p_re[t:t + 1, :] * w_a + p_im[t:t + 1, :] * w_b for t in range(t_n)], axis=0)

    for g in range(gps):
        v_ref[g] = jnp.dot(u_ref[g], table(g, 0).astype(BF16), preferred_element_type=F32)
    a_re = jnp.broadcast_to(dec_ref[:, :, 0:ns2], (gps, batch, ns2))
    a_im = jnp.broadcast_to(dec_ref[:, :, ns2:2 * ns2], (gps, batch, ns2))
    is_fwd = lax.broadcasted_iota(jnp.int32, (gps, batch, ns2), 2) < ns

    def step(k, carry):
        x_re, x_im = carry
        kb = jnp.where(k < n_ctx_chunks, n_ctx_chunks - 1 - k, n_chunks + n_ctx_chunks - 1 - k)
        rf = pl.multiple_of(k * batch, batch)
        rb = pl.multiple_of(kb * batch, batch)
        xin_ref[:, pl.ds(rf, batch), 0:ns] = x_re[:, :, 0:ns]
        xin_ref[:, pl.ds(rb, batch), ns:ns2] = x_re[:, :, ns:ns2]
        xin_ref[:, pl.ds(rf, batch), ns2:ns2 + ns] = x_im[:, :, 0:ns]
        xin_ref[:, pl.ds(rb, batch), ns2 + ns:2 * ns2] = x_im[:, :, ns:ns2]
        v_re = jnp.where(is_fwd, v_ref[:, pl.ds(rf, batch), 0:ns2], v_ref[:, pl.ds(rb, batch), 0:ns2])
        v_im = jnp.where(is_fwd, v_ref[:, pl.ds(rf, batch), ns2:2 * ns2], v_ref[:, pl.ds(rb, batch), ns2:2 * ns2])
        return a_re * x_re - a_im * x_im + v_re, a_re * x_im + a_im * x_re + v_im

    zero = jnp.zeros((gps, batch, ns2), F32)
    lax.fori_loop(0, n_chunks, step, (zero, zero))
    t_in = lax.broadcasted_iota(jnp.int32, (cw, cw), 0) // SSM_GROUP_DIM
    t_out = lax.broadcasted_iota(jnp.int32, (cw, cw), 1) // SSM_GROUP_DIM

    def split(x):
        hi = x.astype(BF16)
        return hi, (x - hi.astype(F32)).astype(BF16)

    def lag_kernel(e, ft, lanes):
        (eh, el), (fh, fl) = split(e[:, lanes]), split(ft[:, lanes])
        return (lax.dot_general(eh, fh, nt, preferred_element_type=F32)
                + lax.dot_general(eh, fl, nt, preferred_element_type=F32)
                + lax.dot_general(el, fh, nt, preferred_element_type=F32))

    for g in range(gps):
        e, ft = table(g, 1), table(g, 2)
        m = (jnp.where(t_out >= t_in, lag_kernel(e, ft, slice(0, ns2)), 0.0)
             + jnp.where(t_in >= t_out, lag_kernel(e, ft, slice(ns2, 2 * ns2)), 0.0)).astype(BF16)
        y_ref[g] = (jnp.dot(u_ref[g], m, preferred_element_type=F32)
                    + lax.dot_general(xin_ref[g].astype(BF16), table(g, 3).astype(BF16), nt,
                                      preferred_element_type=F32)).astype(BF16)


def _s5(u_t, operands, layer, batch, n_chunks, n_ctx_chunks):
    pow_re, pow_im, signed, dec = operands
    gps = SSM_GROUPS_PER_STEP
    rows = n_chunks * batch
    kern = functools.partial(_s5_kernel, batch=batch, n_chunks=n_chunks, n_ctx_chunks=n_ctx_chunks)
    gspec = lambda r, c: pl.BlockSpec((gps, r, c), lambda i: (i, 0, 0))
    lspec = lambda shape: pl.BlockSpec((None, gps) + shape, lambda i: (layer, i) + (0,) * len(shape))
    return pl.pallas_call(
        kern,
        out_shape=jax.ShapeDtypeStruct((SSM_GROUPS, rows, SSM_CW), BF16),
        grid=(SSM_GROUPS // gps,),
        in_specs=[gspec(rows, SSM_CW), lspec(pow_re.shape[2:]), lspec(pow_im.shape[2:]), lspec(signed.shape[2:]),
                  lspec(dec.shape[2:])],
        out_specs=gspec(rows, SSM_CW),
        scratch_shapes=[pltpu.VMEM((gps, rows, SSM_CW), F32), pltpu.VMEM((gps, rows, SSM_CW), F32)],
        compiler_params=_cparams(1),
        name="s5_scan",
    )(u_t, pow_re, pow_im, signed, dec)


def _s5_unpack_kernel(y_ref, perm_ref, *refs, blk, batch, n_lat_blk):
    outs, (ys_ref, tmp_ref) = refs[:-2], refs[-2:]
    p = pl.program_id(0)
    half = SSM_CHUNK // 2
    cpb = blk // SSM_CHUNK
    rows_h = cpb * batch

    lhs = jnp.concatenate(
        [jnp.concatenate([y_ref[s * SSM_PIECES + g, :, hh * LANES:(hh + 1) * LANES]
                          for g in range(SSM_PIECES)], axis=1)
         for s in range(SSM_SLABS) for hh in range(2)], axis=0)
    tmp_ref[...] = jnp.dot(lhs, perm_ref[...], preferred_element_type=F32)

    def slab(s, carry):
        for hh in range(2):
            for i in range(cpb):
                r0 = pl.multiple_of((s * 2 + hh) * rows_h + i * batch, batch)
                t0 = i * SSM_CHUNK + hh * half
                for j in range(half):
                    ys_ref[s, (t0 + j) * batch:(t0 + j + 1) * batch, :] = (
                        tmp_ref[pl.ds(r0, batch), j * LANES:(j + 1) * LANES])
        return carry

    lax.fori_loop(0, SSM_SLABS, slab, 0)

    def write(o_ref):
        for b in range(batch):
            for s in range(SSM_SLABS):
                o_ref[b, :, s * LANES:(s + 1) * LANES] = ys_ref[s, pl.ds(b, blk, stride=batch), :].astype(BF16)

    if len(outs) == 1:
        write(outs[0])
    else:
        pl.when(p < n_lat_blk)(lambda: write(outs[0]))
        pl.when(p >= n_lat_blk)(lambda: write(outs[1]))


def _s5_unpack(tok, y_t, perm, blk, with_ctx):
    n_lat_blk = tok.seq // blk
    n_ctx_blk = tok.n_ctx // blk
    cpb = blk // SSM_CHUNK
    y_map = lambda p: (0, jnp.where(p < n_lat_blk, n_ctx_blk + p, p - n_lat_blk), 0)
    out_blk = lambda m: pl.BlockSpec((tok.batch, blk, D_MODEL), m)
    out_shape = [jax.ShapeDtypeStruct((tok.batch, tok.seq, D_MODEL), BF16)]
    out_specs = [out_blk(lambda p: (0, jnp.minimum(p, n_lat_blk - 1), 0))]
    if with_ctx:
        out_shape.append(jax.ShapeDtypeStruct((tok.batch, tok.n_ctx, D_MODEL), BF16))
        out_specs.append(out_blk(lambda p: (0, jnp.maximum(p - n_lat_blk, 0), 0)))
    kern = functools.partial(_s5_unpack_kernel, blk=blk, batch=tok.batch, n_lat_blk=n_lat_blk)
    outs = pl.pallas_call(
        kern,
        out_shape=out_shape,
        grid=(n_lat_blk + (n_ctx_blk if with_ctx else 0),),
        in_specs=[pl.BlockSpec((SSM_GROUPS, cpb * tok.batch, SSM_CW), y_map), _resident((D_MODEL, D_MODEL))],
        out_specs=out_specs,
        scratch_shapes=[pltpu.VMEM((SSM_SLABS, tok.batch * blk, LANES), F32),
                        pltpu.VMEM((SSM_SLABS * 2 * cpb * tok.batch, D_MODEL), F32)],
        compiler_params=_cparams(1),
        name="s5_unpack",
    )(y_t, perm)
    y_lat = outs[0].reshape(tok.n_lat, D_MODEL)
    return y_lat, (outs[1].reshape(tok.batch * tok.n_ctx, D_MODEL) if with_ctx else None)


def _glu_ffn_kernel(x_ref, yl_ref, yc_ref, d_ref, mod_ref, mpre_ref, mpost_ref, pre_ref, post_ref, wg_ref, w1_ref,
                    w2_ref, o_ref, acc_ref, *, n_lat_tiles):
    x = x_ref[...]
    h = _norm_mod(x, mpre_ref[...], mod_ref[0, 1:2, :], mod_ref[0, 0:1, :])
    y = _pick(n_lat_tiles, yl_ref, yc_ref).astype(F32) + d_ref[...] * h
    z = jnp.dot(jax.nn.gelu(y).astype(BF16), wg_ref[...], preferred_element_type=F32)
    out = z[:, :D_MODEL] * jax.nn.sigmoid(z[:, D_MODEL:])
    x = x + mod_ref[0, 2:3, :] * _rms(out, mpost_ref[...])
    _ffn_body(x, mod_ref, pre_ref, post_ref, w1_ref, w2_ref, o_ref, acc_ref)


def _glu_ffn(tok, xa, n_rows, y_lat, y_ctx, d_skip, mod, mix_pre_g, mix_post_g, pre_g, post_g, glu_all, w1_all,
             w2_all, i, layer, tm):
    if y_ctx is None:
        y_ctx = y_lat
    kern = functools.partial(_glu_ffn_kernel, n_lat_tiles=tok.n_lat // tm)
    return pl.pallas_call(
        kern,
        out_shape=jax.ShapeDtypeStruct((n_rows, D_MODEL), F32),
        grid=(n_rows // tm,),
        in_specs=[_row_spec(tm, D_MODEL), *tok.split_specs(tm, D_MODEL, False), _vec_spec(d_skip),
                  tok.mod_spec(tm, layer),
                  _vec_spec(mix_pre_g), _vec_spec(mix_post_g), _vec_spec(pre_g), _vec_spec(post_g),
                  _layer_resident(i, (D_MODEL, 2 * D_MODEL)),
                  _layer_resident(layer, (D_MODEL, D_FF)), _layer_resident(layer, (D_FF, D_MODEL))],
        out_specs=_row_spec(tm, D_MODEL),
        scratch_shapes=[pltpu.VMEM((tm, D_MODEL), F32)],
        compiler_params=_cparams(1),
        name="glu_ffn",
    )(xa, y_lat, y_ctx, d_skip[0], mod, mix_pre_g[0], mix_post_g[0], pre_g[0], post_g[0], glu_all, w1_all, w2_all)


def _tile(limit, *sizes):
    tm = limit
    while any(s % tm for s in sizes):
        tm //= 2
    return tm


def kernel(x, c, ctx, c_ctx, mod_w, mod_b, mix_pre_g, mix_post_g, ffn_pre_g, ffn_post_g, ffn_w1, ffn_w2,
           even_w_in, even_w_out, even_sink, ssm_a_re, ssm_a_im, ssm_log_dt, ssm_b_re, ssm_b_im, ssm_c_re,
           ssm_c_im, ssm_d, ssm_glu_w):
    batch, seq, _ = x.shape
    n_ctx = ctx.shape[1]
    tok = _Tokens(batch, seq, n_ctx)
    assert seq % WINDOW == 0 and n_ctx % WINDOW == 0 and tok.n_lat % n_ctx == 0
    assert batch == SUBLANES
    tm = _tile(256, seq, n_ctx)
    tm_wide = _tile(512, seq, batch * n_ctx)
    tm_in = _tile(1024, seq, batch * n_ctx)

    x_lat, x_ctx = x.reshape(tok.n_lat, D_MODEL), ctx.astype(x.dtype).reshape(-1, D_MODEL)

    n_cond = 2 * SUBLANES
    cond = jnp.zeros((n_cond, D_MODEL), F32).at[:batch].set(c).at[batch].set(c_ctx)
    mod = _modulation(cond, mod_w, mod_b).reshape(DEPTH, n_cond, N_MOD, D_MODEL)

    rope = _rope_tables(seq, tm_in)
    piece_perm = _piece_perm()
    dft_chan, dft_lat, dft_ctx = _chan_table(), _dft_tables(seq), _dft_tables(n_ctx)
    rows3 = lambda t: t.reshape(t.shape[0], 1, t.shape[1])
    w1_all, w2_all, glu_all = ffn_w1.astype(BF16), ffn_w2.astype(BF16), ssm_glu_w.astype(BF16)
    q0, k0 = FOURIER_WIDTH, FOURIER_WIDTH + ATTN_WIDTH
    w_q, w_k = _pair_heads(even_w_in[:, :, q0:k0], 2), even_w_in[:, :, k0:k0 + KV_WIDTH]
    w_in_all = jnp.concatenate([even_w_in[:, :, :q0], w_q, even_w_in[:, :, k0:],
                                _rotate_half_columns(w_q), _rotate_half_columns(w_k)], axis=2).astype(BF16)
    wf_all = even_w_out[:, :FOURIER_WIDTH].astype(BF16)
    wa_all = _pair_heads(even_w_out[:, FOURIER_WIDTH:], 1).astype(BF16)
    s5_operands = _s5_operands(ssm_a_re, ssm_a_im, ssm_log_dt, ssm_b_re, ssm_b_im, ssm_c_re, ssm_c_im)

    for layer in range(DEPTH):
        need_ctx = layer < DEPTH - 1
        n_rows = tok.n_all if need_ctx else tok.n_lat
        i = layer // 2
        mix_pre, mix_post = (rows3(mix_pre_g), layer), (rows3(mix_post_g), layer)
        ffn_pre, ffn_post = (rows3(ffn_pre_g), layer), (rows3(ffn_post_g), layer)
        if layer % 2 == 0:
            f, q, k, v = _inproj(tok, x_lat, x_ctx, mod, mix_pre, w_in_all, rope, i, layer, tm_in)
            fm_lat, fm_ctx = _fourier(tok, f, dft_chan, dft_lat, dft_ctx)
            ao_lat, ao_ctx = _attention(tok, q, k, v, even_sink, i)
            xa = _mix_ffn(tok, x_lat, x_ctx, fm_lat, fm_ctx, ao_lat, ao_ctx, mod, mix_post, ffn_pre, ffn_post,
                          wf_all, wa_all, w1_all, w2_all, i, layer, tm_wide)
        else:
            u_t = _s5_pre(tok, xa, mod, mix_pre, piece_perm, layer, tm)
            y_t = _s5(u_t, s5_operands, i, batch, (seq + n_ctx) // SSM_CHUNK, n_ctx // SSM_CHUNK)
            y_lat, y_ctx = _s5_unpack(tok, y_t, piece_perm, tm, need_ctx)
            xa = _glu_ffn(tok, xa, n_rows, y_lat, y_ctx, (rows3(ssm_d), i), mod, mix_pre, mix_post, ffn_pre, ffn_post,
                          glu_all, w1_all, w2_all, i, layer, tm_wide)
        x_lat = x_ctx = xa
    return xa[:tok.n_lat].reshape(batch, seq, D_MODEL)
```
